```python
import math
import jax, jax.numpy as jnp
from jax import lax
import numpy as np

D_MODEL = 2048
BATCH = 8
SEQ = 4096
DEPTH = 2

N_A_LAYERS = DEPTH // 2
N_B_LAYERS = DEPTH - N_A_LAYERS
CONV_WIDTH = 31
CONV_DIM = D_MODEL
HEAD_DIM = 128
N_HEADS = D_MODEL // HEAD_DIM
ATTN_DIM = N_HEADS * HEAD_DIM
DILATED_GROUPS = ((128, 1), (512, 4), (2048, 16))
N_GROUPS = len(DILATED_GROUPS)
BLOCK = 128
N_BUCKETS = 32
MAX_EXACT = N_BUCKETS // 2
MAX_DISTANCE = 2048
EPS = 1e-6

kernel_name = "yoco_conformer_dilated_hybrid"


def rmsnorm(x, g):
    xf = x.astype(jnp.float32)
    y = xf * lax.rsqrt(jnp.mean(xf * xf, axis=-1, keepdims=True) + EPS)
    return (y * g.astype(jnp.float32)).astype(x.dtype)


def t5_bucket(dist):
    large = MAX_EXACT + (jnp.log(jnp.maximum(dist, 1).astype(jnp.float32) / MAX_EXACT)
                         / math.log(MAX_DISTANCE / MAX_EXACT)
                         * (N_BUCKETS - MAX_EXACT)).astype(jnp.int32)
    large = jnp.minimum(large, N_BUCKETS - 1)
    return jnp.where(dist < MAX_EXACT, dist, large)


def conformer_mixer(h, w_in, conv_w, conv_b, ln_g, ln_b, w_out):
    proj = jnp.einsum('bsd,de->bse', h, w_in)
    a, b, z = jnp.split(proj, 3, axis=-1)
    u = a * jax.nn.sigmoid(b)
    c = lax.conv_general_dilated(u, conv_w[:, None, :].astype(u.dtype), (1,),
                                 [(CONV_WIDTH - 1, 0)],
                                 dimension_numbers=('NWC', 'WIO', 'NWC'),
                                 feature_group_count=CONV_DIM) + conv_b
    cf = c.astype(jnp.float32)
    mu = jnp.mean(cf, axis=-1, keepdims=True)
    var = jnp.mean(jnp.square(cf - mu), axis=-1, keepdims=True)
    cn = (cf - mu) * lax.rsqrt(var + EPS) * ln_g.astype(jnp.float32) + ln_b.astype(jnp.float32)
    y = jax.nn.silu(cn).astype(h.dtype) * jax.nn.silu(z)
    return jnp.einsum('bse,ed->bsd', y, w_out)


def dilated_branch(q, k, v, window, dil, rel_bias):
    B, S, H, Dh = q.shape
    L = S // dil
    w_sub = window // dil
    nb = -(-L // BLOCK)
    Lp = nb * BLOCK
    N = B * dil

    def to_blocks(t):
        t = t.reshape(B, L, dil, H, Dh).transpose(0, 2, 1, 3, 4).reshape(N, L, H, Dh)
        t = jnp.pad(t, ((0, 0), (0, Lp - L), (0, 0), (0, 0)))
        return t.reshape(N, nb, BLOCK, H, Dh)

    def with_prev(t):
        prev = jnp.pad(t[:, :-1], ((0, 0), (1, 0), (0, 0), (0, 0), (0, 0)))
        return jnp.concatenate([prev, t], axis=2)

    qb = to_blocks(q).astype(jnp.float32)
    kk = with_prev(to_blocks(k)).astype(jnp.float32)
    vv = with_prev(to_blocks(v)).astype(jnp.float32)

    delta = (jnp.arange(BLOCK)[:, None] + BLOCK) - jnp.arange(2 * BLOCK)[None, :]
    local = (delta >= 0) & (delta <= w_sub)
    key_pos = jnp.arange(nb)[:, None] * BLOCK - BLOCK + jnp.arange(2 * BLOCK)[None, :]
    mask = local[None] & (key_pos >= 0)[:, None, :]
    bias = rel_bias[t5_bucket(jnp.clip(delta, 0) * dil)]
    bias = bias.astype(jnp.float32).transpose(2, 0, 1)

    s = jnp.einsum('nbqhd,nbkhd->nbhqk', qb, kk) * (HEAD_DIM ** -0.5) + bias
    s = jnp.where(mask[None, :, None], s, -jnp.inf)
    m = jnp.max(s, axis=-1, keepdims=True)
    p = jnp.exp(s - m)
    den = jnp.sum(p, axis=-1, keepdims=True)
    o = jnp.einsum('nbhqk,nbkhd->nbqhd', p, vv) / den.transpose(0, 1, 3, 2, 4)
    lse = (m + jnp.log(den))[..., 0].transpose(0, 1, 3, 2)

    def from_blocks(t):
        rest = t.shape[3:]
        t = t.reshape(N, Lp, *rest)[:, :L].reshape(B, dil, L, *rest)
        return jnp.swapaxes(t, 1, 2).reshape(B, S, *rest)

    return from_blocks(o), from_blocks(lse)


def shared_kv(x, kv_norm, w_kv):
    B, S, _ = x.shape
    kv = jnp.einsum('bsd,de->bse', rmsnorm(x, kv_norm), w_kv)
    parts = jnp.split(kv, 2 * N_GROUPS, axis=-1)
    return [(parts[2 * g].reshape(B, S, N_HEADS, HEAD_DIM),
             parts[2 * g + 1].reshape(B, S, N_HEADS, HEAD_DIM)) for g in range(N_GROUPS)]


def dilated_mixer(h, w_in, w_out, kv, rel_bias):
    B, S, _ = h.shape
    parts = jnp.split(jnp.einsum('bsd,de->bse', h, w_in), N_GROUPS + 1, axis=-1)
    outs, lses = [], []
    for g, (window, dil) in enumerate(DILATED_GROUPS):
        q = parts[g].reshape(B, S, N_HEADS, HEAD_DIM)
        o, lse = dilated_branch(q, kv[g][0], kv[g][1], window, dil, rel_bias)
        outs.append(o)
        lses.append(lse)
    alpha = jax.nn.softmax(jnp.stack(lses), axis=0)
    o = jnp.sum(alpha[..., None] * jnp.stack(outs), axis=0).reshape(B, S, ATTN_DIM)
    y = o.astype(h.dtype) * jax.nn.silu(parts[-1])
    return jnp.einsum('bse,ed->bsd', y, w_out)


def _fwd_setup_inputs(seed: int = 0) -> dict:
    key = jax.random.key(seed)
    ks = jax.random.split(key, 16)
    f32 = jnp.float32
    D, E = D_MODEL, CONV_DIM
    nrm = lambda k, shape, scale: jax.random.normal(k, shape, f32) * scale
    return {
        "x": nrm(ks[0], (BATCH, SEQ, D), 1.0),
        "a_norm": 1.0 + nrm(ks[1], (N_A_LAYERS, D), 0.02),
        "a_w_in": nrm(ks[2], (N_A_LAYERS, D, 3 * E), D ** -0.5),
        "a_conv_w": nrm(ks[3], (N_A_LAYERS, CONV_WIDTH, E), CONV_WIDTH ** -0.5),
        "a_conv_b": nrm(ks[4], (N_A_LAYERS, E), 0.02),
        "a_ln_g": 1.0 + nrm(ks[5], (N_A_LAYERS, E), 0.02),
        "a_ln_b": nrm(ks[6], (N_A_LAYERS, E), 0.02),
        "a_w_out": nrm(ks[7], (N_A_LAYERS, E, D), E ** -0.5),
        "kv_norm": 1.0 + nrm(ks[8], (D,), 0.02),
        "w_kv": nrm(ks[9], (D, 2 * N_GROUPS * ATTN_DIM), D ** -0.5),
        "b_norm": 1.0 + nrm(ks[10], (N_B_LAYERS, D), 0.02),
        "b_w_in": nrm(ks[11], (N_B_LAYERS, D, (N_GROUPS + 1) * ATTN_DIM), D ** -0.5),
        "b_w_out": nrm(ks[12], (N_B_LAYERS, ATTN_DIM, D), ATTN_DIM ** -0.5),
        "rel_bias": nrm(ks[13], (N_BUCKETS, N_HEADS), 0.2),
        "final_norm": 1.0 + nrm(ks[14], (D,), 0.02),
    }


def _fwd_reference(x, a_norm, a_w_in, a_conv_w, a_conv_b, a_ln_g, a_ln_b, a_w_out,
              kv_norm, w_kv, b_norm, b_w_in, b_w_out, rel_bias, final_norm):
    kv = None
    for layer in range(DEPTH):
        if layer < N_A_LAYERS:
            i = layer
            x = x + conformer_mixer(rmsnorm(x, a_norm[i]), a_w_in[i], a_conv_w[i], a_conv_b[i],
                                    a_ln_g[i], a_ln_b[i], a_w_out[i])
        else:
            if kv is None:
                kv = shared_kv(x, kv_norm, w_kv)
            j = layer - N_A_LAYERS
            x = x + dilated_mixer(rmsnorm(x, b_norm[j]), b_w_in[j], b_w_out[j], kv, rel_bias)
    return rmsnorm(x, final_norm)


import jax as _jax
import jax.numpy as _jnp

TWIN_FORMAT = 'train_step'
FWD_PARAMS = ['x', 'a_norm', 'a_w_in', 'a_conv_w', 'a_conv_b', 'a_ln_g', 'a_ln_b', 'a_w_out', 'kv_norm', 'w_kv', 'b_norm', 'b_w_in', 'b_w_out', 'rel_bias', 'final_norm']
TWIN_WEIGHTS = ['a_norm', 'a_w_in', 'a_conv_w', 'a_conv_b', 'a_ln_g', 'a_ln_b', 'a_w_out', 'kv_norm', 'w_kv', 'b_norm', 'b_w_in', 'b_w_out', 'rel_bias', 'final_norm']
TWIN_DIFF_INPUT = 'x'
TWIN_INPUTS = ['x', 'a_norm', 'a_w_in', 'a_conv_w', 'a_conv_b', 'a_ln_g', 'a_ln_b', 'a_w_out', 'kv_norm', 'w_kv', 'b_norm', 'b_w_in', 'b_w_out', 'rel_bias', 'final_norm', 'loss_target', 'm_a_norm', 'm_a_w_in', 'm_a_conv_w', 'm_a_conv_b', 'm_a_ln_g', 'm_a_ln_b', 'm_a_w_out', 'm_kv_norm', 'm_w_kv', 'm_b_norm', 'm_b_w_in', 'm_b_w_out', 'm_rel_bias', 'm_final_norm', 'v_a_norm', 'v_a_w_in', 'v_a_conv_w', 'v_a_conv_b', 'v_a_ln_g', 'v_a_ln_b', 'v_a_w_out', 'v_kv_norm', 'v_w_kv', 'v_b_norm', 'v_b_w_in', 'v_b_w_out', 'v_rel_bias', 'v_final_norm']
TWIN_OUTPUTS = ['loss', 'grad_x', 'grad_a_norm', 'grad_a_w_in', 'grad_a_conv_w', 'grad_a_conv_b', 'grad_a_ln_g', 'grad_a_ln_b', 'grad_a_w_out', 'grad_kv_norm', 'grad_w_kv', 'grad_b_norm', 'grad_b_w_in', 'grad_b_w_out', 'grad_rel_bias', 'grad_final_norm', 'delta_a_norm', 'delta_a_w_in', 'delta_a_conv_w', 'delta_a_conv_b', 'delta_a_ln_g', 'delta_a_ln_b', 'delta_a_w_out', 'delta_kv_norm', 'delta_w_kv', 'delta_b_norm', 'delta_b_w_in', 'delta_b_w_out', 'delta_rel_bias', 'delta_final_norm', 'new_m_a_norm', 'new_m_a_w_in', 'new_m_a_conv_w', 'new_m_a_conv_b', 'new_m_a_ln_g', 'new_m_a_ln_b', 'new_m_a_w_out', 'new_m_kv_norm', 'new_m_w_kv', 'new_m_b_norm', 'new_m_b_w_in', 'new_m_b_w_out', 'new_m_rel_bias', 'new_m_final_norm', 'new_v_a_norm', 'new_v_a_w_in', 'new_v_a_conv_w', 'new_v_a_conv_b', 'new_v_a_ln_g', 'new_v_a_ln_b', 'new_v_a_w_out', 'new_v_kv_norm', 'new_v_w_kv', 'new_v_b_norm', 'new_v_b_w_in', 'new_v_b_w_out', 'new_v_rel_bias', 'new_v_final_norm']
TWIN_LEAF_KINDS = {'loss': 'loss', 'grad_x': 'grad_x', 'grad_a_norm': 'grad_w', 'grad_a_w_in': 'grad_w', 'grad_a_conv_w': 'grad_w', 'grad_a_conv_b': 'grad_w', 'grad_a_ln_g': 'grad_w', 'grad_a_ln_b': 'grad_w', 'grad_a_w_out': 'grad_w', 'grad_kv_norm': 'grad_w', 'grad_w_kv': 'grad_w', 'grad_b_norm': 'grad_w', 'grad_b_w_in': 'grad_w', 'grad_b_w_out': 'grad_w', 'grad_rel_bias': 'grad_w', 'grad_final_norm': 'grad_w', 'delta_a_norm': 'delta_w', 'delta_a_w_in': 'delta_w', 'delta_a_conv_w': 'delta_w', 'delta_a_conv_b': 'delta_w', 'delta_a_ln_g': 'delta_w', 'delta_a_ln_b': 'delta_w', 'delta_a_w_out': 'delta_w', 'delta_kv_norm': 'delta_w', 'delta_w_kv': 'delta_w', 'delta_b_norm': 'delta_w', 'delta_b_w_in': 'delta_w', 'delta_b_w_out': 'delta_w', 'delta_rel_bias': 'delta_w', 'delta_final_norm': 'delta_w', 'new_m_a_norm': 'new_m', 'new_m_a_w_in': 'new_m', 'new_m_a_conv_w': 'new_m', 'new_m_a_conv_b': 'new_m', 'new_m_a_ln_g': 'new_m', 'new_m_a_ln_b': 'new_m', 'new_m_a_w_out': 'new_m', 'new_m_kv_norm': 'new_m', 'new_m_w_kv': 'new_m', 'new_m_b_norm': 'new_m', 'new_m_b_w_in': 'new_m', 'new_m_b_w_out': 'new_m', 'new_m_rel_bias': 'new_m', 'new_m_final_norm': 'new_m', 'new_v_a_norm': 'new_v', 'new_v_a_w_in': 'new_v', 'new_v_a_conv_w': 'new_v', 'new_v_a_conv_b': 'new_v', 'new_v_a_ln_g': 'new_v', 'new_v_a_ln_b': 'new_v', 'new_v_a_w_out': 'new_v', 'new_v_kv_norm': 'new_v', 'new_v_w_kv': 'new_v', 'new_v_b_norm': 'new_v', 'new_v_b_w_in': 'new_v', 'new_v_b_w_out': 'new_v', 'new_v_rel_bias': 'new_v', 'new_v_final_norm': 'new_v'}


def _forward(args):
    return _fwd_reference(*[args[k] for k in FWD_PARAMS])


def _output_shape():
    def fwd():
        inp = _fwd_setup_inputs(0)
        return _fwd_reference(*[inp[k] for k in FWD_PARAMS])
    out = _jax.eval_shape(fwd)
    return out.shape, out.dtype

N_MICROBATCH = 1
ADAM_LR = 0.001
ADAM_B1 = 0.9
ADAM_B2 = 0.999
ADAM_EPS = 1e-08
ADAM_WD = 0.01
ADAM_STEP = 10
PER_EXAMPLE_BATCH_AXIS = {'x': 0, 'loss_target': 0}
SHARED_INPUTS = []
_WEIGHT_DTYPES = {'a_norm': _jnp.float32, 'a_w_in': _jnp.float32, 'a_conv_w': _jnp.float32, 'a_conv_b': _jnp.float32, 'a_ln_g': _jnp.float32, 'a_ln_b': _jnp.float32, 'a_w_out': _jnp.float32, 'kv_norm': _jnp.float32, 'w_kv': _jnp.float32, 'b_norm': _jnp.float32, 'b_w_in': _jnp.float32, 'b_w_out': _jnp.float32, 'rel_bias': _jnp.float32, 'final_norm': _jnp.float32}
MOMENT_SCALE = {'a_norm': 4.719968e-02, 'a_w_in': 2.646950e-02, 'a_conv_w': 3.120232e-02, 'a_conv_b': 6.806127e-02, 'a_ln_g': 3.621830e-02, 'a_ln_b': 3.104001e-02, 'a_w_out': 3.028166e-02, 'kv_norm': 1.247209e-02, 'w_kv': 5.072616e-03, 'b_norm': 1.270869e-02, 'b_w_in': 6.314181e-03, 'b_w_out': 9.387157e-03, 'rel_bias': 1.440165e-02, 'final_norm': 1.598036e+01}


def _to_microbatches(a, axis):
    t = _jnp.moveaxis(a, axis, 0)
    t = t.reshape((N_MICROBATCH, t.shape[0] // N_MICROBATCH) + t.shape[1:])
    return _jnp.moveaxis(t, 1, axis + 1)


def setup_inputs(seed: int = 0) -> dict:
    inp = _fwd_setup_inputs(seed)
    key = _jax.random.fold_in(_jax.random.key(seed), 7919)
    shape, _ = _output_shape()
    out = dict(inp)
    out["loss_target"] = _jax.random.normal(_jax.random.fold_in(key, 0), shape, _jnp.float32)
    for i, name in enumerate(TWIN_WEIGHTS):
        w = inp[name].astype(_jnp.float32)
        if MOMENT_SCALE is None:
            s = _jnp.sqrt(_jnp.mean(_jnp.square(w)) + 1e-30)
        else:
            s = MOMENT_SCALE[name]
        km, kv = _jax.random.split(_jax.random.fold_in(key, i + 1))
        out[name] = w
        out["m_" + name] = s * _jax.random.normal(km, w.shape, _jnp.float32)
        out["v_" + name] = (s * s) * _jax.random.uniform(kv, w.shape, _jnp.float32, 0.5, 1.5)
    if N_MICROBATCH > 1:
        for name, axis in PER_EXAMPLE_BATCH_AXIS.items():
            out[name] = _to_microbatches(out[name], axis)
    return {'x': out['x'], 'a_norm': out['a_norm'], 'a_w_in': out['a_w_in'], 'a_conv_w': out['a_conv_w'], 'a_conv_b': out['a_conv_b'], 'a_ln_g': out['a_ln_g'], 'a_ln_b': out['a_ln_b'], 'a_w_out': out['a_w_out'], 'kv_norm': out['kv_norm'], 'w_kv': out['w_kv'], 'b_norm': out['b_norm'], 'b_w_in': out['b_w_in'], 'b_w_out': out['b_w_out'], 'rel_bias': out['rel_bias'], 'final_norm': out['final_norm'], 'loss_target': out['loss_target'], 'm_a_norm': out['m_a_norm'], 'm_a_w_in': out['m_a_w_in'], 'm_a_conv_w': out['m_a_conv_w'], 'm_a_conv_b': out['m_a_conv_b'], 'm_a_ln_g': out['m_a_ln_g'], 'm_a_ln_b': out['m_a_ln_b'], 'm_a_w_out': out['m_a_w_out'], 'm_kv_norm': out['m_kv_norm'], 'm_w_kv': out['m_w_kv'], 'm_b_norm': out['m_b_norm'], 'm_b_w_in': out['m_b_w_in'], 'm_b_w_out': out['m_b_w_out'], 'm_rel_bias': out['m_rel_bias'], 'm_final_norm': out['m_final_norm'], 'v_a_norm': out['v_a_norm'], 'v_a_w_in': out['v_a_w_in'], 'v_a_conv_w': out['v_a_conv_w'], 'v_a_conv_b': out['v_a_conv_b'], 'v_a_ln_g': out['v_a_ln_g'], 'v_a_ln_b': out['v_a_ln_b'], 'v_a_w_out': out['v_a_w_out'], 'v_kv_norm': out['v_kv_norm'], 'v_w_kv': out['v_w_kv'], 'v_b_norm': out['v_b_norm'], 'v_b_w_in': out['v_b_w_in'], 'v_b_w_out': out['v_b_w_out'], 'v_rel_bias': out['v_rel_bias'], 'v_final_norm': out['v_final_norm']}


def _loss(weights, diff, rest, loss_target):
    with _jax.named_scope("forward"):
        args = {**rest, TWIN_DIFF_INPUT: diff, **{k: w.astype(_WEIGHT_DTYPES[k]) for k, w in weights.items()}}
        y = _forward(args)
    with _jax.named_scope("loss_head"):
        err = _jnp.square(y.astype(_jnp.float32) - loss_target)
        return 0.5 * _jnp.sum(_jnp.mean(err, axis=-1)) if err.ndim else 0.5 * err


def _adamw(w, g, m, v):
    m = ADAM_B1 * m + (1.0 - ADAM_B1) * g
    v = ADAM_B2 * v + (1.0 - ADAM_B2) * _jnp.square(g)
    m_hat = m / (1.0 - ADAM_B1 ** ADAM_STEP)
    v_hat = v / (1.0 - ADAM_B2 ** ADAM_STEP)
    delta = -ADAM_LR * (m_hat / (_jnp.sqrt(v_hat) + ADAM_EPS) + ADAM_WD * w)
    return delta, m, v


def reference(x, a_norm, a_w_in, a_conv_w, a_conv_b, a_ln_g, a_ln_b, a_w_out, kv_norm, w_kv, b_norm, b_w_in, b_w_out, rel_bias, final_norm, loss_target, m_a_norm, m_a_w_in, m_a_conv_w, m_a_conv_b, m_a_ln_g, m_a_ln_b, m_a_w_out, m_kv_norm, m_w_kv, m_b_norm, m_b_w_in, m_b_w_out, m_rel_bias, m_final_norm, v_a_norm, v_a_w_in, v_a_conv_w, v_a_conv_b, v_a_ln_g, v_a_ln_b, v_a_w_out, v_kv_norm, v_w_kv, v_b_norm, v_b_w_in, v_b_w_out, v_rel_bias, v_final_norm):
    given = dict(x=x, a_norm=a_norm, a_w_in=a_w_in, a_conv_w=a_conv_w, a_conv_b=a_conv_b, a_ln_g=a_ln_g, a_ln_b=a_ln_b, a_w_out=a_w_out, kv_norm=kv_norm, w_kv=w_kv, b_norm=b_norm, b_w_in=b_w_in, b_w_out=b_w_out, rel_bias=rel_bias, final_norm=final_norm, loss_target=loss_target, m_a_norm=m_a_norm, m_a_w_in=m_a_w_in, m_a_conv_w=m_a_conv_w, m_a_conv_b=m_a_conv_b, m_a_ln_g=m_a_ln_g, m_a_ln_b=m_a_ln_b, m_a_w_out=m_a_w_out, m_kv_norm=m_kv_norm, m_w_kv=m_w_kv, m_b_norm=m_b_norm, m_b_w_in=m_b_w_in, m_b_w_out=m_b_w_out, m_rel_bias=m_rel_bias, m_final_norm=m_final_norm, v_a_norm=v_a_norm, v_a_w_in=v_a_w_in, v_a_conv_w=v_a_conv_w, v_a_conv_b=v_a_conv_b, v_a_ln_g=v_a_ln_g, v_a_ln_b=v_a_ln_b, v_a_w_out=v_a_w_out, v_kv_norm=v_kv_norm, v_w_kv=v_w_kv, v_b_norm=v_b_norm, v_b_w_in=v_b_w_in, v_b_w_out=v_b_w_out, v_rel_bias=v_rel_bias, v_final_norm=v_final_norm)
    weights = {n: given[n] for n in TWIN_WEIGHTS}
    shared = {n: given[n] for n in SHARED_INPUTS}
    per_example = {n: given[n] for n in ['x']}
    grad_fn = _jax.value_and_grad(_loss, argnums=(0, 1))

    def one_microbatch(ex, loss_target):
        ex = dict(ex)
        diff = ex.pop(TWIN_DIFF_INPUT)
        return grad_fn(weights, diff, {**shared, **ex}, loss_target)

    if N_MICROBATCH == 1:
        loss, (grad_w, grad_x) = one_microbatch(per_example, given["loss_target"])
    else:
        def body(carry, xs):
            loss_sum, grad_sum = carry
            l_k, (gw_k, gx_k) = one_microbatch(xs[0], xs[1])
            with _jax.named_scope("update"):
                return (loss_sum + l_k, _jax.tree.map(_jnp.add, grad_sum, gw_k)), gx_k

        init = (_jnp.zeros((), _jnp.float32), _jax.tree.map(_jnp.zeros_like, weights))
        (loss, grad_w), grad_x = _jax.lax.scan(body, init, (per_example, given["loss_target"]))
    with _jax.named_scope("update"):
        delta_w, new_m, new_v = {}, {}, {}
        for n in TWIN_WEIGHTS:
            delta_w[n], new_m[n], new_v[n] = _adamw(weights[n], grad_w[n], given["m_" + n], given["v_" + n])
    return (loss, grad_x, *[grad_w[n] for n in TWIN_WEIGHTS], *[delta_w[n] for n in TWIN_WEIGHTS],
            *[new_m[n] for n in TWIN_WEIGHTS], *[new_v[n] for n in TWIN_WEIGHTS])
```

```python
import functools
import math

import numpy as np
import jax
import jax.numpy as jnp
from jax import lax
from jax.experimental import pallas as pl
from jax.experimental.pallas import tpu as pltpu

F32 = jnp.float32
BF16 = jnp.bfloat16

N_DEV = 8
EPS = 1e-6
HEAD_DIM = 128
BLOCK = 128
GROUPS = ((128, 1), (512, 4), (2048, 16))
N_GROUPS = len(GROUPS)
CONV_WIDTH = 31
HALO = 32
N_BUCKETS = 32
MAX_EXACT = N_BUCKETS // 2
MAX_DISTANCE = 2048
V7X_VMEM_BYTES = 64 * 1024 * 1024
VMEM_LIMIT = (V7X_VMEM_BYTES * 7) // 8
LANE = 128

ADAM_LR = 0.001
ADAM_B1 = 0.9
ADAM_B2 = 0.999
ADAM_EPS = 1e-08
ADAM_WD = 0.01
ADAM_STEP = 10

SMALL_SH_ROWS = 40
SMALL_REP_ROWS = 8
SMALL_ROWS = SMALL_SH_ROWS + SMALL_REP_ROWS
MESH = pl.DeviceIdType.MESH


def _tile(dim, pref, unit=LANE):
    if dim <= pref:
        return dim
    t = (pref // unit) * unit
    while dim % t:
        t -= unit
    assert t > 0
    return t


def _cparams(*sem):
    return pltpu.CompilerParams(dimension_semantics=sem if sem else None,
                                vmem_limit_bytes=VMEM_LIMIT)


def _sigmoid(v):
    return jax.nn.sigmoid(v)


def matmul_nn(a, w, name, res=None, out_dtype=F32):
    M, K = a.shape
    nb, _, n = w.shape
    tm = _tile(M, 1024)
    tn = _tile(n, 1024)
    per = n // tn

    def body(*refs):
        if res is None:
            a_ref, w_ref, o_ref = refs
        else:
            a_ref, w_ref, r_ref, o_ref = refs
        acc = jnp.dot(a_ref[...], w_ref[...], preferred_element_type=F32)
        if res is not None:
            acc = r_ref[...] + acc
        o_ref[...] = acc.astype(o_ref.dtype)

    in_specs = [pl.BlockSpec((tm, K), lambda j, i: (i, 0)),
                pl.BlockSpec((None, K, tn), lambda j, i: (j // per, 0, j % per))]
    args = [a, w]
    if res is not None:
        in_specs.append(pl.BlockSpec((tm, tn), lambda j, i: (i, j)))
        args.append(res)
    return pl.pallas_call(
        body, name=name, grid=(nb * per, M // tm),
        in_specs=in_specs,
        out_specs=pl.BlockSpec((tm, tn), lambda j, i: (i, j)),
        out_shape=jax.ShapeDtypeStruct((M, nb * n), out_dtype),
        compiler_params=_cparams("parallel", "parallel"),
    )(*args)


def matmul_nt(dy, w, name):
    M, _ = dy.shape
    nb, K, n = w.shape
    tm = _tile(M, 1024)
    tc = _tile(n, 1024)
    per = n // tc

    def body(dy_ref, w_ref, o_ref):
        j = pl.program_id(1)
        part = lax.dot_general(dy_ref[...], w_ref[...], (((1,), (1,)), ((), ())),
                               preferred_element_type=F32)

        @pl.when(j == 0)
        def _():
            o_ref[...] = part

        @pl.when(j > 0)
        def _():
            o_ref[...] += part

    return pl.pallas_call(
        body, name=name, grid=(M // tm, nb * per),
        in_specs=[pl.BlockSpec((tm, tc), lambda i, j: (i, j)),
                  pl.BlockSpec((None, K, tc), lambda i, j: (j // per, 0, j % per))],
        out_specs=pl.BlockSpec((tm, K), lambda i, j: (i, 0)),
        out_shape=jax.ShapeDtypeStruct((M, K), F32),
        compiler_params=_cparams("parallel", "arbitrary"),
    )(dy, w)


def matmul_tn(a, dy, nb, name, out_dtype=BF16):
    M, K = a.shape
    n = dy.shape[1] // nb
    tn = _tile(n, 1024)
    per = n // tn
    tk = _tile(K, 1024)
    tmc = _tile(M, 2048)
    steps = M // tmc

    def body(a_ref, dy_ref, o_ref, acc_ref):
        s = pl.program_id(2)
        part = lax.dot_general(a_ref[...], dy_ref[...], (((0,), (0,)), ((), ())),
                               preferred_element_type=F32)

        @pl.when(s == 0)
        def _():
            acc_ref[...] = part

        @pl.when(s > 0)
        def _():
            acc_ref[...] += part

        @pl.when(s == steps - 1)
        def _():
            o_ref[...] = acc_ref[...].astype(o_ref.dtype)

    return pl.pallas_call(
        body, name=name, grid=(nb * per, K // tk, steps),
        in_specs=[pl.BlockSpec((tmc, tk), lambda j, k, s: (s, k)),
                  pl.BlockSpec((tmc, tn), lambda j, k, s: (s, j))],
        out_specs=pl.BlockSpec((None, tk, tn), lambda j, k, s: (j // per, k, j % per)),
        out_shape=jax.ShapeDtypeStruct((nb, K, n), out_dtype),
        scratch_shapes=[pltpu.VMEM((tk, tn), F32)],
        compiler_params=_cparams("parallel", "parallel", "arbitrary"),
    )(a, dy)


def small_dot(a, b, contract, name):
    if contract == "nn":
        dims = (((1,), (0,)), ((), ()))
        out = (a.shape[0], b.shape[1])
    else:
        dims = (((1,), (1,)), ((), ()))
        out = (a.shape[0], b.shape[0])

    def body(a_ref, b_ref, o_ref):
        o_ref[...] = lax.dot_general(a_ref[...], b_ref[...], dims, precision=lax.Precision.HIGHEST,
                                     preferred_element_type=F32)

    return pl.pallas_call(body, name=name, out_shape=jax.ShapeDtypeStruct(out, F32),
                          compiler_params=_cparams())(a, b)


def rms_fwd(x, gains, name):
    S, D = x.shape
    T = _tile(S, 512, 8)
    n = len(gains)

    def body(x_ref, *refs):
        xv = x_ref[...]
        xn = xv * lax.rsqrt(jnp.mean(xv * xv, axis=-1, keepdims=True) + EPS)
        for g_ref, o_ref in zip(refs[:n], refs[n:]):
            o_ref[...] = (xn * g_ref[...]).astype(o_ref.dtype)

    row = pl.BlockSpec((T, D), lambda i: (i, 0))
    vec = pl.BlockSpec((1, D), lambda i: (0, 0))
    return pl.pallas_call(
        body, name=name, grid=(S // T,),
        in_specs=[row] + [vec] * n, out_specs=[row] * n,
        out_shape=[jax.ShapeDtypeStruct((S, D), BF16)] * n,
        compiler_params=_cparams("parallel"),
    )(x, *gains)


def rms_bwd(x, dhs, gains, dres, name, want_bf16):
    S, D = x.shape
    T = _tile(S, 256, 8)
    n = len(gains)

    def body(x_ref, *refs):
        dh_refs = refs[:n]
        g_refs = refs[n:2 * n]
        dres_ref = refs[2 * n]
        outs = refs[2 * n + 1:]
        dx_ref, dg_ref = outs[0], outs[-1]
        i = pl.program_id(0)

        @pl.when(i == 0)
        def _():
            dg_ref[...] = jnp.zeros_like(dg_ref)

        xv = x_ref[...]
        r = lax.rsqrt(jnp.mean(xv * xv, axis=-1, keepdims=True) + EPS)
        xn = xv * r
        dxn = jnp.zeros_like(xv)
        for k in range(n):
            dh = dh_refs[k][...]
            dg_ref[k:k + 1, :] += jnp.sum(dh * xn, axis=0, keepdims=True)
            dxn = dxn + dh * g_refs[k][...]
        dx = dres_ref[...] + r * (dxn - xn * jnp.mean(dxn * xn, axis=-1, keepdims=True))
        dx_ref[...] = dx
        if want_bf16:
            outs[1][...] = dx.astype(BF16)

    row = pl.BlockSpec((T, D), lambda i: (i, 0))
    vec = pl.BlockSpec((1, D), lambda i: (0, 0))
    acc = pl.BlockSpec((8, D), lambda i: (0, 0))
    out_specs = [row] + ([row] if want_bf16 else []) + [acc]
    out_shape = ([jax.ShapeDtypeStruct((S, D), F32)]
                 + ([jax.ShapeDtypeStruct((S, D), BF16)] if want_bf16 else [])
                 + [jax.ShapeDtypeStruct((8, D), F32)])
    return pl.pallas_call(
        body, name=name, grid=(S // T,),
        in_specs=[row] + [row] * n + [vec] * n + [row],
        out_specs=out_specs, out_shape=out_shape,
        compiler_params=_cparams("arbitrary"),
    )(x, *dhs, *gains, dres)


def final_loss(x2, gain, target, name):
    S, D = x2.shape
    T = _tile(S, 256, 8)

    def body(x_ref, g_ref, t_ref, dx_ref, dxb_ref, acc_ref):
        i = pl.program_id(0)

        @pl.when(i == 0)
        def _():
            acc_ref[...] = jnp.zeros_like(acc_ref)

        xv = x_ref[...]
        g = g_ref[...]
        r = lax.rsqrt(jnp.mean(xv * xv, axis=-1, keepdims=True) + EPS)
        xn = xv * r
        err = xn * g - t_ref[...]
        dy = err * (1.0 / D)
        acc_ref[0:1, :] += jnp.sum(dy * xn, axis=0, keepdims=True)
        acc_ref[1:2, :] += jnp.full((1, D), 0.5 / D, F32) * jnp.sum(err * err)
        dxn = dy * g
        dx = r * (dxn - xn * jnp.mean(dxn * xn, axis=-1, keepdims=True))
        dx_ref[...] = dx
        dxb_ref[...] = dx.astype(BF16)

    row = pl.BlockSpec((T, D), lambda i: (i, 0))
    return pl.pallas_call(
        body, name=name, grid=(S // T,),
        in_specs=[row, pl.BlockSpec((1, D), lambda i: (0, 0)), row],
        out_specs=[row, row, pl.BlockSpec((8, D), lambda i: (0, 0))],
        out_shape=[jax.ShapeDtypeStruct((S, D), F32), jax.ShapeDtypeStruct((S, D), BF16),
                   jax.ShapeDtypeStruct((8, D), F32)],
        compiler_params=_cparams("arbitrary"),
    )(x2, gain, target)


ROW_CHUNK = 64
LANE_CHUNK = 512


def conf_fwd(proj, cw, cb, lg, lb, name):
    S, E3 = proj.shape
    E = E3 // 3
    T = _tile(S, 256, HALO)
    R = T // HALO
    lc = _tile(E, LANE_CHUNK)
    rc = min(ROW_CHUNK, T)

    def body(a_ref, b_ref, z_ref, ap_ref, bp_ref, cw_ref, cb_ref, lg_ref, lb_ref, c_ref, y_ref, u_scr):
        i = pl.program_id(0)
        up = ap_ref[...] * _sigmoid(bp_ref[...])
        u_scr[0:HALO, :] = jnp.where(i > 0, up, 0.0)
        u_scr[HALO:HALO + T, :] = a_ref[...] * _sigmoid(b_ref[...])
        off = HALO - (CONV_WIDTH - 1)
        for r0 in range(0, T, rc):
            for l0 in range(0, E, lc):
                acc = jnp.broadcast_to(cb_ref[:, l0:l0 + lc], (rc, lc))
                for k in range(CONV_WIDTH):
                    acc = acc + u_scr[r0 + off + k:r0 + off + k + rc, l0:l0 + lc] * cw_ref[k:k + 1, l0:l0 + lc]
                c_ref[r0:r0 + rc, l0:l0 + lc] = acc
        c = c_ref[...]
        mu = jnp.mean(c, axis=-1, keepdims=True)
        d = c - mu
        var = jnp.mean(d * d, axis=-1, keepdims=True)
        cn = d * lax.rsqrt(var + EPS) * lg_ref[...] + lb_ref[...]
        z = z_ref[...]
        y_ref[...] = ((cn * _sigmoid(cn)) * (z * _sigmoid(z))).astype(BF16)

    def col(j):
        return pl.BlockSpec((T, E), lambda i: (i, j))

    def prev(j):
        return pl.BlockSpec((HALO, E), lambda i: (jnp.maximum(i * R - 1, 0), j))

    vec = pl.BlockSpec((1, E), lambda i: (0, 0))
    return pl.pallas_call(
        body, name=name, grid=(S // T,),
        in_specs=[col(0), col(1), col(2), prev(0), prev(1),
                  pl.BlockSpec((HALO, E), lambda i: (0, 0)), vec, vec, vec],
        out_specs=[pl.BlockSpec((T, E), lambda i: (i, 0))] * 2,
        out_shape=[jax.ShapeDtypeStruct((S, E), F32), jax.ShapeDtypeStruct((S, E), BF16)],
        scratch_shapes=[pltpu.VMEM((HALO + T, E), F32)],
        compiler_params=_cparams("parallel"),
    )(proj, proj, proj, proj, proj, cw, cb, lg, lb)


def conf_bwd_ln(c, dy, proj, lg, lb, name):
    S, E = c.shape
    T = _tile(S, 256, 8)

    def body(c_ref, dy_ref, z_ref, lg_ref, lb_ref, dc_ref, dz_ref, acc_ref):
        i = pl.program_id(0)

        @pl.when(i == 0)
        def _():
            acc_ref[...] = jnp.zeros_like(acc_ref)

        cv = c_ref[...]
        mu = jnp.mean(cv, axis=-1, keepdims=True)
        d = cv - mu
        var = jnp.mean(d * d, axis=-1, keepdims=True)
        rstd = lax.rsqrt(var + EPS)
        xh = d * rstd
        lgv = lg_ref[...]
        cn = xh * lgv + lb_ref[...]
        z = z_ref[...]
        dy = dy_ref[...]
        sc = _sigmoid(cn)
        sz = _sigmoid(z)
        dcn = dy * (z * sz) * (sc * (1.0 + cn * (1.0 - sc)))
        dz_ref[...] = (dy * (cn * sc) * (sz * (1.0 + z * (1.0 - sz)))).astype(BF16)
        acc_ref[0:1, :] += jnp.sum(dcn * xh, axis=0, keepdims=True)
        acc_ref[1:2, :] += jnp.sum(dcn, axis=0, keepdims=True)
        dxh = dcn * lgv
        dc = rstd * (dxh - jnp.mean(dxh, axis=-1, keepdims=True)
                     - xh * jnp.mean(dxh * xh, axis=-1, keepdims=True))
        acc_ref[2:3, :] += jnp.sum(dc, axis=0, keepdims=True)
        dc_ref[...] = dc

    row = pl.BlockSpec((T, E), lambda i: (i, 0))
    vec = pl.BlockSpec((1, E), lambda i: (0, 0))
    return pl.pallas_call(
        body, name=name, grid=(S // T,),
        in_specs=[row, row, pl.BlockSpec((T, E), lambda i: (i, 2)), vec, vec],
        out_specs=[row, row, pl.BlockSpec((8, E), lambda i: (0, 0))],
        out_shape=[jax.ShapeDtypeStruct((S, E), F32), jax.ShapeDtypeStruct((S, E), BF16),
                   jax.ShapeDtypeStruct((8, E), F32)],
        compiler_params=_cparams("arbitrary"),
    )(c, dy, proj, lg, lb)


def conf_bwd_conv(proj, dc, dz, cw, name):
    S, E3 = proj.shape
    E = E3 // 3
    T = _tile(S, 256, HALO)
    R = T // HALO
    nt = S // T
    lc = _tile(E, LANE_CHUNK)
    rc = min(ROW_CHUNK, T)

    def body(a_ref, b_ref, ap_ref, bp_ref, dc_ref, dcn_ref, dz_ref, cw_ref, o_ref, dw_ref, u_scr, dc_scr):
        i = pl.program_id(0)

        @pl.when(i == 0)
        def _():
            dw_ref[...] = jnp.zeros_like(dw_ref)

        a = a_ref[...]
        sb = _sigmoid(b_ref[...])
        up = ap_ref[...] * _sigmoid(bp_ref[...])
        u_scr[0:HALO, :] = jnp.where(i > 0, up, 0.0)
        u_scr[HALO:HALO + T, :] = a * sb
        dc_scr[0:T, :] = dc_ref[...]
        dc_scr[T:T + HALO, :] = jnp.where(i < nt - 1, dcn_ref[...], 0.0)
        off = HALO - (CONV_WIDTH - 1)
        for l0 in range(0, E, lc):
            for k in range(CONV_WIDTH):
                prod = u_scr[off + k:off + k + T, l0:l0 + lc] * dc_scr[0:T, l0:l0 + lc]
                dw_ref[k:k + 1, l0:l0 + lc] += jnp.sum(prod, axis=0, keepdims=True)
            for r0 in range(0, T, rc):
                acc = jnp.zeros((rc, lc), F32)
                for k in range(CONV_WIDTH):
                    s0 = r0 + (CONV_WIDTH - 1) - k
                    acc = acc + dc_scr[s0:s0 + rc, l0:l0 + lc] * cw_ref[k:k + 1, l0:l0 + lc]
                av = a[r0:r0 + rc, l0:l0 + lc]
                sv = sb[r0:r0 + rc, l0:l0 + lc]
                o_ref[r0:r0 + rc, l0:l0 + lc] = (acc * sv).astype(BF16)
                o_ref[r0:r0 + rc, E + l0:E + l0 + lc] = (acc * av * sv * (1.0 - sv)).astype(BF16)
        o_ref[:, 2 * E:3 * E] = dz_ref[...]

    def col(j):
        return pl.BlockSpec((T, E), lambda i: (i, j))

    def prev(j):
        return pl.BlockSpec((HALO, E), lambda i: (jnp.maximum(i * R - 1, 0), j))

    row = pl.BlockSpec((T, E), lambda i: (i, 0))
    nxt = pl.BlockSpec((HALO, E), lambda i: (jnp.minimum((i + 1) * R, S // HALO - 1), 0))
    return pl.pallas_call(
        body, name=name, grid=(nt,),
        in_specs=[col(0), col(1), prev(0), prev(1), row, nxt, row,
                  pl.BlockSpec((HALO, E), lambda i: (0, 0))],
        out_specs=[pl.BlockSpec((T, E3), lambda i: (i, 0)), pl.BlockSpec((HALO, E), lambda i: (0, 0))],
        out_shape=[jax.ShapeDtypeStruct((S, E3), BF16), jax.ShapeDtypeStruct((HALO, E), F32)],
        scratch_shapes=[pltpu.VMEM((HALO + T, E), F32), pltpu.VMEM((T + HALO, E), F32)],
        compiler_params=_cparams("arbitrary"),
    )(proj, proj, proj, proj, dc, dc, dz, cw)


def bucket_tables():
    q = np.arange(BLOCK)[:, None]
    k = np.arange(BLOCK)[None, :]
    out = []
    for window, dil in GROUPS:
        w_sub = window // dil
        for delta in (q - k, q + BLOCK - k):
            valid = (delta >= 0) & (delta <= w_sub)
            dist = np.clip(delta, 0, None) * dil
            large = MAX_EXACT + (np.log(np.maximum(dist, 1).astype(np.float32) / MAX_EXACT)
                                 / math.log(MAX_DISTANCE / MAX_EXACT)
                                 * (N_BUCKETS - MAX_EXACT)).astype(np.int32)
            large = np.minimum(large, N_BUCKETS - 1)
            bucket = np.where(dist < MAX_EXACT, dist, large)
            out.append(np.where(valid, bucket, -1).reshape(-1))
    return np.concatenate(out).astype(np.int32)


def _attn_masks():
    ql = lax.broadcasted_iota(jnp.int32, (BLOCK, BLOCK), 0)
    kl = lax.broadcasted_iota(jnp.int32, (BLOCK, BLOCK), 1)
    return kl <= ql, kl >= ql


def _dot_nt(a, b):
    return lax.dot_general(a, b, (((1,), (1,)), ((), ())), preferred_element_type=F32)


def _dot_tn(a, b):
    return lax.dot_general(a, b, (((0,), (0,)), ((), ())), preferred_element_type=F32)


def attn_fwd(qz, kv, bias_c, bias_p, g, dil, name):
    S = qz.shape[0]
    D = qz.shape[1] // (N_GROUPS + 1)
    H = D // HEAD_DIM
    L = S // dil
    nbk = L // BLOCK
    scale = HEAD_DIM ** -0.5
    qz_v = qz.reshape(L, dil * (N_GROUPS + 1) * D)
    kv_v = kv.reshape(L, dil * 2 * N_GROUPS * D)

    def body(q_ref, kc_ref, kp_ref, vc_ref, vp_ref, bc_ref, bp_ref, o_ref, l_ref):
        j = pl.program_id(1)
        mask_c, mask_p = _attn_masks()
        mask_p = mask_p & (j > 0)
        for h in range(H):
            hs = slice(h * HEAD_DIM, (h + 1) * HEAD_DIM)
            q = q_ref[:, hs].astype(BF16)
            s_c = jnp.where(mask_c, _dot_nt(q, kc_ref[:, hs]) * scale + bc_ref[h], -jnp.inf)
            s_p = jnp.where(mask_p, _dot_nt(q, kp_ref[:, hs]) * scale + bp_ref[h], -jnp.inf)
            m = jnp.maximum(jnp.max(s_c, axis=-1, keepdims=True), jnp.max(s_p, axis=-1, keepdims=True))
            p_c = jnp.exp(s_c - m)
            p_p = jnp.exp(s_p - m)
            den = jnp.sum(p_c, axis=-1, keepdims=True) + jnp.sum(p_p, axis=-1, keepdims=True)
            pv = (jnp.dot(p_c.astype(BF16), vc_ref[:, hs], preferred_element_type=F32)
                  + jnp.dot(p_p.astype(BF16), vp_ref[:, hs], preferred_element_type=F32))
            o_ref[:, hs] = pv / den
            l_ref[:, hs] = jnp.broadcast_to(m + jnp.log(den), (BLOCK, HEAD_DIM))

    nq = N_GROUPS + 1
    nk = 2 * N_GROUPS

    def kspec(which, prev):
        if prev:
            return pl.BlockSpec((BLOCK, D), lambda r, j: (jnp.maximum(j - 1, 0), r * nk + 2 * g + which))
        return pl.BlockSpec((BLOCK, D), lambda r, j: (j, r * nk + 2 * g + which))

    bias_spec = pl.BlockSpec((H, BLOCK, BLOCK), lambda r, j: (0, 0, 0))
    out_spec = pl.BlockSpec((BLOCK, D), lambda r, j: (j, r))
    o, lse = pl.pallas_call(
        body, name=name, grid=(dil, nbk),
        in_specs=[pl.BlockSpec((BLOCK, D), lambda r, j: (j, r * nq + g)),
                  kspec(0, False), kspec(0, True), kspec(1, False), kspec(1, True),
                  bias_spec, bias_spec],
        out_specs=[out_spec, out_spec],
        out_shape=[jax.ShapeDtypeStruct((L, dil * D), F32)] * 2,
        compiler_params=_cparams("parallel", "parallel"),
    )(qz_v, kv_v, kv_v, kv_v, kv_v, bias_c, bias_p)
    return o.reshape(S, D), lse.reshape(S, D)


def merge_fwd(os_, lses, qz, name):
    S, D = os_[0].shape
    T = _tile(S, 256, 8)

    def body(o1, o2, o3, l1, l2, l3, z_ref, o_ref, lse_ref, y_ref):
        la, lb_, lc_ = l1[...], l2[...], l3[...]
        m = jnp.maximum(jnp.maximum(la, lb_), lc_)
        ea, eb, ec = jnp.exp(la - m), jnp.exp(lb_ - m), jnp.exp(lc_ - m)
        den = ea + eb + ec
        o = (ea * o1[...] + eb * o2[...] + ec * o3[...]) / den
        z = z_ref[...]
        o_ref[...] = o
        lse_ref[...] = m + jnp.log(den)
        y_ref[...] = (o * (z * _sigmoid(z))).astype(BF16)

    row = pl.BlockSpec((T, D), lambda i: (i, 0))
    return pl.pallas_call(
        body, name=name, grid=(S // T,),
        in_specs=[row] * 6 + [pl.BlockSpec((T, D), lambda i: (i, N_GROUPS))],
        out_specs=[row] * 3,
        out_shape=[jax.ShapeDtypeStruct((S, D), F32), jax.ShapeDtypeStruct((S, D), F32),
                   jax.ShapeDtypeStruct((S, D), BF16)],
        compiler_params=_cparams("parallel"),
    )(*os_, *lses, qz)


def merge_bwd(dy2, o, qz, name):
    S, D = o.shape
    H = D // HEAD_DIM
    T = _tile(S, 256, 8)
    nq = N_GROUPS + 1

    def body(dy_ref, o_ref, z_ref, do_ref, dl_ref, dqz_ref):
        dy = dy_ref[...]
        ov = o_ref[...]
        z = z_ref[...]
        sz = _sigmoid(z)
        do = dy * (z * sz)
        do_ref[...] = do.astype(BF16)
        dqz_ref[...] = (dy * ov * (sz * (1.0 + z * (1.0 - sz)))).astype(BF16)
        prod = do * ov
        for h in range(H):
            hs = slice(h * HEAD_DIM, (h + 1) * HEAD_DIM)
            dl_ref[:, hs] = jnp.broadcast_to(jnp.sum(prod[:, hs], axis=-1, keepdims=True), (T, HEAD_DIM))

    row = pl.BlockSpec((T, D), lambda i: (i, 0))
    last = pl.BlockSpec((T, D), lambda i: (i, N_GROUPS))
    return pl.pallas_call(
        body, name=name, grid=(S // T,),
        in_specs=[row, row, last],
        out_specs=[row, row, last],
        out_shape=[jax.ShapeDtypeStruct((S, D), BF16), jax.ShapeDtypeStruct((S, D), F32),
                   jax.ShapeDtypeStruct((S, nq * D), BF16)],
        compiler_params=_cparams("parallel"),
    )(dy2, o, qz)


def attn_bwd(qz, kv, do, lse, delta, bias_c, bias_p, dqz, dkv, g, dil, name):
    S = qz.shape[0]
    D = qz.shape[1] // (N_GROUPS + 1)
    H = D // HEAD_DIM
    L = S // dil
    nbk = L // BLOCK
    scale = HEAD_DIM ** -0.5
    nq = N_GROUPS + 1
    nk = 2 * N_GROUPS
    qz_v = qz.reshape(L, dil * nq * D)
    kv_v = kv.reshape(L, dil * nk * D)
    dqz_v = dqz.reshape(L, dil * nq * D)
    sd_v = [t.reshape(L, dil * D) for t in (do, lse, delta)]
    have_dkv = dkv is not None

    def body(*refs):
        (q0_ref, q1_ref, do0_ref, do1_ref, l0_ref, l1_ref, d0_ref, d1_ref,
         k_ref, v_ref, bc_ref, bp_ref) = refs[:12]
        n_in = 12 + 1 + (1 if have_dkv else 0)
        dq_ref, dkv_ref, dsc_ref, dsp_ref, carry = refs[n_in:]
        r = pl.program_id(0)
        j = pl.program_id(1)

        @pl.when((r == 0) & (j == 0))
        def _():
            dsc_ref[...] = jnp.zeros_like(dsc_ref)
            dsp_ref[...] = jnp.zeros_like(dsp_ref)

        mask_c, mask_p = _attn_masks()
        mask_p = mask_p & (j < nbk - 1)
        first = j == 0
        for h in range(H):
            hs = slice(h * HEAD_DIM, (h + 1) * HEAD_DIM)
            q0 = q0_ref[:, hs].astype(BF16)
            q1 = q1_ref[:, hs].astype(BF16)
            k = k_ref[:, hs]
            v = v_ref[:, hs]
            do0 = do0_ref[:, hs]
            do1 = do1_ref[:, hs]
            s0 = _dot_nt(q0, k) * scale + bc_ref[h]
            s1 = _dot_nt(q1, k) * scale + bp_ref[h]
            p0 = jnp.where(mask_c, jnp.exp(s0 - l0_ref[:, hs]), 0.0)
            p1 = jnp.where(mask_p, jnp.exp(s1 - l1_ref[:, hs]), 0.0)
            ds0 = p0 * (_dot_nt(do0, v) - d0_ref[:, hs])
            ds1 = p1 * (_dot_nt(do1, v) - d1_ref[:, hs])
            dsc_ref[h] += ds0
            dsp_ref[h] += ds1
            ds0b = ds0.astype(BF16)
            ds1b = ds1.astype(BF16)
            dkv_ref[:, D + h * HEAD_DIM:D + (h + 1) * HEAD_DIM] = (
                _dot_tn(p0.astype(BF16), do0) + _dot_tn(p1.astype(BF16), do1)).astype(BF16)
            dkv_ref[:, hs] = ((_dot_tn(ds0b, q0) + _dot_tn(ds1b, q1)) * scale).astype(BF16)
            dq = jnp.dot(ds0b, k, preferred_element_type=F32) * scale
            dq_ref[:, hs] = (jnp.where(first, 0.0, carry[:, hs]) + dq).astype(BF16)
            carry[:, hs] = jnp.dot(ds1b, k, preferred_element_type=F32) * scale

    def nxt(j):
        return jnp.minimum(j + 1, nbk - 1)

    def qspec(next_):
        if next_:
            return pl.BlockSpec((BLOCK, D), lambda r, j: (nxt(j), r * nq + g))
        return pl.BlockSpec((BLOCK, D), lambda r, j: (j, r * nq + g))

    def sspec(next_):
        if next_:
            return pl.BlockSpec((BLOCK, D), lambda r, j: (nxt(j), r))
        return pl.BlockSpec((BLOCK, D), lambda r, j: (j, r))

    any_spec = pl.BlockSpec(memory_space=pl.ANY)
    bias_spec = pl.BlockSpec((H, BLOCK, BLOCK), lambda r, j: (0, 0, 0))
    in_specs = [qspec(False), qspec(True), sspec(False), sspec(True), sspec(False), sspec(True),
                sspec(False), sspec(True),
                pl.BlockSpec((BLOCK, D), lambda r, j: (j, r * nk + 2 * g)),
                pl.BlockSpec((BLOCK, D), lambda r, j: (j, r * nk + 2 * g + 1)),
                bias_spec, bias_spec, any_spec]
    args = [qz_v, qz_v, sd_v[0], sd_v[0], sd_v[1], sd_v[1], sd_v[2], sd_v[2], kv_v, kv_v,
            bias_c, bias_p, dqz_v]
    aliases = {12: 0}
    if have_dkv:
        in_specs.append(any_spec)
        args.append(dkv.reshape(L, dil * nk * D))
        aliases[13] = 1
    outs = pl.pallas_call(
        body, name=name, grid=(dil, nbk),
        in_specs=in_specs,
        out_specs=[pl.BlockSpec((BLOCK, D), lambda r, j: (j, r * nq + g)),
                   pl.BlockSpec((BLOCK, 2 * D), lambda r, j: (j, r * N_GROUPS + g)),
                   bias_spec, bias_spec],
        out_shape=[jax.ShapeDtypeStruct((L, dil * nq * D), BF16),
                   jax.ShapeDtypeStruct((L, dil * nk * D), BF16),
                   jax.ShapeDtypeStruct((H, BLOCK, BLOCK), F32),
                   jax.ShapeDtypeStruct((H, BLOCK, BLOCK), F32)],
        scratch_shapes=[pltpu.VMEM((BLOCK, D), F32)],
        input_output_aliases=aliases,
        compiler_params=_cparams("arbitrary", "arbitrary"),
    )(*args)
    dqz_new, dkv_new, ds_c, ds_p = outs
    return dqz_new.reshape(S, nq * D), dkv_new.reshape(S, nk * D), ds_c, ds_p


def _place():
    x, y, c = lax.axis_index("x"), lax.axis_index("y"), lax.axis_index("c")
    return x, y, c


def all_gather(shards, name):
    n = len(shards)

    def body(*refs):
        ins, outs = refs[:n], refs[n:2 * n]
        send_sems, recv_sems, local_sems = refs[2 * n:]
        x, y, c = _place()
        me, sibling = (x, y, c), (x, y, 1 - c)
        chips = [(1 - x, y), (x, 1 - y), (1 - x, 1 - y)]

        def slot(a, dev):
            return outs[a].at[4 * dev[0] + 2 * dev[1] + dev[2]]

        def copy(a, k, block, to, src=None):
            return pltpu.make_async_remote_copy(
                src_ref=slot(a, block) if src is None else src, dst_ref=slot(a, block),
                send_sem=send_sems.at[a, k], recv_sem=recv_sems.at[a, k],
                device_id=to, device_id_type=MESH)

        mine, first, passed = [], [], []
        for a in range(n):
            mine.append(pltpu.make_async_copy(ins[a], slot(a, me), local_sems.at[a]))
            mine[a].start()
            first.append([copy(a, 0, me, sibling, src=ins[a])]
                         + [copy(a, 1 + j, me, (*chip, c), src=ins[a]) for j, chip in enumerate(chips)])
            for cp in first[a]:
                cp.start()
        for a in range(n):
            passed.append([copy(a, 4 + j, (*chip, c), sibling) for j, chip in enumerate(chips)])
            for j, chip in enumerate(chips):
                copy(a, 1 + j, (*chip, c), me).wait_recv()
                passed[a][j].start()
        for a in range(n):
            copy(a, 0, sibling, me).wait_recv()
            for j, chip in enumerate(chips):
                copy(a, 4 + j, (*chip, 1 - c), me).wait_recv()
            for cp in first[a] + passed[a]:
                cp.wait_send()
            mine[a].wait()

    any_spec = pl.BlockSpec(memory_space=pl.ANY)
    return pl.pallas_call(
        body, name=name,
        in_specs=[any_spec] * n, out_specs=[any_spec] * n,
        out_shape=[jax.ShapeDtypeStruct((N_DEV,) + s.shape, s.dtype) for s in shards],
        scratch_shapes=[pltpu.SemaphoreType.DMA((n, 7)), pltpu.SemaphoreType.DMA((n, 7)),
                        pltpu.SemaphoreType.DMA((n,))],
    )(*shards)


def exchange_grads(bigs, small, name):
    n = len(bigs)

    def body(*refs):
        ins, small_ref = refs[:n], refs[n]
        outs, small_out = refs[n + 1:2 * n + 1], refs[2 * n + 1]
        send_sems, recv_sems, local_sems = refs[2 * n + 2:]
        x, y, c = _place()
        me = 4 * x + 2 * y + c
        copies = []
        for a in range(n + 1):
            if a < n:
                local = pltpu.make_async_copy(ins[a].at[me], outs[a].at[0], local_sems.at[a])
            else:
                local = pltpu.make_async_copy(small_ref, small_out.at[me], local_sems.at[a])
            local.start()
            copies.append(local)
        for rel in range(1, N_DEV):
            px = 1 - x if rel & 4 else x
            py = 1 - y if rel & 2 else y
            pc = 1 - c if rel & 1 else c
            for a in range(n + 1):
                if a < n:
                    src, dst = ins[a].at[4 * px + 2 * py + pc], outs[a].at[rel]
                else:
                    src, dst = small_ref, small_out.at[me]
                cp = pltpu.make_async_remote_copy(
                    src_ref=src, dst_ref=dst, send_sem=send_sems.at[a, rel - 1],
                    recv_sem=recv_sems.at[a, rel - 1], device_id=(px, py, pc), device_id_type=MESH)
                cp.start()
                copies.append(cp)
        for cp in copies:
            cp.wait()

    any_spec = pl.BlockSpec(memory_space=pl.ANY)
    return pl.pallas_call(
        body, name=name,
        in_specs=[any_spec] * (n + 1), out_specs=[any_spec] * (n + 1),
        out_shape=[jax.ShapeDtypeStruct(b.shape, b.dtype) for b in bigs]
                  + [jax.ShapeDtypeStruct((N_DEV,) + small.shape, small.dtype)],
        scratch_shapes=[pltpu.SemaphoreType.DMA((n + 1, 7)), pltpu.SemaphoreType.DMA((n + 1, 7)),
                        pltpu.SemaphoreType.DMA((n + 1,))],
    )(*bigs, small)


def _adamw(w, g, m, v):
    m = ADAM_B1 * m + (1.0 - ADAM_B1) * g
    v = ADAM_B2 * v + (1.0 - ADAM_B2) * (g * g)
    m_hat = m / (1.0 - ADAM_B1 ** ADAM_STEP)
    v_hat = v / (1.0 - ADAM_B2 ** ADAM_STEP)
    delta = -ADAM_LR * (m_hat / (jnp.sqrt(v_hat) + ADAM_EPS) + ADAM_WD * w)
    return delta, m, v


def adamw_reduce(parts, w, m, v, name):
    K, n = w.shape
    tk = _tile(K, 256, 8)

    def body(p_ref, w_ref, m_ref, v_ref, g_ref, d_ref, nm_ref, nv_ref):
        g = p_ref[0].astype(F32)
        for r in range(1, N_DEV):
            g = g + p_ref[r].astype(F32)
        d, nm, nv = _adamw(w_ref[...], g, m_ref[...], v_ref[...])
        g_ref[...] = g
        d_ref[...] = d
        nm_ref[...] = nm
        nv_ref[...] = nv

    blk = pl.BlockSpec((tk, n), lambda i: (i, 0))
    return pl.pallas_call(
        body, name=name, grid=(K // tk,),
        in_specs=[pl.BlockSpec((N_DEV, tk, n), lambda i: (0, i, 0)), blk, blk, blk],
        out_specs=[blk] * 4, out_shape=[jax.ShapeDtypeStruct((K, n), F32)] * 4,
        compiler_params=_cparams("parallel"),
    )(parts, w, m, v)


def sum_parts(parts, name):
    _, R, D = parts.shape

    def body(p_ref, o_ref):
        g = p_ref[0]
        for r in range(1, N_DEV):
            g = g + p_ref[r]
        o_ref[...] = g

    return pl.pallas_call(body, name=name, out_shape=jax.ShapeDtypeStruct((R, D), F32),
                          compiler_params=_cparams())(parts)


def adamw_small(w, g, m, v, name):
    def body(w_ref, g_ref, m_ref, v_ref, d_ref, nm_ref, nv_ref):
        d, nm, nv = _adamw(w_ref[...], g_ref[...], m_ref[...], v_ref[...])
        d_ref[...] = d
        nm_ref[...] = nm
        nv_ref[...] = nv

    return pl.pallas_call(body, name=name, out_shape=[jax.ShapeDtypeStruct(w.shape, F32)] * 3,
                          compiler_params=_cparams())(w, g, m, v)


def local_step(x, target, wa_in, wa_out, w_kv, wb_in, wb_out, a_norm, cw, cb, lg, lb,
               kv_norm, b_norm, rel_bias, final_norm):
    S, D = x.shape
    H = D // HEAD_DIM

    (h0,) = rms_fwd(x, [a_norm], "rms_a")
    proj = matmul_nn(h0, wa_in, "a_in")
    c, y = conf_fwd(proj, cw, cb, lg, lb, "conf_fwd")
    x1 = matmul_nn(y, wa_out, "a_out", res=x)
    hk, hb = rms_fwd(x1, [kv_norm, b_norm], "rms_b")
    kv = matmul_nn(hk, w_kv, "kv_proj", out_dtype=BF16)
    qz = matmul_nn(hb, wb_in, "b_in")

    bt = jnp.asarray(bucket_tables())
    onehot = (bt[None, :] == jnp.arange(N_BUCKETS, dtype=jnp.int32)[:, None]).astype(F32)
    bias = small_dot(rel_bias.T, onehot, "nn", "bias_table").reshape(H, 2 * N_GROUPS, BLOCK, BLOCK)
    bias_c = [bias[:, 2 * g] for g in range(N_GROUPS)]
    bias_p = [bias[:, 2 * g + 1] for g in range(N_GROUPS)]

    os_, lses = [], []
    for g, (_, dil) in enumerate(GROUPS):
        o_g, l_g = attn_fwd(qz, kv, bias_c[g], bias_p[g], g, dil, "attn_fwd%d" % g)
        os_.append(o_g)
        lses.append(l_g)
    o, lse, y2 = merge_fwd(os_, lses, qz, "merge_fwd")
    x2 = matmul_nn(y2, wb_out, "b_out", res=x1)
    dx2, dx2b, fin_acc = final_loss(x2, final_norm, target, "final_loss")

    dy2 = matmul_nt(dx2b, wb_out, "b_out_dx")
    dwb_out = matmul_tn(y2, dx2b, wb_out.shape[0], "b_out_dw")
    do, delta, dqz = merge_bwd(dy2, o, qz, "merge_bwd")
    dkv = None
    ds_tabs = []
    for g, (_, dil) in enumerate(GROUPS):
        dqz, dkv, ds_c, ds_p = attn_bwd(qz, kv, do, lse, delta, bias_c[g], bias_p[g], dqz, dkv, g, dil,
                                        "attn_bwd%d" % g)
        ds_tabs += [ds_c.reshape(H, BLOCK * BLOCK), ds_p.reshape(H, BLOCK * BLOCK)]
    d_rel = small_dot(onehot, jnp.concatenate(ds_tabs, axis=1), "nt", "bias_grad")
    dhb = matmul_nt(dqz, wb_in, "b_in_dx")
    dwb_in = matmul_tn(hb, dqz, wb_in.shape[0], "b_in_dw")
    dhk = matmul_nt(dkv, w_kv, "kv_dx")
    dw_kv = matmul_tn(hk, dkv, w_kv.shape[0], "kv_dw")
    dx1, dx1b, norm_acc = rms_bwd(x1, [dhk, dhb], [kv_norm, b_norm], dx2, "rms_b_bwd", True)

    dy = matmul_nt(dx1b, wa_out, "a_out_dx")
    dwa_out = matmul_tn(y, dx1b, wa_out.shape[0], "a_out_dw")
    dc, dz, ln_acc = conf_bwd_ln(c, dy, proj, lg, lb, "conf_bwd_ln")
    dproj, dcw = conf_bwd_conv(proj, dc, dz, cw, "conf_bwd_conv")
    dh0 = matmul_nt(dproj, wa_in, "a_in_dx")
    dwa_in = matmul_tn(h0, dproj, wa_in.shape[0], "a_in_dw")
    grad_x, a_acc = rms_bwd(x, [dh0], [a_norm], dx1, "rms_a_bwd", False)

    small = dict(a_norm=a_acc[0:1], conv_b=ln_acc[2:3], ln_g=ln_acc[0:1], ln_b=ln_acc[1:2],
                 conv_w=dcw[0:CONV_WIDTH], kv_norm=norm_acc[0:1], b_norm=norm_acc[1:2],
                 final_norm=fin_acc[0:1], rel_bias=d_rel, loss=fin_acc[1:2])
    return grad_x, (dwa_in, dwa_out, dw_kv, dwb_in, dwb_out), small


def _pack_sharded(norm, conv_w, conv_b, ln_g, ln_b):
    n = norm.shape[-1]
    rows = jnp.concatenate([norm.reshape(1, n), conv_b.reshape(1, n), ln_g.reshape(1, n), ln_b.reshape(1, n),
                            conv_w.reshape(CONV_WIDTH, n)], axis=0)
    return jnp.pad(rows, ((0, SMALL_SH_ROWS - rows.shape[0]), (0, 0)))


def _pack_replicated(kv_norm, b_norm, final_norm, rel_bias, D):
    rb = jnp.pad(rel_bias.reshape(1, -1), ((0, 0), (0, D - rel_bias.size)))
    rows = jnp.concatenate([kv_norm.reshape(1, D), b_norm.reshape(1, D), final_norm.reshape(1, D), rb], axis=0)
    return jnp.pad(rows, ((0, SMALL_REP_ROWS - rows.shape[0]), (0, 0)))


def kernel(x, a_norm, a_w_in, a_conv_w, a_conv_b, a_ln_g, a_ln_b, a_w_out, kv_norm, w_kv, b_norm, b_w_in, b_w_out, rel_bias, final_norm, loss_target, m_a_norm, m_a_w_in, m_a_conv_w, m_a_conv_b, m_a_ln_g, m_a_ln_b, m_a_w_out, m_kv_norm, m_w_kv, m_b_norm, m_b_w_in, m_b_w_out, m_rel_bias, m_final_norm, v_a_norm, v_a_w_in, v_a_conv_w, v_a_conv_b, v_a_ln_g, v_a_ln_b, v_a_w_out, v_kv_norm, v_w_kv, v_b_norm, v_b_w_in, v_b_w_out, v_rel_bias, v_final_norm):
    _, S, D = x.shape
    E = D
    nsh = D // N_DEV
    xs = x.reshape(S, D)
    tgt = loss_target.reshape(S, D)

    big_w = [a_w_in[0], a_w_out[0], w_kv, b_w_in[0], b_w_out[0]]
    big_m = [m_a_w_in[0], m_a_w_out[0], m_w_kv, m_b_w_in[0], m_b_w_out[0]]
    big_v = [v_a_w_in[0], v_a_w_out[0], v_w_kv, v_b_w_in[0], v_b_w_out[0]]
    sh_w = _pack_sharded(a_norm, a_conv_w, a_conv_b, a_ln_g, a_ln_b)
    gathered = all_gather([w.astype(BF16) for w in big_w] + [sh_w], "gather_weights")
    wa_in, wa_out, wkv, wb_in, wb_out, sh_all = gathered
    wa_out = wa_out.reshape(1, E, D)
    wb_out = wb_out.reshape(1, D, D)
    sh_full = sh_all.transpose(1, 0, 2).reshape(SMALL_SH_ROWS, D)
    cw = jnp.pad(sh_full[4:4 + CONV_WIDTH], ((0, HALO - CONV_WIDTH), (0, 0)))

    grad_x, dws, small = local_step(
        xs, tgt, wa_in, wa_out, wkv, wb_in, wb_out, sh_full[0:1], cw, sh_full[1:2], sh_full[2:3],
        sh_full[3:4], kv_norm.reshape(1, D), b_norm.reshape(1, D), rel_bias, final_norm.reshape(1, D))

    dwa_in, dwa_out, dw_kv, dwb_in, dwb_out = dws
    dwa_out = dwa_out.reshape(N_DEV, E // N_DEV, D)
    dwb_out = dwb_out.reshape(N_DEV, D // N_DEV, D)
    rb = jnp.pad(small["rel_bias"].reshape(1, -1), ((0, 0), (0, D - rel_bias.size)))
    small_rows = jnp.concatenate(
        [small["a_norm"], small["conv_b"], small["ln_g"], small["ln_b"], small["conv_w"],
         jnp.zeros((SMALL_SH_ROWS - 4 - CONV_WIDTH, D), F32),
         small["kv_norm"], small["b_norm"], small["final_norm"], rb, small["loss"],
         jnp.zeros((SMALL_REP_ROWS - 5, D), F32)], axis=0)
    *parts, small_parts = exchange_grads([dwa_in, dwa_out, dw_kv, dwb_in, dwb_out], small_rows, "exchange_grads")

    names = ["a_w_in", "a_w_out", "w_kv", "b_w_in", "b_w_out"]
    big_out = {}
    for nm, p, w, m, v in zip(names, parts, big_w, big_m, big_v):
        big_out[nm] = adamw_reduce(p, w, m, v, "adamw_" + nm)

    gsum = sum_parts(small_parts, "sum_small")
    me = 4 * lax.axis_index("x") + 2 * lax.axis_index("y") + lax.axis_index("c")
    g_sh = lax.dynamic_slice(gsum, (0, me * nsh), (SMALL_SH_ROWS, nsh))
    g_rep = gsum[SMALL_SH_ROWS:]
    loss = g_rep[4, 0]
    sh_m = _pack_sharded(m_a_norm, m_a_conv_w, m_a_conv_b, m_a_ln_g, m_a_ln_b)
    sh_v = _pack_sharded(v_a_norm, v_a_conv_w, v_a_conv_b, v_a_ln_g, v_a_ln_b)
    sh_d, sh_nm, sh_nv = adamw_small(sh_w, g_sh, sh_m, sh_v, "adamw_sharded")
    rep_w = _pack_replicated(kv_norm, b_norm, final_norm, rel_bias, D)
    rep_m = _pack_replicated(m_kv_norm, m_b_norm, m_final_norm, m_rel_bias, D)
    rep_v = _pack_replicated(v_kv_norm, v_b_norm, v_final_norm, v_rel_bias, D)
    g_rep_w = g_rep.at[4].set(0.0)
    rep_d, rep_nm, rep_nv = adamw_small(rep_w, g_rep_w, rep_m, rep_v, "adamw_replicated")

    def unpack(kind):
        sh = (g_sh, sh_d, sh_nm, sh_nv)[kind]
        rep = (g_rep_w, rep_d, rep_nm, rep_nv)[kind]
        big = {nm: big_out[nm][kind] for nm in names}
        nb = rel_bias.size
        return [
            sh[0:1],
            big["a_w_in"][None],
            sh[4:4 + CONV_WIDTH][None],
            sh[1:2], sh[2:3], sh[3:4],
            big["a_w_out"][None],
            rep[0],
            big["w_kv"],
            rep[1:2],
            big["b_w_in"][None],
            big["b_w_out"][None],
            rep[3, :nb].reshape(rel_bias.shape),
            rep[2],
        ]

    return (loss, grad_x.reshape(1, S, D), *unpack(0), *unpack(1), *unpack(2), *unpack(3))
```

```python
import functools
import math

import numpy as np
import jax
import jax.numpy as jnp
from jax import lax
from jax.experimental import pallas as pl
from jax.experimental.pallas import tpu as pltpu

F32 = jnp.float32
BF16 = jnp.bfloat16

N_DEV = 8
EPS = 1e-6
HEAD_DIM = 128
BLOCK = 128
GROUPS = ((128, 1), (512, 4), (2048, 16))
N_GROUPS = len(GROUPS)
CONV_WIDTH = 31
HALO = 32
N_BUCKETS = 32
MAX_EXACT = N_BUCKETS // 2
MAX_DISTANCE = 2048
V7X_VMEM_BYTES = 64 * 1024 * 1024
VMEM_LIMIT = (V7X_VMEM_BYTES * 7) // 8
LANE = 128

ADAM_LR = 0.001
ADAM_B1 = 0.9
ADAM_B2 = 0.999
ADAM_EPS = 1e-08
ADAM_WD = 0.01
ADAM_STEP = 10

SMALL_SH_ROWS = 48
SMALL_REP_ROWS = 24
MESH = pl.DeviceIdType.MESH


def _tile(dim, pref, unit=LANE):
    if dim <= pref:
        return dim
    t = (pref // unit) * unit
    while dim % t:
        t -= unit
    assert t > 0
    return t


def _cparams(*sem):
    return pltpu.CompilerParams(dimension_semantics=sem if sem else None,
                                vmem_limit_bytes=VMEM_LIMIT)


def _sigmoid(v):
    return jax.nn.sigmoid(v)


def _kv_split_index(tw, D):
    pd = D // tw

    def index(j):
        return (j // pd) % 2, (j // (2 * pd)) * pd + j % pd

    return index


def _col_tile(n, also, pref):
    t = (min(pref, n) // LANE) * LANE
    while n % t or (also is not None and also % t):
        t -= LANE
    assert t > 0
    return t


def matmul_nn(a, w, name, res=None, out_dtype=F32, kv_split=False):
    M, K = a.shape
    nb, _, n = w.shape
    D = nb * n // (2 * N_GROUPS)
    tn = _col_tile(n, D if kv_split else None, 1024)
    tm = _tile(M, 2048 if tn <= 512 else 1024)
    per = n // tn

    def body(*refs):
        if res is None:
            a_ref, w_ref, o_ref = refs
        else:
            a_ref, w_ref, r_ref, o_ref = refs
        acc = jnp.dot(a_ref[...], w_ref[...], preferred_element_type=F32)
        if res is not None:
            acc = r_ref[...] + acc
        o_ref[...] = acc.astype(o_ref.dtype)

    in_specs = [pl.BlockSpec((tm, K), lambda i, j: (i, 0)),
                pl.BlockSpec((None, K, tn), lambda i, j: (j // per, 0, j % per))]
    args = [a, w]
    if res is not None:
        in_specs.append(pl.BlockSpec((tm, tn), lambda i, j: (i, j)))
        args.append(res)
    if kv_split:
        split = _kv_split_index(tn, D)
        out_spec = pl.BlockSpec((None, tm, tn), lambda i, j: (split(j)[0], i, split(j)[1]))
        out_shape = jax.ShapeDtypeStruct((2, M, N_GROUPS * D), out_dtype)
    else:
        out_spec = pl.BlockSpec((tm, tn), lambda i, j: (i, j))
        out_shape = jax.ShapeDtypeStruct((M, nb * n), out_dtype)
    return pl.pallas_call(
        body, name=name, grid=(M // tm, nb * per),
        in_specs=in_specs, out_specs=out_spec, out_shape=out_shape,
        compiler_params=_cparams("parallel", "parallel"),
    )(*args)


def matmul_nt(dy, w, name, kv_split=False):
    M = dy.shape[-2]
    nb, K, n = w.shape
    D = nb * n // (2 * N_GROUPS)
    tm = _tile(M, 1024)
    tc = _col_tile(n, D if kv_split else None, 1024)
    per = n // tc

    def body(dy_ref, w_ref, o_ref):
        j = pl.program_id(1)
        part = lax.dot_general(dy_ref[...], w_ref[...], (((1,), (1,)), ((), ())),
                               preferred_element_type=F32)

        @pl.when(j == 0)
        def _():
            o_ref[...] = part

        @pl.when(j > 0)
        def _():
            o_ref[...] += part

    if kv_split:
        split = _kv_split_index(tc, D)
        dy_spec = pl.BlockSpec((None, tm, tc), lambda i, j: (split(j)[0], i, split(j)[1]))
    else:
        dy_spec = pl.BlockSpec((tm, tc), lambda i, j: (i, j))
    return pl.pallas_call(
        body, name=name, grid=(M // tm, nb * per),
        in_specs=[dy_spec, pl.BlockSpec((None, K, tc), lambda i, j: (j // per, 0, j % per))],
        out_specs=pl.BlockSpec((tm, K), lambda i, j: (i, 0)),
        out_shape=jax.ShapeDtypeStruct((M, K), F32),
        compiler_params=_cparams("parallel", "arbitrary"),
    )(dy, w)


def matmul_tn(a, dy, nb, name, out_dtype=BF16, kv_split=False):
    M, K = a.shape
    N = 2 * dy.shape[-1] if kv_split else dy.shape[-1]
    n = N // nb
    D = N // (2 * N_GROUPS)
    tn = _col_tile(n, D if kv_split else None, 1024)
    per = n // tn
    tk = _tile(K, 1024)
    tmc = _tile(M, 2048)
    steps = M // tmc

    def body(a_ref, dy_ref, o_ref, acc_ref):
        s = pl.program_id(2)
        part = lax.dot_general(a_ref[...], dy_ref[...], (((0,), (0,)), ((), ())),
                               preferred_element_type=F32)

        @pl.when(s == 0)
        def _():
            acc_ref[...] = part

        @pl.when(s > 0)
        def _():
            acc_ref[...] += part

        @pl.when(s == steps - 1)
        def _():
            o_ref[...] = acc_ref[...].astype(o_ref.dtype)

    if kv_split:
        split = _kv_split_index(tn, D)
        dy_spec = pl.BlockSpec((None, tmc, tn), lambda j, k, s: (split(j)[0], s, split(j)[1]))
    else:
        dy_spec = pl.BlockSpec((tmc, tn), lambda j, k, s: (s, j))
    return pl.pallas_call(
        body, name=name, grid=(nb * per, K // tk, steps),
        in_specs=[pl.BlockSpec((tmc, tk), lambda j, k, s: (s, k)), dy_spec],
        out_specs=pl.BlockSpec((None, tk, tn), lambda j, k, s: (j // per, k, j % per)),
        out_shape=jax.ShapeDtypeStruct((nb, K, n), out_dtype),
        scratch_shapes=[pltpu.VMEM((tk, tn), F32)],
        compiler_params=_cparams("parallel", "parallel", "arbitrary"),
    )(a, dy)


def small_dot(a, b, contract, name):
    if contract == "nn":
        dims = (((1,), (0,)), ((), ()))
        out = (a.shape[0], b.shape[1])
    else:
        dims = (((1,), (1,)), ((), ()))
        out = (a.shape[0], b.shape[0])

    def body(a_ref, b_ref, o_ref):
        o_ref[...] = lax.dot_general(a_ref[...], b_ref[...], dims, precision=lax.Precision.HIGHEST,
                                     preferred_element_type=F32)

    return pl.pallas_call(body, name=name, out_shape=jax.ShapeDtypeStruct(out, F32),
                          compiler_params=_cparams())(a, b)


def rms_fwd(x, gains, name):
    S, D = x.shape
    T = _tile(S, 512, 8)
    n = len(gains)

    def body(x_ref, *refs):
        xv = x_ref[...]
        xn = xv * lax.rsqrt(jnp.mean(xv * xv, axis=-1, keepdims=True) + EPS)
        for g_ref, o_ref in zip(refs[:n], refs[n:]):
            o_ref[...] = (xn * g_ref[...]).astype(o_ref.dtype)

    row = pl.BlockSpec((T, D), lambda i: (i, 0))
    vec = pl.BlockSpec((1, D), lambda i: (0, 0))
    return pl.pallas_call(
        body, name=name, grid=(S // T,),
        in_specs=[row] + [vec] * n, out_specs=[row] * n,
        out_shape=[jax.ShapeDtypeStruct((S, D), BF16)] * n,
        compiler_params=_cparams("parallel"),
    )(x, *gains)


def rms_bwd(x, dhs, gains, dres, name, want_bf16):
    S, D = x.shape
    T = _tile(S, 256, 8)
    n = len(gains)

    def body(x_ref, *refs):
        dh_refs = refs[:n]
        g_refs = refs[n:2 * n]
        dres_ref = refs[2 * n]
        outs = refs[2 * n + 1:]
        dx_ref, dg_ref = outs[0], outs[-1]
        i = pl.program_id(0)

        @pl.when(i == 0)
        def _():
            dg_ref[...] = jnp.zeros_like(dg_ref)

        xv = x_ref[...]
        r = lax.rsqrt(jnp.mean(xv * xv, axis=-1, keepdims=True) + EPS)
        xn = xv * r
        dxn = jnp.zeros_like(xv)
        for k in range(n):
            dh = dh_refs[k][...]
            dg_ref[k:k + 1, :] += jnp.sum(dh * xn, axis=0, keepdims=True)
            dxn = dxn + dh * g_refs[k][...]
        dx = dres_ref[...] + r * (dxn - xn * jnp.mean(dxn * xn, axis=-1, keepdims=True))
        dx_ref[...] = dx
        if want_bf16:
            outs[1][...] = dx.astype(BF16)

    row = pl.BlockSpec((T, D), lambda i: (i, 0))
    vec = pl.BlockSpec((1, D), lambda i: (0, 0))
    acc = pl.BlockSpec((8, D), lambda i: (0, 0))
    out_specs = [row] + ([row] if want_bf16 else []) + [acc]
    out_shape = ([jax.ShapeDtypeStruct((S, D), F32)]
                 + ([jax.ShapeDtypeStruct((S, D), BF16)] if want_bf16 else [])
                 + [jax.ShapeDtypeStruct((8, D), F32)])
    return pl.pallas_call(
        body, name=name, grid=(S // T,),
        in_specs=[row] + [row] * n + [vec] * n + [row],
        out_specs=out_specs, out_shape=out_shape,
        compiler_params=_cparams("arbitrary"),
    )(x, *dhs, *gains, dres)


def final_loss(x2, gain, target, name):
    S, D = x2.shape
    T = _tile(S, 256, 8)

    def body(x_ref, g_ref, t_ref, dx_ref, dxb_ref, acc_ref):
        i = pl.program_id(0)

        @pl.when(i == 0)
        def _():
            acc_ref[...] = jnp.zeros_like(acc_ref)

        xv = x_ref[...]
        g = g_ref[...]
        r = lax.rsqrt(jnp.mean(xv * xv, axis=-1, keepdims=True) + EPS)
        xn = xv * r
        err = xn * g - t_ref[...]
        dy = err * (1.0 / D)
        acc_ref[0:1, :] += jnp.sum(dy * xn, axis=0, keepdims=True)
        acc_ref[1:2, :] += jnp.full((1, D), 0.5 / D, F32) * jnp.sum(err * err)
        dxn = dy * g
        dx = r * (dxn - xn * jnp.mean(dxn * xn, axis=-1, keepdims=True))
        dx_ref[...] = dx
        dxb_ref[...] = dx.astype(BF16)

    row = pl.BlockSpec((T, D), lambda i: (i, 0))
    return pl.pallas_call(
        body, name=name, grid=(S // T,),
        in_specs=[row, pl.BlockSpec((1, D), lambda i: (0, 0)), row],
        out_specs=[row, row, pl.BlockSpec((8, D), lambda i: (0, 0))],
        out_shape=[jax.ShapeDtypeStruct((S, D), F32), jax.ShapeDtypeStruct((S, D), BF16),
                   jax.ShapeDtypeStruct((8, D), F32)],
        compiler_params=_cparams("arbitrary"),
    )(x2, gain, target)


ROW_CHUNK = 64
LANE_CHUNK = 512


def conf_fwd(proj, cw, cb, lg, lb, name):
    S, E3 = proj.shape
    E = E3 // 3
    T = _tile(S, 256, HALO)
    R = T // HALO
    lc = _tile(E, LANE_CHUNK)
    rc = min(ROW_CHUNK, T)

    def body(a_ref, b_ref, z_ref, ap_ref, bp_ref, cw_ref, cb_ref, lg_ref, lb_ref, c_ref, y_ref, u_scr):
        i = pl.program_id(0)
        up = ap_ref[...] * _sigmoid(bp_ref[...])
        u_scr[0:HALO, :] = jnp.where(i > 0, up, 0.0)
        u_scr[HALO:HALO + T, :] = a_ref[...] * _sigmoid(b_ref[...])
        off = HALO - (CONV_WIDTH - 1)
        for r0 in range(0, T, rc):
            for l0 in range(0, E, lc):
                acc = jnp.broadcast_to(cb_ref[:, l0:l0 + lc], (rc, lc))
                for k in range(CONV_WIDTH):
                    acc = acc + u_scr[r0 + off + k:r0 + off + k + rc, l0:l0 + lc] * cw_ref[k:k + 1, l0:l0 + lc]
                c_ref[r0:r0 + rc, l0:l0 + lc] = acc
        c = c_ref[...]
        mu = jnp.mean(c, axis=-1, keepdims=True)
        d = c - mu
        var = jnp.mean(d * d, axis=-1, keepdims=True)
        cn = d * lax.rsqrt(var + EPS) * lg_ref[...] + lb_ref[...]
        z = z_ref[...]
        y_ref[...] = ((cn * _sigmoid(cn)) * (z * _sigmoid(z))).astype(BF16)

    def col(j):
        return pl.BlockSpec((T, E), lambda i: (i, j))

    def prev(j):
        return pl.BlockSpec((HALO, E), lambda i: (jnp.maximum(i * R - 1, 0), j))

    vec = pl.BlockSpec((1, E), lambda i: (0, 0))
    return pl.pallas_call(
        body, name=name, grid=(S // T,),
        in_specs=[col(0), col(1), col(2), prev(0), prev(1),
                  pl.BlockSpec((HALO, E), lambda i: (0, 0)), vec, vec, vec],
        out_specs=[pl.BlockSpec((T, E), lambda i: (i, 0))] * 2,
        out_shape=[jax.ShapeDtypeStruct((S, E), F32), jax.ShapeDtypeStruct((S, E), BF16)],
        scratch_shapes=[pltpu.VMEM((HALO + T, E), F32)],
        compiler_params=_cparams("parallel"),
    )(proj, proj, proj, proj, proj, cw, cb, lg, lb)


def conf_bwd_ln(c, dy, proj, lg, lb, name):
    S, E = c.shape
    T = _tile(S, 256, 8)

    def body(c_ref, dy_ref, z_ref, lg_ref, lb_ref, dc_ref, dz_ref, acc_ref):
        i = pl.program_id(0)

        @pl.when(i == 0)
        def _():
            acc_ref[...] = jnp.zeros_like(acc_ref)

        cv = c_ref[...]
        mu = jnp.mean(cv, axis=-1, keepdims=True)
        d = cv - mu
        var = jnp.mean(d * d, axis=-1, keepdims=True)
        rstd = lax.rsqrt(var + EPS)
        xh = d * rstd
        lgv = lg_ref[...]
        cn = xh * lgv + lb_ref[...]
        z = z_ref[...]
        dy = dy_ref[...]
        sc = _sigmoid(cn)
        sz = _sigmoid(z)
        dcn = dy * (z * sz) * (sc * (1.0 + cn * (1.0 - sc)))
        dz_ref[...] = (dy * (cn * sc) * (sz * (1.0 + z * (1.0 - sz)))).astype(BF16)
        acc_ref[0:1, :] += jnp.sum(dcn * xh, axis=0, keepdims=True)
        acc_ref[1:2, :] += jnp.sum(dcn, axis=0, keepdims=True)
        dxh = dcn * lgv
        dc = rstd * (dxh - jnp.mean(dxh, axis=-1, keepdims=True)
                     - xh * jnp.mean(dxh * xh, axis=-1, keepdims=True))
        acc_ref[2:3, :] += jnp.sum(dc, axis=0, keepdims=True)
        dc_ref[...] = dc

    row = pl.BlockSpec((T, E), lambda i: (i, 0))
    vec = pl.BlockSpec((1, E), lambda i: (0, 0))
    return pl.pallas_call(
        body, name=name, grid=(S // T,),
        in_specs=[row, row, pl.BlockSpec((T, E), lambda i: (i, 2)), vec, vec],
        out_specs=[row, row, pl.BlockSpec((8, E), lambda i: (0, 0))],
        out_shape=[jax.ShapeDtypeStruct((S, E), F32), jax.ShapeDtypeStruct((S, E), BF16),
                   jax.ShapeDtypeStruct((8, E), F32)],
        compiler_params=_cparams("arbitrary"),
    )(c, dy, proj, lg, lb)


def conf_bwd_conv(proj, dc, dz, cw, name):
    S, E3 = proj.shape
    E = E3 // 3
    T = _tile(S, 256, HALO)
    R = T // HALO
    nt = S // T
    lc = _tile(E, LANE_CHUNK)
    rc = min(ROW_CHUNK, T)

    def body(a_ref, b_ref, ap_ref, bp_ref, dc_ref, dcn_ref, dz_ref, cw_ref, o_ref, dw_ref, u_scr, dc_scr):
        i = pl.program_id(0)

        @pl.when(i == 0)
        def _():
            dw_ref[...] = jnp.zeros_like(dw_ref)

        a = a_ref[...]
        sb = _sigmoid(b_ref[...])
        up = ap_ref[...] * _sigmoid(bp_ref[...])
        u_scr[0:HALO, :] = jnp.where(i > 0, up, 0.0)
        u_scr[HALO:HALO + T, :] = a * sb
        dc_scr[0:T, :] = dc_ref[...]
        dc_scr[T:T + HALO, :] = jnp.where(i < nt - 1, dcn_ref[...], 0.0)
        off = HALO - (CONV_WIDTH - 1)
        for l0 in range(0, E, lc):
            for k in range(CONV_WIDTH):
                prod = u_scr[off + k:off + k + T, l0:l0 + lc] * dc_scr[0:T, l0:l0 + lc]
                dw_ref[k:k + 1, l0:l0 + lc] += jnp.sum(prod, axis=0, keepdims=True)
            for r0 in range(0, T, rc):
                acc = jnp.zeros((rc, lc), F32)
                for k in range(CONV_WIDTH):
                    s0 = r0 + (CONV_WIDTH - 1) - k
                    acc = acc + dc_scr[s0:s0 + rc, l0:l0 + lc] * cw_ref[k:k + 1, l0:l0 + lc]
                av = a[r0:r0 + rc, l0:l0 + lc]
                sv = sb[r0:r0 + rc, l0:l0 + lc]
                o_ref[r0:r0 + rc, l0:l0 + lc] = (acc * sv).astype(BF16)
                o_ref[r0:r0 + rc, E + l0:E + l0 + lc] = (acc * av * sv * (1.0 - sv)).astype(BF16)
        o_ref[:, 2 * E:3 * E] = dz_ref[...]

    def col(j):
        return pl.BlockSpec((T, E), lambda i: (i, j))

    def prev(j):
        return pl.BlockSpec((HALO, E), lambda i: (jnp.maximum(i * R - 1, 0), j))

    row = pl.BlockSpec((T, E), lambda i: (i, 0))
    nxt = pl.BlockSpec((HALO, E), lambda i: (jnp.minimum((i + 1) * R, S // HALO - 1), 0))
    return pl.pallas_call(
        body, name=name, grid=(nt,),
        in_specs=[col(0), col(1), prev(0), prev(1), row, nxt, row,
                  pl.BlockSpec((HALO, E), lambda i: (0, 0))],
        out_specs=[pl.BlockSpec((T, E3), lambda i: (i, 0)), pl.BlockSpec((HALO, E), lambda i: (0, 0))],
        out_shape=[jax.ShapeDtypeStruct((S, E3), BF16), jax.ShapeDtypeStruct((HALO, E), F32)],
        scratch_shapes=[pltpu.VMEM((HALO + T, E), F32), pltpu.VMEM((T + HALO, E), F32)],
        compiler_params=_cparams("arbitrary"),
    )(proj, proj, proj, proj, dc, dc, dz, cw)


def bucket_tables():
    q = np.arange(BLOCK)[:, None]
    k = np.arange(BLOCK)[None, :]
    out = []
    for window, dil in GROUPS:
        w_sub = window // dil
        for delta in (q - k, q + BLOCK - k):
            valid = (delta >= 0) & (delta <= w_sub)
            dist = np.clip(delta, 0, None) * dil
            large = MAX_EXACT + (np.log(np.maximum(dist, 1).astype(np.float32) / MAX_EXACT)
                                 / math.log(MAX_DISTANCE / MAX_EXACT)
                                 * (N_BUCKETS - MAX_EXACT)).astype(np.int32)
            large = np.minimum(large, N_BUCKETS - 1)
            bucket = np.where(dist < MAX_EXACT, dist, large)
            out.append(np.where(valid, bucket, -1).reshape(-1))
    return np.concatenate(out).astype(np.int32)


def _attn_masks():
    ql = lax.broadcasted_iota(jnp.int32, (BLOCK, BLOCK), 0)
    kl = lax.broadcasted_iota(jnp.int32, (BLOCK, BLOCK), 1)
    return kl <= ql, kl >= ql


def _dot_nt(a, b):
    return lax.dot_general(a, b, (((1,), (1,)), ((), ())), preferred_element_type=F32)


def _dot_tn(a, b):
    return lax.dot_general(a, b, (((0,), (0,)), ((), ())), preferred_element_type=F32)


ATTN_ROWS = 2048


def _rows(start, dil):
    return pl.ds(start, BLOCK) if dil == 1 else pl.ds(start, BLOCK, stride=dil)


def _attn_geometry(S, dil):
    halo = BLOCK * dil
    rows = max(min(S, ATTN_ROWS), halo)
    return halo, rows, S // rows, rows // halo


def attn_fwd(qz, kv, bias_c, bias_p, g, dil, name):
    S = qz.shape[0]
    D = qz.shape[1] // (N_GROUPS + 1)
    H = D // HEAD_DIM
    halo, rows, nsb, nblk = _attn_geometry(S, dil)
    scale = HEAD_DIM ** -0.5

    def body(q_ref, kvc_ref, kvp_ref, bc_ref, bp_ref, o_ref, l_ref, ks, vs):
        sb = pl.program_id(1)
        ks[0:halo, :] = kvp_ref[0].astype(F32)
        ks[halo:, :] = kvc_ref[0].astype(F32)
        vs[0:halo, :] = kvp_ref[1].astype(F32)
        vs[halo:, :] = kvc_ref[1].astype(F32)
        mask_c, mask_p0 = _attn_masks()
        bc = bc_ref[0]
        bp = bp_ref[0]
        for jj in range(nblk):
            mask_p = mask_p0 if jj > 0 else mask_p0 & (sb > 0)
            for r in range(dil):
                cur = _rows(halo + jj * halo + r, dil)
                prv = _rows(jj * halo + r, dil)
                out = _rows(jj * halo + r, dil)
                q = q_ref[out, :].astype(BF16)
                s_c = jnp.where(mask_c, _dot_nt(q, ks[cur, :].astype(BF16)) * scale + bc, -jnp.inf)
                s_p = jnp.where(mask_p, _dot_nt(q, ks[prv, :].astype(BF16)) * scale + bp, -jnp.inf)
                m = jnp.maximum(jnp.max(s_c, axis=-1, keepdims=True), jnp.max(s_p, axis=-1, keepdims=True))
                p_c = jnp.exp(s_c - m)
                p_p = jnp.exp(s_p - m)
                den = jnp.sum(p_c, axis=-1, keepdims=True) + jnp.sum(p_p, axis=-1, keepdims=True)
                pv = (jnp.dot(p_c.astype(BF16), vs[cur, :].astype(BF16), preferred_element_type=F32)
                      + jnp.dot(p_p.astype(BF16), vs[prv, :].astype(BF16), preferred_element_type=F32))
                o_ref[out, :] = pv / den
                l_ref[out, :] = jnp.broadcast_to(m + jnp.log(den), (BLOCK, HEAD_DIM))

    per = rows // halo
    bias_spec = pl.BlockSpec((1, BLOCK, BLOCK), lambda h, sb: (h, 0, 0))
    out_spec = pl.BlockSpec((rows, HEAD_DIM), lambda h, sb: (sb, h))
    return pl.pallas_call(
        body, name=name, grid=(H, nsb),
        in_specs=[pl.BlockSpec((rows, HEAD_DIM), lambda h, sb: (sb, g * H + h)),
                  pl.BlockSpec((2, rows, HEAD_DIM), lambda h, sb: (0, sb, g * H + h)),
                  pl.BlockSpec((2, halo, HEAD_DIM), lambda h, sb: (0, jnp.maximum(sb * per - 1, 0), g * H + h)),
                  bias_spec, bias_spec],
        out_specs=[out_spec, out_spec],
        out_shape=[jax.ShapeDtypeStruct((S, D), F32)] * 2,
        scratch_shapes=[pltpu.VMEM((halo + rows, HEAD_DIM), F32)] * 2,
        compiler_params=_cparams("parallel", "arbitrary"),
    )(qz, kv, kv, bias_c, bias_p)


def merge_fwd(os_, lses, qz, name):
    S, D = os_[0].shape
    T = _tile(S, 256, 8)

    def body(o1, o2, o3, l1, l2, l3, z_ref, o_ref, lse_ref, y_ref):
        la, lb_, lc_ = l1[...], l2[...], l3[...]
        m = jnp.maximum(jnp.maximum(la, lb_), lc_)
        ea, eb, ec = jnp.exp(la - m), jnp.exp(lb_ - m), jnp.exp(lc_ - m)
        den = ea + eb + ec
        o = (ea * o1[...] + eb * o2[...] + ec * o3[...]) / den
        z = z_ref[...]
        o_ref[...] = o
        lse_ref[...] = m + jnp.log(den)
        y_ref[...] = (o * (z * _sigmoid(z))).astype(BF16)

    row = pl.BlockSpec((T, D), lambda i: (i, 0))
    return pl.pallas_call(
        body, name=name, grid=(S // T,),
        in_specs=[row] * 6 + [pl.BlockSpec((T, D), lambda i: (i, N_GROUPS))],
        out_specs=[row] * 3,
        out_shape=[jax.ShapeDtypeStruct((S, D), F32), jax.ShapeDtypeStruct((S, D), F32),
                   jax.ShapeDtypeStruct((S, D), BF16)],
        compiler_params=_cparams("parallel"),
    )(*os_, *lses, qz)


def merge_bwd(dy2, o, qz, name):
    S, D = o.shape
    H = D // HEAD_DIM
    T = _tile(S, 256, 8)
    nq = N_GROUPS + 1

    def body(dy_ref, o_ref, z_ref, do_ref, dl_ref, dqz_ref):
        dy = dy_ref[...]
        ov = o_ref[...]
        z = z_ref[...]
        sz = _sigmoid(z)
        do = dy * (z * sz)
        do_ref[...] = do.astype(BF16)
        dqz_ref[...] = (dy * ov * (sz * (1.0 + z * (1.0 - sz)))).astype(BF16)
        prod = do * ov
        for h in range(H):
            hs = slice(h * HEAD_DIM, (h + 1) * HEAD_DIM)
            dl_ref[:, hs] = jnp.broadcast_to(jnp.sum(prod[:, hs], axis=-1, keepdims=True), (T, HEAD_DIM))

    row = pl.BlockSpec((T, D), lambda i: (i, 0))
    last = pl.BlockSpec((T, D), lambda i: (i, N_GROUPS))
    return pl.pallas_call(
        body, name=name, grid=(S // T,),
        in_specs=[row, row, last],
        out_specs=[row, row, last],
        out_shape=[jax.ShapeDtypeStruct((S, D), BF16), jax.ShapeDtypeStruct((S, D), F32),
                   jax.ShapeDtypeStruct((S, nq * D), BF16)],
        compiler_params=_cparams("parallel"),
    )(dy2, o, qz)


def attn_bwd(qz, kv, do, lse, delta, bias_c, bias_p, dqz, dkv, g, dil, name):
    S = qz.shape[0]
    D = qz.shape[1] // (N_GROUPS + 1)
    H = D // HEAD_DIM
    halo, rows, nsb, nblk = _attn_geometry(S, dil)
    scale = HEAD_DIM ** -0.5
    have_dkv = dkv is not None

    def body(*refs):
        q_ref, do_ref, l_ref, d_ref, kvc_ref, kvp_ref, bc_ref, bp_ref = refs[:8]
        n_in = 8 + 1 + (1 if have_dkv else 0)
        dq_ref, dkv_ref, dsc_ref, dsp_ref, ks, vs, dks, dvs, dos, dqs, carry_k, carry_v = refs[n_in:]
        i = pl.program_id(1)
        sb = nsb - 1 - i

        @pl.when(i == 0)
        def _():
            dsc_ref[...] = jnp.zeros_like(dsc_ref)
            dsp_ref[...] = jnp.zeros_like(dsp_ref)

        ks[0:halo, :] = kvp_ref[0].astype(F32)
        ks[halo:, :] = kvc_ref[0].astype(F32)
        vs[0:halo, :] = kvp_ref[1].astype(F32)
        vs[halo:, :] = kvc_ref[1].astype(F32)
        dos[...] = do_ref[...].astype(F32)
        dks[...] = jnp.zeros_like(dks)
        dvs[...] = jnp.zeros_like(dvs)
        mask_c, mask_p0 = _attn_masks()
        bc = bc_ref[0]
        bp = bp_ref[0]
        for jj in range(nblk):
            mask_p = mask_p0 if jj > 0 else mask_p0 & (sb > 0)
            for r in range(dil):
                cur = _rows(halo + jj * halo + r, dil)
                prv = _rows(jj * halo + r, dil)
                own = _rows(jj * halo + r, dil)
                q = q_ref[own, :].astype(BF16)
                do = dos[own, :].astype(BF16)
                lse = l_ref[own, :]
                dlt = d_ref[own, :]
                kc = ks[cur, :].astype(BF16)
                kp = ks[prv, :].astype(BF16)
                vc = vs[cur, :].astype(BF16)
                vp = vs[prv, :].astype(BF16)
                p_c = jnp.where(mask_c, jnp.exp(_dot_nt(q, kc) * scale + bc - lse), 0.0)
                p_p = jnp.where(mask_p, jnp.exp(_dot_nt(q, kp) * scale + bp - lse), 0.0)
                ds_c = p_c * (_dot_nt(do, vc) - dlt)
                ds_p = p_p * (_dot_nt(do, vp) - dlt)
                dsc_ref[0] += ds_c
                dsp_ref[0] += ds_p
                ds_cb = ds_c.astype(BF16)
                ds_pb = ds_p.astype(BF16)
                dqs[own, :] = (jnp.dot(ds_cb, kc, preferred_element_type=F32)
                               + jnp.dot(ds_pb, kp, preferred_element_type=F32)) * scale
                dks[cur, :] += _dot_tn(ds_cb, q) * scale
                dks[prv, :] += _dot_tn(ds_pb, q) * scale
                dvs[cur, :] += _dot_tn(p_c.astype(BF16), do)
                dvs[prv, :] += _dot_tn(p_p.astype(BF16), do)

        @pl.when(i > 0)
        def _():
            dks[rows:rows + halo, :] += carry_k[...]
            dvs[rows:rows + halo, :] += carry_v[...]

        dkv_ref[0] = dks[halo:, :].astype(BF16)
        dkv_ref[1] = dvs[halo:, :].astype(BF16)
        carry_k[...] = dks[0:halo, :]
        carry_v[...] = dvs[0:halo, :]
        dq_ref[...] = dqs[...].astype(BF16)

    per = rows // halo

    def rev(i):
        return nsb - 1 - i

    any_spec = pl.BlockSpec(memory_space=pl.ANY)
    bias_spec = pl.BlockSpec((1, BLOCK, BLOCK), lambda h, i: (h, 0, 0))
    row_spec = pl.BlockSpec((rows, HEAD_DIM), lambda h, i: (rev(i), h))
    in_specs = [pl.BlockSpec((rows, HEAD_DIM), lambda h, i: (rev(i), g * H + h)),
                row_spec, row_spec, row_spec,
                pl.BlockSpec((2, rows, HEAD_DIM), lambda h, i: (0, rev(i), g * H + h)),
                pl.BlockSpec((2, halo, HEAD_DIM),
                             lambda h, i: (0, jnp.maximum(rev(i) * per - 1, 0), g * H + h)),
                bias_spec, bias_spec, any_spec]
    args = [qz, do, lse, delta, kv, kv, bias_c, bias_p, dqz]
    aliases = {8: 0}
    if have_dkv:
        in_specs.append(any_spec)
        args.append(dkv)
        aliases[9] = 1
    blk = (halo + rows, HEAD_DIM)
    return pl.pallas_call(
        body, name=name, grid=(H, nsb),
        in_specs=in_specs,
        out_specs=[pl.BlockSpec((rows, HEAD_DIM), lambda h, i: (rev(i), g * H + h)),
                   pl.BlockSpec((2, rows, HEAD_DIM), lambda h, i: (0, rev(i), g * H + h)),
                   bias_spec, bias_spec],
        out_shape=[jax.ShapeDtypeStruct(qz.shape, BF16),
                   jax.ShapeDtypeStruct((2, S, N_GROUPS * D), BF16),
                   jax.ShapeDtypeStruct((H, BLOCK, BLOCK), F32),
                   jax.ShapeDtypeStruct((H, BLOCK, BLOCK), F32)],
        scratch_shapes=[pltpu.VMEM(blk, F32), pltpu.VMEM(blk, F32), pltpu.VMEM(blk, F32), pltpu.VMEM(blk, F32),
                        pltpu.VMEM((rows, HEAD_DIM), F32), pltpu.VMEM((rows, HEAD_DIM), F32),
                        pltpu.VMEM((halo, HEAD_DIM), F32), pltpu.VMEM((halo, HEAD_DIM), F32)],
        input_output_aliases=aliases,
        compiler_params=_cparams("parallel", "arbitrary"),
    )(*args)


def _place():
    x, y, c = lax.axis_index("x"), lax.axis_index("y"), lax.axis_index("c")
    return x, y, c


def all_gather(shards, name):
    n = len(shards)

    def body(*refs):
        ins, outs = refs[:n], refs[n:2 * n]
        send_sems, recv_sems, local_sems = refs[2 * n:]
        x, y, c = _place()
        me, sibling = (x, y, c), (x, y, 1 - c)
        chips = [(1 - x, y), (x, 1 - y), (1 - x, 1 - y)]

        def slot(a, dev):
            return outs[a].at[4 * dev[0] + 2 * dev[1] + dev[2]]

        def copy(a, k, block, to, src=None):
            return pltpu.make_async_remote_copy(
                src_ref=slot(a, block) if src is None else src, dst_ref=slot(a, block),
                send_sem=send_sems.at[a, k], recv_sem=recv_sems.at[a, k],
                device_id=to, device_id_type=MESH)

        mine, first, passed = [], [], []
        for a in range(n):
            mine.append(pltpu.make_async_copy(ins[a], slot(a, me), local_sems.at[a]))
            mine[a].start()
            first.append([copy(a, 0, me, sibling, src=ins[a])]
                         + [copy(a, 1 + j, me, (*chip, c), src=ins[a]) for j, chip in enumerate(chips)])
            for cp in first[a]:
                cp.start()
        for a in range(n):
            passed.append([copy(a, 4 + j, (*chip, c), sibling) for j, chip in enumerate(chips)])
            for j, chip in enumerate(chips):
                copy(a, 1 + j, (*chip, c), me).wait_recv()
                passed[a][j].start()
        for a in range(n):
            copy(a, 0, sibling, me).wait_recv()
            for j, chip in enumerate(chips):
                copy(a, 4 + j, (*chip, 1 - c), me).wait_recv()
            for cp in first[a] + passed[a]:
                cp.wait_send()
            mine[a].wait()

    any_spec = pl.BlockSpec(memory_space=pl.ANY)
    return pl.pallas_call(
        body, name=name,
        in_specs=[any_spec] * n, out_specs=[any_spec] * n,
        out_shape=[jax.ShapeDtypeStruct((N_DEV,) + s.shape, s.dtype) for s in shards],
        scratch_shapes=[pltpu.SemaphoreType.DMA((n, 7)), pltpu.SemaphoreType.DMA((n, 7)),
                        pltpu.SemaphoreType.DMA((n,))],
    )(*shards)


def exchange_grads(bigs, small, name):
    n = len(bigs)

    def body(*refs):
        ins, small_ref = refs[:n], refs[n]
        outs, small_out = refs[n + 1:2 * n + 1], refs[2 * n + 1]
        send_sems, recv_sems, local_sems = refs[2 * n + 2:]
        x, y, c = _place()
        me = 4 * x + 2 * y + c
        copies = []
        for a in range(n + 1):
            if a < n:
                local = pltpu.make_async_copy(ins[a].at[me], outs[a].at[0], local_sems.at[a])
            else:
                local = pltpu.make_async_copy(small_ref, small_out.at[me], local_sems.at[a])
            local.start()
            copies.append(local)
        for rel in range(1, N_DEV):
            px = 1 - x if rel & 4 else x
            py = 1 - y if rel & 2 else y
            pc = 1 - c if rel & 1 else c
            for a in range(n + 1):
                if a < n:
                    src, dst = ins[a].at[4 * px + 2 * py + pc], outs[a].at[rel]
                else:
                    src, dst = small_ref, small_out.at[me]
                cp = pltpu.make_async_remote_copy(
                    src_ref=src, dst_ref=dst, send_sem=send_sems.at[a, rel - 1],
                    recv_sem=recv_sems.at[a, rel - 1], device_id=(px, py, pc), device_id_type=MESH)
                cp.start()
                copies.append(cp)
        for cp in copies:
            cp.wait()

    any_spec = pl.BlockSpec(memory_space=pl.ANY)
    return pl.pallas_call(
        body, name=name,
        in_specs=[any_spec] * (n + 1), out_specs=[any_spec] * (n + 1),
        out_shape=[jax.ShapeDtypeStruct(b.shape, b.dtype) for b in bigs]
                  + [jax.ShapeDtypeStruct((N_DEV,) + small.shape, small.dtype)],
        scratch_shapes=[pltpu.SemaphoreType.DMA((n + 1, 7)), pltpu.SemaphoreType.DMA((n + 1, 7)),
                        pltpu.SemaphoreType.DMA((n + 1,))],
    )(*bigs, small)


def _adamw(w, g, m, v):
    m = ADAM_B1 * m + (1.0 - ADAM_B1) * g
    v = ADAM_B2 * v + (1.0 - ADAM_B2) * (g * g)
    m_hat = m / (1.0 - ADAM_B1 ** ADAM_STEP)
    v_hat = v / (1.0 - ADAM_B2 ** ADAM_STEP)
    delta = -ADAM_LR * (m_hat / (jnp.sqrt(v_hat) + ADAM_EPS) + ADAM_WD * w)
    return delta, m, v


def adamw_reduce(parts, w, m, v, name):
    K, n = w.shape
    tk = _tile(K, 256, 8)

    def body(p_ref, w_ref, m_ref, v_ref, g_ref, d_ref, nm_ref, nv_ref):
        g = p_ref[0].astype(F32)
        for r in range(1, N_DEV):
            g = g + p_ref[r].astype(F32)
        d, nm, nv = _adamw(w_ref[...], g, m_ref[...], v_ref[...])
        g_ref[...] = g
        d_ref[...] = d
        nm_ref[...] = nm
        nv_ref[...] = nv

    blk = pl.BlockSpec((tk, n), lambda i: (i, 0))
    return pl.pallas_call(
        body, name=name, grid=(K // tk,),
        in_specs=[pl.BlockSpec((N_DEV, tk, n), lambda i: (0, i, 0)), blk, blk, blk],
        out_specs=[blk] * 4, out_shape=[jax.ShapeDtypeStruct((K, n), F32)] * 4,
        compiler_params=_cparams("parallel"),
    )(parts, w, m, v)


def sum_parts(parts, name):
    _, R, D = parts.shape

    def body(p_ref, o_ref):
        g = p_ref[0]
        for r in range(1, N_DEV):
            g = g + p_ref[r]
        o_ref[...] = g

    return pl.pallas_call(body, name=name, out_shape=jax.ShapeDtypeStruct((R, D), F32),
                          compiler_params=_cparams())(parts)


def adamw_small(w, g, m, v, name):
    def body(w_ref, g_ref, m_ref, v_ref, d_ref, nm_ref, nv_ref):
        d, nm, nv = _adamw(w_ref[...], g_ref[...], m_ref[...], v_ref[...])
        d_ref[...] = d
        nm_ref[...] = nm
        nv_ref[...] = nv

    return pl.pallas_call(body, name=name, out_shape=[jax.ShapeDtypeStruct(w.shape, F32)] * 3,
                          compiler_params=_cparams())(w, g, m, v)


def local_step(x, target, wa_in, wa_out, w_kv, wb_in, wb_out, a_norm, cw, cb, lg, lb,
               kv_norm, b_norm, rel_bias, final_norm):
    S, D = x.shape
    H = D // HEAD_DIM

    (h0,) = rms_fwd(x, [a_norm], "rms_a")
    proj = matmul_nn(h0, wa_in, "a_in")
    c, y = conf_fwd(proj, cw, cb, lg, lb, "conf_fwd")
    x1 = matmul_nn(y, wa_out, "a_out", res=x)
    hk, hb = rms_fwd(x1, [kv_norm, b_norm], "rms_b")
    kv = matmul_nn(hk, w_kv, "kv_proj", out_dtype=BF16, kv_split=True)
    qz = matmul_nn(hb, wb_in, "b_in")

    bt = jnp.asarray(bucket_tables())
    onehot = (bt[None, :] == jnp.arange(N_BUCKETS, dtype=jnp.int32)[:, None]).astype(F32)
    bias = small_dot(rel_bias.T, onehot, "nn", "bias_table").reshape(H, 2 * N_GROUPS, BLOCK, BLOCK)
    bias_c = [bias[:, 2 * g] for g in range(N_GROUPS)]
    bias_p = [bias[:, 2 * g + 1] for g in range(N_GROUPS)]

    os_, lses = [], []
    for g, (_, dil) in enumerate(GROUPS):
        o_g, l_g = attn_fwd(qz, kv, bias_c[g], bias_p[g], g, dil, "attn_fwd%d" % g)
        os_.append(o_g)
        lses.append(l_g)
    o, lse, y2 = merge_fwd(os_, lses, qz, "merge_fwd")
    x2 = matmul_nn(y2, wb_out, "b_out", res=x1)
    dx2, dx2b, fin_acc = final_loss(x2, final_norm, target, "final_loss")

    dy2 = matmul_nt(dx2b, wb_out, "b_out_dx")
    dwb_out = matmul_tn(y2, dx2b, wb_out.shape[0], "b_out_dw")
    do, delta, dqz = merge_bwd(dy2, o, qz, "merge_bwd")
    dkv = None
    ds_tabs = []
    for g, (_, dil) in enumerate(GROUPS):
        dqz, dkv, ds_c, ds_p = attn_bwd(qz, kv, do, lse, delta, bias_c[g], bias_p[g], dqz, dkv, g, dil,
                                        "attn_bwd%d" % g)
        ds_tabs += [ds_c.reshape(H, BLOCK * BLOCK), ds_p.reshape(H, BLOCK * BLOCK)]
    d_rel = small_dot(onehot, jnp.concatenate(ds_tabs, axis=1), "nt", "bias_grad")
    dhb = matmul_nt(dqz, wb_in, "b_in_dx")
    dwb_in = matmul_tn(hb, dqz, wb_in.shape[0], "b_in_dw")
    dhk = matmul_nt(dkv, w_kv, "kv_dx", kv_split=True)
    dw_kv = matmul_tn(hk, dkv, w_kv.shape[0], "kv_dw", kv_split=True)
    dx1, dx1b, norm_acc = rms_bwd(x1, [dhk, dhb], [kv_norm, b_norm], dx2, "rms_b_bwd", True)

    dy = matmul_nt(dx1b, wa_out, "a_out_dx")
    dwa_out = matmul_tn(y, dx1b, wa_out.shape[0], "a_out_dw")
    dc, dz, ln_acc = conf_bwd_ln(c, dy, proj, lg, lb, "conf_bwd_ln")
    dproj, dcw = conf_bwd_conv(proj, dc, dz, cw, "conf_bwd_conv")
    dh0 = matmul_nt(dproj, wa_in, "a_in_dx")
    dwa_in = matmul_tn(h0, dproj, wa_in.shape[0], "a_in_dw")
    grad_x, a_acc = rms_bwd(x, [dh0], [a_norm], dx1, "rms_a_bwd", False)

    return grad_x, (dwa_in, dwa_out, dw_kv, dwb_in, dwb_out), (a_acc, ln_acc, dcw, norm_acc, fin_acc, d_rel)


def _row(v, at):
    return jnp.pad(v.reshape(1, -1), ((at, 7 - at), (0, 0)))


def _pack_sharded(norm, conv_w, conv_b, ln_g, ln_b):
    n = norm.shape[-1]
    taps = jnp.pad(conv_w.reshape(CONV_WIDTH, n), ((0, HALO - CONV_WIDTH), (0, 0)))
    return jnp.concatenate([_row(norm, 0), _row(ln_g, 0) + _row(ln_b, 1) + _row(conv_b, 2), taps], axis=0)


def _pack_rel(rel_bias, D):
    return jnp.pad(rel_bias.reshape(1, -1), ((0, 7), (0, D - rel_bias.size)))


def _pack_replicated(kv_norm, b_norm, final_norm, rel_bias, D):
    return jnp.concatenate([_row(kv_norm, 0) + _row(b_norm, 1), _row(final_norm, 0), _pack_rel(rel_bias, D)], axis=0)


def kernel(x, a_norm, a_w_in, a_conv_w, a_conv_b, a_ln_g, a_ln_b, a_w_out, kv_norm, w_kv, b_norm, b_w_in, b_w_out, rel_bias, final_norm, loss_target, m_a_norm, m_a_w_in, m_a_conv_w, m_a_conv_b, m_a_ln_g, m_a_ln_b, m_a_w_out, m_kv_norm, m_w_kv, m_b_norm, m_b_w_in, m_b_w_out, m_rel_bias, m_final_norm, v_a_norm, v_a_w_in, v_a_conv_w, v_a_conv_b, v_a_ln_g, v_a_ln_b, v_a_w_out, v_kv_norm, v_w_kv, v_b_norm, v_b_w_in, v_b_w_out, v_rel_bias, v_final_norm):
    _, S, D = x.shape
    E = D
    nsh = D // N_DEV
    xs = x.reshape(S, D)
    tgt = loss_target.reshape(S, D)

    big_w = [a_w_in[0], a_w_out[0], w_kv, b_w_in[0], b_w_out[0]]
    big_m = [m_a_w_in[0], m_a_w_out[0], m_w_kv, m_b_w_in[0], m_b_w_out[0]]
    big_v = [v_a_w_in[0], v_a_w_out[0], v_w_kv, v_b_w_in[0], v_b_w_out[0]]
    sh_w = _pack_sharded(a_norm, a_conv_w, a_conv_b, a_ln_g, a_ln_b)
    gathered = all_gather([w.astype(BF16) for w in big_w] + [sh_w], "gather_weights")
    wa_in, wa_out, wkv, wb_in, wb_out, sh_all = gathered
    wa_out = wa_out.reshape(1, E, D)
    wb_out = wb_out.reshape(1, D, D)
    sh_full = sh_all.transpose(1, 0, 2).reshape(SMALL_SH_ROWS, D)

    grad_x, dws, small = local_step(
        xs, tgt, wa_in, wa_out, wkv, wb_in, wb_out, sh_full[0:1], sh_full[16:16 + HALO], sh_full[10:11],
        sh_full[8:9], sh_full[9:10], kv_norm.reshape(1, D), b_norm.reshape(1, D), rel_bias,
        final_norm.reshape(1, D))

    dwa_in, dwa_out, dw_kv, dwb_in, dwb_out = dws
    dwa_out = dwa_out.reshape(N_DEV, E // N_DEV, D)
    dwb_out = dwb_out.reshape(N_DEV, D // N_DEV, D)
    a_acc, ln_acc, dcw, norm_acc, fin_acc, d_rel = small
    small_rows = jnp.concatenate([a_acc, ln_acc, dcw, norm_acc, fin_acc, _pack_rel(d_rel, D)], axis=0)
    *parts, small_parts = exchange_grads([dwa_in, dwa_out, dw_kv, dwb_in, dwb_out], small_rows, "exchange_grads")

    names = ["a_w_in", "a_w_out", "w_kv", "b_w_in", "b_w_out"]
    big_out = {}
    for nm, p, w, m, v in zip(names, parts, big_w, big_m, big_v):
        big_out[nm] = adamw_reduce(p, w, m, v, "adamw_" + nm)

    gsum = sum_parts(small_parts, "sum_small")
    me = 4 * lax.axis_index("x") + 2 * lax.axis_index("y") + lax.axis_index("c")
    g_sh = lax.dynamic_slice(gsum, (0, me * nsh), (SMALL_SH_ROWS, nsh))
    g_rep = gsum[SMALL_SH_ROWS:]
    loss = g_rep[9, 0]
    sh_m = _pack_sharded(m_a_norm, m_a_conv_w, m_a_conv_b, m_a_ln_g, m_a_ln_b)
    sh_v = _pack_sharded(v_a_norm, v_a_conv_w, v_a_conv_b, v_a_ln_g, v_a_ln_b)
    sh_d, sh_nm, sh_nv = adamw_small(sh_w, g_sh, sh_m, sh_v, "adamw_sharded")
    rep_w = _pack_replicated(kv_norm, b_norm, final_norm, rel_bias, D)
    rep_m = _pack_replicated(m_kv_norm, m_b_norm, m_final_norm, m_rel_bias, D)
    rep_v = _pack_replicated(v_kv_norm, v_b_norm, v_final_norm, v_rel_bias, D)
    rep_d, rep_nm, rep_nv = adamw_small(rep_w, g_rep, rep_m, rep_v, "adamw_replicated")

    def unpack(kind):
        sh = (g_sh, sh_d, sh_nm, sh_nv)[kind]
        rep = (g_rep, rep_d, rep_nm, rep_nv)[kind]
        big = {nm: big_out[nm][kind] for nm in names}
        nb = rel_bias.size
        return [
            sh[0:1],
            big["a_w_in"][None],
            sh[16:16 + CONV_WIDTH][None],
            sh[10:11], sh[8:9], sh[9:10],
            big["a_w_out"][None],
            rep[0],
            big["w_kv"],
            rep[1:2],
            big["b_w_in"][None],
            big["b_w_out"][None],
            rep[16, :nb].reshape(rel_bias.shape),
            rep[8],
        ]

    return (loss, grad_x.reshape(1, S, D), *unpack(0), *unpack(1), *unpack(2), *unpack(3))
```

```python
import functools
import math

import numpy as np
import jax
import jax.numpy as jnp
from jax import lax
from jax.experimental import pallas as pl
from jax.experimental.pallas import tpu as pltpu

F32 = jnp.float32
BF16 = jnp.bfloat16

N_DEV = 8
N_CHIPS = 4
EPS = 1e-6
HEAD_DIM = 128
BLOCK = 128
GROUPS = ((128, 1), (512, 4), (2048, 16))
N_GROUPS = len(GROUPS)
CONV_WIDTH = 31
HALO = 32
N_BUCKETS = 32
MAX_EXACT = N_BUCKETS // 2
MAX_DISTANCE = 2048
V7X_VMEM_BYTES = 64 * 1024 * 1024
VMEM_LIMIT = (V7X_VMEM_BYTES * 7) // 8
LANE = 128

ADAM_LR = 0.001
ADAM_B1 = 0.9
ADAM_B2 = 0.999
ADAM_EPS = 1e-08
ADAM_WD = 0.01
ADAM_STEP = 10

SMALL_SH_ROWS = 48
SMALL_REP_ROWS = 24
MESH = pl.DeviceIdType.MESH
ANY_SPEC = pl.BlockSpec(memory_space=pl.ANY)
VMEM_SPEC = pl.BlockSpec(memory_space=pltpu.VMEM)


def _tile(dim, pref, unit=LANE):
    if dim <= pref:
        return dim
    t = (pref // unit) * unit
    while dim % t:
        t -= unit
    assert t > 0
    return t


def _sigmoid(v):
    return jax.nn.sigmoid(v)


def _place():
    return lax.axis_index("x"), lax.axis_index("y"), lax.axis_index("c")


def _flip(v, bit):
    return 1 - v if bit else v


class Job:
    def __init__(self, srcs, dsts, n_sems, build):
        self.srcs, self.dsts, self.n_sems, self.build = list(srcs), list(dsts), n_sems, build
        self.results = None


def _remote(src, dst, send_sems, recv_sems, k, peer):
    return pltpu.make_async_remote_copy(src_ref=src, dst_ref=dst, send_sem=send_sems.at[k],
                                        recv_sem=recv_sems.at[k], device_id=peer, device_id_type=MESH)


def _call(body, *, name, in_specs, out_specs, out_shape, args, grid=(), scratch_shapes=(), sem=(),
          aliases=None, jobs=()):
    n_in, n_out, n_scr = len(in_specs), len(out_specs), len(scratch_shapes)
    aliases = dict(aliases or {})
    x_in, x_out, x_scr = [], [], []
    for job in jobs:
        job.in_at = n_in + len(x_in)
        x_in += job.srcs
        job.out_at = n_out + len(x_out)
        for d in job.dsts:
            if not isinstance(d, jax.ShapeDtypeStruct):
                aliases[n_in + len(x_in)] = n_out + len(x_out)
                x_in.append(d)
            x_out.append(jax.ShapeDtypeStruct(d.shape, d.dtype))
        job.scr_at = n_scr + len(x_scr)
        x_scr += [pltpu.SemaphoreType.DMA((job.n_sems,))] * 3

    def wrapped(*refs):
        ins = refs[:n_in + len(x_in)]
        outs = refs[len(ins):len(ins) + n_out + len(x_out)]
        scr = refs[len(ins) + len(outs):]
        core = ins[:n_in] + outs[:n_out] + scr[:n_scr]
        if not jobs:
            body(*core)
            return
        copies = []
        for job in jobs:
            copies += job.build(ins[job.in_at:job.in_at + len(job.srcs)],
                                outs[job.out_at:job.out_at + len(job.dsts)],
                                *scr[job.scr_at:job.scr_at + 3])
        if grid:
            pids = [pl.program_id(d) for d in range(len(grid))]
            first = functools.reduce(jnp.logical_and, [p == 0 for p in pids])
            last = functools.reduce(jnp.logical_and, [p == g - 1 for p, g in zip(pids, grid)])

            @pl.when(first)
            def _():
                for cp in copies:
                    cp.start()

            body(*core)

            @pl.when(last)
            def _():
                for cp in copies:
                    cp.wait()
        else:
            for cp in copies:
                cp.start()
            body(*core)
            for cp in copies:
                cp.wait()

    if jobs:
        sem = ("arbitrary",) * len(grid)
    kwargs = dict(grid=grid) if grid else {}
    if aliases:
        kwargs["input_output_aliases"] = aliases
    outs = pl.pallas_call(
        wrapped, name=name,
        in_specs=list(in_specs) + [ANY_SPEC] * len(x_in),
        out_specs=list(out_specs) + [ANY_SPEC] * len(x_out),
        out_shape=list(out_shape) + x_out,
        scratch_shapes=list(scratch_shapes) + x_scr,
        compiler_params=pltpu.CompilerParams(dimension_semantics=sem if sem else None,
                                             vmem_limit_bytes=VMEM_LIMIT),
        **kwargs,
    )(*args, *x_in)
    for job in jobs:
        job.results = list(outs[job.out_at:job.out_at + len(job.dsts)])
    return list(outs[:n_out])


def gather_chips_job(shard):
    def build(srcs, dsts, send, recv, loc):
        (src,), (out,) = srcs, dsts
        x, y, c = _place()
        mine = out.at[4 * x + 2 * y + c]
        peers = [(x, y, 1 - c), (1 - x, y, c), (x, 1 - y, c), (1 - x, 1 - y, c)]
        return ([pltpu.make_async_copy(src, mine, loc.at[0])]
                + [_remote(src, mine, send, recv, k, p) for k, p in enumerate(peers)])

    return Job([shard], [jax.ShapeDtypeStruct((N_DEV,) + shard.shape, shard.dtype)], 4, build)


def gather_sibling_job(buf):
    def build(srcs, dsts, send, recv, loc):
        (out,) = dsts
        x, y, c = _place()
        copies = []
        for k, (cx, cy) in enumerate([(1 - x, y), (x, 1 - y), (1 - x, 1 - y)]):
            blk = out.at[4 * cx + 2 * cy + c]
            copies.append(_remote(blk, blk, send, recv, k, (x, y, 1 - c)))
        return copies

    return Job([], [buf], 3, build)


def reduce_sibling_job(dw):
    blk = jax.ShapeDtypeStruct((N_CHIPS,) + dw.shape[1:], dw.dtype)

    def build(srcs, dsts, send, recv, loc):
        (src,), (own, got) = srcs, dsts
        x, y, c = _place()
        copies = []
        for r in range(N_CHIPS):
            base = 4 * _flip(x, r & 2) + 2 * _flip(y, r & 1)
            copies.append(pltpu.make_async_copy(src.at[base + c], own.at[r], loc.at[r]))
            copies.append(_remote(src.at[base + 1 - c], got.at[r], send, recv, r, (x, y, 1 - c)))
        return copies

    return Job([dw], [blk, blk], N_CHIPS, build)


def reduce_chips_job(part):
    def build(srcs, dsts, send, recv, loc):
        (src,), (got,) = srcs, dsts
        x, y, c = _place()
        return [_remote(src.at[r], got.at[r - 1], send, recv, r - 1, (_flip(x, r & 2), _flip(y, r & 1), c))
                for r in range(1, N_CHIPS)]

    return Job([part], [jax.ShapeDtypeStruct((N_CHIPS - 1,) + part.shape[1:], part.dtype)], N_CHIPS - 1, build)


def share_small_job(small):
    def build(srcs, dsts, send, recv, loc):
        (src,), (out,) = srcs, dsts
        x, y, c = _place()
        mine = out.at[4 * x + 2 * y + c]
        copies = [pltpu.make_async_copy(src, mine, loc.at[0])]
        for rel in range(1, N_DEV):
            peer = (_flip(x, rel & 4), _flip(y, rel & 2), _flip(c, rel & 1))
            copies.append(_remote(src, mine, send, recv, rel - 1, peer))
        return copies

    return Job([small], [jax.ShapeDtypeStruct((N_DEV,) + small.shape, small.dtype)], N_DEV - 1, build)


def all_gather(shards, name):
    n = len(shards)

    def body(*refs):
        ins, outs = refs[:n], refs[n:2 * n]
        send_sems, recv_sems, local_sems = refs[2 * n:]
        x, y, c = _place()
        me, sibling = (x, y, c), (x, y, 1 - c)
        chips = [(1 - x, y), (x, 1 - y), (1 - x, 1 - y)]

        def slot(a, dev):
            return outs[a].at[4 * dev[0] + 2 * dev[1] + dev[2]]

        def copy(a, k, block, to, src=None):
            return pltpu.make_async_remote_copy(
                src_ref=slot(a, block) if src is None else src, dst_ref=slot(a, block),
                send_sem=send_sems.at[a, k], recv_sem=recv_sems.at[a, k],
                device_id=to, device_id_type=MESH)

        mine, first, passed = [], [], []
        for a in range(n):
            mine.append(pltpu.make_async_copy(ins[a], slot(a, me), local_sems.at[a]))
            mine[a].start()
            first.append([copy(a, 0, me, sibling, src=ins[a])]
                         + [copy(a, 1 + j, me, (*chip, c), src=ins[a]) for j, chip in enumerate(chips)])
            for cp in first[a]:
                cp.start()
        for a in range(n):
            passed.append([copy(a, 4 + j, (*chip, c), sibling) for j, chip in enumerate(chips)])
            for j, chip in enumerate(chips):
                copy(a, 1 + j, (*chip, c), me).wait_recv()
                passed[a][j].start()
        for a in range(n):
            copy(a, 0, sibling, me).wait_recv()
            for j, chip in enumerate(chips):
                copy(a, 4 + j, (*chip, 1 - c), me).wait_recv()
            for cp in first[a] + passed[a]:
                cp.wait_send()
            mine[a].wait()

    return pl.pallas_call(
        body, name=name,
        in_specs=[ANY_SPEC] * n, out_specs=[ANY_SPEC] * n,
        out_shape=[jax.ShapeDtypeStruct((N_DEV,) + s.shape, s.dtype) for s in shards],
        scratch_shapes=[pltpu.SemaphoreType.DMA((n, 7)), pltpu.SemaphoreType.DMA((n, 7)),
                        pltpu.SemaphoreType.DMA((n,))],
    )(*shards)


def _kv_split_index(tw, D):
    pd = D // tw

    def index(j):
        return (j // pd) % 2, (j // (2 * pd)) * pd + j % pd

    return index


def _col_tile(n, also, pref):
    t = (min(pref, n) // LANE) * LANE
    while n % t or (also is not None and also % t):
        t -= LANE
    assert t > 0
    return t


def matmul_nn(a, w, name, res=None, out_dtype=F32, kv_split=False, jobs=()):
    M, K = a.shape
    nb, _, n = w.shape
    D = nb * n // (2 * N_GROUPS)
    tn = _col_tile(n, D if kv_split else None, 1024)
    tm = _tile(M, 2048 if tn <= 512 else 1024)
    per = n // tn

    def body(*refs):
        if res is None:
            a_ref, w_ref, o_ref = refs
        else:
            a_ref, w_ref, r_ref, o_ref = refs
        acc = jnp.dot(a_ref[...], w_ref[...], preferred_element_type=F32)
        if res is not None:
            acc = r_ref[...] + acc
        o_ref[...] = acc.astype(o_ref.dtype)

    in_specs = [pl.BlockSpec((tm, K), lambda i, j: (i, 0)),
                pl.BlockSpec((None, K, tn), lambda i, j: (j // per, 0, j % per))]
    args = [a, w]
    if res is not None:
        in_specs.append(pl.BlockSpec((tm, tn), lambda i, j: (i, j)))
        args.append(res)
    if kv_split:
        split = _kv_split_index(tn, D)
        out_spec = pl.BlockSpec((None, tm, tn), lambda i, j: (split(j)[0], i, split(j)[1]))
        out_shape = jax.ShapeDtypeStruct((2, M, N_GROUPS * D), out_dtype)
    else:
        out_spec = pl.BlockSpec((tm, tn), lambda i, j: (i, j))
        out_shape = jax.ShapeDtypeStruct((M, nb * n), out_dtype)
    return _call(body, name=name, grid=(M // tm, nb * per), in_specs=in_specs, out_specs=[out_spec],
                 out_shape=[out_shape], args=args, sem=("parallel", "parallel"), jobs=jobs)[0]


def matmul_nt(dy, w, name, kv_split=False, jobs=()):
    M = dy.shape[-2]
    nb, K, n = w.shape
    D = nb * n // (2 * N_GROUPS)
    tm = _tile(M, 1024)
    tc = _col_tile(n, D if kv_split else None, 1024)
    per = n // tc

    def body(dy_ref, w_ref, o_ref):
        j = pl.program_id(1)
        part = lax.dot_general(dy_ref[...], w_ref[...], (((1,), (1,)), ((), ())),
                               preferred_element_type=F32)

        @pl.when(j == 0)
        def _():
            o_ref[...] = part

        @pl.when(j > 0)
        def _():
            o_ref[...] += part

    if kv_split:
        split = _kv_split_index(tc, D)
        dy_spec = pl.BlockSpec((None, tm, tc), lambda i, j: (split(j)[0], i, split(j)[1]))
    else:
        dy_spec = pl.BlockSpec((tm, tc), lambda i, j: (i, j))
    return _call(body, name=name, grid=(M // tm, nb * per),
                 in_specs=[dy_spec, pl.BlockSpec((None, K, tc), lambda i, j: (j // per, 0, j % per))],
                 out_specs=[pl.BlockSpec((tm, K), lambda i, j: (i, 0))],
                 out_shape=[jax.ShapeDtypeStruct((M, K), F32)], args=[dy, w],
                 sem=("parallel", "arbitrary"), jobs=jobs)[0]


def matmul_tn(a, dy, nb, name, out_dtype=BF16, kv_split=False, jobs=()):
    M, K = a.shape
    N = 2 * dy.shape[-1] if kv_split else dy.shape[-1]
    n = N // nb
    D = N // (2 * N_GROUPS)
    tn = _col_tile(n, D if kv_split else None, 1024)
    per = n // tn
    tk = _tile(K, 1024)
    tmc = _tile(M, 2048)
    steps = M // tmc

    def body(a_ref, dy_ref, o_ref, acc_ref):
        s = pl.program_id(2)
        part = lax.dot_general(a_ref[...], dy_ref[...], (((0,), (0,)), ((), ())),
                               preferred_element_type=F32)

        @pl.when(s == 0)
        def _():
            acc_ref[...] = part

        @pl.when(s > 0)
        def _():
            acc_ref[...] += part

        @pl.when(s == steps - 1)
        def _():
            o_ref[...] = acc_ref[...].astype(o_ref.dtype)

    if kv_split:
        split = _kv_split_index(tn, D)
        dy_spec = pl.BlockSpec((None, tmc, tn), lambda j, k, s: (split(j)[0], s, split(j)[1]))
    else:
        dy_spec = pl.BlockSpec((tmc, tn), lambda j, k, s: (s, j))
    return _call(body, name=name, grid=(nb * per, K // tk, steps),
                 in_specs=[pl.BlockSpec((tmc, tk), lambda j, k, s: (s, k)), dy_spec],
                 out_specs=[pl.BlockSpec((None, tk, tn), lambda j, k, s: (j // per, k, j % per))],
                 out_shape=[jax.ShapeDtypeStruct((nb, K, n), out_dtype)], args=[a, dy],
                 scratch_shapes=[pltpu.VMEM((tk, tn), F32)],
                 sem=("parallel", "parallel", "arbitrary"), jobs=jobs)[0]


def small_dot(a, b, contract, name, jobs=()):
    if contract == "nn":
        dims = (((1,), (0,)), ((), ()))
        out = (a.shape[0], b.shape[1])
    else:
        dims = (((1,), (1,)), ((), ()))
        out = (a.shape[0], b.shape[0])

    def body(a_ref, b_ref, o_ref):
        o_ref[...] = lax.dot_general(a_ref[...], b_ref[...], dims, precision=lax.Precision.HIGHEST,
                                     preferred_element_type=F32)

    return _call(body, name=name, in_specs=[VMEM_SPEC, VMEM_SPEC], out_specs=[VMEM_SPEC],
                 out_shape=[jax.ShapeDtypeStruct(out, F32)], args=[a, b], jobs=jobs)[0]


def rms_fwd(x, gains, name, jobs=()):
    S, D = x.shape
    T = _tile(S, 512, 8)
    n = len(gains)

    def body(x_ref, *refs):
        xv = x_ref[...]
        xn = xv * lax.rsqrt(jnp.mean(xv * xv, axis=-1, keepdims=True) + EPS)
        for g_ref, o_ref in zip(refs[:n], refs[n:]):
            o_ref[...] = (xn * g_ref[...]).astype(o_ref.dtype)

    row = pl.BlockSpec((T, D), lambda i: (i, 0))
    vec = pl.BlockSpec((1, D), lambda i: (0, 0))
    return _call(body, name=name, grid=(S // T,), in_specs=[row] + [vec] * n, out_specs=[row] * n,
                 out_shape=[jax.ShapeDtypeStruct((S, D), BF16)] * n, args=[x, *gains],
                 sem=("parallel",), jobs=jobs)


def rms_bwd(x, dhs, gains, dres, name, want_bf16, jobs=()):
    S, D = x.shape
    T = _tile(S, 256, 8)
    n = len(gains)

    def body(x_ref, *refs):
        dh_refs = refs[:n]
        g_refs = refs[n:2 * n]
        dres_ref = refs[2 * n]
        outs = refs[2 * n + 1:]
        dx_ref, dg_ref = outs[0], outs[-1]
        i = pl.program_id(0)

        @pl.when(i == 0)
        def _():
            dg_ref[...] = jnp.zeros_like(dg_ref)

        xv = x_ref[...]
        r = lax.rsqrt(jnp.mean(xv * xv, axis=-1, keepdims=True) + EPS)
        xn = xv * r
        dxn = jnp.zeros_like(xv)
        for k in range(n):
            dh = dh_refs[k][...]
            dg_ref[k:k + 1, :] += jnp.sum(dh * xn, axis=0, keepdims=True)
            dxn = dxn + dh * g_refs[k][...]
        dx = dres_ref[...] + r * (dxn - xn * jnp.mean(dxn * xn, axis=-1, keepdims=True))
        dx_ref[...] = dx
        if want_bf16:
            outs[1][...] = dx.astype(BF16)

    row = pl.BlockSpec((T, D), lambda i: (i, 0))
    vec = pl.BlockSpec((1, D), lambda i: (0, 0))
    acc = pl.BlockSpec((8, D), lambda i: (0, 0))
    out_specs = [row] + ([row] if want_bf16 else []) + [acc]
    out_shape = ([jax.ShapeDtypeStruct((S, D), F32)]
                 + ([jax.ShapeDtypeStruct((S, D), BF16)] if want_bf16 else [])
                 + [jax.ShapeDtypeStruct((8, D), F32)])
    return _call(body, name=name, grid=(S // T,), in_specs=[row] + [row] * n + [vec] * n + [row],
                 out_specs=out_specs, out_shape=out_shape, args=[x, *dhs, *gains, dres],
                 sem=("arbitrary",), jobs=jobs)


def final_loss(x2, gain, target, name, jobs=()):
    S, D = x2.shape
    T = _tile(S, 256, 8)

    def body(x_ref, g_ref, t_ref, dx_ref, dxb_ref, acc_ref):
        i = pl.program_id(0)

        @pl.when(i == 0)
        def _():
            acc_ref[...] = jnp.zeros_like(acc_ref)

        xv = x_ref[...]
        g = g_ref[...]
        r = lax.rsqrt(jnp.mean(xv * xv, axis=-1, keepdims=True) + EPS)
        xn = xv * r
        err = xn * g - t_ref[...]
        dy = err * (1.0 / D)
        acc_ref[0:1, :] += jnp.sum(dy * xn, axis=0, keepdims=True)
        acc_ref[1:2, :] += jnp.full((1, D), 0.5 / D, F32) * jnp.sum(err * err)
        dxn = dy * g
        dx = r * (dxn - xn * jnp.mean(dxn * xn, axis=-1, keepdims=True))
        dx_ref[...] = dx
        dxb_ref[...] = dx.astype(BF16)

    row = pl.BlockSpec((T, D), lambda i: (i, 0))
    return _call(body, name=name, grid=(S // T,),
                 in_specs=[row, pl.BlockSpec((1, D), lambda i: (0, 0)), row],
                 out_specs=[row, row, pl.BlockSpec((8, D), lambda i: (0, 0))],
                 out_shape=[jax.ShapeDtypeStruct((S, D), F32), jax.ShapeDtypeStruct((S, D), BF16),
                            jax.ShapeDtypeStruct((8, D), F32)],
                 args=[x2, gain, target], sem=("arbitrary",), jobs=jobs)


ROW_CHUNK = 64
LANE_CHUNK = 512


def conf_fwd(proj, cw, cb, lg, lb, name, jobs=()):
    S, E3 = proj.shape
    E = E3 // 3
    T = _tile(S, 256, HALO)
    R = T // HALO
    lc = _tile(E, LANE_CHUNK)
    rc = min(ROW_CHUNK, T)

    def body(a_ref, b_ref, z_ref, ap_ref, bp_ref, cw_ref, cb_ref, lg_ref, lb_ref, c_ref, y_ref, u_scr):
        i = pl.program_id(0)
        up = ap_ref[...] * _sigmoid(bp_ref[...])
        u_scr[0:HALO, :] = jnp.where(i > 0, up, 0.0)
        u_scr[HALO:HALO + T, :] = a_ref[...] * _sigmoid(b_ref[...])
        off = HALO - (CONV_WIDTH - 1)
        for r0 in range(0, T, rc):
            for l0 in range(0, E, lc):
                acc = jnp.broadcast_to(cb_ref[:, l0:l0 + lc], (rc, lc))
                for k in range(CONV_WIDTH):
                    acc = acc + u_scr[r0 + off + k:r0 + off + k + rc, l0:l0 + lc] * cw_ref[k:k + 1, l0:l0 + lc]
                c_ref[r0:r0 + rc, l0:l0 + lc] = acc
        c = c_ref[...]
        mu = jnp.mean(c, axis=-1, keepdims=True)
        d = c - mu
        var = jnp.mean(d * d, axis=-1, keepdims=True)
        cn = d * lax.rsqrt(var + EPS) * lg_ref[...] + lb_ref[...]
        z = z_ref[...]
        y_ref[...] = ((cn * _sigmoid(cn)) * (z * _sigmoid(z))).astype(BF16)

    def col(j):
        return pl.BlockSpec((T, E), lambda i: (i, j))

    def prev(j):
        return pl.BlockSpec((HALO, E), lambda i: (jnp.maximum(i * R - 1, 0), j))

    vec = pl.BlockSpec((1, E), lambda i: (0, 0))
    return _call(body, name=name, grid=(S // T,),
                 in_specs=[col(0), col(1), col(2), prev(0), prev(1),
                           pl.BlockSpec((HALO, E), lambda i: (0, 0)), vec, vec, vec],
                 out_specs=[pl.BlockSpec((T, E), lambda i: (i, 0))] * 2,
                 out_shape=[jax.ShapeDtypeStruct((S, E), F32), jax.ShapeDtypeStruct((S, E), BF16)],
                 scratch_shapes=[pltpu.VMEM((HALO + T, E), F32)],
                 args=[proj, proj, proj, proj, proj, cw, cb, lg, lb], sem=("parallel",), jobs=jobs)


def conf_bwd_ln(c, dy, proj, lg, lb, name, jobs=()):
    S, E = c.shape
    T = _tile(S, 256, 8)

    def body(c_ref, dy_ref, z_ref, lg_ref, lb_ref, dc_ref, dz_ref, acc_ref):
        i = pl.program_id(0)

        @pl.when(i == 0)
        def _():
            acc_ref[...] = jnp.zeros_like(acc_ref)

        cv = c_ref[...]
        mu = jnp.mean(cv, axis=-1, keepdims=True)
        d = cv - mu
        var = jnp.mean(d * d, axis=-1, keepdims=True)
        rstd = lax.rsqrt(var + EPS)
        xh = d * rstd
        lgv = lg_ref[...]
        cn = xh * lgv + lb_ref[...]
        z = z_ref[...]
        dy = dy_ref[...]
        sc = _sigmoid(cn)
        sz = _sigmoid(z)
        dcn = dy * (z * sz) * (sc * (1.0 + cn * (1.0 - sc)))
        dz_ref[...] = (dy * (cn * sc) * (sz * (1.0 + z * (1.0 - sz)))).astype(BF16)
        acc_ref[0:1, :] += jnp.sum(dcn * xh, axis=0, keepdims=True)
        acc_ref[1:2, :] += jnp.sum(dcn, axis=0, keepdims=True)
        dxh = dcn * lgv
        dc = rstd * (dxh - jnp.mean(dxh, axis=-1, keepdims=True)
                     - xh * jnp.mean(dxh * xh, axis=-1, keepdims=True))
        acc_ref[2:3, :] += jnp.sum(dc, axis=0, keepdims=True)
        dc_ref[...] = dc

    row = pl.BlockSpec((T, E), lambda i: (i, 0))
    vec = pl.BlockSpec((1, E), lambda i: (0, 0))
    return _call(body, name=name, grid=(S // T,),
                 in_specs=[row, row, pl.BlockSpec((T, E), lambda i: (i, 2)), vec, vec],
                 out_specs=[row, row, pl.BlockSpec((8, E), lambda i: (0, 0))],
                 out_shape=[jax.ShapeDtypeStruct((S, E), F32), jax.ShapeDtypeStruct((S, E), BF16),
                            jax.ShapeDtypeStruct((8, E), F32)],
                 args=[c, dy, proj, lg, lb], sem=("arbitrary",), jobs=jobs)


def conf_bwd_conv(proj, dc, dz, cw, name, jobs=()):
    S, E3 = proj.shape
    E = E3 // 3
    T = _tile(S, 256, HALO)
    R = T // HALO
    nt = S // T
    lc = _tile(E, LANE_CHUNK)
    rc = min(ROW_CHUNK, T)

    def body(a_ref, b_ref, ap_ref, bp_ref, dc_ref, dcn_ref, dz_ref, cw_ref, o_ref, dw_ref, u_scr, dc_scr):
        i = pl.program_id(0)

        @pl.when(i == 0)
        def _():
            dw_ref[...] = jnp.zeros_like(dw_ref)

        a = a_ref[...]
        sb = _sigmoid(b_ref[...])
        up = ap_ref[...] * _sigmoid(bp_ref[...])
        u_scr[0:HALO, :] = jnp.where(i > 0, up, 0.0)
        u_scr[HALO:HALO + T, :] = a * sb
        dc_scr[0:T, :] = dc_ref[...]
        dc_scr[T:T + HALO, :] = jnp.where(i < nt - 1, dcn_ref[...], 0.0)
        off = HALO - (CONV_WIDTH - 1)
        for l0 in range(0, E, lc):
            for k in range(CONV_WIDTH):
                prod = u_scr[off + k:off + k + T, l0:l0 + lc] * dc_scr[0:T, l0:l0 + lc]
                dw_ref[k:k + 1, l0:l0 + lc] += jnp.sum(prod, axis=0, keepdims=True)
            for r0 in range(0, T, rc):
                acc = jnp.zeros((rc, lc), F32)
                for k in range(CONV_WIDTH):
                    s0 = r0 + (CONV_WIDTH - 1) - k
                    acc = acc + dc_scr[s0:s0 + rc, l0:l0 + lc] * cw_ref[k:k + 1, l0:l0 + lc]
                av = a[r0:r0 + rc, l0:l0 + lc]
                sv = sb[r0:r0 + rc, l0:l0 + lc]
                o_ref[r0:r0 + rc, l0:l0 + lc] = (acc * sv).astype(BF16)
                o_ref[r0:r0 + rc, E + l0:E + l0 + lc] = (acc * av * sv * (1.0 - sv)).astype(BF16)
        o_ref[:, 2 * E:3 * E] = dz_ref[...]

    def col(j):
        return pl.BlockSpec((T, E), lambda i: (i, j))

    def prev(j):
        return pl.BlockSpec((HALO, E), lambda i: (jnp.maximum(i * R - 1, 0), j))

    row = pl.BlockSpec((T, E), lambda i: (i, 0))
    nxt = pl.BlockSpec((HALO, E), lambda i: (jnp.minimum((i + 1) * R, S // HALO - 1), 0))
    return _call(body, name=name, grid=(nt,),
                 in_specs=[col(0), col(1), prev(0), prev(1), row, nxt, row,
                           pl.BlockSpec((HALO, E), lambda i: (0, 0))],
                 out_specs=[pl.BlockSpec((T, E3), lambda i: (i, 0)), pl.BlockSpec((HALO, E), lambda i: (0, 0))],
                 out_shape=[jax.ShapeDtypeStruct((S, E3), BF16), jax.ShapeDtypeStruct((HALO, E), F32)],
                 scratch_shapes=[pltpu.VMEM((HALO + T, E), F32), pltpu.VMEM((T + HALO, E), F32)],
                 args=[proj, proj, proj, proj, dc, dc, dz, cw], sem=("arbitrary",), jobs=jobs)


def bucket_tables():
    q = np.arange(BLOCK)[:, None]
    k = np.arange(BLOCK)[None, :]
    out = []
    for window, dil in GROUPS:
        w_sub = window // dil
        for delta in (q - k, q + BLOCK - k):
            valid = (delta >= 0) & (delta <= w_sub)
            dist = np.clip(delta, 0, None) * dil
            large = MAX_EXACT + (np.log(np.maximum(dist, 1).astype(np.float32) / MAX_EXACT)
                                 / math.log(MAX_DISTANCE / MAX_EXACT)
                                 * (N_BUCKETS - MAX_EXACT)).astype(np.int32)
            large = np.minimum(large, N_BUCKETS - 1)
            bucket = np.where(dist < MAX_EXACT, dist, large)
            out.append(np.where(valid, bucket, -1).reshape(-1))
    return np.concatenate(out).astype(np.int32)


def _attn_masks():
    ql = lax.broadcasted_iota(jnp.int32, (BLOCK, BLOCK), 0)
    kl = lax.broadcasted_iota(jnp.int32, (BLOCK, BLOCK), 1)
    return kl <= ql, kl >= ql


def _dot_nt(a, b):
    return lax.dot_general(a, b, (((1,), (1,)), ((), ())), preferred_element_type=F32)


def _dot_tn(a, b):
    return lax.dot_general(a, b, (((0,), (0,)), ((), ())), preferred_element_type=F32)


ATTN_ROWS = 2048


def _rows(start, dil):
    return pl.ds(start, BLOCK) if dil == 1 else pl.ds(start, BLOCK, stride=dil)


def _attn_geometry(S, dil):
    halo = BLOCK * dil
    rows = max(min(S, ATTN_ROWS), halo)
    return halo, rows, S // rows, rows // halo


def attn_fwd(qz, kv, bias_c, bias_p, g, dil, name, jobs=()):
    S = qz.shape[0]
    D = qz.shape[1] // (N_GROUPS + 1)
    H = D // HEAD_DIM
    halo, rows, nsb, nblk = _attn_geometry(S, dil)
    scale = HEAD_DIM ** -0.5

    def body(q_ref, kvc_ref, kvp_ref, bc_ref, bp_ref, o_ref, l_ref, ks, vs):
        sb = pl.program_id(1)
        ks[0:halo, :] = kvp_ref[0].astype(F32)
        ks[halo:, :] = kvc_ref[0].astype(F32)
        vs[0:halo, :] = kvp_ref[1].astype(F32)
        vs[halo:, :] = kvc_ref[1].astype(F32)
        mask_c, mask_p0 = _attn_masks()
        bc = bc_ref[0]
        bp = bp_ref[0]
        for jj in range(nblk):
            mask_p = mask_p0 if jj > 0 else mask_p0 & (sb > 0)
            for r in range(dil):
                cur = _rows(halo + jj * halo + r, dil)
                prv = _rows(jj * halo + r, dil)
                out = _rows(jj * halo + r, dil)
                q = q_ref[out, :].astype(BF16)
                s_c = jnp.where(mask_c, _dot_nt(q, ks[cur, :].astype(BF16)) * scale + bc, -jnp.inf)
                s_p = jnp.where(mask_p, _dot_nt(q, ks[prv, :].astype(BF16)) * scale + bp, -jnp.inf)
                m = jnp.maximum(jnp.max(s_c, axis=-1, keepdims=True), jnp.max(s_p, axis=-1, keepdims=True))
                p_c = jnp.exp(s_c - m)
                p_p = jnp.exp(s_p - m)
                den = jnp.sum(p_c, axis=-1, keepdims=True) + jnp.sum(p_p, axis=-1, keepdims=True)
                pv = (jnp.dot(p_c.astype(BF16), vs[cur, :].astype(BF16), preferred_element_type=F32)
                      + jnp.dot(p_p.astype(BF16), vs[prv, :].astype(BF16), preferred_element_type=F32))
                o_ref[out, :] = pv / den
                l_ref[out, :] = jnp.broadcast_to(m + jnp.log(den), (BLOCK, HEAD_DIM))

    per = rows // halo
    bias_spec = pl.BlockSpec((1, BLOCK, BLOCK), lambda h, sb: (h, 0, 0))
    out_spec = pl.BlockSpec((rows, HEAD_DIM), lambda h, sb: (sb, h))
    return _call(body, name=name, grid=(H, nsb),
                 in_specs=[pl.BlockSpec((rows, HEAD_DIM), lambda h, sb: (sb, g * H + h)),
                           pl.BlockSpec((2, rows, HEAD_DIM), lambda h, sb: (0, sb, g * H + h)),
                           pl.BlockSpec((2, halo, HEAD_DIM),
                                        lambda h, sb: (0, jnp.maximum(sb * per - 1, 0), g * H + h)),
                           bias_spec, bias_spec],
                 out_specs=[out_spec, out_spec],
                 out_shape=[jax.ShapeDtypeStruct((S, D), F32)] * 2,
                 scratch_shapes=[pltpu.VMEM((halo + rows, HEAD_DIM), F32)] * 2,
                 args=[qz, kv, kv, bias_c, bias_p], sem=("parallel", "arbitrary"), jobs=jobs)


def merge_fwd(os_, lses, qz, name, jobs=()):
    S, D = os_[0].shape
    T = _tile(S, 256, 8)

    def body(o1, o2, o3, l1, l2, l3, z_ref, o_ref, lse_ref, y_ref):
        la, lb_, lc_ = l1[...], l2[...], l3[...]
        m = jnp.maximum(jnp.maximum(la, lb_), lc_)
        ea, eb, ec = jnp.exp(la - m), jnp.exp(lb_ - m), jnp.exp(lc_ - m)
        den = ea + eb + ec
        o = (ea * o1[...] + eb * o2[...] + ec * o3[...]) / den
        z = z_ref[...]
        o_ref[...] = o
        lse_ref[...] = m + jnp.log(den)
        y_ref[...] = (o * (z * _sigmoid(z))).astype(BF16)

    row = pl.BlockSpec((T, D), lambda i: (i, 0))
    return _call(body, name=name, grid=(S // T,),
                 in_specs=[row] * 6 + [pl.BlockSpec((T, D), lambda i: (i, N_GROUPS))],
                 out_specs=[row] * 3,
                 out_shape=[jax.ShapeDtypeStruct((S, D), F32), jax.ShapeDtypeStruct((S, D), F32),
                            jax.ShapeDtypeStruct((S, D), BF16)],
                 args=[*os_, *lses, qz], sem=("parallel",), jobs=jobs)


def merge_bwd(dy2, o, qz, name, jobs=()):
    S, D = o.shape
    H = D // HEAD_DIM
    T = _tile(S, 256, 8)
    nq = N_GROUPS + 1

    def body(dy_ref, o_ref, z_ref, do_ref, dl_ref, dqz_ref):
        dy = dy_ref[...]
        ov = o_ref[...]
        z = z_ref[...]
        sz = _sigmoid(z)
        do = dy * (z * sz)
        do_ref[...] = do.astype(BF16)
        dqz_ref[...] = (dy * ov * (sz * (1.0 + z * (1.0 - sz)))).astype(BF16)
        prod = do * ov
        for h in range(H):
            hs = slice(h * HEAD_DIM, (h + 1) * HEAD_DIM)
            dl_ref[:, hs] = jnp.broadcast_to(jnp.sum(prod[:, hs], axis=-1, keepdims=True), (T, HEAD_DIM))

    row = pl.BlockSpec((T, D), lambda i: (i, 0))
    last = pl.BlockSpec((T, D), lambda i: (i, N_GROUPS))
    return _call(body, name=name, grid=(S // T,), in_specs=[row, row, last], out_specs=[row, row, last],
                 out_shape=[jax.ShapeDtypeStruct((S, D), BF16), jax.ShapeDtypeStruct((S, D), F32),
                            jax.ShapeDtypeStruct((S, nq * D), BF16)],
                 args=[dy2, o, qz], sem=("parallel",), jobs=jobs)


def attn_bwd(qz, kv, do, lse, delta, bias_c, bias_p, dqz, dkv, g, dil, name, jobs=()):
    S = qz.shape[0]
    D = qz.shape[1] // (N_GROUPS + 1)
    H = D // HEAD_DIM
    halo, rows, nsb, nblk = _attn_geometry(S, dil)
    scale = HEAD_DIM ** -0.5
    have_dkv = dkv is not None

    def body(*refs):
        q_ref, do_ref, l_ref, d_ref, kvc_ref, kvp_ref, bc_ref, bp_ref = refs[:8]
        n_in = 8 + 1 + (1 if have_dkv else 0)
        dq_ref, dkv_ref, dsc_ref, dsp_ref, ks, vs, dks, dvs, dos, dqs, carry_k, carry_v = refs[n_in:]
        i = pl.program_id(1)
        sb = nsb - 1 - i

        @pl.when(i == 0)
        def _():
            dsc_ref[...] = jnp.zeros_like(dsc_ref)
            dsp_ref[...] = jnp.zeros_like(dsp_ref)

        ks[0:halo, :] = kvp_ref[0].astype(F32)
        ks[halo:, :] = kvc_ref[0].astype(F32)
        vs[0:halo, :] = kvp_ref[1].astype(F32)
        vs[halo:, :] = kvc_ref[1].astype(F32)
        dos[...] = do_ref[...].astype(F32)
        dks[...] = jnp.zeros_like(dks)
        dvs[...] = jnp.zeros_like(dvs)
        mask_c, mask_p0 = _attn_masks()
        bc = bc_ref[0]
        bp = bp_ref[0]
        for jj in range(nblk):
            mask_p = mask_p0 if jj > 0 else mask_p0 & (sb > 0)
            for r in range(dil):
                cur = _rows(halo + jj * halo + r, dil)
                prv = _rows(jj * halo + r, dil)
                own = _rows(jj * halo + r, dil)
                q = q_ref[own, :].astype(BF16)
                do = dos[own, :].astype(BF16)
                lse = l_ref[own, :]
                dlt = d_ref[own, :]
                kc = ks[cur, :].astype(BF16)
                kp = ks[prv, :].astype(BF16)
                vc = vs[cur, :].astype(BF16)
                vp = vs[prv, :].astype(BF16)
                p_c = jnp.where(mask_c, jnp.exp(_dot_nt(q, kc) * scale + bc - lse), 0.0)
                p_p = jnp.where(mask_p, jnp.exp(_dot_nt(q, kp) * scale + bp - lse), 0.0)
                ds_c = p_c * (_dot_nt(do, vc) - dlt)
                ds_p = p_p * (_dot_nt(do, vp) - dlt)
                dsc_ref[0] += ds_c
                dsp_ref[0] += ds_p
                ds_cb = ds_c.astype(BF16)
                ds_pb = ds_p.astype(BF16)
                dqs[own, :] = (jnp.dot(ds_cb, kc, preferred_element_type=F32)
                               + jnp.dot(ds_pb, kp, preferred_element_type=F32)) * scale
                dks[cur, :] += _dot_tn(ds_cb, q) * scale
                dks[prv, :] += _dot_tn(ds_pb, q) * scale
                dvs[cur, :] += _dot_tn(p_c.astype(BF16), do)
                dvs[prv, :] += _dot_tn(p_p.astype(BF16), do)

        @pl.when(i > 0)
        def _():
            dks[rows:rows + halo, :] += carry_k[...]
            dvs[rows:rows + halo, :] += carry_v[...]

        dkv_ref[0] = dks[halo:, :].astype(BF16)
        dkv_ref[1] = dvs[halo:, :].astype(BF16)
        carry_k[...] = dks[0:halo, :]
        carry_v[...] = dvs[0:halo, :]
        dq_ref[...] = dqs[...].astype(BF16)

    per = rows // halo

    def rev(i):
        return nsb - 1 - i

    bias_spec = pl.BlockSpec((1, BLOCK, BLOCK), lambda h, i: (h, 0, 0))
    row_spec = pl.BlockSpec((rows, HEAD_DIM), lambda h, i: (rev(i), h))
    in_specs = [pl.BlockSpec((rows, HEAD_DIM), lambda h, i: (rev(i), g * H + h)),
                row_spec, row_spec, row_spec,
                pl.BlockSpec((2, rows, HEAD_DIM), lambda h, i: (0, rev(i), g * H + h)),
                pl.BlockSpec((2, halo, HEAD_DIM),
                             lambda h, i: (0, jnp.maximum(rev(i) * per - 1, 0), g * H + h)),
                bias_spec, bias_spec, ANY_SPEC]
    args = [qz, do, lse, delta, kv, kv, bias_c, bias_p, dqz]
    aliases = {8: 0}
    if have_dkv:
        in_specs.append(ANY_SPEC)
        args.append(dkv)
        aliases[9] = 1
    blk = (halo + rows, HEAD_DIM)
    return _call(body, name=name, grid=(H, nsb), in_specs=in_specs,
                 out_specs=[pl.BlockSpec((rows, HEAD_DIM), lambda h, i: (rev(i), g * H + h)),
                            pl.BlockSpec((2, rows, HEAD_DIM), lambda h, i: (0, rev(i), g * H + h)),
                            bias_spec, bias_spec],
                 out_shape=[jax.ShapeDtypeStruct(qz.shape, BF16),
                            jax.ShapeDtypeStruct((2, S, N_GROUPS * D), BF16),
                            jax.ShapeDtypeStruct((H, BLOCK, BLOCK), F32),
                            jax.ShapeDtypeStruct((H, BLOCK, BLOCK), F32)],
                 scratch_shapes=[pltpu.VMEM(blk, F32), pltpu.VMEM(blk, F32), pltpu.VMEM(blk, F32),
                                 pltpu.VMEM(blk, F32),
                                 pltpu.VMEM((rows, HEAD_DIM), F32), pltpu.VMEM((rows, HEAD_DIM), F32),
                                 pltpu.VMEM((halo, HEAD_DIM), F32), pltpu.VMEM((halo, HEAD_DIM), F32)],
                 aliases=aliases, args=args, sem=("parallel", "arbitrary"), jobs=jobs)


def _adamw(w, g, m, v):
    m = ADAM_B1 * m + (1.0 - ADAM_B1) * g
    v = ADAM_B2 * v + (1.0 - ADAM_B2) * (g * g)
    m_hat = m / (1.0 - ADAM_B1 ** ADAM_STEP)
    v_hat = v / (1.0 - ADAM_B2 ** ADAM_STEP)
    delta = -ADAM_LR * (m_hat / (jnp.sqrt(v_hat) + ADAM_EPS) + ADAM_WD * w)
    return delta, m, v


def add_pairs(own, got, name, jobs=()):
    _, K, n = own.shape
    tk = _tile(K, 512, 16)

    def body(a_ref, b_ref, o_ref):
        o_ref[...] = (a_ref[...].astype(F32) + b_ref[...].astype(F32)).astype(o_ref.dtype)

    blk = pl.BlockSpec((None, tk, n), lambda r, i: (r, i, 0))
    return _call(body, name=name, grid=(N_CHIPS, K // tk), in_specs=[blk, blk], out_specs=[blk],
                 out_shape=[jax.ShapeDtypeStruct(own.shape, own.dtype)], args=[own, got],
                 sem=("parallel", "parallel"), jobs=jobs)[0]


def adamw_reduce(part, got, w, m, v, name, jobs=()):
    K, n = w.shape
    tk = _tile(K, 256, 8)

    def body(p_ref, r_ref, w_ref, m_ref, v_ref, g_ref, d_ref, nm_ref, nv_ref):
        g = p_ref[...].astype(F32)
        for r in range(N_CHIPS - 1):
            g = g + r_ref[r].astype(F32)
        d, nm, nv = _adamw(w_ref[...], g, m_ref[...], v_ref[...])
        g_ref[...] = g
        d_ref[...] = d
        nm_ref[...] = nm
        nv_ref[...] = nv

    blk = pl.BlockSpec((tk, n), lambda i: (i, 0))
    return _call(body, name=name, grid=(K // tk,),
                 in_specs=[pl.BlockSpec((None, tk, n), lambda i: (0, i, 0)),
                           pl.BlockSpec((N_CHIPS - 1, tk, n), lambda i: (0, i, 0)), blk, blk, blk],
                 out_specs=[blk] * 4, out_shape=[jax.ShapeDtypeStruct((K, n), F32)] * 4,
                 args=[part, got, w, m, v], sem=("parallel",), jobs=jobs)


def sum_parts(parts, name):
    _, R, D = parts.shape

    def body(p_ref, o_ref):
        g = p_ref[0]
        for r in range(1, N_DEV):
            g = g + p_ref[r]
        o_ref[...] = g

    return _call(body, name=name, in_specs=[VMEM_SPEC], out_specs=[VMEM_SPEC],
                 out_shape=[jax.ShapeDtypeStruct((R, D), F32)], args=[parts])[0]


def adamw_small(w, g, m, v, name):
    def body(w_ref, g_ref, m_ref, v_ref, d_ref, nm_ref, nv_ref):
        d, nm, nv = _adamw(w_ref[...], g_ref[...], m_ref[...], v_ref[...])
        d_ref[...] = d
        nm_ref[...] = nm
        nv_ref[...] = nv

    return _call(body, name=name, in_specs=[VMEM_SPEC] * 4, out_specs=[VMEM_SPEC] * 3,
                 out_shape=[jax.ShapeDtypeStruct(w.shape, F32)] * 3, args=[w, g, m, v])


def _row(v, at):
    return jnp.pad(v.reshape(1, -1), ((at, 7 - at), (0, 0)))


def _pack_sharded(norm, conv_w, conv_b, ln_g, ln_b):
    n = norm.shape[-1]
    taps = jnp.pad(conv_w.reshape(CONV_WIDTH, n), ((0, HALO - CONV_WIDTH), (0, 0)))
    return jnp.concatenate([_row(norm, 0), _row(ln_g, 0) + _row(ln_b, 1) + _row(conv_b, 2), taps], axis=0)


def _pack_rel(rel_bias, D):
    return jnp.pad(rel_bias.reshape(1, -1), ((0, 7), (0, D - rel_bias.size)))


def _pack_replicated(kv_norm, b_norm, final_norm, rel_bias, D):
    return jnp.concatenate([_row(kv_norm, 0) + _row(b_norm, 1), _row(final_norm, 0), _pack_rel(rel_bias, D)], axis=0)


def kernel(x, a_norm, a_w_in, a_conv_w, a_conv_b, a_ln_g, a_ln_b, a_w_out, kv_norm, w_kv, b_norm, b_w_in, b_w_out, rel_bias, final_norm, loss_target, m_a_norm, m_a_w_in, m_a_conv_w, m_a_conv_b, m_a_ln_g, m_a_ln_b, m_a_w_out, m_kv_norm, m_w_kv, m_b_norm, m_b_w_in, m_b_w_out, m_rel_bias, m_final_norm, v_a_norm, v_a_w_in, v_a_conv_w, v_a_conv_b, v_a_ln_g, v_a_ln_b, v_a_w_out, v_kv_norm, v_w_kv, v_b_norm, v_b_w_in, v_b_w_out, v_rel_bias, v_final_norm):
    _, S, D = x.shape
    E = D
    H = D // HEAD_DIM
    nsh = D // N_DEV
    x0 = x.reshape(S, D)
    target = loss_target.reshape(S, D)
    kvn, bn, fn = kv_norm.reshape(1, D), b_norm.reshape(1, D), final_norm.reshape(1, D)

    names = ["a_w_in", "a_w_out", "w_kv", "b_w_in", "b_w_out"]
    big_w = dict(zip(names, [a_w_in[0], a_w_out[0], w_kv, b_w_in[0], b_w_out[0]]))
    big_m = dict(zip(names, [m_a_w_in[0], m_a_w_out[0], m_w_kv, m_b_w_in[0], m_b_w_out[0]]))
    big_v = dict(zip(names, [v_a_w_in[0], v_a_w_out[0], v_w_kv, v_b_w_in[0], v_b_w_out[0]]))
    shard = {nm: w.astype(BF16) for nm, w in big_w.items()}
    sh_w = _pack_sharded(a_norm, a_conv_w, a_conv_b, a_ln_g, a_ln_b)

    wa_in, sh_all = all_gather([shard["a_w_in"], sh_w], "gather_first")
    sh_full = sh_all.transpose(1, 0, 2).reshape(SMALL_SH_ROWS, D)
    an, cw, cb = sh_full[0:1], sh_full[16:16 + HALO], sh_full[10:11]
    lg, lb = sh_full[8:9], sh_full[9:10]

    (h0,) = rms_fwd(x0, [an], "rms_a")
    g_a_out = gather_chips_job(shard["a_w_out"])
    proj = matmul_nn(h0, wa_in, "a_in", jobs=[g_a_out])
    g_a_out2 = gather_sibling_job(g_a_out.results[0])
    g_kv = gather_chips_job(shard["w_kv"])
    c, y = conf_fwd(proj, cw, cb, lg, lb, "conf_fwd", jobs=[g_a_out2, g_kv])
    wa_out = g_a_out2.results[0].reshape(1, E, D)
    g_kv2 = gather_sibling_job(g_kv.results[0])
    x1 = matmul_nn(y, wa_out, "a_out", res=x0, jobs=[g_kv2])
    wkv = g_kv2.results[0]
    hk, hb = rms_fwd(x1, [kvn, bn], "rms_b")
    g_b_in = gather_chips_job(shard["b_w_in"])
    kv = matmul_nn(hk, wkv, "kv_proj", out_dtype=BF16, kv_split=True, jobs=[g_b_in])

    bt = jnp.asarray(bucket_tables())
    onehot = (bt[None, :] == jnp.arange(N_BUCKETS, dtype=jnp.int32)[:, None]).astype(F32)
    g_b_in2 = gather_sibling_job(g_b_in.results[0])
    bias = small_dot(rel_bias.T, onehot, "nn", "bias_table", jobs=[g_b_in2])
    bias = bias.reshape(H, 2 * N_GROUPS, BLOCK, BLOCK)
    bias_c = [bias[:, 2 * g] for g in range(N_GROUPS)]
    bias_p = [bias[:, 2 * g + 1] for g in range(N_GROUPS)]
    wb_in = g_b_in2.results[0]
    g_b_out = gather_chips_job(shard["b_w_out"])
    qz = matmul_nn(hb, wb_in, "b_in", jobs=[g_b_out])
    g_b_out2 = gather_sibling_job(g_b_out.results[0])

    os_, lses = [], []
    for g, (_, dil) in enumerate(GROUPS):
        o_g, l_g = attn_fwd(qz, kv, bias_c[g], bias_p[g], g, dil, "attn_fwd%d" % g,
                            jobs=[g_b_out2] if g == 0 else ())
        os_.append(o_g)
        lses.append(l_g)
    wb_out = g_b_out2.results[0].reshape(1, D, D)
    o, lse, y2 = merge_fwd(os_, lses, qz, "merge_fwd")
    x2 = matmul_nn(y2, wb_out, "b_out", res=x1)
    dx2, dx2b, fin_acc = final_loss(x2, fn, target, "final_loss")

    dy2 = matmul_nt(dx2b, wb_out, "b_out_dx")
    dwb_out = matmul_tn(y2, dx2b, 1, "b_out_dw").reshape(N_DEV, D // N_DEV, D)
    r_b_out = reduce_sibling_job(dwb_out)
    do, delta, dqz = merge_bwd(dy2, o, qz, "merge_bwd", jobs=[r_b_out])
    p_b_out = add_pairs(*r_b_out.results, "pairs_b_out")
    r_b_out2 = reduce_chips_job(p_b_out)
    dkv = None
    ds_tabs = []
    for g, (_, dil) in enumerate(GROUPS):
        dqz, dkv, ds_c, ds_p = attn_bwd(qz, kv, do, lse, delta, bias_c[g], bias_p[g], dqz, dkv, g, dil,
                                        "attn_bwd%d" % g, jobs=[r_b_out2] if g == 0 else ())
        ds_tabs += [ds_c.reshape(H, BLOCK * BLOCK), ds_p.reshape(H, BLOCK * BLOCK)]
    d_rel = small_dot(onehot, jnp.concatenate(ds_tabs, axis=1), "nt", "bias_grad")
    dhb = matmul_nt(dqz, wb_in, "b_in_dx")
    dwb_in = matmul_tn(hb, dqz, N_DEV, "b_in_dw")
    r_b_in = reduce_sibling_job(dwb_in)
    dhk = matmul_nt(dkv, wkv, "kv_dx", kv_split=True, jobs=[r_b_in])
    p_b_in = add_pairs(*r_b_in.results, "pairs_b_in")
    r_b_in2 = reduce_chips_job(p_b_in)
    dw_kv = matmul_tn(hk, dkv, N_DEV, "kv_dw", kv_split=True, jobs=[r_b_in2])
    r_kv = reduce_sibling_job(dw_kv)
    dx1, dx1b, norm_acc = rms_bwd(x1, [dhk, dhb], [kvn, bn], dx2, "rms_b_bwd", True, jobs=[r_kv])
    p_kv = add_pairs(*r_kv.results, "pairs_kv")
    r_kv2 = reduce_chips_job(p_kv)

    dy = matmul_nt(dx1b, wa_out, "a_out_dx")
    dwa_out = matmul_tn(y, dx1b, 1, "a_out_dw").reshape(N_DEV, E // N_DEV, D)
    r_a_out = reduce_sibling_job(dwa_out)
    dc, dz, ln_acc = conf_bwd_ln(c, dy, proj, lg, lb, "conf_bwd_ln", jobs=[r_a_out])
    p_a_out = add_pairs(*r_a_out.results, "pairs_a_out")
    r_a_out2 = reduce_chips_job(p_a_out)
    dproj, dcw = conf_bwd_conv(proj, dc, dz, cw, "conf_bwd_conv", jobs=[r_kv2, r_a_out2])
    dwa_in = matmul_tn(h0, dproj, N_DEV, "a_in_dw")
    r_a_in = reduce_sibling_job(dwa_in)
    dh0 = matmul_nt(dproj, wa_in, "a_in_dx", jobs=[r_a_in])
    p_a_in = add_pairs(*r_a_in.results, "pairs_a_in")
    r_a_in2 = reduce_chips_job(p_a_in)
    grad_x, a_acc = rms_bwd(x0, [dh0], [an], dx1, "rms_a_bwd", False, jobs=[r_a_in2])

    small_rows = jnp.concatenate([a_acc, ln_acc, dcw, norm_acc, fin_acc, _pack_rel(d_rel, D)], axis=0)
    share = share_small_job(small_rows)
    sums = {"w_kv": (p_kv, r_kv2), "b_w_in": (p_b_in, r_b_in2), "b_w_out": (p_b_out, r_b_out2),
            "a_w_out": (p_a_out, r_a_out2), "a_w_in": (p_a_in, r_a_in2)}
    big_out = {}
    for nm, (part, job) in sums.items():
        big_out[nm] = adamw_reduce(part, job.results[0], big_w[nm], big_m[nm], big_v[nm], "adamw_" + nm,
                                   jobs=[share] if nm == "b_w_out" else ())

    gsum = sum_parts(share.results[0], "sum_small")
    me = 4 * lax.axis_index("x") + 2 * lax.axis_index("y") + lax.axis_index("c")
    g_sh = lax.dynamic_slice(gsum, (0, me * nsh), (SMALL_SH_ROWS, nsh))
    g_rep = gsum[SMALL_SH_ROWS:]
    loss = g_rep[9, 0]
    sh_m = _pack_sharded(m_a_norm, m_a_conv_w, m_a_conv_b, m_a_ln_g, m_a_ln_b)
    sh_v = _pack_sharded(v_a_norm, v_a_conv_w, v_a_conv_b, v_a_ln_g, v_a_ln_b)
    sh_d, sh_nm, sh_nv = adamw_small(sh_w, g_sh, sh_m, sh_v, "adamw_sharded")
    rep_w = _pack_replicated(kv_norm, b_norm, final_norm, rel_bias, D)
    rep_m = _pack_replicated(m_kv_norm, m_b_norm, m_final_norm, m_rel_bias, D)
    rep_v = _pack_replicated(v_kv_norm, v_b_norm, v_final_norm, v_rel_bias, D)
    rep_d, rep_nm, rep_nv = adamw_small(rep_w, g_rep, rep_m, rep_v, "adamw_replicated")

    def unpack(kind):
        sh = (g_sh, sh_d, sh_nm, sh_nv)[kind]
        rep = (g_rep, rep_d, rep_nm, rep_nv)[kind]
        big = {nm: big_out[nm][kind] for nm in names}
        nb = rel_bias.size
        return [
            sh[0:1],
            big["a_w_in"][None],
            sh[16:16 + CONV_WIDTH][None],
            sh[10:11], sh[8:9], sh[9:10],
            big["a_w_out"][None],
            rep[0],
            big["w_kv"],
            rep[1:2],
            big["b_w_in"][None],
            big["b_w_out"][None],
            rep[16, :nb].reshape(rel_bias.shape),
            rep[8],
        ]

    return (loss, grad_x.reshape(1, S, D), *unpack(0), *unpack(1), *unpack(2), *unpack(3))
```

```python
import functools
import math

import numpy as np
import jax
import jax.numpy as jnp
from jax import lax
from jax.experimental import pallas as pl
from jax.experimental.pallas import tpu as pltpu

F32 = jnp.float32
BF16 = jnp.bfloat16

N_DEV = 8
N_CHIPS = 4
EPS = 1e-6
HEAD_DIM = 128
BLOCK = 128
GROUPS = ((128, 1), (512, 4), (2048, 16))
N_GROUPS = len(GROUPS)
CONV_WIDTH = 31
HALO = 32
N_BUCKETS = 32
MAX_EXACT = N_BUCKETS // 2
MAX_DISTANCE = 2048
V7X_VMEM_BYTES = 64 * 1024 * 1024
VMEM_LIMIT = (V7X_VMEM_BYTES * 7) // 8
LANE = 128

ADAM_LR = 0.001
ADAM_B1 = 0.9
ADAM_B2 = 0.999
ADAM_EPS = 1e-08
ADAM_WD = 0.01
ADAM_STEP = 10

SMALL_SH_ROWS = 48
SMALL_REP_ROWS = 24
MESH = pl.DeviceIdType.MESH
ANY_SPEC = pl.BlockSpec(memory_space=pl.ANY)
VMEM_SPEC = pl.BlockSpec(memory_space=pltpu.VMEM)


def _tile(dim, pref, unit=LANE):
    if dim <= pref:
        return dim
    t = (pref // unit) * unit
    while dim % t:
        t -= unit
    assert t > 0
    return t


def _sigmoid(v):
    return jax.nn.sigmoid(v)


def _place():
    return lax.axis_index("x"), lax.axis_index("y"), lax.axis_index("c")


def _flip(v, bit):
    return 1 - v if bit else v


class Job:
    def __init__(self, srcs, dsts, n_sems, build):
        self.srcs, self.dsts, self.n_sems, self.build = list(srcs), list(dsts), n_sems, build
        self.results = None


def _remote(src, dst, send_sems, recv_sems, k, peer):
    return pltpu.make_async_remote_copy(src_ref=src, dst_ref=dst, send_sem=send_sems.at[k],
                                        recv_sem=recv_sems.at[k], device_id=peer, device_id_type=MESH)


def _call(body, *, name, in_specs, out_specs, out_shape, args, grid=(), scratch_shapes=(), sem=(),
          aliases=None, jobs=()):
    n_in, n_out, n_scr = len(in_specs), len(out_specs), len(scratch_shapes)
    aliases = dict(aliases or {})
    x_in, x_out, x_scr = [], [], []
    for job in jobs:
        job.in_at = n_in + len(x_in)
        x_in += job.srcs
        job.out_at = n_out + len(x_out)
        for d in job.dsts:
            if not isinstance(d, jax.ShapeDtypeStruct):
                aliases[n_in + len(x_in)] = n_out + len(x_out)
                x_in.append(d)
            x_out.append(jax.ShapeDtypeStruct(d.shape, d.dtype))
        job.scr_at = n_scr + len(x_scr)
        x_scr += [pltpu.SemaphoreType.DMA((job.n_sems,))] * 3

    def wrapped(*refs):
        ins = refs[:n_in + len(x_in)]
        outs = refs[len(ins):len(ins) + n_out + len(x_out)]
        scr = refs[len(ins) + len(outs):]
        core = ins[:n_in] + outs[:n_out] + scr[:n_scr]
        if not jobs:
            body(*core)
            return
        copies = []
        for job in jobs:
            copies += job.build(ins[job.in_at:job.in_at + len(job.srcs)],
                                outs[job.out_at:job.out_at + len(job.dsts)],
                                *scr[job.scr_at:job.scr_at + 3])
        if grid:
            pids = [pl.program_id(d) for d in range(len(grid))]
            first = functools.reduce(jnp.logical_and, [p == 0 for p in pids])
            last = functools.reduce(jnp.logical_and, [p == g - 1 for p, g in zip(pids, grid)])

            @pl.when(first)
            def _():
                for cp in copies:
                    cp.start()

            body(*core)

            @pl.when(last)
            def _():
                for cp in copies:
                    cp.wait()
        else:
            for cp in copies:
                cp.start()
            body(*core)
            for cp in copies:
                cp.wait()

    if jobs:
        sem = ("arbitrary",) * len(grid)
    kwargs = dict(grid=grid) if grid else {}
    if aliases:
        kwargs["input_output_aliases"] = aliases
    outs = pl.pallas_call(
        wrapped, name=name,
        in_specs=list(in_specs) + [ANY_SPEC] * len(x_in),
        out_specs=list(out_specs) + [ANY_SPEC] * len(x_out),
        out_shape=list(out_shape) + x_out,
        scratch_shapes=list(scratch_shapes) + x_scr,
        compiler_params=pltpu.CompilerParams(dimension_semantics=sem if sem else None,
                                             vmem_limit_bytes=VMEM_LIMIT),
        **kwargs,
    )(*args, *x_in)
    for job in jobs:
        job.results = list(outs[job.out_at:job.out_at + len(job.dsts)])
    return list(outs[:n_out])


def gather_chips_job(shard):
    def build(srcs, dsts, send, recv, loc):
        (src,), (out,) = srcs, dsts
        x, y, c = _place()
        mine = out.at[4 * x + 2 * y + c]
        peers = [(x, y, 1 - c), (1 - x, y, c), (x, 1 - y, c), (1 - x, 1 - y, c)]
        return ([pltpu.make_async_copy(src, mine, loc.at[0])]
                + [_remote(src, mine, send, recv, k, p) for k, p in enumerate(peers)])

    return Job([shard], [jax.ShapeDtypeStruct((N_DEV,) + shard.shape, shard.dtype)], 4, build)


def gather_sibling_job(buf):
    def build(srcs, dsts, send, recv, loc):
        (out,) = dsts
        x, y, c = _place()
        copies = []
        for k, (cx, cy) in enumerate([(1 - x, y), (x, 1 - y), (1 - x, 1 - y)]):
            blk = out.at[4 * cx + 2 * cy + c]
            copies.append(_remote(blk, blk, send, recv, k, (x, y, 1 - c)))
        return copies

    return Job([], [buf], 3, build)


def reduce_sibling_job(dw):
    def build(srcs, dsts, send, recv, loc):
        (src,), (got,) = srcs, dsts
        x, y, c = _place()
        return [_remote(src.at[4 * _flip(x, r & 2) + 2 * _flip(y, r & 1) + 1 - c], got.at[r], send, recv, r,
                        (x, y, 1 - c)) for r in range(N_CHIPS)]

    return Job([dw], [jax.ShapeDtypeStruct((N_CHIPS,) + dw.shape[1:], dw.dtype)], N_CHIPS, build)


def reduce_chips_job(part):
    def build(srcs, dsts, send, recv, loc):
        (src,), (got,) = srcs, dsts
        x, y, c = _place()
        return [_remote(src.at[r], got.at[r - 1], send, recv, r - 1, (_flip(x, r & 2), _flip(y, r & 1), c))
                for r in range(1, N_CHIPS)]

    return Job([part], [jax.ShapeDtypeStruct((N_CHIPS - 1,) + part.shape[1:], part.dtype)], N_CHIPS - 1, build)


def share_small_job(small):
    def build(srcs, dsts, send, recv, loc):
        (src,), (out,) = srcs, dsts
        x, y, c = _place()
        mine = out.at[4 * x + 2 * y + c]
        copies = [pltpu.make_async_copy(src, mine, loc.at[0])]
        for rel in range(1, N_DEV):
            peer = (_flip(x, rel & 4), _flip(y, rel & 2), _flip(c, rel & 1))
            copies.append(_remote(src, mine, send, recv, rel - 1, peer))
        return copies

    return Job([small], [jax.ShapeDtypeStruct((N_DEV,) + small.shape, small.dtype)], N_DEV - 1, build)


def all_gather(shards, name):
    n = len(shards)

    def body(*refs):
        ins, outs = refs[:n], refs[n:2 * n]
        send_sems, recv_sems, local_sems = refs[2 * n:]
        x, y, c = _place()
        me, sibling = (x, y, c), (x, y, 1 - c)
        chips = [(1 - x, y), (x, 1 - y), (1 - x, 1 - y)]

        def slot(a, dev):
            return outs[a].at[4 * dev[0] + 2 * dev[1] + dev[2]]

        def copy(a, k, block, to, src=None):
            return pltpu.make_async_remote_copy(
                src_ref=slot(a, block) if src is None else src, dst_ref=slot(a, block),
                send_sem=send_sems.at[a, k], recv_sem=recv_sems.at[a, k],
                device_id=to, device_id_type=MESH)

        mine, first, passed = [], [], []
        for a in range(n):
            mine.append(pltpu.make_async_copy(ins[a], slot(a, me), local_sems.at[a]))
            mine[a].start()
            first.append([copy(a, 0, me, sibling, src=ins[a])]
                         + [copy(a, 1 + j, me, (*chip, c), src=ins[a]) for j, chip in enumerate(chips)])
            for cp in first[a]:
                cp.start()
        for a in range(n):
            passed.append([copy(a, 4 + j, (*chip, c), sibling) for j, chip in enumerate(chips)])
            for j, chip in enumerate(chips):
                copy(a, 1 + j, (*chip, c), me).wait_recv()
                passed[a][j].start()
        for a in range(n):
            copy(a, 0, sibling, me).wait_recv()
            for j, chip in enumerate(chips):
                copy(a, 4 + j, (*chip, 1 - c), me).wait_recv()
            for cp in first[a] + passed[a]:
                cp.wait_send()
            mine[a].wait()

    return pl.pallas_call(
        body, name=name,
        in_specs=[ANY_SPEC] * n, out_specs=[ANY_SPEC] * n,
        out_shape=[jax.ShapeDtypeStruct((N_DEV,) + s.shape, s.dtype) for s in shards],
        scratch_shapes=[pltpu.SemaphoreType.DMA((n, 7)), pltpu.SemaphoreType.DMA((n, 7)),
                        pltpu.SemaphoreType.DMA((n,))],
    )(*shards)


def _kv_split_index(tw, D):
    pd = D // tw

    def index(j):
        return (j // pd) % 2, (j // (2 * pd)) * pd + j % pd

    return index


def _col_tile(n, also, pref):
    t = (min(pref, n) // LANE) * LANE
    while n % t or (also is not None and also % t):
        t -= LANE
    assert t > 0
    return t


def matmul_nn(a, w, name, res=None, out_dtype=F32, kv_split=False, jobs=()):
    M, K = a.shape
    nb, _, n = w.shape
    D = nb * n // (2 * N_GROUPS)
    tn = _col_tile(n, D if kv_split else None, 1024)
    tm = _tile(M, 2048 if tn <= 512 else 1024)
    per = n // tn

    def body(*refs):
        if res is None:
            a_ref, w_ref, o_ref = refs
        else:
            a_ref, w_ref, r_ref, o_ref = refs
        acc = jnp.dot(a_ref[...], w_ref[...], preferred_element_type=F32)
        if res is not None:
            acc = r_ref[...] + acc
        o_ref[...] = acc.astype(o_ref.dtype)

    in_specs = [pl.BlockSpec((tm, K), lambda i, j: (i, 0)),
                pl.BlockSpec((None, K, tn), lambda i, j: (j // per, 0, j % per))]
    args = [a, w]
    if res is not None:
        in_specs.append(pl.BlockSpec((tm, tn), lambda i, j: (i, j)))
        args.append(res)
    if kv_split:
        split = _kv_split_index(tn, D)
        out_spec = pl.BlockSpec((None, tm, tn), lambda i, j: (split(j)[0], i, split(j)[1]))
        out_shape = jax.ShapeDtypeStruct((2, M, N_GROUPS * D), out_dtype)
    else:
        out_spec = pl.BlockSpec((tm, tn), lambda i, j: (i, j))
        out_shape = jax.ShapeDtypeStruct((M, nb * n), out_dtype)
    return _call(body, name=name, grid=(M // tm, nb * per), in_specs=in_specs, out_specs=[out_spec],
                 out_shape=[out_shape], args=args, sem=("parallel", "parallel"), jobs=jobs)[0]


def matmul_nt(dy, w, name, kv_split=False, jobs=()):
    M = dy.shape[-2]
    nb, K, n = w.shape
    D = nb * n // (2 * N_GROUPS)
    tm = _tile(M, 1024)
    tc = _col_tile(n, D if kv_split else None, 1024)
    per = n // tc

    def body(dy_ref, w_ref, o_ref):
        j = pl.program_id(1)
        part = lax.dot_general(dy_ref[...], w_ref[...], (((1,), (1,)), ((), ())),
                               preferred_element_type=F32)

        @pl.when(j == 0)
        def _():
            o_ref[...] = part

        @pl.when(j > 0)
        def _():
            o_ref[...] += part

    if kv_split:
        split = _kv_split_index(tc, D)
        dy_spec = pl.BlockSpec((None, tm, tc), lambda i, j: (split(j)[0], i, split(j)[1]))
    else:
        dy_spec = pl.BlockSpec((tm, tc), lambda i, j: (i, j))
    return _call(body, name=name, grid=(M // tm, nb * per),
                 in_specs=[dy_spec, pl.BlockSpec((None, K, tc), lambda i, j: (j // per, 0, j % per))],
                 out_specs=[pl.BlockSpec((tm, K), lambda i, j: (i, 0))],
                 out_shape=[jax.ShapeDtypeStruct((M, K), F32)], args=[dy, w],
                 sem=("parallel", "arbitrary"), jobs=jobs)[0]


def matmul_tn(a, dy, nb, name, out_dtype=BF16, kv_split=False, jobs=()):
    M, K = a.shape
    N = 2 * dy.shape[-1] if kv_split else dy.shape[-1]
    n = N // nb
    D = N // (2 * N_GROUPS)
    tn = _col_tile(n, D if kv_split else None, 1024)
    per = n // tn
    tk = _tile(K, 1024)
    tmc = _tile(M, 2048)
    steps = M // tmc

    def body(a_ref, dy_ref, o_ref, acc_ref):
        s = pl.program_id(2)
        part = lax.dot_general(a_ref[...], dy_ref[...], (((0,), (0,)), ((), ())),
                               preferred_element_type=F32)

        @pl.when(s == 0)
        def _():
            acc_ref[...] = part

        @pl.when(s > 0)
        def _():
            acc_ref[...] += part

        @pl.when(s == steps - 1)
        def _():
            o_ref[...] = acc_ref[...].astype(o_ref.dtype)

    if kv_split:
        split = _kv_split_index(tn, D)
        dy_spec = pl.BlockSpec((None, tmc, tn), lambda j, k, s: (split(j)[0], s, split(j)[1]))
    else:
        dy_spec = pl.BlockSpec((tmc, tn), lambda j, k, s: (s, j))
    return _call(body, name=name, grid=(nb * per, K // tk, steps),
                 in_specs=[pl.BlockSpec((tmc, tk), lambda j, k, s: (s, k)), dy_spec],
                 out_specs=[pl.BlockSpec((None, tk, tn), lambda j, k, s: (j // per, k, j % per))],
                 out_shape=[jax.ShapeDtypeStruct((nb, K, n), out_dtype)], args=[a, dy],
                 scratch_shapes=[pltpu.VMEM((tk, tn), F32)],
                 sem=("parallel", "parallel", "arbitrary"), jobs=jobs)[0]


def small_dot(a, b, contract, name, jobs=()):
    if contract == "nn":
        dims = (((1,), (0,)), ((), ()))
        out = (a.shape[0], b.shape[1])
    else:
        dims = (((1,), (1,)), ((), ()))
        out = (a.shape[0], b.shape[0])

    def body(a_ref, b_ref, o_ref):
        o_ref[...] = lax.dot_general(a_ref[...], b_ref[...], dims, precision=lax.Precision.HIGHEST,
                                     preferred_element_type=F32)

    return _call(body, name=name, in_specs=[VMEM_SPEC, VMEM_SPEC], out_specs=[VMEM_SPEC],
                 out_shape=[jax.ShapeDtypeStruct(out, F32)], args=[a, b], jobs=jobs)[0]


def rms_fwd(x, gains, name, jobs=()):
    S, D = x.shape
    T = _tile(S, 512, 8)
    n = len(gains)

    def body(x_ref, *refs):
        xv = x_ref[...]
        xn = xv * lax.rsqrt(jnp.mean(xv * xv, axis=-1, keepdims=True) + EPS)
        for g_ref, o_ref in zip(refs[:n], refs[n:]):
            o_ref[...] = (xn * g_ref[...]).astype(o_ref.dtype)

    row = pl.BlockSpec((T, D), lambda i: (i, 0))
    vec = pl.BlockSpec((1, D), lambda i: (0, 0))
    return _call(body, name=name, grid=(S // T,), in_specs=[row] + [vec] * n, out_specs=[row] * n,
                 out_shape=[jax.ShapeDtypeStruct((S, D), BF16)] * n, args=[x, *gains],
                 sem=("parallel",), jobs=jobs)


def rms_bwd(x, dhs, gains, dres, name, want_bf16, jobs=()):
    S, D = x.shape
    T = _tile(S, 256, 8)
    n = len(gains)

    def body(x_ref, *refs):
        dh_refs = refs[:n]
        g_refs = refs[n:2 * n]
        dres_ref = refs[2 * n]
        outs = refs[2 * n + 1:]
        dx_ref, dg_ref = outs[0], outs[-1]
        i = pl.program_id(0)

        @pl.when(i == 0)
        def _():
            dg_ref[...] = jnp.zeros_like(dg_ref)

        xv = x_ref[...]
        r = lax.rsqrt(jnp.mean(xv * xv, axis=-1, keepdims=True) + EPS)
        xn = xv * r
        dxn = jnp.zeros_like(xv)
        for k in range(n):
            dh = dh_refs[k][...]
            dg_ref[k:k + 1, :] += jnp.sum(dh * xn, axis=0, keepdims=True)
            dxn = dxn + dh * g_refs[k][...]
        dx = dres_ref[...] + r * (dxn - xn * jnp.mean(dxn * xn, axis=-1, keepdims=True))
        dx_ref[...] = dx
        if want_bf16:
            outs[1][...] = dx.astype(BF16)

    row = pl.BlockSpec((T, D), lambda i: (i, 0))
    vec = pl.BlockSpec((1, D), lambda i: (0, 0))
    acc = pl.BlockSpec((8, D), lambda i: (0, 0))
    out_specs = [row] + ([row] if want_bf16 else []) + [acc]
    out_shape = ([jax.ShapeDtypeStruct((S, D), F32)]
                 + ([jax.ShapeDtypeStruct((S, D), BF16)] if want_bf16 else [])
                 + [jax.ShapeDtypeStruct((8, D), F32)])
    return _call(body, name=name, grid=(S // T,), in_specs=[row] + [row] * n + [vec] * n + [row],
                 out_specs=out_specs, out_shape=out_shape, args=[x, *dhs, *gains, dres],
                 sem=("arbitrary",), jobs=jobs)


def final_loss(x2, gain, target, name, jobs=()):
    S, D = x2.shape
    T = _tile(S, 256, 8)

    def body(x_ref, g_ref, t_ref, dx_ref, dxb_ref, acc_ref):
        i = pl.program_id(0)

        @pl.when(i == 0)
        def _():
            acc_ref[...] = jnp.zeros_like(acc_ref)

        xv = x_ref[...]
        g = g_ref[...]
        r = lax.rsqrt(jnp.mean(xv * xv, axis=-1, keepdims=True) + EPS)
        xn = xv * r
        err = xn * g - t_ref[...]
        dy = err * (1.0 / D)
        acc_ref[0:1, :] += jnp.sum(dy * xn, axis=0, keepdims=True)
        acc_ref[1:2, :] += jnp.full((1, D), 0.5 / D, F32) * jnp.sum(err * err)
        dxn = dy * g
        dx = r * (dxn - xn * jnp.mean(dxn * xn, axis=-1, keepdims=True))
        dx_ref[...] = dx
        dxb_ref[...] = dx.astype(BF16)

    row = pl.BlockSpec((T, D), lambda i: (i, 0))
    return _call(body, name=name, grid=(S // T,),
                 in_specs=[row, pl.BlockSpec((1, D), lambda i: (0, 0)), row],
                 out_specs=[row, row, pl.BlockSpec((8, D), lambda i: (0, 0))],
                 out_shape=[jax.ShapeDtypeStruct((S, D), F32), jax.ShapeDtypeStruct((S, D), BF16),
                            jax.ShapeDtypeStruct((8, D), F32)],
                 args=[x2, gain, target], sem=("arbitrary",), jobs=jobs)


ROW_CHUNK = 64
LANE_CHUNK = 512


def conf_fwd(proj, cw, cb, lg, lb, name, jobs=()):
    S, E3 = proj.shape
    E = E3 // 3
    T = _tile(S, 256, HALO)
    R = T // HALO
    lc = _tile(E, LANE_CHUNK)
    rc = min(ROW_CHUNK, T)

    def body(a_ref, b_ref, z_ref, ap_ref, bp_ref, cw_ref, cb_ref, lg_ref, lb_ref, c_ref, y_ref, u_scr):
        i = pl.program_id(0)
        up = ap_ref[...] * _sigmoid(bp_ref[...])
        u_scr[0:HALO, :] = jnp.where(i > 0, up, 0.0)
        u_scr[HALO:HALO + T, :] = a_ref[...] * _sigmoid(b_ref[...])
        off = HALO - (CONV_WIDTH - 1)
        for r0 in range(0, T, rc):
            for l0 in range(0, E, lc):
                acc = jnp.broadcast_to(cb_ref[:, l0:l0 + lc], (rc, lc))
                for k in range(CONV_WIDTH):
                    acc = acc + u_scr[r0 + off + k:r0 + off + k + rc, l0:l0 + lc] * cw_ref[k:k + 1, l0:l0 + lc]
                c_ref[r0:r0 + rc, l0:l0 + lc] = acc
        c = c_ref[...]
        mu = jnp.mean(c, axis=-1, keepdims=True)
        d = c - mu
        var = jnp.mean(d * d, axis=-1, keepdims=True)
        cn = d * lax.rsqrt(var + EPS) * lg_ref[...] + lb_ref[...]
        z = z_ref[...]
        y_ref[...] = ((cn * _sigmoid(cn)) * (z * _sigmoid(z))).astype(BF16)

    def col(j):
        return pl.BlockSpec((T, E), lambda i: (i, j))

    def prev(j):
        return pl.BlockSpec((HALO, E), lambda i: (jnp.maximum(i * R - 1, 0), j))

    vec = pl.BlockSpec((1, E), lambda i: (0, 0))
    return _call(body, name=name, grid=(S // T,),
                 in_specs=[col(0), col(1), col(2), prev(0), prev(1),
                           pl.BlockSpec((HALO, E), lambda i: (0, 0)), vec, vec, vec],
                 out_specs=[pl.BlockSpec((T, E), lambda i: (i, 0))] * 2,
                 out_shape=[jax.ShapeDtypeStruct((S, E), F32), jax.ShapeDtypeStruct((S, E), BF16)],
                 scratch_shapes=[pltpu.VMEM((HALO + T, E), F32)],
                 args=[proj, proj, proj, proj, proj, cw, cb, lg, lb], sem=("parallel",), jobs=jobs)


def conf_bwd_ln(c, dy, proj, lg, lb, name, jobs=()):
    S, E = c.shape
    T = _tile(S, 256, 8)

    def body(c_ref, dy_ref, z_ref, lg_ref, lb_ref, dc_ref, dz_ref, acc_ref):
        i = pl.program_id(0)

        @pl.when(i == 0)
        def _():
            acc_ref[...] = jnp.zeros_like(acc_ref)

        cv = c_ref[...]
        mu = jnp.mean(cv, axis=-1, keepdims=True)
        d = cv - mu
        var = jnp.mean(d * d, axis=-1, keepdims=True)
        rstd = lax.rsqrt(var + EPS)
        xh = d * rstd
        lgv = lg_ref[...]
        cn = xh * lgv + lb_ref[...]
        z = z_ref[...]
        dy = dy_ref[...]
        sc = _sigmoid(cn)
        sz = _sigmoid(z)
        dcn = dy * (z * sz) * (sc * (1.0 + cn * (1.0 - sc)))
        dz_ref[...] = (dy * (cn * sc) * (sz * (1.0 + z * (1.0 - sz)))).astype(BF16)
        acc_ref[0:1, :] += jnp.sum(dcn * xh, axis=0, keepdims=True)
        acc_ref[1:2, :] += jnp.sum(dcn, axis=0, keepdims=True)
        dxh = dcn * lgv
        dc = rstd * (dxh - jnp.mean(dxh, axis=-1, keepdims=True)
                     - xh * jnp.mean(dxh * xh, axis=-1, keepdims=True))
        acc_ref[2:3, :] += jnp.sum(dc, axis=0, keepdims=True)
        dc_ref[...] = dc

    row = pl.BlockSpec((T, E), lambda i: (i, 0))
    vec = pl.BlockSpec((1, E), lambda i: (0, 0))
    return _call(body, name=name, grid=(S // T,),
                 in_specs=[row, row, pl.BlockSpec((T, E), lambda i: (i, 2)), vec, vec],
                 out_specs=[row, row, pl.BlockSpec((8, E), lambda i: (0, 0))],
                 out_shape=[jax.ShapeDtypeStruct((S, E), F32), jax.ShapeDtypeStruct((S, E), BF16),
                            jax.ShapeDtypeStruct((8, E), F32)],
                 args=[c, dy, proj, lg, lb], sem=("arbitrary",), jobs=jobs)


def conf_bwd_conv(proj, dc, dz, cw, name, jobs=()):
    S, E3 = proj.shape
    E = E3 // 3
    T = _tile(S, 256, HALO)
    R = T // HALO
    nt = S // T
    lc = _tile(E, LANE_CHUNK)
    rc = min(ROW_CHUNK, T)

    def body(a_ref, b_ref, ap_ref, bp_ref, dc_ref, dcn_ref, dz_ref, cw_ref, o_ref, dw_ref, u_scr, dc_scr):
        i = pl.program_id(0)

        @pl.when(i == 0)
        def _():
            dw_ref[...] = jnp.zeros_like(dw_ref)

        a = a_ref[...]
        sb = _sigmoid(b_ref[...])
        up = ap_ref[...] * _sigmoid(bp_ref[...])
        u_scr[0:HALO, :] = jnp.where(i > 0, up, 0.0)
        u_scr[HALO:HALO + T, :] = a * sb
        dc_scr[0:T, :] = dc_ref[...]
        dc_scr[T:T + HALO, :] = jnp.where(i < nt - 1, dcn_ref[...], 0.0)
        off = HALO - (CONV_WIDTH - 1)
        for l0 in range(0, E, lc):
            for k in range(CONV_WIDTH):
                prod = u_scr[off + k:off + k + T, l0:l0 + lc] * dc_scr[0:T, l0:l0 + lc]
                dw_ref[k:k + 1, l0:l0 + lc] += jnp.sum(prod, axis=0, keepdims=True)
            for r0 in range(0, T, rc):
                acc = jnp.zeros((rc, lc), F32)
                for k in range(CONV_WIDTH):
                    s0 = r0 + (CONV_WIDTH - 1) - k
                    acc = acc + dc_scr[s0:s0 + rc, l0:l0 + lc] * cw_ref[k:k + 1, l0:l0 + lc]
                av = a[r0:r0 + rc, l0:l0 + lc]
                sv = sb[r0:r0 + rc, l0:l0 + lc]
                o_ref[r0:r0 + rc, l0:l0 + lc] = (acc * sv).astype(BF16)
                o_ref[r0:r0 + rc, E + l0:E + l0 + lc] = (acc * av * sv * (1.0 - sv)).astype(BF16)
        o_ref[:, 2 * E:3 * E] = dz_ref[...]

    def col(j):
        return pl.BlockSpec((T, E), lambda i: (i, j))

    def prev(j):
        return pl.BlockSpec((HALO, E), lambda i: (jnp.maximum(i * R - 1, 0), j))

    row = pl.BlockSpec((T, E), lambda i: (i, 0))
    nxt = pl.BlockSpec((HALO, E), lambda i: (jnp.minimum((i + 1) * R, S // HALO - 1), 0))
    return _call(body, name=name, grid=(nt,),
                 in_specs=[col(0), col(1), prev(0), prev(1), row, nxt, row,
                           pl.BlockSpec((HALO, E), lambda i: (0, 0))],
                 out_specs=[pl.BlockSpec((T, E3), lambda i: (i, 0)), pl.BlockSpec((HALO, E), lambda i: (0, 0))],
                 out_shape=[jax.ShapeDtypeStruct((S, E3), BF16), jax.ShapeDtypeStruct((HALO, E), F32)],
                 scratch_shapes=[pltpu.VMEM((HALO + T, E), F32), pltpu.VMEM((T + HALO, E), F32)],
                 args=[proj, proj, proj, proj, dc, dc, dz, cw], sem=("arbitrary",), jobs=jobs)


def bucket_tables():
    q = np.arange(BLOCK)[:, None]
    k = np.arange(BLOCK)[None, :]
    out = []
    for window, dil in GROUPS:
        w_sub = window // dil
        for delta in (q - k, q + BLOCK - k):
            valid = (delta >= 0) & (delta <= w_sub)
            dist = np.clip(delta, 0, None) * dil
            large = MAX_EXACT + (np.log(np.maximum(dist, 1).astype(np.float32) / MAX_EXACT)
                                 / math.log(MAX_DISTANCE / MAX_EXACT)
                                 * (N_BUCKETS - MAX_EXACT)).astype(np.int32)
            large = np.minimum(large, N_BUCKETS - 1)
            bucket = np.where(dist < MAX_EXACT, dist, large)
            out.append(np.where(valid, bucket, -1).reshape(-1))
    return np.concatenate(out).astype(np.int32)


def _attn_masks():
    ql = lax.broadcasted_iota(jnp.int32, (BLOCK, BLOCK), 0)
    kl = lax.broadcasted_iota(jnp.int32, (BLOCK, BLOCK), 1)
    return kl <= ql, kl >= ql


def _dot_nt(a, b):
    return lax.dot_general(a, b, (((1,), (1,)), ((), ())), preferred_element_type=F32)


def _dot_tn(a, b):
    return lax.dot_general(a, b, (((0,), (0,)), ((), ())), preferred_element_type=F32)


ATTN_ROWS = 2048


def _rows(start, dil):
    return pl.ds(start, BLOCK) if dil == 1 else pl.ds(start, BLOCK, stride=dil)


def _attn_geometry(S, dil):
    halo = BLOCK * dil
    rows = max(min(S, ATTN_ROWS), halo)
    return halo, rows, S // rows, rows // halo


def attn_fwd(qz, kv, bias_c, bias_p, g, dil, name, jobs=()):
    S = qz.shape[0]
    D = qz.shape[1] // (N_GROUPS + 1)
    H = D // HEAD_DIM
    halo, rows, nsb, nblk = _attn_geometry(S, dil)
    scale = HEAD_DIM ** -0.5

    def body(q_ref, kvc_ref, kvp_ref, bc_ref, bp_ref, o_ref, l_ref, ks, vs):
        sb = pl.program_id(1)
        ks[0:halo, :] = kvp_ref[0].astype(F32)
        ks[halo:, :] = kvc_ref[0].astype(F32)
        vs[0:halo, :] = kvp_ref[1].astype(F32)
        vs[halo:, :] = kvc_ref[1].astype(F32)
        mask_c, mask_p0 = _attn_masks()
        bc = bc_ref[0]
        bp = bp_ref[0]
        for jj in range(nblk):
            mask_p = mask_p0 if jj > 0 else mask_p0 & (sb > 0)
            for r in range(dil):
                cur = _rows(halo + jj * halo + r, dil)
                prv = _rows(jj * halo + r, dil)
                out = _rows(jj * halo + r, dil)
                q = q_ref[out, :].astype(BF16)
                s_c = jnp.where(mask_c, _dot_nt(q, ks[cur, :].astype(BF16)) * scale + bc, -jnp.inf)
                s_p = jnp.where(mask_p, _dot_nt(q, ks[prv, :].astype(BF16)) * scale + bp, -jnp.inf)
                m = jnp.maximum(jnp.max(s_c, axis=-1, keepdims=True), jnp.max(s_p, axis=-1, keepdims=True))
                p_c = jnp.exp(s_c - m)
                p_p = jnp.exp(s_p - m)
                den = jnp.sum(p_c, axis=-1, keepdims=True) + jnp.sum(p_p, axis=-1, keepdims=True)
                pv = (jnp.dot(p_c.astype(BF16), vs[cur, :].astype(BF16), preferred_element_type=F32)
                      + jnp.dot(p_p.astype(BF16), vs[prv, :].astype(BF16), preferred_element_type=F32))
                o_ref[out, :] = pv / den
                l_ref[out, :] = jnp.broadcast_to(m + jnp.log(den), (BLOCK, HEAD_DIM))

    per = rows // halo
    bias_spec = pl.BlockSpec((1, BLOCK, BLOCK), lambda h, sb: (h, 0, 0))
    out_spec = pl.BlockSpec((rows, HEAD_DIM), lambda h, sb: (sb, h))
    return _call(body, name=name, grid=(H, nsb),
                 in_specs=[pl.BlockSpec((rows, HEAD_DIM), lambda h, sb: (sb, g * H + h)),
                           pl.BlockSpec((2, rows, HEAD_DIM), lambda h, sb: (0, sb, g * H + h)),
                           pl.BlockSpec((2, halo, HEAD_DIM),
                                        lambda h, sb: (0, jnp.maximum(sb * per - 1, 0), g * H + h)),
                           bias_spec, bias_spec],
                 out_specs=[out_spec, out_spec],
                 out_shape=[jax.ShapeDtypeStruct((S, D), F32)] * 2,
                 scratch_shapes=[pltpu.VMEM((halo + rows, HEAD_DIM), F32)] * 2,
                 args=[qz, kv, kv, bias_c, bias_p], sem=("parallel", "arbitrary"), jobs=jobs)


def merge_fwd(os_, lses, qz, name, jobs=()):
    S, D = os_[0].shape
    T = _tile(S, 256, 8)

    def body(o1, o2, o3, l1, l2, l3, z_ref, o_ref, lse_ref, y_ref):
        la, lb_, lc_ = l1[...], l2[...], l3[...]
        m = jnp.maximum(jnp.maximum(la, lb_), lc_)
        ea, eb, ec = jnp.exp(la - m), jnp.exp(lb_ - m), jnp.exp(lc_ - m)
        den = ea + eb + ec
        o = (ea * o1[...] + eb * o2[...] + ec * o3[...]) / den
        z = z_ref[...]
        o_ref[...] = o
        lse_ref[...] = m + jnp.log(den)
        y_ref[...] = (o * (z * _sigmoid(z))).astype(BF16)

    row = pl.BlockSpec((T, D), lambda i: (i, 0))
    return _call(body, name=name, grid=(S // T,),
                 in_specs=[row] * 6 + [pl.BlockSpec((T, D), lambda i: (i, N_GROUPS))],
                 out_specs=[row] * 3,
                 out_shape=[jax.ShapeDtypeStruct((S, D), F32), jax.ShapeDtypeStruct((S, D), F32),
                            jax.ShapeDtypeStruct((S, D), BF16)],
                 args=[*os_, *lses, qz], sem=("parallel",), jobs=jobs)


def merge_bwd(dy2, o, qz, name, jobs=()):
    S, D = o.shape
    H = D // HEAD_DIM
    T = _tile(S, 256, 8)
    nq = N_GROUPS + 1

    def body(dy_ref, o_ref, z_ref, do_ref, dl_ref, dqz_ref):
        dy = dy_ref[...]
        ov = o_ref[...]
        z = z_ref[...]
        sz = _sigmoid(z)
        do = dy * (z * sz)
        do_ref[...] = do.astype(BF16)
        dqz_ref[...] = (dy * ov * (sz * (1.0 + z * (1.0 - sz)))).astype(BF16)
        prod = do * ov
        for h in range(H):
            hs = slice(h * HEAD_DIM, (h + 1) * HEAD_DIM)
            dl_ref[:, hs] = jnp.broadcast_to(jnp.sum(prod[:, hs], axis=-1, keepdims=True), (T, HEAD_DIM))

    row = pl.BlockSpec((T, D), lambda i: (i, 0))
    last = pl.BlockSpec((T, D), lambda i: (i, N_GROUPS))
    return _call(body, name=name, grid=(S // T,), in_specs=[row, row, last], out_specs=[row, row, last],
                 out_shape=[jax.ShapeDtypeStruct((S, D), BF16), jax.ShapeDtypeStruct((S, D), F32),
                            jax.ShapeDtypeStruct((S, nq * D), BF16)],
                 args=[dy2, o, qz], sem=("parallel",), jobs=jobs)


def attn_bwd(qz, kv, do, lse, delta, bias_c, bias_p, dqz, dkv, g, dil, name, jobs=()):
    S = qz.shape[0]
    D = qz.shape[1] // (N_GROUPS + 1)
    H = D // HEAD_DIM
    halo, rows, nsb, nblk = _attn_geometry(S, dil)
    scale = HEAD_DIM ** -0.5
    have_dkv = dkv is not None

    def body(*refs):
        q_ref, do_ref, l_ref, d_ref, kvc_ref, kvp_ref, bc_ref, bp_ref = refs[:8]
        n_in = 8 + 1 + (1 if have_dkv else 0)
        dq_ref, dkv_ref, dsc_ref, dsp_ref, ks, vs, dks, dvs, dos, dqs, carry_k, carry_v = refs[n_in:]
        i = pl.program_id(1)
        sb = nsb - 1 - i

        @pl.when(i == 0)
        def _():
            dsc_ref[...] = jnp.zeros_like(dsc_ref)
            dsp_ref[...] = jnp.zeros_like(dsp_ref)

        ks[0:halo, :] = kvp_ref[0].astype(F32)
        ks[halo:, :] = kvc_ref[0].astype(F32)
        vs[0:halo, :] = kvp_ref[1].astype(F32)
        vs[halo:, :] = kvc_ref[1].astype(F32)
        dos[...] = do_ref[...].astype(F32)
        dks[...] = jnp.zeros_like(dks)
        dvs[...] = jnp.zeros_like(dvs)
        mask_c, mask_p0 = _attn_masks()
        bc = bc_ref[0]
        bp = bp_ref[0]
        for jj in range(nblk):
            mask_p = mask_p0 if jj > 0 else mask_p0 & (sb > 0)
            for r in range(dil):
                cur = _rows(halo + jj * halo + r, dil)
                prv = _rows(jj * halo + r, dil)
                own = _rows(jj * halo + r, dil)
                q = q_ref[own, :].astype(BF16)
                do = dos[own, :].astype(BF16)
                lse = l_ref[own, :]
                dlt = d_ref[own, :]
                kc = ks[cur, :].astype(BF16)
                kp = ks[prv, :].astype(BF16)
                vc = vs[cur, :].astype(BF16)
                vp = vs[prv, :].astype(BF16)
                p_c = jnp.where(mask_c, jnp.exp(_dot_nt(q, kc) * scale + bc - lse), 0.0)
                p_p = jnp.where(mask_p, jnp.exp(_dot_nt(q, kp) * scale + bp - lse), 0.0)
                ds_c = p_c * (_dot_nt(do, vc) - dlt)
                ds_p = p_p * (_dot_nt(do, vp) - dlt)
                dsc_ref[0] += ds_c
                dsp_ref[0] += ds_p
                ds_cb = ds_c.astype(BF16)
                ds_pb = ds_p.astype(BF16)
                dqs[own, :] = (jnp.dot(ds_cb, kc, preferred_element_type=F32)
                               + jnp.dot(ds_pb, kp, preferred_element_type=F32)) * scale
                dks[cur, :] += _dot_tn(ds_cb, q) * scale
                dks[prv, :] += _dot_tn(ds_pb, q) * scale
                dvs[cur, :] += _dot_tn(p_c.astype(BF16), do)
                dvs[prv, :] += _dot_tn(p_p.astype(BF16), do)

        @pl.when(i > 0)
        def _():
            dks[rows:rows + halo, :] += carry_k[...]
            dvs[rows:rows + halo, :] += carry_v[...]

        dkv_ref[0] = dks[halo:, :].astype(BF16)
        dkv_ref[1] = dvs[halo:, :].astype(BF16)
        carry_k[...] = dks[0:halo, :]
        carry_v[...] = dvs[0:halo, :]
        dq_ref[...] = dqs[...].astype(BF16)

    per = rows // halo

    def rev(i):
        return nsb - 1 - i

    bias_spec = pl.BlockSpec((1, BLOCK, BLOCK), lambda h, i: (h, 0, 0))
    row_spec = pl.BlockSpec((rows, HEAD_DIM), lambda h, i: (rev(i), h))
    in_specs = [pl.BlockSpec((rows, HEAD_DIM), lambda h, i: (rev(i), g * H + h)),
                row_spec, row_spec, row_spec,
                pl.BlockSpec((2, rows, HEAD_DIM), lambda h, i: (0, rev(i), g * H + h)),
                pl.BlockSpec((2, halo, HEAD_DIM),
                             lambda h, i: (0, jnp.maximum(rev(i) * per - 1, 0), g * H + h)),
                bias_spec, bias_spec, ANY_SPEC]
    args = [qz, do, lse, delta, kv, kv, bias_c, bias_p, dqz]
    aliases = {8: 0}
    if have_dkv:
        in_specs.append(ANY_SPEC)
        args.append(dkv)
        aliases[9] = 1
    blk = (halo + rows, HEAD_DIM)
    return _call(body, name=name, grid=(H, nsb), in_specs=in_specs,
                 out_specs=[pl.BlockSpec((rows, HEAD_DIM), lambda h, i: (rev(i), g * H + h)),
                            pl.BlockSpec((2, rows, HEAD_DIM), lambda h, i: (0, rev(i), g * H + h)),
                            bias_spec, bias_spec],
                 out_shape=[jax.ShapeDtypeStruct(qz.shape, BF16),
                            jax.ShapeDtypeStruct((2, S, N_GROUPS * D), BF16),
                            jax.ShapeDtypeStruct((H, BLOCK, BLOCK), F32),
                            jax.ShapeDtypeStruct((H, BLOCK, BLOCK), F32)],
                 scratch_shapes=[pltpu.VMEM(blk, F32), pltpu.VMEM(blk, F32), pltpu.VMEM(blk, F32),
                                 pltpu.VMEM(blk, F32),
                                 pltpu.VMEM((rows, HEAD_DIM), F32), pltpu.VMEM((rows, HEAD_DIM), F32),
                                 pltpu.VMEM((halo, HEAD_DIM), F32), pltpu.VMEM((halo, HEAD_DIM), F32)],
                 aliases=aliases, args=args, sem=("parallel", "arbitrary"), jobs=jobs)


def _adamw(w, g, m, v):
    m = ADAM_B1 * m + (1.0 - ADAM_B1) * g
    v = ADAM_B2 * v + (1.0 - ADAM_B2) * (g * g)
    m_hat = m / (1.0 - ADAM_B1 ** ADAM_STEP)
    v_hat = v / (1.0 - ADAM_B2 ** ADAM_STEP)
    delta = -ADAM_LR * (m_hat / (jnp.sqrt(v_hat) + ADAM_EPS) + ADAM_WD * w)
    return delta, m, v


def add_pairs(dw, got, name, jobs=()):
    _, K, n = got.shape
    tk = _tile(K, 512, 16)

    def own_block(r, i):
        x, y, c = _place()
        return 4 * ((x + r // 2) % 2) + 2 * ((y + r % 2) % 2) + c, i, 0

    def body(a_ref, b_ref, o_ref):
        o_ref[...] = (a_ref[...].astype(F32) + b_ref[...].astype(F32)).astype(o_ref.dtype)

    blk = pl.BlockSpec((None, tk, n), lambda r, i: (r, i, 0))
    return _call(body, name=name, grid=(N_CHIPS, K // tk),
                 in_specs=[pl.BlockSpec((None, tk, n), own_block), blk], out_specs=[blk],
                 out_shape=[jax.ShapeDtypeStruct(got.shape, got.dtype)], args=[dw, got],
                 sem=("parallel", "parallel"), jobs=jobs)[0]


def adamw_reduce(part, got, w, m, v, name, jobs=()):
    K, n = w.shape
    tk = _tile(K, 256, 8)

    def body(p_ref, r_ref, w_ref, m_ref, v_ref, g_ref, d_ref, nm_ref, nv_ref):
        g = p_ref[...].astype(F32)
        for r in range(N_CHIPS - 1):
            g = g + r_ref[r].astype(F32)
        d, nm, nv = _adamw(w_ref[...], g, m_ref[...], v_ref[...])
        g_ref[...] = g
        d_ref[...] = d
        nm_ref[...] = nm
        nv_ref[...] = nv

    blk = pl.BlockSpec((tk, n), lambda i: (i, 0))
    return _call(body, name=name, grid=(K // tk,),
                 in_specs=[pl.BlockSpec((None, tk, n), lambda i: (0, i, 0)),
                           pl.BlockSpec((N_CHIPS - 1, tk, n), lambda i: (0, i, 0)), blk, blk, blk],
                 out_specs=[blk] * 4, out_shape=[jax.ShapeDtypeStruct((K, n), F32)] * 4,
                 args=[part, got, w, m, v], sem=("parallel",), jobs=jobs)


def sum_parts(parts, name):
    _, R, D = parts.shape

    def body(p_ref, o_ref):
        g = p_ref[0]
        for r in range(1, N_DEV):
            g = g + p_ref[r]
        o_ref[...] = g

    return _call(body, name=name, in_specs=[VMEM_SPEC], out_specs=[VMEM_SPEC],
                 out_shape=[jax.ShapeDtypeStruct((R, D), F32)], args=[parts])[0]


def adamw_small(w, g, m, v, name):
    def body(w_ref, g_ref, m_ref, v_ref, d_ref, nm_ref, nv_ref):
        d, nm, nv = _adamw(w_ref[...], g_ref[...], m_ref[...], v_ref[...])
        d_ref[...] = d
        nm_ref[...] = nm
        nv_ref[...] = nv

    return _call(body, name=name, in_specs=[VMEM_SPEC] * 4, out_specs=[VMEM_SPEC] * 3,
                 out_shape=[jax.ShapeDtypeStruct(w.shape, F32)] * 3, args=[w, g, m, v])


def _row(v, at):
    return jnp.pad(v.reshape(1, -1), ((at, 7 - at), (0, 0)))


def _pack_sharded(norm, conv_w, conv_b, ln_g, ln_b):
    n = norm.shape[-1]
    taps = jnp.pad(conv_w.reshape(CONV_WIDTH, n), ((0, HALO - CONV_WIDTH), (0, 0)))
    return jnp.concatenate([_row(norm, 0), _row(ln_g, 0) + _row(ln_b, 1) + _row(conv_b, 2), taps], axis=0)


def _pack_rel(rel_bias, D):
    return jnp.pad(rel_bias.reshape(1, -1), ((0, 7), (0, D - rel_bias.size)))


def _pack_replicated(kv_norm, b_norm, final_norm, rel_bias, D):
    return jnp.concatenate([_row(kv_norm, 0) + _row(b_norm, 1), _row(final_norm, 0), _pack_rel(rel_bias, D)], axis=0)


def kernel(x, a_norm, a_w_in, a_conv_w, a_conv_b, a_ln_g, a_ln_b, a_w_out, kv_norm, w_kv, b_norm, b_w_in, b_w_out, rel_bias, final_norm, loss_target, m_a_norm, m_a_w_in, m_a_conv_w, m_a_conv_b, m_a_ln_g, m_a_ln_b, m_a_w_out, m_kv_norm, m_w_kv, m_b_norm, m_b_w_in, m_b_w_out, m_rel_bias, m_final_norm, v_a_norm, v_a_w_in, v_a_conv_w, v_a_conv_b, v_a_ln_g, v_a_ln_b, v_a_w_out, v_kv_norm, v_w_kv, v_b_norm, v_b_w_in, v_b_w_out, v_rel_bias, v_final_norm):
    _, S, D = x.shape
    E = D
    H = D // HEAD_DIM
    nsh = D // N_DEV
    x0 = x.reshape(S, D)
    target = loss_target.reshape(S, D)
    kvn, bn, fn = kv_norm.reshape(1, D), b_norm.reshape(1, D), final_norm.reshape(1, D)

    names = ["a_w_in", "a_w_out", "w_kv", "b_w_in", "b_w_out"]
    big_w = dict(zip(names, [a_w_in[0], a_w_out[0], w_kv, b_w_in[0], b_w_out[0]]))
    big_m = dict(zip(names, [m_a_w_in[0], m_a_w_out[0], m_w_kv, m_b_w_in[0], m_b_w_out[0]]))
    big_v = dict(zip(names, [v_a_w_in[0], v_a_w_out[0], v_w_kv, v_b_w_in[0], v_b_w_out[0]]))
    shard = {nm: w.astype(BF16) for nm, w in big_w.items()}
    sh_w = _pack_sharded(a_norm, a_conv_w, a_conv_b, a_ln_g, a_ln_b)

    wa_in, sh_all = all_gather([shard["a_w_in"], sh_w], "gather_first")
    sh_full = sh_all.transpose(1, 0, 2).reshape(SMALL_SH_ROWS, D)
    an, cw, cb = sh_full[0:1], sh_full[16:16 + HALO], sh_full[10:11]
    lg, lb = sh_full[8:9], sh_full[9:10]

    (h0,) = rms_fwd(x0, [an], "rms_a")
    g_a_out = gather_chips_job(shard["a_w_out"])
    proj = matmul_nn(h0, wa_in, "a_in", jobs=[g_a_out])
    g_a_out2 = gather_sibling_job(g_a_out.results[0])
    g_kv = gather_chips_job(shard["w_kv"])
    c, y = conf_fwd(proj, cw, cb, lg, lb, "conf_fwd", jobs=[g_a_out2, g_kv])
    wa_out = g_a_out2.results[0].reshape(1, E, D)
    g_kv2 = gather_sibling_job(g_kv.results[0])
    x1 = matmul_nn(y, wa_out, "a_out", res=x0, jobs=[g_kv2])
    wkv = g_kv2.results[0]
    hk, hb = rms_fwd(x1, [kvn, bn], "rms_b")
    g_b_in = gather_chips_job(shard["b_w_in"])
    kv = matmul_nn(hk, wkv, "kv_proj", out_dtype=BF16, kv_split=True, jobs=[g_b_in])

    bt = jnp.asarray(bucket_tables())
    onehot = (bt[None, :] == jnp.arange(N_BUCKETS, dtype=jnp.int32)[:, None]).astype(F32)
    g_b_in2 = gather_sibling_job(g_b_in.results[0])
    bias = small_dot(rel_bias.T, onehot, "nn", "bias_table", jobs=[g_b_in2])
    bias = bias.reshape(H, 2 * N_GROUPS, BLOCK, BLOCK)
    bias_c = [bias[:, 2 * g] for g in range(N_GROUPS)]
    bias_p = [bias[:, 2 * g + 1] for g in range(N_GROUPS)]
    wb_in = g_b_in2.results[0]
    g_b_out = gather_chips_job(shard["b_w_out"])
    qz = matmul_nn(hb, wb_in, "b_in", jobs=[g_b_out])
    g_b_out2 = gather_sibling_job(g_b_out.results[0])

    os_, lses = [], []
    for g, (_, dil) in enumerate(GROUPS):
        o_g, l_g = attn_fwd(qz, kv, bias_c[g], bias_p[g], g, dil, "attn_fwd%d" % g,
                            jobs=[g_b_out2] if g == 0 else ())
        os_.append(o_g)
        lses.append(l_g)
    wb_out = g_b_out2.results[0].reshape(1, D, D)
    o, lse, y2 = merge_fwd(os_, lses, qz, "merge_fwd")
    x2 = matmul_nn(y2, wb_out, "b_out", res=x1)
    dx2, dx2b, fin_acc = final_loss(x2, fn, target, "final_loss")

    dy2 = matmul_nt(dx2b, wb_out, "b_out_dx")
    dwb_out = matmul_tn(y2, dx2b, 1, "b_out_dw").reshape(N_DEV, D // N_DEV, D)
    r_b_out = reduce_sibling_job(dwb_out)
    do, delta, dqz = merge_bwd(dy2, o, qz, "merge_bwd", jobs=[r_b_out])
    p_b_out = add_pairs(dwb_out, r_b_out.results[0], "pairs_b_out")
    r_b_out2 = reduce_chips_job(p_b_out)
    dkv = None
    ds_tabs = []
    for g, (_, dil) in enumerate(GROUPS):
        dqz, dkv, ds_c, ds_p = attn_bwd(qz, kv, do, lse, delta, bias_c[g], bias_p[g], dqz, dkv, g, dil,
                                        "attn_bwd%d" % g, jobs=[r_b_out2] if g == 0 else ())
        ds_tabs += [ds_c.reshape(H, BLOCK * BLOCK), ds_p.reshape(H, BLOCK * BLOCK)]
    d_rel = small_dot(onehot, jnp.concatenate(ds_tabs, axis=1), "nt", "bias_grad")
    dhb = matmul_nt(dqz, wb_in, "b_in_dx")
    dwb_in = matmul_tn(hb, dqz, N_DEV, "b_in_dw")
    r_b_in = reduce_sibling_job(dwb_in)
    dhk = matmul_nt(dkv, wkv, "kv_dx", kv_split=True, jobs=[r_b_in])
    p_b_in = add_pairs(dwb_in, r_b_in.results[0], "pairs_b_in")
    r_b_in2 = reduce_chips_job(p_b_in)
    dw_kv = matmul_tn(hk, dkv, N_DEV, "kv_dw", kv_split=True, jobs=[r_b_in2])
    r_kv = reduce_sibling_job(dw_kv)
    dx1, dx1b, norm_acc = rms_bwd(x1, [dhk, dhb], [kvn, bn], dx2, "rms_b_bwd", True, jobs=[r_kv])
    p_kv = add_pairs(dw_kv, r_kv.results[0], "pairs_kv")
    r_kv2 = reduce_chips_job(p_kv)

    dy = matmul_nt(dx1b, wa_out, "a_out_dx")
    dwa_out = matmul_tn(y, dx1b, 1, "a_out_dw").reshape(N_DEV, E // N_DEV, D)
    r_a_out = reduce_sibling_job(dwa_out)
    dc, dz, ln_acc = conf_bwd_ln(c, dy, proj, lg, lb, "conf_bwd_ln", jobs=[r_a_out])
    p_a_out = add_pairs(dwa_out, r_a_out.results[0], "pairs_a_out")
    r_a_out2 = reduce_chips_job(p_a_out)
    dproj, dcw = conf_bwd_conv(proj, dc, dz, cw, "conf_bwd_conv", jobs=[r_kv2, r_a_out2])
    dwa_in = matmul_tn(h0, dproj, N_DEV, "a_in_dw")
    r_a_in = reduce_sibling_job(dwa_in)
    dh0 = matmul_nt(dproj, wa_in, "a_in_dx", jobs=[r_a_in])
    p_a_in = add_pairs(dwa_in, r_a_in.results[0], "pairs_a_in")
    r_a_in2 = reduce_chips_job(p_a_in)
    grad_x, a_acc = rms_bwd(x0, [dh0], [an], dx1, "rms_a_bwd", False, jobs=[r_a_in2])

    small_rows = jnp.concatenate([a_acc, ln_acc, dcw, norm_acc, fin_acc, _pack_rel(d_rel, D)], axis=0)
    share = share_small_job(small_rows)
    sums = {"w_kv": (p_kv, r_kv2), "b_w_in": (p_b_in, r_b_in2), "b_w_out": (p_b_out, r_b_out2),
            "a_w_out": (p_a_out, r_a_out2), "a_w_in": (p_a_in, r_a_in2)}
    big_out = {}
    for nm, (part, job) in sums.items():
        big_out[nm] = adamw_reduce(part, job.results[0], big_w[nm], big_m[nm], big_v[nm], "adamw_" + nm,
                                   jobs=[share] if nm == "b_w_out" else ())

    gsum = sum_parts(share.results[0], "sum_small")
    me = 4 * lax.axis_index("x") + 2 * lax.axis_index("y") + lax.axis_index("c")
    g_sh = lax.dynamic_slice(gsum, (0, me * nsh), (SMALL_SH_ROWS, nsh))
    g_rep = gsum[SMALL_SH_ROWS:]
    loss = g_rep[9, 0]
    sh_m = _pack_sharded(m_a_norm, m_a_conv_w, m_a_conv_b, m_a_ln_g, m_a_ln_b)
    sh_v = _pack_sharded(v_a_norm, v_a_conv_w, v_a_conv_b, v_a_ln_g, v_a_ln_b)
    sh_d, sh_nm, sh_nv = adamw_small(sh_w, g_sh, sh_m, sh_v, "adamw_sharded")
    rep_w = _pack_replicated(kv_norm, b_norm, final_norm, rel_bias, D)
    rep_m = _pack_replicated(m_kv_norm, m_b_norm, m_final_norm, m_rel_bias, D)
    rep_v = _pack_replicated(v_kv_norm, v_b_norm, v_final_norm, v_rel_bias, D)
    rep_d, rep_nm, rep_nv = adamw_small(rep_w, g_rep, rep_m, rep_v, "adamw_replicated")

    def unpack(kind):
        sh = (g_sh, sh_d, sh_nm, sh_nv)[kind]
        rep = (g_rep, rep_d, rep_nm, rep_nv)[kind]
        big = {nm: big_out[nm][kind] for nm in names}
        nb = rel_bias.size
        return [
            sh[0:1],
            big["a_w_in"][None],
            sh[16:16 + CONV_WIDTH][None],
            sh[10:11], sh[8:9], sh[9:10],
            big["a_w_out"][None],
            rep[0],
            big["w_kv"],
            rep[1:2],
            big["b_w_in"][None],
            big["b_w_out"][None],
            rep[16, :nb].reshape(rel_bias.shape),
            rep[8],
        ]

    return (loss, grad_x.reshape(1, S, D), *unpack(0), *unpack(1), *unpack(2), *unpack(3))
```

```python
import functools
import math

import numpy as np
import jax
import jax.numpy as jnp
from jax import lax
from jax.experimental import pallas as pl
from jax.experimental.pallas import tpu as pltpu

F32 = jnp.float32
BF16 = jnp.bfloat16

N_DEV = 8
N_CHIPS = 4
EPS = 1e-6
HEAD_DIM = 128
BLOCK = 128
GROUPS = ((128, 1), (512, 4), (2048, 16))
N_GROUPS = len(GROUPS)
CONV_WIDTH = 31
HALO = 32
N_BUCKETS = 32
MAX_EXACT = N_BUCKETS // 2
MAX_DISTANCE = 2048
V7X_VMEM_BYTES = 64 * 1024 * 1024
VMEM_LIMIT = (V7X_VMEM_BYTES * 7) // 8
MATMUL_VMEM_BUDGET = (V7X_VMEM_BYTES * 11) // 16
LANE = 128

ADAM_LR = 0.001
ADAM_B1 = 0.9
ADAM_B2 = 0.999
ADAM_EPS = 1e-08
ADAM_WD = 0.01
ADAM_STEP = 10

SMALL_SH_ROWS = 48
SMALL_REP_ROWS = 24
MESH = pl.DeviceIdType.MESH
ANY_SPEC = pl.BlockSpec(memory_space=pl.ANY)
VMEM_SPEC = pl.BlockSpec(memory_space=pltpu.VMEM)


def _tile(dim, pref, unit=LANE):
    if dim <= pref:
        return dim
    t = (pref // unit) * unit
    while dim % t:
        t -= unit
    assert t > 0
    return t


def _sigmoid(v):
    return jax.nn.sigmoid(v)


def _place():
    return lax.axis_index("x"), lax.axis_index("y"), lax.axis_index("c")


def _flip(v, bit):
    return 1 - v if bit else v


class Job:
    def __init__(self, srcs, dsts, n_sems, build):
        self.srcs, self.dsts, self.n_sems, self.build = list(srcs), list(dsts), n_sems, build
        self.results = None


def _remote(src, dst, send_sems, recv_sems, k, peer):
    return pltpu.make_async_remote_copy(src_ref=src, dst_ref=dst, send_sem=send_sems.at[k],
                                        recv_sem=recv_sems.at[k], device_id=peer, device_id_type=MESH)


def _call(body, *, name, in_specs, out_specs, out_shape, args, grid=(), scratch_shapes=(), sem=(),
          aliases=None, jobs=()):
    n_in, n_out, n_scr = len(in_specs), len(out_specs), len(scratch_shapes)
    aliases = dict(aliases or {})
    x_in, x_out, x_scr = [], [], []
    for job in jobs:
        job.in_at = n_in + len(x_in)
        x_in += job.srcs
        job.out_at = n_out + len(x_out)
        for d in job.dsts:
            if not isinstance(d, jax.ShapeDtypeStruct):
                aliases[n_in + len(x_in)] = n_out + len(x_out)
                x_in.append(d)
            x_out.append(jax.ShapeDtypeStruct(d.shape, d.dtype))
        job.scr_at = n_scr + len(x_scr)
        x_scr += [pltpu.SemaphoreType.DMA((job.n_sems,))] * 3

    def wrapped(*refs):
        ins = refs[:n_in + len(x_in)]
        outs = refs[len(ins):len(ins) + n_out + len(x_out)]
        scr = refs[len(ins) + len(outs):]
        core = ins[:n_in] + outs[:n_out] + scr[:n_scr]
        if not jobs:
            body(*core)
            return
        copies = []
        for job in jobs:
            copies += job.build(ins[job.in_at:job.in_at + len(job.srcs)],
                                outs[job.out_at:job.out_at + len(job.dsts)],
                                *scr[job.scr_at:job.scr_at + 3])
        if grid:
            pids = [pl.program_id(d) for d in range(len(grid))]
            first = functools.reduce(jnp.logical_and, [p == 0 for p in pids])
            last = functools.reduce(jnp.logical_and, [p == g - 1 for p, g in zip(pids, grid)])

            @pl.when(first)
            def _():
                for cp in copies:
                    cp.start()

            body(*core)

            @pl.when(last)
            def _():
                for cp in copies:
                    cp.wait()
        else:
            for cp in copies:
                cp.start()
            body(*core)
            for cp in copies:
                cp.wait()

    if jobs:
        sem = ("arbitrary",) * len(grid)
    kwargs = dict(grid=grid) if grid else {}
    if aliases:
        kwargs["input_output_aliases"] = aliases
    outs = pl.pallas_call(
        wrapped, name=name,
        in_specs=list(in_specs) + [ANY_SPEC] * len(x_in),
        out_specs=list(out_specs) + [ANY_SPEC] * len(x_out),
        out_shape=list(out_shape) + x_out,
        scratch_shapes=list(scratch_shapes) + x_scr,
        compiler_params=pltpu.CompilerParams(dimension_semantics=sem if sem else None,
                                             vmem_limit_bytes=VMEM_LIMIT),
        **kwargs,
    )(*args, *x_in)
    for job in jobs:
        job.results = list(outs[job.out_at:job.out_at + len(job.dsts)])
    return list(outs[:n_out])


def gather_chips_job(shard):
    def build(srcs, dsts, send, recv, loc):
        (src,), (out,) = srcs, dsts
        x, y, c = _place()
        mine = out.at[4 * x + 2 * y + c]
        peers = [(x, y, 1 - c), (1 - x, y, c), (x, 1 - y, c), (1 - x, 1 - y, c)]
        return ([pltpu.make_async_copy(src, mine, loc.at[0])]
                + [_remote(src, mine, send, recv, k, p) for k, p in enumerate(peers)])

    return Job([shard], [jax.ShapeDtypeStruct((N_DEV,) + shard.shape, shard.dtype)], 4, build)


def gather_sibling_job(buf):
    def build(srcs, dsts, send, recv, loc):
        (out,) = dsts
        x, y, c = _place()
        copies = []
        for k, (cx, cy) in enumerate([(1 - x, y), (x, 1 - y), (1 - x, 1 - y)]):
            blk = out.at[4 * cx + 2 * cy + c]
            copies.append(_remote(blk, blk, send, recv, k, (x, y, 1 - c)))
        return copies

    return Job([], [buf], 3, build)


def reduce_sibling_job(dw):
    def build(srcs, dsts, send, recv, loc):
        (src,), (got,) = srcs, dsts
        x, y, c = _place()
        return [_remote(src.at[4 * _flip(x, r & 2) + 2 * _flip(y, r & 1) + 1 - c], got.at[r], send, recv, r,
                        (x, y, 1 - c)) for r in range(N_CHIPS)]

    return Job([dw], [jax.ShapeDtypeStruct((N_CHIPS,) + dw.shape[1:], dw.dtype)], N_CHIPS, build)


def reduce_chips_job(part):
    def build(srcs, dsts, send, recv, loc):
        (src,), (got,) = srcs, dsts
        x, y, c = _place()
        return [_remote(src.at[r], got.at[r - 1], send, recv, r - 1, (_flip(x, r & 2), _flip(y, r & 1), c))
                for r in range(1, N_CHIPS)]

    return Job([part], [jax.ShapeDtypeStruct((N_CHIPS - 1,) + part.shape[1:], part.dtype)], N_CHIPS - 1, build)


def share_small_job(small):
    def build(srcs, dsts, send, recv, loc):
        (src,), (out,) = srcs, dsts
        x, y, c = _place()
        mine = out.at[4 * x + 2 * y + c]
        copies = [pltpu.make_async_copy(src, mine, loc.at[0])]
        for rel in range(1, N_DEV):
            peer = (_flip(x, rel & 4), _flip(y, rel & 2), _flip(c, rel & 1))
            copies.append(_remote(src, mine, send, recv, rel - 1, peer))
        return copies

    return Job([small], [jax.ShapeDtypeStruct((N_DEV,) + small.shape, small.dtype)], N_DEV - 1, build)


def all_gather(shards, name):
    n = len(shards)

    def body(*refs):
        ins, outs = refs[:n], refs[n:2 * n]
        send_sems, recv_sems, local_sems = refs[2 * n:]
        x, y, c = _place()
        me, sibling = (x, y, c), (x, y, 1 - c)
        chips = [(1 - x, y), (x, 1 - y), (1 - x, 1 - y)]

        def slot(a, dev):
            return outs[a].at[4 * dev[0] + 2 * dev[1] + dev[2]]

        def copy(a, k, block, to, src=None):
            return pltpu.make_async_remote_copy(
                src_ref=slot(a, block) if src is None else src, dst_ref=slot(a, block),
                send_sem=send_sems.at[a, k], recv_sem=recv_sems.at[a, k],
                device_id=to, device_id_type=MESH)

        mine, first, passed = [], [], []
        for a in range(n):
            mine.append(pltpu.make_async_copy(ins[a], slot(a, me), local_sems.at[a]))
            mine[a].start()
            first.append([copy(a, 0, me, sibling, src=ins[a])]
                         + [copy(a, 1 + j, me, (*chip, c), src=ins[a]) for j, chip in enumerate(chips)])
            for cp in first[a]:
                cp.start()
        for a in range(n):
            passed.append([copy(a, 4 + j, (*chip, c), sibling) for j, chip in enumerate(chips)])
            for j, chip in enumerate(chips):
                copy(a, 1 + j, (*chip, c), me).wait_recv()
                passed[a][j].start()
        for a in range(n):
            copy(a, 0, sibling, me).wait_recv()
            for j, chip in enumerate(chips):
                copy(a, 4 + j, (*chip, 1 - c), me).wait_recv()
            for cp in first[a] + passed[a]:
                cp.wait_send()
            mine[a].wait()

    return pl.pallas_call(
        body, name=name,
        in_specs=[ANY_SPEC] * n, out_specs=[ANY_SPEC] * n,
        out_shape=[jax.ShapeDtypeStruct((N_DEV,) + s.shape, s.dtype) for s in shards],
        scratch_shapes=[pltpu.SemaphoreType.DMA((n, 7)), pltpu.SemaphoreType.DMA((n, 7)),
                        pltpu.SemaphoreType.DMA((n,))],
    )(*shards)


def _kv_split_index(tw, D):
    pd = D // tw

    def index(j):
        return (j // pd) % 2, (j // (2 * pd)) * pd + j % pd

    return index


def _col_tile(n, also, pref):
    t = (min(pref, n) // LANE) * LANE
    while n % t or (also is not None and also % t):
        t -= LANE
    assert t > 0
    return t


def matmul_nn(a, w, name, res=None, out_dtype=F32, kv_split=False, jobs=()):
    M, K = a.shape
    nb, _, n = w.shape
    D = nb * n // (2 * N_GROUPS)
    tn = _col_tile(n, D if kv_split else None, 1024)
    out_bytes = jnp.dtype(out_dtype).itemsize + (4 if res is not None else 0)
    tm = _tile(M, 2048)
    if 2 * (tm * K * 2 + K * tn * 2 + tm * tn * out_bytes) > MATMUL_VMEM_BUDGET:
        tm = _tile(M, 1024)
    per = n // tn

    def body(*refs):
        if res is None:
            a_ref, w_ref, o_ref = refs
        else:
            a_ref, w_ref, r_ref, o_ref = refs
        acc = jnp.dot(a_ref[...], w_ref[...], preferred_element_type=F32)
        if res is not None:
            acc = r_ref[...] + acc
        o_ref[...] = acc.astype(o_ref.dtype)

    in_specs = [pl.BlockSpec((tm, K), lambda i, j: (i, 0)),
                pl.BlockSpec((None, K, tn), lambda i, j: (j // per, 0, j % per))]
    args = [a, w]
    if res is not None:
        in_specs.append(pl.BlockSpec((tm, tn), lambda i, j: (i, j)))
        args.append(res)
    if kv_split:
        split = _kv_split_index(tn, D)
        out_spec = pl.BlockSpec((None, tm, tn), lambda i, j: (split(j)[0], i, split(j)[1]))
        out_shape = jax.ShapeDtypeStruct((2, M, N_GROUPS * D), out_dtype)
    else:
        out_spec = pl.BlockSpec((tm, tn), lambda i, j: (i, j))
        out_shape = jax.ShapeDtypeStruct((M, nb * n), out_dtype)
    return _call(body, name=name, grid=(M // tm, nb * per), in_specs=in_specs, out_specs=[out_spec],
                 out_shape=[out_shape], args=args, sem=("parallel", "parallel"), jobs=jobs)[0]


def matmul_nt(dy, w, name, kv_split=False, jobs=()):
    M = dy.shape[-2]
    nb, K, n = w.shape
    D = nb * n // (2 * N_GROUPS)
    tm = _tile(M, 1024)
    tc = _col_tile(n, D if kv_split else None, 1024)
    per = n // tc
    pair = 2 if (nb * per) % 2 == 0 else 1

    def body(*refs):
        o_ref = refs[-1]
        j = pl.program_id(1)
        part = None
        for u in range(pair):
            d = lax.dot_general(refs[u][...], refs[pair + u][...], (((1,), (1,)), ((), ())),
                                preferred_element_type=F32)
            part = d if part is None else part + d

        @pl.when(j == 0)
        def _():
            o_ref[...] = part

        @pl.when(j > 0)
        def _():
            o_ref[...] += part

    def dy_spec(u):
        if kv_split:
            split = _kv_split_index(tc, D)
            return pl.BlockSpec((None, tm, tc),
                                lambda i, j: (split(pair * j + u)[0], i, split(pair * j + u)[1]))
        return pl.BlockSpec((tm, tc), lambda i, j: (i, pair * j + u))

    def w_spec(u):
        return pl.BlockSpec((None, K, tc), lambda i, j: ((pair * j + u) // per, 0, (pair * j + u) % per))

    return _call(body, name=name, grid=(M // tm, nb * per // pair),
                 in_specs=[dy_spec(u) for u in range(pair)] + [w_spec(u) for u in range(pair)],
                 out_specs=[pl.BlockSpec((tm, K), lambda i, j: (i, 0))],
                 out_shape=[jax.ShapeDtypeStruct((M, K), F32)], args=[dy] * pair + [w] * pair,
                 sem=("parallel", "arbitrary"), jobs=jobs)[0]


def matmul_tn(a, dy, nb, name, out_dtype=BF16, kv_split=False, jobs=()):
    M, K = a.shape
    N = 2 * dy.shape[-1] if kv_split else dy.shape[-1]
    n = N // nb
    D = N // (2 * N_GROUPS)
    tn = _col_tile(n, D if kv_split else None, 1024)
    per = n // tn
    tk = _tile(K, 1024)

    def body(a_ref, dy_ref, o_ref):
        o_ref[...] = lax.dot_general(a_ref[...], dy_ref[...], (((0,), (0,)), ((), ())),
                                     preferred_element_type=F32).astype(o_ref.dtype)

    if kv_split:
        split = _kv_split_index(tn, D)
        dy_spec = pl.BlockSpec((None, M, tn), lambda k, j: (split(j)[0], 0, split(j)[1]))
    else:
        dy_spec = pl.BlockSpec((M, tn), lambda k, j: (0, j))
    return _call(body, name=name, grid=(K // tk, nb * per),
                 in_specs=[pl.BlockSpec((M, tk), lambda k, j: (0, k)), dy_spec],
                 out_specs=[pl.BlockSpec((None, tk, tn), lambda k, j: (j // per, k, j % per))],
                 out_shape=[jax.ShapeDtypeStruct((nb, K, n), out_dtype)], args=[a, dy],
                 sem=("parallel", "parallel"), jobs=jobs)[0]


def small_dot(a, b, contract, name, jobs=()):
    if contract == "nn":
        dims = (((1,), (0,)), ((), ()))
        out = (a.shape[0], b.shape[1])
    else:
        dims = (((1,), (1,)), ((), ()))
        out = (a.shape[0], b.shape[0])

    def body(a_ref, b_ref, o_ref):
        o_ref[...] = lax.dot_general(a_ref[...], b_ref[...], dims, precision=lax.Precision.HIGHEST,
                                     preferred_element_type=F32)

    return _call(body, name=name, in_specs=[VMEM_SPEC, VMEM_SPEC], out_specs=[VMEM_SPEC],
                 out_shape=[jax.ShapeDtypeStruct(out, F32)], args=[a, b], jobs=jobs)[0]


def rms_fwd(x, gains, name, jobs=()):
    S, D = x.shape
    T = _tile(S, 512, 8)
    n = len(gains)

    def body(x_ref, *refs):
        xv = x_ref[...]
        xn = xv * lax.rsqrt(jnp.mean(xv * xv, axis=-1, keepdims=True) + EPS)
        for g_ref, o_ref in zip(refs[:n], refs[n:]):
            o_ref[...] = (xn * g_ref[...]).astype(o_ref.dtype)

    row = pl.BlockSpec((T, D), lambda i: (i, 0))
    vec = pl.BlockSpec((1, D), lambda i: (0, 0))
    return _call(body, name=name, grid=(S // T,), in_specs=[row] + [vec] * n, out_specs=[row] * n,
                 out_shape=[jax.ShapeDtypeStruct((S, D), BF16)] * n, args=[x, *gains],
                 sem=("parallel",), jobs=jobs)


def rms_bwd(x, dhs, gains, dres, name, want_bf16, jobs=()):
    S, D = x.shape
    T = _tile(S, 256, 8)
    n = len(gains)

    def body(x_ref, *refs):
        dh_refs = refs[:n]
        g_refs = refs[n:2 * n]
        dres_ref = refs[2 * n]
        outs = refs[2 * n + 1:]
        dx_ref, dg_ref = outs[0], outs[-1]
        i = pl.program_id(0)

        @pl.when(i == 0)
        def _():
            dg_ref[...] = jnp.zeros_like(dg_ref)

        xv = x_ref[...]
        r = lax.rsqrt(jnp.mean(xv * xv, axis=-1, keepdims=True) + EPS)
        xn = xv * r
        dxn = jnp.zeros_like(xv)
        for k in range(n):
            dh = dh_refs[k][...]
            dg_ref[k:k + 1, :] += jnp.sum(dh * xn, axis=0, keepdims=True)
            dxn = dxn + dh * g_refs[k][...]
        dx = dres_ref[...] + r * (dxn - xn * jnp.mean(dxn * xn, axis=-1, keepdims=True))
        dx_ref[...] = dx
        if want_bf16:
            outs[1][...] = dx.astype(BF16)

    row = pl.BlockSpec((T, D), lambda i: (i, 0))
    vec = pl.BlockSpec((1, D), lambda i: (0, 0))
    acc = pl.BlockSpec((8, D), lambda i: (0, 0))
    out_specs = [row] + ([row] if want_bf16 else []) + [acc]
    out_shape = ([jax.ShapeDtypeStruct((S, D), F32)]
                 + ([jax.ShapeDtypeStruct((S, D), BF16)] if want_bf16 else [])
                 + [jax.ShapeDtypeStruct((8, D), F32)])
    return _call(body, name=name, grid=(S // T,), in_specs=[row] + [row] * n + [vec] * n + [row],
                 out_specs=out_specs, out_shape=out_shape, args=[x, *dhs, *gains, dres],
                 sem=("arbitrary",), jobs=jobs)


def final_loss(x2, gain, target, name, jobs=()):
    S, D = x2.shape
    T = _tile(S, 256, 8)

    def body(x_ref, g_ref, t_ref, dx_ref, dxb_ref, acc_ref):
        i = pl.program_id(0)

        @pl.when(i == 0)
        def _():
            acc_ref[...] = jnp.zeros_like(acc_ref)

        xv = x_ref[...]
        g = g_ref[...]
        r = lax.rsqrt(jnp.mean(xv * xv, axis=-1, keepdims=True) + EPS)
        xn = xv * r
        err = xn * g - t_ref[...]
        dy = err * (1.0 / D)
        acc_ref[0:1, :] += jnp.sum(dy * xn, axis=0, keepdims=True)
        acc_ref[1:2, :] += jnp.full((1, D), 0.5 / D, F32) * jnp.sum(err * err)
        dxn = dy * g
        dx = r * (dxn - xn * jnp.mean(dxn * xn, axis=-1, keepdims=True))
        dx_ref[...] = dx
        dxb_ref[...] = dx.astype(BF16)

    row = pl.BlockSpec((T, D), lambda i: (i, 0))
    return _call(body, name=name, grid=(S // T,),
                 in_specs=[row, pl.BlockSpec((1, D), lambda i: (0, 0)), row],
                 out_specs=[row, row, pl.BlockSpec((8, D), lambda i: (0, 0))],
                 out_shape=[jax.ShapeDtypeStruct((S, D), F32), jax.ShapeDtypeStruct((S, D), BF16),
                            jax.ShapeDtypeStruct((8, D), F32)],
                 args=[x2, gain, target], sem=("arbitrary",), jobs=jobs)


ROW_CHUNK = 64
LANE_CHUNK = 512
SUBLANES = 8


def _shifted_copies(buf, sh_scr, l0, lc):
    n = buf.shape[0] - SUBLANES
    for b in range(1, SUBLANES):
        sh_scr[b - 1, 0:n, :] = buf[b:b + n, l0:l0 + lc]


def _shifted(buf, sh_scr, start, rows, l0, lc):
    a8, b = (start // SUBLANES) * SUBLANES, start % SUBLANES
    if b == 0:
        return buf[a8:a8 + rows, l0:l0 + lc]
    return sh_scr[b - 1, a8:a8 + rows, :]


def conf_fwd(proj, cw, cb, lg, lb, name, jobs=()):
    S, E3 = proj.shape
    E = E3 // 3
    T = _tile(S, 256, HALO)
    R = T // HALO
    lc = _tile(E, LANE_CHUNK)
    rc = min(ROW_CHUNK, T)

    def body(a_ref, b_ref, z_ref, ap_ref, bp_ref, cw_ref, cb_ref, lg_ref, lb_ref, c_ref, y_ref, u_scr, sh_scr):
        i = pl.program_id(0)
        up = ap_ref[...] * _sigmoid(bp_ref[...])
        u_scr[0:HALO, :] = jnp.where(i > 0, up, 0.0)
        u_scr[HALO:HALO + T, :] = a_ref[...] * _sigmoid(b_ref[...])
        off = HALO - (CONV_WIDTH - 1)
        for l0 in range(0, E, lc):
            _shifted_copies(u_scr, sh_scr, l0, lc)
            for r0 in range(0, T, rc):
                acc = jnp.broadcast_to(cb_ref[:, l0:l0 + lc], (rc, lc))
                for k in range(CONV_WIDTH):
                    acc = acc + _shifted(u_scr, sh_scr, r0 + off + k, rc, l0, lc) * cw_ref[k:k + 1, l0:l0 + lc]
                c_ref[r0:r0 + rc, l0:l0 + lc] = acc
        c = c_ref[...]
        mu = jnp.mean(c, axis=-1, keepdims=True)
        d = c - mu
        var = jnp.mean(d * d, axis=-1, keepdims=True)
        cn = d * lax.rsqrt(var + EPS) * lg_ref[...] + lb_ref[...]
        z = z_ref[...]
        y_ref[...] = ((cn * _sigmoid(cn)) * (z * _sigmoid(z))).astype(BF16)

    def col(j):
        return pl.BlockSpec((T, E), lambda i: (i, j))

    def prev(j):
        return pl.BlockSpec((HALO, E), lambda i: (jnp.maximum(i * R - 1, 0), j))

    vec = pl.BlockSpec((1, E), lambda i: (0, 0))
    return _call(body, name=name, grid=(S // T,),
                 in_specs=[col(0), col(1), col(2), prev(0), prev(1),
                           pl.BlockSpec((HALO, E), lambda i: (0, 0)), vec, vec, vec],
                 out_specs=[pl.BlockSpec((T, E), lambda i: (i, 0))] * 2,
                 out_shape=[jax.ShapeDtypeStruct((S, E), F32), jax.ShapeDtypeStruct((S, E), BF16)],
                 scratch_shapes=[pltpu.VMEM((HALO + T, E), F32), pltpu.VMEM((SUBLANES - 1, HALO + T, lc), F32)],
                 args=[proj, proj, proj, proj, proj, cw, cb, lg, lb], sem=("parallel",), jobs=jobs)


def conf_bwd_ln(c, dy, proj, lg, lb, name, jobs=()):
    S, E = c.shape
    T = _tile(S, 256, 8)

    def body(c_ref, dy_ref, z_ref, lg_ref, lb_ref, dc_ref, dz_ref, acc_ref):
        i = pl.program_id(0)

        @pl.when(i == 0)
        def _():
            acc_ref[...] = jnp.zeros_like(acc_ref)

        cv = c_ref[...]
        mu = jnp.mean(cv, axis=-1, keepdims=True)
        d = cv - mu
        var = jnp.mean(d * d, axis=-1, keepdims=True)
        rstd = lax.rsqrt(var + EPS)
        xh = d * rstd
        lgv = lg_ref[...]
        cn = xh * lgv + lb_ref[...]
        z = z_ref[...]
        dy = dy_ref[...]
        sc = _sigmoid(cn)
        sz = _sigmoid(z)
        dcn = dy * (z * sz) * (sc * (1.0 + cn * (1.0 - sc)))
        dz_ref[...] = (dy * (cn * sc) * (sz * (1.0 + z * (1.0 - sz)))).astype(BF16)
        acc_ref[0:1, :] += jnp.sum(dcn * xh, axis=0, keepdims=True)
        acc_ref[1:2, :] += jnp.sum(dcn, axis=0, keepdims=True)
        dxh = dcn * lgv
        dc = rstd * (dxh - jnp.mean(dxh, axis=-1, keepdims=True)
                     - xh * jnp.mean(dxh * xh, axis=-1, keepdims=True))
        acc_ref[2:3, :] += jnp.sum(dc, axis=0, keepdims=True)
        dc_ref[...] = dc

    row = pl.BlockSpec((T, E), lambda i: (i, 0))
    vec = pl.BlockSpec((1, E), lambda i: (0, 0))
    return _call(body, name=name, grid=(S // T,),
                 in_specs=[row, row, pl.BlockSpec((T, E), lambda i: (i, 2)), vec, vec],
                 out_specs=[row, row, pl.BlockSpec((8, E), lambda i: (0, 0))],
                 out_shape=[jax.ShapeDtypeStruct((S, E), F32), jax.ShapeDtypeStruct((S, E), BF16),
                            jax.ShapeDtypeStruct((8, E), F32)],
                 args=[c, dy, proj, lg, lb], sem=("arbitrary",), jobs=jobs)


def conf_bwd_conv(proj, dc, dz, cw, name, jobs=()):
    S, E3 = proj.shape
    E = E3 // 3
    T = _tile(S, 256, HALO)
    R = T // HALO
    nt = S // T
    lc = _tile(E, LANE_CHUNK)
    rc = min(ROW_CHUNK, T)

    def body(a_ref, b_ref, ap_ref, bp_ref, dc_ref, dcn_ref, dz_ref, cw_ref, o_ref, dw_ref, u_scr, dc_scr, sh_scr):
        i = pl.program_id(0)

        @pl.when(i == 0)
        def _():
            dw_ref[...] = jnp.zeros_like(dw_ref)

        a = a_ref[...]
        sb = _sigmoid(b_ref[...])
        up = ap_ref[...] * _sigmoid(bp_ref[...])
        u_scr[0:HALO, :] = jnp.where(i > 0, up, 0.0)
        u_scr[HALO:HALO + T, :] = a * sb
        dc_scr[0:T, :] = dc_ref[...]
        dc_scr[T:T + HALO, :] = jnp.where(i < nt - 1, dcn_ref[...], 0.0)
        off = HALO - (CONV_WIDTH - 1)
        for l0 in range(0, E, lc):
            _shifted_copies(u_scr, sh_scr, l0, lc)
            for k in range(CONV_WIDTH):
                prod = _shifted(u_scr, sh_scr, off + k, T, l0, lc) * dc_scr[0:T, l0:l0 + lc]
                dw_ref[k:k + 1, l0:l0 + lc] += jnp.sum(prod, axis=0, keepdims=True)
            _shifted_copies(dc_scr, sh_scr, l0, lc)
            for r0 in range(0, T, rc):
                acc = jnp.zeros((rc, lc), F32)
                for k in range(CONV_WIDTH):
                    s0 = r0 + (CONV_WIDTH - 1) - k
                    acc = acc + _shifted(dc_scr, sh_scr, s0, rc, l0, lc) * cw_ref[k:k + 1, l0:l0 + lc]
                av = a[r0:r0 + rc, l0:l0 + lc]
                sv = sb[r0:r0 + rc, l0:l0 + lc]
                o_ref[r0:r0 + rc, l0:l0 + lc] = (acc * sv).astype(BF16)
                o_ref[r0:r0 + rc, E + l0:E + l0 + lc] = (acc * av * sv * (1.0 - sv)).astype(BF16)
        o_ref[:, 2 * E:3 * E] = dz_ref[...]

    def col(j):
        return pl.BlockSpec((T, E), lambda i: (i, j))

    def prev(j):
        return pl.BlockSpec((HALO, E), lambda i: (jnp.maximum(i * R - 1, 0), j))

    row = pl.BlockSpec((T, E), lambda i: (i, 0))
    nxt = pl.BlockSpec((HALO, E), lambda i: (jnp.minimum((i + 1) * R, S // HALO - 1), 0))
    return _call(body, name=name, grid=(nt,),
                 in_specs=[col(0), col(1), prev(0), prev(1), row, nxt, row,
                           pl.BlockSpec((HALO, E), lambda i: (0, 0))],
                 out_specs=[pl.BlockSpec((T, E3), lambda i: (i, 0)), pl.BlockSpec((HALO, E), lambda i: (0, 0))],
                 out_shape=[jax.ShapeDtypeStruct((S, E3), BF16), jax.ShapeDtypeStruct((HALO, E), F32)],
                 scratch_shapes=[pltpu.VMEM((HALO + T, E), F32), pltpu.VMEM((T + HALO, E), F32),
                                 pltpu.VMEM((SUBLANES - 1, HALO + T, lc), F32)],
                 args=[proj, proj, proj, proj, dc, dc, dz, cw], sem=("arbitrary",), jobs=jobs)


def bucket_tables():
    q = np.arange(BLOCK)[:, None]
    k = np.arange(BLOCK)[None, :]
    out = []
    for window, dil in GROUPS:
        w_sub = window // dil
        for delta in (q - k, q + BLOCK - k):
            valid = (delta >= 0) & (delta <= w_sub)
            dist = np.clip(delta, 0, None) * dil
            large = MAX_EXACT + (np.log(np.maximum(dist, 1).astype(np.float32) / MAX_EXACT)
                                 / math.log(MAX_DISTANCE / MAX_EXACT)
                                 * (N_BUCKETS - MAX_EXACT)).astype(np.int32)
            large = np.minimum(large, N_BUCKETS - 1)
            bucket = np.where(dist < MAX_EXACT, dist, large)
            out.append(np.where(valid, bucket, -1).reshape(-1))
    return np.concatenate(out).astype(np.int32)


def _attn_masks():
    ql = lax.broadcasted_iota(jnp.int32, (BLOCK, BLOCK), 0)
    kl = lax.broadcasted_iota(jnp.int32, (BLOCK, BLOCK), 1)
    return kl <= ql, kl >= ql


def _dot_nt(a, b):
    return lax.dot_general(a, b, (((1,), (1,)), ((), ())), preferred_element_type=F32)


def _dot_tn(a, b):
    return lax.dot_general(a, b, (((0,), (0,)), ((), ())), preferred_element_type=F32)


ATTN_ROWS = 2048


def _rows(start, dil):
    return pl.ds(start, BLOCK) if dil == 1 else pl.ds(start, BLOCK, stride=dil)


def _attn_geometry(S, dil):
    halo = BLOCK * dil
    rows = max(min(S, ATTN_ROWS), halo)
    return halo, rows, S // rows, rows // halo


def attn_fwd(qz, kv, bias_c, bias_p, g, dil, name, jobs=()):
    S = qz.shape[0]
    D = qz.shape[1] // (N_GROUPS + 1)
    H = D // HEAD_DIM
    halo, rows, nsb, nblk = _attn_geometry(S, dil)
    scale = HEAD_DIM ** -0.5

    def body(q_ref, kvc_ref, kvp_ref, bc_ref, bp_ref, o_ref, l_ref, ks, vs):
        sb = pl.program_id(1)
        ks[0:halo, :] = kvp_ref[0].astype(F32)
        ks[halo:, :] = kvc_ref[0].astype(F32)
        vs[0:halo, :] = kvp_ref[1].astype(F32)
        vs[halo:, :] = kvc_ref[1].astype(F32)
        mask_c, mask_p0 = _attn_masks()
        bc = bc_ref[0]
        bp = bp_ref[0]
        for jj in range(nblk):
            mask_p = mask_p0 if jj > 0 else mask_p0 & (sb > 0)
            for r in range(dil):
                cur = _rows(halo + jj * halo + r, dil)
                prv = _rows(jj * halo + r, dil)
                out = _rows(jj * halo + r, dil)
                q = q_ref[out, :].astype(BF16)
                s_c = jnp.where(mask_c, _dot_nt(q, ks[cur, :].astype(BF16)) * scale + bc, -jnp.inf)
                s_p = jnp.where(mask_p, _dot_nt(q, ks[prv, :].astype(BF16)) * scale + bp, -jnp.inf)
                m = jnp.maximum(jnp.max(s_c, axis=-1, keepdims=True), jnp.max(s_p, axis=-1, keepdims=True))
                p_c = jnp.exp(s_c - m)
                p_p = jnp.exp(s_p - m)
                den = jnp.sum(p_c, axis=-1, keepdims=True) + jnp.sum(p_p, axis=-1, keepdims=True)
                pv = (jnp.dot(p_c.astype(BF16), vs[cur, :].astype(BF16), preferred_element_type=F32)
                      + jnp.dot(p_p.astype(BF16), vs[prv, :].astype(BF16), preferred_element_type=F32))
                o_ref[out, :] = pv / den
                l_ref[out, :] = jnp.broadcast_to(m + jnp.log(den), (BLOCK, HEAD_DIM))

    per = rows // halo
    bias_spec = pl.BlockSpec((1, BLOCK, BLOCK), lambda h, sb: (h, 0, 0))
    out_spec = pl.BlockSpec((rows, HEAD_DIM), lambda h, sb: (sb, h))
    return _call(body, name=name, grid=(H, nsb),
                 in_specs=[pl.BlockSpec((rows, HEAD_DIM), lambda h, sb: (sb, g * H + h)),
                           pl.BlockSpec((2, rows, HEAD_DIM), lambda h, sb: (0, sb, g * H + h)),
                           pl.BlockSpec((2, halo, HEAD_DIM),
                                        lambda h, sb: (0, jnp.maximum(sb * per - 1, 0), g * H + h)),
                           bias_spec, bias_spec],
                 out_specs=[out_spec, out_spec],
                 out_shape=[jax.ShapeDtypeStruct((S, D), F32)] * 2,
                 scratch_shapes=[pltpu.VMEM((halo + rows, HEAD_DIM), F32)] * 2,
                 args=[qz, kv, kv, bias_c, bias_p], sem=("parallel", "arbitrary"), jobs=jobs)


def merge_fwd(os_, lses, qz, name, jobs=()):
    S, D = os_[0].shape
    T = _tile(S, 256, 8)

    def body(o1, o2, o3, l1, l2, l3, z_ref, o_ref, lse_ref, y_ref):
        la, lb_, lc_ = l1[...], l2[...], l3[...]
        m = jnp.maximum(jnp.maximum(la, lb_), lc_)
        ea, eb, ec = jnp.exp(la - m), jnp.exp(lb_ - m), jnp.exp(lc_ - m)
        den = ea + eb + ec
        o = (ea * o1[...] + eb * o2[...] + ec * o3[...]) / den
        z = z_ref[...]
        o_ref[...] = o
        lse_ref[...] = m + jnp.log(den)
        y_ref[...] = (o * (z * _sigmoid(z))).astype(BF16)

    row = pl.BlockSpec((T, D), lambda i: (i, 0))
    return _call(body, name=name, grid=(S // T,),
                 in_specs=[row] * 6 + [pl.BlockSpec((T, D), lambda i: (i, N_GROUPS))],
                 out_specs=[row] * 3,
                 out_shape=[jax.ShapeDtypeStruct((S, D), F32), jax.ShapeDtypeStruct((S, D), F32),
                            jax.ShapeDtypeStruct((S, D), BF16)],
                 args=[*os_, *lses, qz], sem=("parallel",), jobs=jobs)


def merge_bwd(dy2, o, qz, name, jobs=()):
    S, D = o.shape
    H = D // HEAD_DIM
    T = _tile(S, 256, 8)
    nq = N_GROUPS + 1

    def body(dy_ref, o_ref, z_ref, do_ref, dl_ref, dqz_ref):
        dy = dy_ref[...]
        ov = o_ref[...]
        z = z_ref[...]
        sz = _sigmoid(z)
        do = dy * (z * sz)
        do_ref[...] = do.astype(BF16)
        dqz_ref[...] = (dy * ov * (sz * (1.0 + z * (1.0 - sz)))).astype(BF16)
        prod = do * ov
        for h in range(H):
            hs = slice(h * HEAD_DIM, (h + 1) * HEAD_DIM)
            dl_ref[:, hs] = jnp.broadcast_to(jnp.sum(prod[:, hs], axis=-1, keepdims=True), (T, HEAD_DIM))

    row = pl.BlockSpec((T, D), lambda i: (i, 0))
    last = pl.BlockSpec((T, D), lambda i: (i, N_GROUPS))
    return _call(body, name=name, grid=(S // T,), in_specs=[row, row, last], out_specs=[row, row, last],
                 out_shape=[jax.ShapeDtypeStruct((S, D), BF16), jax.ShapeDtypeStruct((S, D), F32),
                            jax.ShapeDtypeStruct((S, nq * D), BF16)],
                 args=[dy2, o, qz], sem=("parallel",), jobs=jobs)


def attn_bwd(qz, kv, do, lse, delta, bias_c, bias_p, dqz, dkv, g, dil, name, jobs=()):
    S = qz.shape[0]
    D = qz.shape[1] // (N_GROUPS + 1)
    H = D // HEAD_DIM
    halo, rows, nsb, nblk = _attn_geometry(S, dil)
    scale = HEAD_DIM ** -0.5
    have_dkv = dkv is not None

    def body(*refs):
        q_ref, do_ref, l_ref, d_ref, kvc_ref, kvp_ref, bc_ref, bp_ref = refs[:8]
        n_in = 8 + 1 + (1 if have_dkv else 0)
        dq_ref, dkv_ref, dsc_ref, dsp_ref, ks, vs, dks, dvs, dos, dqs, carry_k, carry_v = refs[n_in:]
        i = pl.program_id(1)
        sb = nsb - 1 - i

        @pl.when(i == 0)
        def _():
            dsc_ref[...] = jnp.zeros_like(dsc_ref)
            dsp_ref[...] = jnp.zeros_like(dsp_ref)

        ks[0:halo, :] = kvp_ref[0].astype(F32)
        ks[halo:, :] = kvc_ref[0].astype(F32)
        vs[0:halo, :] = kvp_ref[1].astype(F32)
        vs[halo:, :] = kvc_ref[1].astype(F32)
        dos[...] = do_ref[...].astype(F32)
        dks[...] = jnp.zeros_like(dks)
        dvs[...] = jnp.zeros_like(dvs)
        mask_c, mask_p0 = _attn_masks()
        bc = bc_ref[0]
        bp = bp_ref[0]
        for jj in range(nblk):
            mask_p = mask_p0 if jj > 0 else mask_p0 & (sb > 0)
            for r in range(dil):
                cur = _rows(halo + jj * halo + r, dil)
                prv = _rows(jj * halo + r, dil)
                own = _rows(jj * halo + r, dil)
                q = q_ref[own, :].astype(BF16)
                do = dos[own, :].astype(BF16)
                lse = l_ref[own, :]
                dlt = d_ref[own, :]
                kc = ks[cur, :].astype(BF16)
                kp = ks[prv, :].astype(BF16)
                vc = vs[cur, :].astype(BF16)
                vp = vs[prv, :].astype(BF16)
                p_c = jnp.where(mask_c, jnp.exp(_dot_nt(q, kc) * scale + bc - lse), 0.0)
                p_p = jnp.where(mask_p, jnp.exp(_dot_nt(q, kp) * scale + bp - lse), 0.0)
                ds_c = p_c * (_dot_nt(do, vc) - dlt)
                ds_p = p_p * (_dot_nt(do, vp) - dlt)
                dsc_ref[0] += ds_c
                dsp_ref[0] += ds_p
                ds_cb = ds_c.astype(BF16)
                ds_pb = ds_p.astype(BF16)
                dqs[own, :] = (jnp.dot(ds_cb, kc, preferred_element_type=F32)
                               + jnp.dot(ds_pb, kp, preferred_element_type=F32)) * scale
                dks[cur, :] += _dot_tn(ds_cb, q) * scale
                dks[prv, :] += _dot_tn(ds_pb, q) * scale
                dvs[cur, :] += _dot_tn(p_c.astype(BF16), do)
                dvs[prv, :] += _dot_tn(p_p.astype(BF16), do)

        @pl.when(i > 0)
        def _():
            dks[rows:rows + halo, :] += carry_k[...]
            dvs[rows:rows + halo, :] += carry_v[...]

        dkv_ref[0] = dks[halo:, :].astype(BF16)
        dkv_ref[1] = dvs[halo:, :].astype(BF16)
        carry_k[...] = dks[0:halo, :]
        carry_v[...] = dvs[0:halo, :]
        dq_ref[...] = dqs[...].astype(BF16)

    per = rows // halo

    def rev(i):
        return nsb - 1 - i

    bias_spec = pl.BlockSpec((1, BLOCK, BLOCK), lambda h, i: (h, 0, 0))
    row_spec = pl.BlockSpec((rows, HEAD_DIM), lambda h, i: (rev(i), h))
    in_specs = [pl.BlockSpec((rows, HEAD_DIM), lambda h, i: (rev(i), g * H + h)),
                row_spec, row_spec, row_spec,
                pl.BlockSpec((2, rows, HEAD_DIM), lambda h, i: (0, rev(i), g * H + h)),
                pl.BlockSpec((2, halo, HEAD_DIM),
                             lambda h, i: (0, jnp.maximum(rev(i) * per - 1, 0), g * H + h)),
                bias_spec, bias_spec, ANY_SPEC]
    args = [qz, do, lse, delta, kv, kv, bias_c, bias_p, dqz]
    aliases = {8: 0}
    if have_dkv:
        in_specs.append(ANY_SPEC)
        args.append(dkv)
        aliases[9] = 1
    blk = (halo + rows, HEAD_DIM)
    return _call(body, name=name, grid=(H, nsb), in_specs=in_specs,
                 out_specs=[pl.BlockSpec((rows, HEAD_DIM), lambda h, i: (rev(i), g * H + h)),
                            pl.BlockSpec((2, rows, HEAD_DIM), lambda h, i: (0, rev(i), g * H + h)),
                            bias_spec, bias_spec],
                 out_shape=[jax.ShapeDtypeStruct(qz.shape, BF16),
                            jax.ShapeDtypeStruct((2, S, N_GROUPS * D), BF16),
                            jax.ShapeDtypeStruct((H, BLOCK, BLOCK), F32),
                            jax.ShapeDtypeStruct((H, BLOCK, BLOCK), F32)],
                 scratch_shapes=[pltpu.VMEM(blk, F32), pltpu.VMEM(blk, F32), pltpu.VMEM(blk, F32),
                                 pltpu.VMEM(blk, F32),
                                 pltpu.VMEM((rows, HEAD_DIM), F32), pltpu.VMEM((rows, HEAD_DIM), F32),
                                 pltpu.VMEM((halo, HEAD_DIM), F32), pltpu.VMEM((halo, HEAD_DIM), F32)],
                 aliases=aliases, args=args, sem=("parallel", "arbitrary"), jobs=jobs)


def _adamw(w, g, m, v):
    m = ADAM_B1 * m + (1.0 - ADAM_B1) * g
    v = ADAM_B2 * v + (1.0 - ADAM_B2) * (g * g)
    m_hat = m / (1.0 - ADAM_B1 ** ADAM_STEP)
    v_hat = v / (1.0 - ADAM_B2 ** ADAM_STEP)
    delta = -ADAM_LR * (m_hat / (jnp.sqrt(v_hat) + ADAM_EPS) + ADAM_WD * w)
    return delta, m, v


def add_pairs(dw, got, name, jobs=()):
    _, K, n = got.shape
    tk = _tile(K, 512, 16)

    def own_block(r, i):
        x, y, c = _place()
        return 4 * ((x + r // 2) % 2) + 2 * ((y + r % 2) % 2) + c, i, 0

    def body(a_ref, b_ref, o_ref):
        o_ref[...] = (a_ref[...].astype(F32) + b_ref[...].astype(F32)).astype(o_ref.dtype)

    blk = pl.BlockSpec((None, tk, n), lambda r, i: (r, i, 0))
    return _call(body, name=name, grid=(N_CHIPS, K // tk),
                 in_specs=[pl.BlockSpec((None, tk, n), own_block), blk], out_specs=[blk],
                 out_shape=[jax.ShapeDtypeStruct(got.shape, got.dtype)], args=[dw, got],
                 sem=("parallel", "parallel"), jobs=jobs)[0]


def adamw_reduce(part, got, w, m, v, name, jobs=()):
    K, n = w.shape
    tk = _tile(K, 256, 8)

    def body(p_ref, r_ref, w_ref, m_ref, v_ref, g_ref, d_ref, nm_ref, nv_ref):
        g = p_ref[...].astype(F32)
        for r in range(N_CHIPS - 1):
            g = g + r_ref[r].astype(F32)
        d, nm, nv = _adamw(w_ref[...], g, m_ref[...], v_ref[...])
        g_ref[...] = g
        d_ref[...] = d
        nm_ref[...] = nm
        nv_ref[...] = nv

    blk = pl.BlockSpec((tk, n), lambda i: (i, 0))
    return _call(body, name=name, grid=(K // tk,),
                 in_specs=[pl.BlockSpec((None, tk, n), lambda i: (0, i, 0)),
                           pl.BlockSpec((N_CHIPS - 1, tk, n), lambda i: (0, i, 0)), blk, blk, blk],
                 out_specs=[blk] * 4, out_shape=[jax.ShapeDtypeStruct((K, n), F32)] * 4,
                 args=[part, got, w, m, v], sem=("parallel",), jobs=jobs)


def sum_parts(parts, name):
    _, R, D = parts.shape

    def body(p_ref, o_ref):
        g = p_ref[0]
        for r in range(1, N_DEV):
            g = g + p_ref[r]
        o_ref[...] = g

    return _call(body, name=name, in_specs=[VMEM_SPEC], out_specs=[VMEM_SPEC],
                 out_shape=[jax.ShapeDtypeStruct((R, D), F32)], args=[parts])[0]


def adamw_small(w, g, m, v, name):
    def body(w_ref, g_ref, m_ref, v_ref, d_ref, nm_ref, nv_ref):
        d, nm, nv = _adamw(w_ref[...], g_ref[...], m_ref[...], v_ref[...])
        d_ref[...] = d
        nm_ref[...] = nm
        nv_ref[...] = nv

    return _call(body, name=name, in_specs=[VMEM_SPEC] * 4, out_specs=[VMEM_SPEC] * 3,
                 out_shape=[jax.ShapeDtypeStruct(w.shape, F32)] * 3, args=[w, g, m, v])


def _row(v, at):
    return jnp.pad(v.reshape(1, -1), ((at, 7 - at), (0, 0)))


def _pack_sharded(norm, conv_w, conv_b, ln_g, ln_b):
    n = norm.shape[-1]
    taps = jnp.pad(conv_w.reshape(CONV_WIDTH, n), ((0, HALO - CONV_WIDTH), (0, 0)))
    return jnp.concatenate([_row(norm, 0), _row(ln_g, 0) + _row(ln_b, 1) + _row(conv_b, 2), taps], axis=0)


def _pack_rel(rel_bias, D):
    return jnp.pad(rel_bias.reshape(1, -1), ((0, 7), (0, D - rel_bias.size)))


def _pack_replicated(kv_norm, b_norm, final_norm, rel_bias, D):
    return jnp.concatenate([_row(kv_norm, 0) + _row(b_norm, 1), _row(final_norm, 0), _pack_rel(rel_bias, D)], axis=0)


def kernel(x, a_norm, a_w_in, a_conv_w, a_conv_b, a_ln_g, a_ln_b, a_w_out, kv_norm, w_kv, b_norm, b_w_in, b_w_out, rel_bias, final_norm, loss_target, m_a_norm, m_a_w_in, m_a_conv_w, m_a_conv_b, m_a_ln_g, m_a_ln_b, m_a_w_out, m_kv_norm, m_w_kv, m_b_norm, m_b_w_in, m_b_w_out, m_rel_bias, m_final_norm, v_a_norm, v_a_w_in, v_a_conv_w, v_a_conv_b, v_a_ln_g, v_a_ln_b, v_a_w_out, v_kv_norm, v_w_kv, v_b_norm, v_b_w_in, v_b_w_out, v_rel_bias, v_final_norm):
    _, S, D = x.shape
    E = D
    H = D // HEAD_DIM
    nsh = D // N_DEV
    x0 = x.reshape(S, D)
    target = loss_target.reshape(S, D)
    kvn, bn, fn = kv_norm.reshape(1, D), b_norm.reshape(1, D), final_norm.reshape(1, D)

    names = ["a_w_in", "a_w_out", "w_kv", "b_w_in", "b_w_out"]
    big_w = dict(zip(names, [a_w_in[0], a_w_out[0], w_kv, b_w_in[0], b_w_out[0]]))
    big_m = dict(zip(names, [m_a_w_in[0], m_a_w_out[0], m_w_kv, m_b_w_in[0], m_b_w_out[0]]))
    big_v = dict(zip(names, [v_a_w_in[0], v_a_w_out[0], v_w_kv, v_b_w_in[0], v_b_w_out[0]]))
    shard = {nm: w.astype(BF16) for nm, w in big_w.items()}
    sh_w = _pack_sharded(a_norm, a_conv_w, a_conv_b, a_ln_g, a_ln_b)

    wa_in, sh_all = all_gather([shard["a_w_in"], sh_w], "gather_first")
    sh_full = sh_all.transpose(1, 0, 2).reshape(SMALL_SH_ROWS, D)
    an, cw, cb = sh_full[0:1], sh_full[16:16 + HALO], sh_full[10:11]
    lg, lb = sh_full[8:9], sh_full[9:10]

    (h0,) = rms_fwd(x0, [an], "rms_a")
    g_a_out = gather_chips_job(shard["a_w_out"])
    proj = matmul_nn(h0, wa_in, "a_in", jobs=[g_a_out])
    g_a_out2 = gather_sibling_job(g_a_out.results[0])
    g_kv = gather_chips_job(shard["w_kv"])
    c, y = conf_fwd(proj, cw, cb, lg, lb, "conf_fwd", jobs=[g_a_out2, g_kv])
    wa_out = g_a_out2.results[0].reshape(1, E, D)
    g_kv2 = gather_sibling_job(g_kv.results[0])
    x1 = matmul_nn(y, wa_out, "a_out", res=x0, jobs=[g_kv2])
    wkv = g_kv2.results[0]
    hk, hb = rms_fwd(x1, [kvn, bn], "rms_b")
    g_b_in = gather_chips_job(shard["b_w_in"])
    kv = matmul_nn(hk, wkv, "kv_proj", out_dtype=BF16, kv_split=True, jobs=[g_b_in])

    bt = jnp.asarray(bucket_tables())
    onehot = (bt[None, :] == jnp.arange(N_BUCKETS, dtype=jnp.int32)[:, None]).astype(F32)
    g_b_in2 = gather_sibling_job(g_b_in.results[0])
    bias = small_dot(rel_bias.T, onehot, "nn", "bias_table", jobs=[g_b_in2])
    bias = bias.reshape(H, 2 * N_GROUPS, BLOCK, BLOCK)
    bias_c = [bias[:, 2 * g] for g in range(N_GROUPS)]
    bias_p = [bias[:, 2 * g + 1] for g in range(N_GROUPS)]
    wb_in = g_b_in2.results[0]
    g_b_out = gather_chips_job(shard["b_w_out"])
    qz = matmul_nn(hb, wb_in, "b_in", jobs=[g_b_out])
    g_b_out2 = gather_sibling_job(g_b_out.results[0])

    os_, lses = [], []
    for g, (_, dil) in enumerate(GROUPS):
        o_g, l_g = attn_fwd(qz, kv, bias_c[g], bias_p[g], g, dil, "attn_fwd%d" % g,
                            jobs=[g_b_out2] if g == 0 else ())
        os_.append(o_g)
        lses.append(l_g)
    wb_out = g_b_out2.results[0].reshape(1, D, D)
    o, lse, y2 = merge_fwd(os_, lses, qz, "merge_fwd")
    x2 = matmul_nn(y2, wb_out, "b_out", res=x1)
    dx2, dx2b, fin_acc = final_loss(x2, fn, target, "final_loss")

    dy2 = matmul_nt(dx2b, wb_out, "b_out_dx")
    dwb_out = matmul_tn(y2, dx2b, 1, "b_out_dw").reshape(N_DEV, D // N_DEV, D)
    r_b_out = reduce_sibling_job(dwb_out)
    do, delta, dqz = merge_bwd(dy2, o, qz, "merge_bwd", jobs=[r_b_out])
    p_b_out = add_pairs(dwb_out, r_b_out.results[0], "pairs_b_out")
    r_b_out2 = reduce_chips_job(p_b_out)
    dkv = None
    ds_tabs = []
    for g, (_, dil) in enumerate(GROUPS):
        dqz, dkv, ds_c, ds_p = attn_bwd(qz, kv, do, lse, delta, bias_c[g], bias_p[g], dqz, dkv, g, dil,
                                        "attn_bwd%d" % g, jobs=[r_b_out2] if g == 0 else ())
        ds_tabs += [ds_c.reshape(H, BLOCK * BLOCK), ds_p.reshape(H, BLOCK * BLOCK)]
    d_rel = small_dot(onehot, jnp.concatenate(ds_tabs, axis=1), "nt", "bias_grad")
    dhb = matmul_nt(dqz, wb_in, "b_in_dx")
    dwb_in = matmul_tn(hb, dqz, N_DEV, "b_in_dw")
    r_b_in = reduce_sibling_job(dwb_in)
    dhk = matmul_nt(dkv, wkv, "kv_dx", kv_split=True, jobs=[r_b_in])
    p_b_in = add_pairs(dwb_in, r_b_in.results[0], "pairs_b_in")
    r_b_in2 = reduce_chips_job(p_b_in)
    dw_kv = matmul_tn(hk, dkv, N_DEV, "kv_dw", kv_split=True, jobs=[r_b_in2])
    r_kv = reduce_sibling_job(dw_kv)
    dx1, dx1b, norm_acc = rms_bwd(x1, [dhk, dhb], [kvn, bn], dx2, "rms_b_bwd", True, jobs=[r_kv])
    p_kv = add_pairs(dw_kv, r_kv.results[0], "pairs_kv")
    r_kv2 = reduce_chips_job(p_kv)

    dy = matmul_nt(dx1b, wa_out, "a_out_dx")
    dwa_out = matmul_tn(y, dx1b, 1, "a_out_dw").reshape(N_DEV, E // N_DEV, D)
    r_a_out = reduce_sibling_job(dwa_out)
    dc, dz, ln_acc = conf_bwd_ln(c, dy, proj, lg, lb, "conf_bwd_ln", jobs=[r_a_out])
    p_a_out = add_pairs(dwa_out, r_a_out.results[0], "pairs_a_out")
    r_a_out2 = reduce_chips_job(p_a_out)
    dproj, dcw = conf_bwd_conv(proj, dc, dz, cw, "conf_bwd_conv", jobs=[r_kv2, r_a_out2])
    share = share_small_job(jnp.concatenate([ln_acc, dcw, norm_acc, fin_acc, _pack_rel(d_rel, D)], axis=0))
    dwa_in = matmul_tn(h0, dproj, N_DEV, "a_in_dw", jobs=[share])
    r_a_in = reduce_sibling_job(dwa_in)
    dh0 = matmul_nt(dproj, wa_in, "a_in_dx", jobs=[r_a_in])
    p_a_in = add_pairs(dwa_in, r_a_in.results[0], "pairs_a_in")
    r_a_in2 = reduce_chips_job(p_a_in)
    grad_x, a_acc = rms_bwd(x0, [dh0], [an], dx1, "rms_a_bwd", False, jobs=[r_a_in2])

    share_a = share_small_job(a_acc)
    sums = {"w_kv": (p_kv, r_kv2), "b_w_in": (p_b_in, r_b_in2), "b_w_out": (p_b_out, r_b_out2),
            "a_w_out": (p_a_out, r_a_out2), "a_w_in": (p_a_in, r_a_in2)}
    big_out = {}
    for nm, (part, job) in sums.items():
        big_out[nm] = adamw_reduce(part, job.results[0], big_w[nm], big_m[nm], big_v[nm], "adamw_" + nm,
                                   jobs=[share_a] if nm == "a_w_out" else ())

    gsum = jnp.concatenate([sum_parts(share_a.results[0], "sum_small_a"),
                            sum_parts(share.results[0], "sum_small")], axis=0)
    me = 4 * lax.axis_index("x") + 2 * lax.axis_index("y") + lax.axis_index("c")
    g_sh = lax.dynamic_slice(gsum, (0, me * nsh), (SMALL_SH_ROWS, nsh))
    g_rep = gsum[SMALL_SH_ROWS:]
    loss = g_rep[9, 0]
    sh_m = _pack_sharded(m_a_norm, m_a_conv_w, m_a_conv_b, m_a_ln_g, m_a_ln_b)
    sh_v = _pack_sharded(v_a_norm, v_a_conv_w, v_a_conv_b, v_a_ln_g, v_a_ln_b)
    sh_d, sh_nm, sh_nv = adamw_small(sh_w, g_sh, sh_m, sh_v, "adamw_sharded")
    rep_w = _pack_replicated(kv_norm, b_norm, final_norm, rel_bias, D)
    rep_m = _pack_replicated(m_kv_norm, m_b_norm, m_final_norm, m_rel_bias, D)
    rep_v = _pack_replicated(v_kv_norm, v_b_norm, v_final_norm, v_rel_bias, D)
    rep_d, rep_nm, rep_nv = adamw_small(rep_w, g_rep, rep_m, rep_v, "adamw_replicated")

    def unpack(kind):
        sh = (g_sh, sh_d, sh_nm, sh_nv)[kind]
        rep = (g_rep, rep_d, rep_nm, rep_nv)[kind]
        big = {nm: big_out[nm][kind] for nm in names}
        nb = rel_bias.size
        return [
            sh[0:1],
            big["a_w_in"][None],
            sh[16:16 + CONV_WIDTH][None],
            sh[10:11], sh[8:9], sh[9:10],
            big["a_w_out"][None],
            rep[0],
            big["w_kv"],
            rep[1:2],
            big["b_w_in"][None],
            big["b_w_out"][None],
            rep[16, :nb].reshape(rel_bias.shape),
            rep[8],
        ]

    return (loss, grad_x.reshape(1, S, D), *unpack(0), *unpack(1), *unpack(2), *unpack(3))
```

```python
import functools
import math

import numpy as np
import jax
import jax.numpy as jnp
from jax import lax
from jax.experimental import pallas as pl
from jax.experimental.pallas import tpu as pltpu

F32 = jnp.float32
BF16 = jnp.bfloat16

N_DEV = 8
N_CHIPS = 4
EPS = 1e-6
HEAD_DIM = 128
BLOCK = 128
GROUPS = ((128, 1), (512, 4), (2048, 16))
N_GROUPS = len(GROUPS)
CONV_WIDTH = 31
HALO = 32
N_BUCKETS = 32
MAX_EXACT = N_BUCKETS // 2
MAX_DISTANCE = 2048
V7X_VMEM_BYTES = 64 * 1024 * 1024
VMEM_LIMIT = (V7X_VMEM_BYTES * 7) // 8
MATMUL_VMEM_BUDGET = (V7X_VMEM_BYTES * 11) // 16
LANE = 128

ADAM_LR = 0.001
ADAM_B1 = 0.9
ADAM_B2 = 0.999
ADAM_EPS = 1e-08
ADAM_WD = 0.01
ADAM_STEP = 10

SMALL_SH_ROWS = 48
SMALL_REP_ROWS = 24
MESH = pl.DeviceIdType.MESH
ANY_SPEC = pl.BlockSpec(memory_space=pl.ANY)
VMEM_SPEC = pl.BlockSpec(memory_space=pltpu.VMEM)


def _tile(dim, pref, unit=LANE):
    if dim <= pref:
        return dim
    t = (pref // unit) * unit
    while dim % t:
        t -= unit
    assert t > 0
    return t


def _sigmoid(v):
    return jax.nn.sigmoid(v)


def _place():
    return lax.axis_index("x"), lax.axis_index("y"), lax.axis_index("c")


def _flip(v, bit):
    return 1 - v if bit else v


class Job:
    def __init__(self, srcs, dsts, n_sems, build):
        self.srcs, self.dsts, self.n_sems, self.build = list(srcs), list(dsts), n_sems, build
        self.results = None


def _remote(src, dst, send_sems, recv_sems, k, peer):
    return pltpu.make_async_remote_copy(src_ref=src, dst_ref=dst, send_sem=send_sems.at[k],
                                        recv_sem=recv_sems.at[k], device_id=peer, device_id_type=MESH)


def _call(body, *, name, in_specs, out_specs, out_shape, args, grid=(), scratch_shapes=(), sem=(),
          aliases=None, jobs=()):
    n_in, n_out, n_scr = len(in_specs), len(out_specs), len(scratch_shapes)
    aliases = dict(aliases or {})
    x_in, x_out, x_scr = [], [], []
    for job in jobs:
        job.in_at = n_in + len(x_in)
        x_in += job.srcs
        job.out_at = n_out + len(x_out)
        for d in job.dsts:
            if not isinstance(d, jax.ShapeDtypeStruct):
                aliases[n_in + len(x_in)] = n_out + len(x_out)
                x_in.append(d)
            x_out.append(jax.ShapeDtypeStruct(d.shape, d.dtype))
        job.scr_at = n_scr + len(x_scr)
        x_scr += [pltpu.SemaphoreType.DMA((job.n_sems,))] * 3

    def wrapped(*refs):
        ins = refs[:n_in + len(x_in)]
        outs = refs[len(ins):len(ins) + n_out + len(x_out)]
        scr = refs[len(ins) + len(outs):]
        core = ins[:n_in] + outs[:n_out] + scr[:n_scr]
        if not jobs:
            body(*core)
            return
        copies = []
        for job in jobs:
            copies += job.build(ins[job.in_at:job.in_at + len(job.srcs)],
                                outs[job.out_at:job.out_at + len(job.dsts)],
                                *scr[job.scr_at:job.scr_at + 3])
        if grid:
            pids = [pl.program_id(d) for d in range(len(grid))]
            first = functools.reduce(jnp.logical_and, [p == 0 for p in pids])
            last = functools.reduce(jnp.logical_and, [p == g - 1 for p, g in zip(pids, grid)])

            @pl.when(first)
            def _():
                for cp in copies:
                    cp.start()

            body(*core)

            @pl.when(last)
            def _():
                for cp in copies:
                    cp.wait()
        else:
            for cp in copies:
                cp.start()
            body(*core)
            for cp in copies:
                cp.wait()

    if jobs:
        sem = ("arbitrary",) * len(grid)
    kwargs = dict(grid=grid) if grid else {}
    if aliases:
        kwargs["input_output_aliases"] = aliases
    outs = pl.pallas_call(
        wrapped, name=name,
        in_specs=list(in_specs) + [ANY_SPEC] * len(x_in),
        out_specs=list(out_specs) + [ANY_SPEC] * len(x_out),
        out_shape=list(out_shape) + x_out,
        scratch_shapes=list(scratch_shapes) + x_scr,
        compiler_params=pltpu.CompilerParams(dimension_semantics=sem if sem else None,
                                             vmem_limit_bytes=VMEM_LIMIT),
        **kwargs,
    )(*args, *x_in)
    for job in jobs:
        job.results = list(outs[job.out_at:job.out_at + len(job.dsts)])
    return list(outs[:n_out])


def gather_chips_job(shard):
    def build(srcs, dsts, send, recv, loc):
        (src,), (out,) = srcs, dsts
        x, y, c = _place()
        mine = out.at[4 * x + 2 * y + c]
        peers = [(x, y, 1 - c), (1 - x, y, c), (x, 1 - y, c), (1 - x, 1 - y, c)]
        return ([pltpu.make_async_copy(src, mine, loc.at[0])]
                + [_remote(src, mine, send, recv, k, p) for k, p in enumerate(peers)])

    return Job([shard], [jax.ShapeDtypeStruct((N_DEV,) + shard.shape, shard.dtype)], 4, build)


def gather_sibling_job(buf):
    def build(srcs, dsts, send, recv, loc):
        (out,) = dsts
        x, y, c = _place()
        copies = []
        for k, (cx, cy) in enumerate([(1 - x, y), (x, 1 - y), (1 - x, 1 - y)]):
            blk = out.at[4 * cx + 2 * cy + c]
            copies.append(_remote(blk, blk, send, recv, k, (x, y, 1 - c)))
        return copies

    return Job([], [buf], 3, build)


def reduce_sibling_job(dw):
    def build(srcs, dsts, send, recv, loc):
        (src,), (got,) = srcs, dsts
        x, y, c = _place()
        return [_remote(src.at[4 * _flip(x, r & 2) + 2 * _flip(y, r & 1) + 1 - c], got.at[r], send, recv, r,
                        (x, y, 1 - c)) for r in range(N_CHIPS)]

    return Job([dw], [jax.ShapeDtypeStruct((N_CHIPS,) + dw.shape[1:], dw.dtype)], N_CHIPS, build)


def reduce_chips_job(part):
    def build(srcs, dsts, send, recv, loc):
        (src,), (got,) = srcs, dsts
        x, y, c = _place()
        return [_remote(src.at[r], got.at[r - 1], send, recv, r - 1, (_flip(x, r & 2), _flip(y, r & 1), c))
                for r in range(1, N_CHIPS)]

    return Job([part], [jax.ShapeDtypeStruct((N_CHIPS - 1,) + part.shape[1:], part.dtype)], N_CHIPS - 1, build)


def share_small_job(small):
    def build(srcs, dsts, send, recv, loc):
        (src,), (out,) = srcs, dsts
        x, y, c = _place()
        mine = out.at[4 * x + 2 * y + c]
        copies = [pltpu.make_async_copy(src, mine, loc.at[0])]
        for rel in range(1, N_DEV):
            peer = (_flip(x, rel & 4), _flip(y, rel & 2), _flip(c, rel & 1))
            copies.append(_remote(src, mine, send, recv, rel - 1, peer))
        return copies

    return Job([small], [jax.ShapeDtypeStruct((N_DEV,) + small.shape, small.dtype)], N_DEV - 1, build)


def all_gather(shards, name):
    n = len(shards)

    def body(*refs):
        ins, outs = refs[:n], refs[n:2 * n]
        send_sems, recv_sems, local_sems = refs[2 * n:]
        x, y, c = _place()
        me, sibling = (x, y, c), (x, y, 1 - c)
        chips = [(1 - x, y), (x, 1 - y), (1 - x, 1 - y)]

        def slot(a, dev):
            return outs[a].at[4 * dev[0] + 2 * dev[1] + dev[2]]

        def copy(a, k, block, to, src=None):
            return pltpu.make_async_remote_copy(
                src_ref=slot(a, block) if src is None else src, dst_ref=slot(a, block),
                send_sem=send_sems.at[a, k], recv_sem=recv_sems.at[a, k],
                device_id=to, device_id_type=MESH)

        mine, first, passed = [], [], []
        for a in range(n):
            mine.append(pltpu.make_async_copy(ins[a], slot(a, me), local_sems.at[a]))
            mine[a].start()
            first.append([copy(a, 0, me, sibling, src=ins[a])]
                         + [copy(a, 1 + j, me, (*chip, c), src=ins[a]) for j, chip in enumerate(chips)])
            for cp in first[a]:
                cp.start()
        for a in range(n):
            passed.append([copy(a, 4 + j, (*chip, c), sibling) for j, chip in enumerate(chips)])
            for j, chip in enumerate(chips):
                copy(a, 1 + j, (*chip, c), me).wait_recv()
                passed[a][j].start()
        for a in range(n):
            copy(a, 0, sibling, me).wait_recv()
            for j, chip in enumerate(chips):
                copy(a, 4 + j, (*chip, 1 - c), me).wait_recv()
            for cp in first[a] + passed[a]:
                cp.wait_send()
            mine[a].wait()

    return pl.pallas_call(
        body, name=name,
        in_specs=[ANY_SPEC] * n, out_specs=[ANY_SPEC] * n,
        out_shape=[jax.ShapeDtypeStruct((N_DEV,) + s.shape, s.dtype) for s in shards],
        scratch_shapes=[pltpu.SemaphoreType.DMA((n, 7)), pltpu.SemaphoreType.DMA((n, 7)),
                        pltpu.SemaphoreType.DMA((n,))],
    )(*shards)


def _kv_split_index(tw, D):
    pd = D // tw

    def index(j):
        return (j // pd) % 2, (j // (2 * pd)) * pd + j % pd

    return index


def _col_tile(n, also, pref):
    t = (min(pref, n) // LANE) * LANE
    while n % t or (also is not None and also % t):
        t -= LANE
    assert t > 0
    return t


def matmul_nn(a, w, name, res=None, out_dtype=F32, kv_split=False, jobs=()):
    M, K = a.shape
    nb, _, n = w.shape
    D = nb * n // (2 * N_GROUPS)
    tn = _col_tile(n, D if kv_split else None, 1024)
    out_bytes = jnp.dtype(out_dtype).itemsize + (4 if res is not None else 0)
    tm = _tile(M, 2048)
    if 2 * (tm * K * 2 + K * tn * 2 + tm * tn * out_bytes) > MATMUL_VMEM_BUDGET:
        tm = _tile(M, 1024)
    per = n // tn

    def body(*refs):
        if res is None:
            a_ref, w_ref, o_ref = refs
        else:
            a_ref, w_ref, r_ref, o_ref = refs
        acc = jnp.dot(a_ref[...], w_ref[...], preferred_element_type=F32)
        if res is not None:
            acc = r_ref[...] + acc
        o_ref[...] = acc.astype(o_ref.dtype)

    in_specs = [pl.BlockSpec((tm, K), lambda i, j: (i, 0)),
                pl.BlockSpec((None, K, tn), lambda i, j: (j // per, 0, j % per))]
    args = [a, w]
    if res is not None:
        in_specs.append(pl.BlockSpec((tm, tn), lambda i, j: (i, j)))
        args.append(res)
    if kv_split:
        split = _kv_split_index(tn, D)
        out_spec = pl.BlockSpec((None, tm, tn), lambda i, j: (split(j)[0], i, split(j)[1]))
        out_shape = jax.ShapeDtypeStruct((2, M, N_GROUPS * D), out_dtype)
    else:
        out_spec = pl.BlockSpec((tm, tn), lambda i, j: (i, j))
        out_shape = jax.ShapeDtypeStruct((M, nb * n), out_dtype)
    return _call(body, name=name, grid=(M // tm, nb * per), in_specs=in_specs, out_specs=[out_spec],
                 out_shape=[out_shape], args=args, sem=("parallel", "parallel"), jobs=jobs)[0]


def matmul_nt(dy, w, name, kv_split=False, jobs=()):
    M = dy.shape[-2]
    nb, K, n = w.shape
    D = nb * n // (2 * N_GROUPS)
    tm = _tile(M, 1024)
    tc = _col_tile(n, D if kv_split else None, 1024)
    per = n // tc
    pair = 2 if (nb * per) % 2 == 0 else 1

    def body(*refs):
        o_ref = refs[-1]
        j = pl.program_id(1)
        part = None
        for u in range(pair):
            d = lax.dot_general(refs[u][...], refs[pair + u][...], (((1,), (1,)), ((), ())),
                                preferred_element_type=F32)
            part = d if part is None else part + d

        @pl.when(j == 0)
        def _():
            o_ref[...] = part

        @pl.when(j > 0)
        def _():
            o_ref[...] += part

    def dy_spec(u):
        if kv_split:
            split = _kv_split_index(tc, D)
            return pl.BlockSpec((None, tm, tc),
                                lambda i, j: (split(pair * j + u)[0], i, split(pair * j + u)[1]))
        return pl.BlockSpec((tm, tc), lambda i, j: (i, pair * j + u))

    def w_spec(u):
        return pl.BlockSpec((None, K, tc), lambda i, j: ((pair * j + u) // per, 0, (pair * j + u) % per))

    return _call(body, name=name, grid=(M // tm, nb * per // pair),
                 in_specs=[dy_spec(u) for u in range(pair)] + [w_spec(u) for u in range(pair)],
                 out_specs=[pl.BlockSpec((tm, K), lambda i, j: (i, 0))],
                 out_shape=[jax.ShapeDtypeStruct((M, K), F32)], args=[dy] * pair + [w] * pair,
                 sem=("parallel", "arbitrary"), jobs=jobs)[0]


def matmul_tn(a, dy, nb, name, out_dtype=BF16, kv_split=False, jobs=()):
    M, K = a.shape
    N = 2 * dy.shape[-1] if kv_split else dy.shape[-1]
    n = N // nb
    D = N // (2 * N_GROUPS)
    tn = _col_tile(n, D if kv_split else None, 1024)
    per = n // tn
    tk = _tile(K, 1024)

    def body(a_ref, dy_ref, o_ref):
        o_ref[...] = lax.dot_general(a_ref[...], dy_ref[...], (((0,), (0,)), ((), ())),
                                     preferred_element_type=F32).astype(o_ref.dtype)

    if kv_split:
        split = _kv_split_index(tn, D)
        dy_spec = pl.BlockSpec((None, M, tn), lambda k, j: (split(j)[0], 0, split(j)[1]))
    else:
        dy_spec = pl.BlockSpec((M, tn), lambda k, j: (0, j))
    return _call(body, name=name, grid=(K // tk, nb * per),
                 in_specs=[pl.BlockSpec((M, tk), lambda k, j: (0, k)), dy_spec],
                 out_specs=[pl.BlockSpec((None, tk, tn), lambda k, j: (j // per, k, j % per))],
                 out_shape=[jax.ShapeDtypeStruct((nb, K, n), out_dtype)], args=[a, dy],
                 sem=("parallel", "parallel"), jobs=jobs)[0]


def small_dot(a, b, contract, name, jobs=()):
    if contract == "nn":
        dims = (((1,), (0,)), ((), ()))
        out = (a.shape[0], b.shape[1])
    else:
        dims = (((1,), (1,)), ((), ()))
        out = (a.shape[0], b.shape[0])

    def body(a_ref, b_ref, o_ref):
        o_ref[...] = lax.dot_general(a_ref[...], b_ref[...], dims, precision=lax.Precision.HIGHEST,
                                     preferred_element_type=F32)

    return _call(body, name=name, in_specs=[VMEM_SPEC, VMEM_SPEC], out_specs=[VMEM_SPEC],
                 out_shape=[jax.ShapeDtypeStruct(out, F32)], args=[a, b], jobs=jobs)[0]


def rms_fwd(x, gains, name, jobs=()):
    S, D = x.shape
    T = _tile(S, 512, 8)
    n = len(gains)

    def body(x_ref, *refs):
        xv = x_ref[...]
        xn = xv * lax.rsqrt(jnp.mean(xv * xv, axis=-1, keepdims=True) + EPS)
        for g_ref, o_ref in zip(refs[:n], refs[n:]):
            o_ref[...] = (xn * g_ref[...]).astype(o_ref.dtype)

    row = pl.BlockSpec((T, D), lambda i: (i, 0))
    vec = pl.BlockSpec((1, D), lambda i: (0, 0))
    return _call(body, name=name, grid=(S // T,), in_specs=[row] + [vec] * n, out_specs=[row] * n,
                 out_shape=[jax.ShapeDtypeStruct((S, D), BF16)] * n, args=[x, *gains],
                 sem=("parallel",), jobs=jobs)


def rms_bwd(x, dhs, gains, dres, name, want_bf16, jobs=()):
    S, D = x.shape
    T = _tile(S, 256, 8)
    n = len(gains)

    def body(x_ref, *refs):
        dh_refs = refs[:n]
        g_refs = refs[n:2 * n]
        dres_ref = refs[2 * n]
        outs = refs[2 * n + 1:]
        dx_ref, dg_ref = outs[0], outs[-1]
        i = pl.program_id(0)

        @pl.when(i == 0)
        def _():
            dg_ref[...] = jnp.zeros_like(dg_ref)

        xv = x_ref[...]
        r = lax.rsqrt(jnp.mean(xv * xv, axis=-1, keepdims=True) + EPS)
        xn = xv * r
        dxn = jnp.zeros_like(xv)
        for k in range(n):
            dh = dh_refs[k][...]
            dg_ref[k:k + 1, :] += jnp.sum(dh * xn, axis=0, keepdims=True)
            dxn = dxn + dh * g_refs[k][...]
        dx = dres_ref[...] + r * (dxn - xn * jnp.mean(dxn * xn, axis=-1, keepdims=True))
        dx_ref[...] = dx
        if want_bf16:
            outs[1][...] = dx.astype(BF16)

    row = pl.BlockSpec((T, D), lambda i: (i, 0))
    vec = pl.BlockSpec((1, D), lambda i: (0, 0))
    acc = pl.BlockSpec((8, D), lambda i: (0, 0))
    out_specs = [row] + ([row] if want_bf16 else []) + [acc]
    out_shape = ([jax.ShapeDtypeStruct((S, D), F32)]
                 + ([jax.ShapeDtypeStruct((S, D), BF16)] if want_bf16 else [])
                 + [jax.ShapeDtypeStruct((8, D), F32)])
    return _call(body, name=name, grid=(S // T,), in_specs=[row] + [row] * n + [vec] * n + [row],
                 out_specs=out_specs, out_shape=out_shape, args=[x, *dhs, *gains, dres],
                 sem=("arbitrary",), jobs=jobs)


def final_loss(x2, gain, target, name, jobs=()):
    S, D = x2.shape
    T = _tile(S, 256, 8)

    def body(x_ref, g_ref, t_ref, dx_ref, dxb_ref, acc_ref):
        i = pl.program_id(0)

        @pl.when(i == 0)
        def _():
            acc_ref[...] = jnp.zeros_like(acc_ref)

        xv = x_ref[...]
        g = g_ref[...]
        r = lax.rsqrt(jnp.mean(xv * xv, axis=-1, keepdims=True) + EPS)
        xn = xv * r
        err = xn * g - t_ref[...]
        dy = err * (1.0 / D)
        acc_ref[0:1, :] += jnp.sum(dy * xn, axis=0, keepdims=True)
        acc_ref[1:2, :] += jnp.full((1, D), 0.5 / D, F32) * jnp.sum(err * err)
        dxn = dy * g
        dx = r * (dxn - xn * jnp.mean(dxn * xn, axis=-1, keepdims=True))
        dx_ref[...] = dx
        dxb_ref[...] = dx.astype(BF16)

    row = pl.BlockSpec((T, D), lambda i: (i, 0))
    return _call(body, name=name, grid=(S // T,),
                 in_specs=[row, pl.BlockSpec((1, D), lambda i: (0, 0)), row],
                 out_specs=[row, row, pl.BlockSpec((8, D), lambda i: (0, 0))],
                 out_shape=[jax.ShapeDtypeStruct((S, D), F32), jax.ShapeDtypeStruct((S, D), BF16),
                            jax.ShapeDtypeStruct((8, D), F32)],
                 args=[x2, gain, target], sem=("arbitrary",), jobs=jobs)


ROW_CHUNK = 64
LANE_CHUNK = 512
SUBLANES = 8


def _shifted_copies(buf, sh_scr, l0, lc):
    n = buf.shape[0] - SUBLANES
    for b in range(1, SUBLANES):
        sh_scr[b - 1, 0:n, :] = buf[b:b + n, l0:l0 + lc]


def _shifted(buf, sh_scr, start, rows, l0, lc):
    a8, b = (start // SUBLANES) * SUBLANES, start % SUBLANES
    if b == 0:
        return buf[a8:a8 + rows, l0:l0 + lc]
    return sh_scr[b - 1, a8:a8 + rows, :]


def conf_fwd(proj, cw, cb, lg, lb, name, jobs=()):
    S, E3 = proj.shape
    E = E3 // 3
    T = _tile(S, 256, HALO)
    R = T // HALO
    lc = _tile(E, LANE_CHUNK)
    rc = min(ROW_CHUNK, T)

    def body(a_ref, b_ref, z_ref, ap_ref, bp_ref, cw_ref, cb_ref, lg_ref, lb_ref, c_ref, y_ref, u_scr, sh_scr):
        i = pl.program_id(0)
        up = ap_ref[...] * _sigmoid(bp_ref[...])
        u_scr[0:HALO, :] = jnp.where(i > 0, up, 0.0)
        u_scr[HALO:HALO + T, :] = a_ref[...] * _sigmoid(b_ref[...])
        off = HALO - (CONV_WIDTH - 1)
        for l0 in range(0, E, lc):
            _shifted_copies(u_scr, sh_scr, l0, lc)
            for r0 in range(0, T, rc):
                acc = jnp.broadcast_to(cb_ref[:, l0:l0 + lc], (rc, lc))
                for k in range(CONV_WIDTH):
                    acc = acc + _shifted(u_scr, sh_scr, r0 + off + k, rc, l0, lc) * cw_ref[k:k + 1, l0:l0 + lc]
                c_ref[r0:r0 + rc, l0:l0 + lc] = acc
        c = c_ref[...]
        mu = jnp.mean(c, axis=-1, keepdims=True)
        d = c - mu
        var = jnp.mean(d * d, axis=-1, keepdims=True)
        cn = d * lax.rsqrt(var + EPS) * lg_ref[...] + lb_ref[...]
        z = z_ref[...]
        y_ref[...] = ((cn * _sigmoid(cn)) * (z * _sigmoid(z))).astype(BF16)

    def col(j):
        return pl.BlockSpec((T, E), lambda i: (i, j))

    def prev(j):
        return pl.BlockSpec((HALO, E), lambda i: (jnp.maximum(i * R - 1, 0), j))

    vec = pl.BlockSpec((1, E), lambda i: (0, 0))
    return _call(body, name=name, grid=(S // T,),
                 in_specs=[col(0), col(1), col(2), prev(0), prev(1),
                           pl.BlockSpec((HALO, E), lambda i: (0, 0)), vec, vec, vec],
                 out_specs=[pl.BlockSpec((T, E), lambda i: (i, 0))] * 2,
                 out_shape=[jax.ShapeDtypeStruct((S, E), F32), jax.ShapeDtypeStruct((S, E), BF16)],
                 scratch_shapes=[pltpu.VMEM((HALO + T, E), F32), pltpu.VMEM((SUBLANES - 1, HALO + T, lc), F32)],
                 args=[proj, proj, proj, proj, proj, cw, cb, lg, lb], sem=("parallel",), jobs=jobs)


def conf_bwd_ln(c, dy, proj, lg, lb, name, jobs=()):
    S, E = c.shape
    T = _tile(S, 256, 8)

    def body(c_ref, dy_ref, z_ref, lg_ref, lb_ref, dc_ref, dz_ref, acc_ref):
        i = pl.program_id(0)

        @pl.when(i == 0)
        def _():
            acc_ref[...] = jnp.zeros_like(acc_ref)

        cv = c_ref[...]
        mu = jnp.mean(cv, axis=-1, keepdims=True)
        d = cv - mu
        var = jnp.mean(d * d, axis=-1, keepdims=True)
        rstd = lax.rsqrt(var + EPS)
        xh = d * rstd
        lgv = lg_ref[...]
        cn = xh * lgv + lb_ref[...]
        z = z_ref[...]
        dy = dy_ref[...]
        sc = _sigmoid(cn)
        sz = _sigmoid(z)
        dcn = dy * (z * sz) * (sc * (1.0 + cn * (1.0 - sc)))
        dz_ref[...] = (dy * (cn * sc) * (sz * (1.0 + z * (1.0 - sz)))).astype(BF16)
        acc_ref[0:1, :] += jnp.sum(dcn * xh, axis=0, keepdims=True)
        acc_ref[1:2, :] += jnp.sum(dcn, axis=0, keepdims=True)
        dxh = dcn * lgv
        dc = rstd * (dxh - jnp.mean(dxh, axis=-1, keepdims=True)
                     - xh * jnp.mean(dxh * xh, axis=-1, keepdims=True))
        acc_ref[2:3, :] += jnp.sum(dc, axis=0, keepdims=True)
        dc_ref[...] = dc

    row = pl.BlockSpec((T, E), lambda i: (i, 0))
    vec = pl.BlockSpec((1, E), lambda i: (0, 0))
    return _call(body, name=name, grid=(S // T,),
                 in_specs=[row, row, pl.BlockSpec((T, E), lambda i: (i, 2)), vec, vec],
                 out_specs=[row, row, pl.BlockSpec((8, E), lambda i: (0, 0))],
                 out_shape=[jax.ShapeDtypeStruct((S, E), F32), jax.ShapeDtypeStruct((S, E), BF16),
                            jax.ShapeDtypeStruct((8, E), F32)],
                 args=[c, dy, proj, lg, lb], sem=("arbitrary",), jobs=jobs)


def conf_bwd_conv(proj, dc, dz, cw, name, jobs=()):
    S, E3 = proj.shape
    E = E3 // 3
    T = _tile(S, 256, HALO)
    R = T // HALO
    nt = S // T
    lc = _tile(E, LANE_CHUNK)
    rc = min(ROW_CHUNK, T)

    def body(a_ref, b_ref, ap_ref, bp_ref, dc_ref, dcn_ref, dz_ref, cw_ref, o_ref, dw_ref, u_scr, dc_scr, sh_scr):
        i = pl.program_id(0)

        @pl.when(i == 0)
        def _():
            dw_ref[...] = jnp.zeros_like(dw_ref)

        a = a_ref[...]
        sb = _sigmoid(b_ref[...])
        up = ap_ref[...] * _sigmoid(bp_ref[...])
        u_scr[0:HALO, :] = jnp.where(i > 0, up, 0.0)
        u_scr[HALO:HALO + T, :] = a * sb
        dc_scr[0:T, :] = dc_ref[...]
        dc_scr[T:T + HALO, :] = jnp.where(i < nt - 1, dcn_ref[...], 0.0)
        off = HALO - (CONV_WIDTH - 1)
        for l0 in range(0, E, lc):
            _shifted_copies(u_scr, sh_scr, l0, lc)
            for k in range(CONV_WIDTH):
                prod = _shifted(u_scr, sh_scr, off + k, T, l0, lc) * dc_scr[0:T, l0:l0 + lc]
                dw_ref[k:k + 1, l0:l0 + lc] += jnp.sum(prod, axis=0, keepdims=True)
            _shifted_copies(dc_scr, sh_scr, l0, lc)
            for r0 in range(0, T, rc):
                acc = jnp.zeros((rc, lc), F32)
                for k in range(CONV_WIDTH):
                    s0 = r0 + (CONV_WIDTH - 1) - k
                    acc = acc + _shifted(dc_scr, sh_scr, s0, rc, l0, lc) * cw_ref[k:k + 1, l0:l0 + lc]
                av = a[r0:r0 + rc, l0:l0 + lc]
                sv = sb[r0:r0 + rc, l0:l0 + lc]
                o_ref[r0:r0 + rc, l0:l0 + lc] = (acc * sv).astype(BF16)
                o_ref[r0:r0 + rc, E + l0:E + l0 + lc] = (acc * av * sv * (1.0 - sv)).astype(BF16)
        o_ref[:, 2 * E:3 * E] = dz_ref[...]

    def col(j):
        return pl.BlockSpec((T, E), lambda i: (i, j))

    def prev(j):
        return pl.BlockSpec((HALO, E), lambda i: (jnp.maximum(i * R - 1, 0), j))

    row = pl.BlockSpec((T, E), lambda i: (i, 0))
    nxt = pl.BlockSpec((HALO, E), lambda i: (jnp.minimum((i + 1) * R, S // HALO - 1), 0))
    return _call(body, name=name, grid=(nt,),
                 in_specs=[col(0), col(1), prev(0), prev(1), row, nxt, row,
                           pl.BlockSpec((HALO, E), lambda i: (0, 0))],
                 out_specs=[pl.BlockSpec((T, E3), lambda i: (i, 0)), pl.BlockSpec((HALO, E), lambda i: (0, 0))],
                 out_shape=[jax.ShapeDtypeStruct((S, E3), BF16), jax.ShapeDtypeStruct((HALO, E), F32)],
                 scratch_shapes=[pltpu.VMEM((HALO + T, E), F32), pltpu.VMEM((T + HALO, E), F32),
                                 pltpu.VMEM((SUBLANES - 1, HALO + T, lc), F32)],
                 args=[proj, proj, proj, proj, dc, dc, dz, cw], sem=("arbitrary",), jobs=jobs)


def bucket_tables():
    q = np.arange(BLOCK)[:, None]
    k = np.arange(2 * BLOCK)[None, :]
    out = []
    for window, dil in GROUPS:
        delta = q + BLOCK - k
        valid = (delta >= 0) & (delta <= window // dil)
        dist = np.clip(delta, 0, None) * dil
        large = MAX_EXACT + (np.log(np.maximum(dist, 1).astype(np.float32) / MAX_EXACT)
                             / math.log(MAX_DISTANCE / MAX_EXACT)
                             * (N_BUCKETS - MAX_EXACT)).astype(np.int32)
        large = np.minimum(large, N_BUCKETS - 1)
        bucket = np.where(dist < MAX_EXACT, dist, large)
        out.append(np.where(valid, bucket, -1).reshape(-1))
    return np.concatenate(out).astype(np.int32)


def _band_masks(has_previous):
    ql = lax.broadcasted_iota(jnp.int32, (BLOCK, 2 * BLOCK), 0)
    kk = lax.broadcasted_iota(jnp.int32, (BLOCK, 2 * BLOCK), 1)
    band = (kk >= ql) & (kk <= ql + BLOCK)
    return band, band & ((kk >= BLOCK) | has_previous)


def _dot_nt(a, b):
    return lax.dot_general(a, b, (((1,), (1,)), ((), ())), preferred_element_type=F32)


def _dot_tn(a, b):
    return lax.dot_general(a, b, (((0,), (0,)), ((), ())), preferred_element_type=F32)


ATTN_ROWS = 2048


def _sub(start, size, dil):
    return pl.ds(start, size) if dil == 1 else pl.ds(start, size, stride=dil)


def _attn_geometry(S, dil):
    halo = BLOCK * dil
    rows = max(min(S, ATTN_ROWS), halo)
    return halo, rows, S // rows, rows // halo


MAX_ROW_STRIDE = 8


def _split(dst, src, n, dil, tmp):
    if dil <= MAX_ROW_STRIDE:
        for r in range(dil):
            dst[r] = src[_sub(r, n, dil), :].astype(dst.dtype)
        return
    f, g = 4, dil // 4
    for r1 in range(f):
        tmp[0:n * g, :] = src[_sub(r1, n * g, f), :]
        for r2 in range(g):
            dst[r2 * f + r1] = tmp[_sub(r2, n, g), :].astype(dst.dtype)


def _merge(dst, src, n, dil, tmp):
    if dil <= MAX_ROW_STRIDE:
        for r in range(dil):
            dst[_sub(r, n, dil), :] = src[r]
        return
    f, g = 4, dil // 4
    for r1 in range(f):
        for r2 in range(g):
            tmp[_sub(r2, n, g), :] = src[r2 * f + r1]
        dst[_sub(r1, n * g, f), :] = tmp[0:n * g, :]


def _split_tmp(sub, dil):
    rows = (sub + BLOCK) * (dil // 4) if dil > MAX_ROW_STRIDE else SUBLANES
    return pltpu.VMEM((rows, HEAD_DIM), F32)


def attn_fwd(qz, kv, bias, g, dil, name, jobs=()):
    S = qz.shape[0]
    D = qz.shape[1] // (N_GROUPS + 1)
    H = D // HEAD_DIM
    halo, rows, nsb, nblk = _attn_geometry(S, dil)
    sub = nblk * BLOCK
    scale = HEAD_DIM ** -0.5

    def body(q_ref, kvc_ref, kvp_ref, b_ref, o_ref, l_ref, ks, vs, qd, kd, vd, od, ld, tmp):
        sb = pl.program_id(1)
        ks[0:halo, :] = kvp_ref[0].astype(F32)
        ks[halo:, :] = kvc_ref[0].astype(F32)
        vs[0:halo, :] = kvp_ref[1].astype(F32)
        vs[halo:, :] = kvc_ref[1].astype(F32)
        _split(kd, ks, sub + BLOCK, dil, tmp)
        _split(vd, vs, sub + BLOCK, dil, tmp)
        _split(qd, q_ref, sub, dil, tmp)
        band, first = _band_masks(sb > 0)
        bias_t = b_ref[0]
        for r in range(dil):
            for jj in range(nblk):
                q = qd[r, jj * BLOCK:(jj + 1) * BLOCK, :]
                keys = slice(jj * BLOCK, (jj + 2) * BLOCK)
                s = _dot_nt(q, kd[r, keys, :]) * scale + bias_t
                s = jnp.where(band if jj > 0 else first, s, -jnp.inf)
                m = jnp.max(s, axis=-1, keepdims=True)
                p = jnp.exp(s - m)
                den = jnp.sum(p, axis=-1, keepdims=True)
                pv = jnp.dot(p.astype(BF16), vd[r, keys, :], preferred_element_type=F32)
                own = slice(jj * BLOCK, (jj + 1) * BLOCK)
                od[r, own, :] = pv / den
                ld[r, own, :] = jnp.broadcast_to(m + jnp.log(den), (BLOCK, HEAD_DIM))
        _merge(o_ref, od, sub, dil, tmp)
        _merge(l_ref, ld, sub, dil, tmp)

    per = rows // halo
    out_spec = pl.BlockSpec((rows, HEAD_DIM), lambda h, sb: (sb, h))
    return _call(body, name=name, grid=(H, nsb),
                 in_specs=[pl.BlockSpec((rows, HEAD_DIM), lambda h, sb: (sb, g * H + h)),
                           pl.BlockSpec((2, rows, HEAD_DIM), lambda h, sb: (0, sb, g * H + h)),
                           pl.BlockSpec((2, halo, HEAD_DIM),
                                        lambda h, sb: (0, jnp.maximum(sb * per - 1, 0), g * H + h)),
                           pl.BlockSpec((1, BLOCK, 2 * BLOCK), lambda h, sb: (h, 0, 0))],
                 out_specs=[out_spec, out_spec],
                 out_shape=[jax.ShapeDtypeStruct((S, D), F32)] * 2,
                 scratch_shapes=[pltpu.VMEM((halo + rows, HEAD_DIM), F32)] * 2
                 + [pltpu.VMEM((dil, sub, HEAD_DIM), BF16)]
                 + [pltpu.VMEM((dil, sub + BLOCK, HEAD_DIM), BF16)] * 2
                 + [pltpu.VMEM((dil, sub, HEAD_DIM), F32)] * 2 + [_split_tmp(sub, dil)],
                 args=[qz, kv, kv, bias], sem=("parallel", "arbitrary"), jobs=jobs)


def merge_fwd(os_, lses, qz, name, jobs=()):
    S, D = os_[0].shape
    T = _tile(S, 256, 8)

    def body(o1, o2, o3, l1, l2, l3, z_ref, o_ref, lse_ref, y_ref):
        la, lb_, lc_ = l1[...], l2[...], l3[...]
        m = jnp.maximum(jnp.maximum(la, lb_), lc_)
        ea, eb, ec = jnp.exp(la - m), jnp.exp(lb_ - m), jnp.exp(lc_ - m)
        den = ea + eb + ec
        o = (ea * o1[...] + eb * o2[...] + ec * o3[...]) / den
        z = z_ref[...]
        o_ref[...] = o
        lse_ref[...] = m + jnp.log(den)
        y_ref[...] = (o * (z * _sigmoid(z))).astype(BF16)

    row = pl.BlockSpec((T, D), lambda i: (i, 0))
    return _call(body, name=name, grid=(S // T,),
                 in_specs=[row] * 6 + [pl.BlockSpec((T, D), lambda i: (i, N_GROUPS))],
                 out_specs=[row] * 3,
                 out_shape=[jax.ShapeDtypeStruct((S, D), F32), jax.ShapeDtypeStruct((S, D), F32),
                            jax.ShapeDtypeStruct((S, D), BF16)],
                 args=[*os_, *lses, qz], sem=("parallel",), jobs=jobs)


def merge_bwd(dy2, o, qz, name, jobs=()):
    S, D = o.shape
    H = D // HEAD_DIM
    T = _tile(S, 256, 8)
    nq = N_GROUPS + 1

    def body(dy_ref, o_ref, z_ref, do_ref, dl_ref, dqz_ref):
        dy = dy_ref[...]
        ov = o_ref[...]
        z = z_ref[...]
        sz = _sigmoid(z)
        do = dy * (z * sz)
        do_ref[...] = do.astype(BF16)
        dqz_ref[...] = (dy * ov * (sz * (1.0 + z * (1.0 - sz)))).astype(BF16)
        prod = do * ov
        for h in range(H):
            hs = slice(h * HEAD_DIM, (h + 1) * HEAD_DIM)
            dl_ref[:, hs] = jnp.broadcast_to(jnp.sum(prod[:, hs], axis=-1, keepdims=True), (T, HEAD_DIM))

    row = pl.BlockSpec((T, D), lambda i: (i, 0))
    last = pl.BlockSpec((T, D), lambda i: (i, N_GROUPS))
    return _call(body, name=name, grid=(S // T,), in_specs=[row, row, last], out_specs=[row, row, last],
                 out_shape=[jax.ShapeDtypeStruct((S, D), BF16), jax.ShapeDtypeStruct((S, D), F32),
                            jax.ShapeDtypeStruct((S, nq * D), BF16)],
                 args=[dy2, o, qz], sem=("parallel",), jobs=jobs)


def attn_bwd(qz, kv, do, lse, delta, bias, dqz, dkv, g, dil, name, jobs=()):
    S = qz.shape[0]
    D = qz.shape[1] // (N_GROUPS + 1)
    H = D // HEAD_DIM
    halo, rows, nsb, nblk = _attn_geometry(S, dil)
    sub = nblk * BLOCK
    scale = HEAD_DIM ** -0.5
    have_dkv = dkv is not None

    def body(*refs):
        q_ref, do_ref, l_ref, d_ref, kvc_ref, kvp_ref, b_ref = refs[:7]
        n_in = 7 + 1 + (1 if have_dkv else 0)
        (dq_ref, dkv_ref, ds_ref, ks, vs, dks, dvs, dos, dqs, carry_k, carry_v,
         qd, dod, ld, dd, dqd, kd, vd, dkd, dvd, tmp) = refs[n_in:]
        i = pl.program_id(1)
        sb = nsb - 1 - i

        @pl.when(i == 0)
        def _():
            ds_ref[...] = jnp.zeros_like(ds_ref)

        ks[0:halo, :] = kvp_ref[0].astype(F32)
        ks[halo:, :] = kvc_ref[0].astype(F32)
        vs[0:halo, :] = kvp_ref[1].astype(F32)
        vs[halo:, :] = kvc_ref[1].astype(F32)
        dos[...] = do_ref[...].astype(F32)
        _split(kd, ks, sub + BLOCK, dil, tmp)
        _split(vd, vs, sub + BLOCK, dil, tmp)
        for dst, src in ((qd, q_ref), (dod, dos), (ld, l_ref), (dd, d_ref)):
            _split(dst, src, sub, dil, tmp)
        dkd[...] = jnp.zeros_like(dkd)
        dvd[...] = jnp.zeros_like(dvd)
        band, first = _band_masks(sb > 0)
        bias_t = b_ref[0]
        for r in range(dil):
            for jj in range(nblk):
                own = slice(jj * BLOCK, (jj + 1) * BLOCK)
                keys = slice(jj * BLOCK, (jj + 2) * BLOCK)
                q = qd[r, own, :]
                do = dod[r, own, :]
                k = kd[r, keys, :]
                v = vd[r, keys, :]
                lse = ld[r, own, :]
                dlt = dd[r, own, :]
                s = _dot_nt(q, k) * scale + bias_t - jnp.concatenate([lse, lse], axis=-1)
                p = jnp.where(band if jj > 0 else first, jnp.exp(s), 0.0)
                ds = p * (_dot_nt(do, v) - jnp.concatenate([dlt, dlt], axis=-1))
                ds_ref[0] += ds
                dsb = ds.astype(BF16)
                dqd[r, own, :] = jnp.dot(dsb, k, preferred_element_type=F32) * scale
                dkd[r, keys, :] += _dot_tn(dsb, q) * scale
                dvd[r, keys, :] += _dot_tn(p.astype(BF16), do)
        _merge(dqs, dqd, sub, dil, tmp)
        _merge(dks, dkd, sub + BLOCK, dil, tmp)
        _merge(dvs, dvd, sub + BLOCK, dil, tmp)

        @pl.when(i > 0)
        def _():
            dks[rows:rows + halo, :] += carry_k[...]
            dvs[rows:rows + halo, :] += carry_v[...]

        dkv_ref[0] = dks[halo:, :].astype(BF16)
        dkv_ref[1] = dvs[halo:, :].astype(BF16)
        carry_k[...] = dks[0:halo, :]
        carry_v[...] = dvs[0:halo, :]
        dq_ref[...] = dqs[...].astype(BF16)

    per = rows // halo

    def rev(i):
        return nsb - 1 - i

    bias_spec = pl.BlockSpec((1, BLOCK, 2 * BLOCK), lambda h, i: (h, 0, 0))
    row_spec = pl.BlockSpec((rows, HEAD_DIM), lambda h, i: (rev(i), h))
    in_specs = [pl.BlockSpec((rows, HEAD_DIM), lambda h, i: (rev(i), g * H + h)),
                row_spec, row_spec, row_spec,
                pl.BlockSpec((2, rows, HEAD_DIM), lambda h, i: (0, rev(i), g * H + h)),
                pl.BlockSpec((2, halo, HEAD_DIM),
                             lambda h, i: (0, jnp.maximum(rev(i) * per - 1, 0), g * H + h)),
                bias_spec, ANY_SPEC]
    args = [qz, do, lse, delta, kv, kv, bias, dqz]
    aliases = {7: 0}
    if have_dkv:
        in_specs.append(ANY_SPEC)
        args.append(dkv)
        aliases[8] = 1
    blk = (halo + rows, HEAD_DIM)
    own, keys = (dil, sub, HEAD_DIM), (dil, sub + BLOCK, HEAD_DIM)
    return _call(body, name=name, grid=(H, nsb), in_specs=in_specs,
                 out_specs=[pl.BlockSpec((rows, HEAD_DIM), lambda h, i: (rev(i), g * H + h)),
                            pl.BlockSpec((2, rows, HEAD_DIM), lambda h, i: (0, rev(i), g * H + h)),
                            bias_spec],
                 out_shape=[jax.ShapeDtypeStruct(qz.shape, BF16),
                            jax.ShapeDtypeStruct((2, S, N_GROUPS * D), BF16),
                            jax.ShapeDtypeStruct((H, BLOCK, 2 * BLOCK), F32)],
                 scratch_shapes=[pltpu.VMEM(blk, F32), pltpu.VMEM(blk, F32), pltpu.VMEM(blk, F32),
                                 pltpu.VMEM(blk, F32),
                                 pltpu.VMEM((rows, HEAD_DIM), F32), pltpu.VMEM((rows, HEAD_DIM), F32),
                                 pltpu.VMEM((halo, HEAD_DIM), F32), pltpu.VMEM((halo, HEAD_DIM), F32),
                                 pltpu.VMEM(own, BF16), pltpu.VMEM(own, BF16), pltpu.VMEM(own, F32),
                                 pltpu.VMEM(own, F32), pltpu.VMEM(own, F32),
                                 pltpu.VMEM(keys, BF16), pltpu.VMEM(keys, BF16),
                                 pltpu.VMEM(keys, F32), pltpu.VMEM(keys, F32), _split_tmp(sub, dil)],
                 aliases=aliases, args=args, sem=("parallel", "arbitrary"), jobs=jobs)


def _adamw(w, g, m, v):
    m = ADAM_B1 * m + (1.0 - ADAM_B1) * g
    v = ADAM_B2 * v + (1.0 - ADAM_B2) * (g * g)
    m_hat = m / (1.0 - ADAM_B1 ** ADAM_STEP)
    v_hat = v / (1.0 - ADAM_B2 ** ADAM_STEP)
    delta = -ADAM_LR * (m_hat / (jnp.sqrt(v_hat) + ADAM_EPS) + ADAM_WD * w)
    return delta, m, v


def add_pairs(dw, got, name, jobs=()):
    _, K, n = got.shape
    tk = _tile(K, 512, 16)

    def own_block(r, i):
        x, y, c = _place()
        return 4 * ((x + r // 2) % 2) + 2 * ((y + r % 2) % 2) + c, i, 0

    def body(a_ref, b_ref, o_ref):
        o_ref[...] = (a_ref[...].astype(F32) + b_ref[...].astype(F32)).astype(o_ref.dtype)

    blk = pl.BlockSpec((None, tk, n), lambda r, i: (r, i, 0))
    return _call(body, name=name, grid=(N_CHIPS, K // tk),
                 in_specs=[pl.BlockSpec((None, tk, n), own_block), blk], out_specs=[blk],
                 out_shape=[jax.ShapeDtypeStruct(got.shape, got.dtype)], args=[dw, got],
                 sem=("parallel", "parallel"), jobs=jobs)[0]


def adamw_reduce(part, got, w, m, v, name, jobs=()):
    K, n = w.shape
    tk = _tile(K, 256, 8)

    def body(p_ref, r_ref, w_ref, m_ref, v_ref, g_ref, d_ref, nm_ref, nv_ref):
        g = p_ref[...].astype(F32)
        for r in range(N_CHIPS - 1):
            g = g + r_ref[r].astype(F32)
        d, nm, nv = _adamw(w_ref[...], g, m_ref[...], v_ref[...])
        g_ref[...] = g
        d_ref[...] = d
        nm_ref[...] = nm
        nv_ref[...] = nv

    blk = pl.BlockSpec((tk, n), lambda i: (i, 0))
    return _call(body, name=name, grid=(K // tk,),
                 in_specs=[pl.BlockSpec((None, tk, n), lambda i: (0, i, 0)),
                           pl.BlockSpec((N_CHIPS - 1, tk, n), lambda i: (0, i, 0)), blk, blk, blk],
                 out_specs=[blk] * 4, out_shape=[jax.ShapeDtypeStruct((K, n), F32)] * 4,
                 args=[part, got, w, m, v], sem=("parallel",), jobs=jobs)


def sum_parts(parts, name):
    _, R, D = parts.shape

    def body(p_ref, o_ref):
        g = p_ref[0]
        for r in range(1, N_DEV):
            g = g + p_ref[r]
        o_ref[...] = g

    return _call(body, name=name, in_specs=[VMEM_SPEC], out_specs=[VMEM_SPEC],
                 out_shape=[jax.ShapeDtypeStruct((R, D), F32)], args=[parts])[0]


def adamw_small(w, g, m, v, name):
    def body(w_ref, g_ref, m_ref, v_ref, d_ref, nm_ref, nv_ref):
        d, nm, nv = _adamw(w_ref[...], g_ref[...], m_ref[...], v_ref[...])
        d_ref[...] = d
        nm_ref[...] = nm
        nv_ref[...] = nv

    return _call(body, name=name, in_specs=[VMEM_SPEC] * 4, out_specs=[VMEM_SPEC] * 3,
                 out_shape=[jax.ShapeDtypeStruct(w.shape, F32)] * 3, args=[w, g, m, v])


def _row(v, at):
    return jnp.pad(v.reshape(1, -1), ((at, 7 - at), (0, 0)))


def _pack_sharded(norm, conv_w, conv_b, ln_g, ln_b):
    n = norm.shape[-1]
    taps = jnp.pad(conv_w.reshape(CONV_WIDTH, n), ((0, HALO - CONV_WIDTH), (0, 0)))
    return jnp.concatenate([_row(norm, 0), _row(ln_g, 0) + _row(ln_b, 1) + _row(conv_b, 2), taps], axis=0)


def _pack_rel(rel_bias, D):
    return jnp.pad(rel_bias.reshape(1, -1), ((0, 7), (0, D - rel_bias.size)))


def _pack_replicated(kv_norm, b_norm, final_norm, rel_bias, D):
    return jnp.concatenate([_row(kv_norm, 0) + _row(b_norm, 1), _row(final_norm, 0), _pack_rel(rel_bias, D)], axis=0)


def kernel(x, a_norm, a_w_in, a_conv_w, a_conv_b, a_ln_g, a_ln_b, a_w_out, kv_norm, w_kv, b_norm, b_w_in, b_w_out, rel_bias, final_norm, loss_target, m_a_norm, m_a_w_in, m_a_conv_w, m_a_conv_b, m_a_ln_g, m_a_ln_b, m_a_w_out, m_kv_norm, m_w_kv, m_b_norm, m_b_w_in, m_b_w_out, m_rel_bias, m_final_norm, v_a_norm, v_a_w_in, v_a_conv_w, v_a_conv_b, v_a_ln_g, v_a_ln_b, v_a_w_out, v_kv_norm, v_w_kv, v_b_norm, v_b_w_in, v_b_w_out, v_rel_bias, v_final_norm):
    _, S, D = x.shape
    E = D
    H = D // HEAD_DIM
    nsh = D // N_DEV
    x0 = x.reshape(S, D)
    target = loss_target.reshape(S, D)
    kvn, bn, fn = kv_norm.reshape(1, D), b_norm.reshape(1, D), final_norm.reshape(1, D)

    names = ["a_w_in", "a_w_out", "w_kv", "b_w_in", "b_w_out"]
    big_w = dict(zip(names, [a_w_in[0], a_w_out[0], w_kv, b_w_in[0], b_w_out[0]]))
    big_m = dict(zip(names, [m_a_w_in[0], m_a_w_out[0], m_w_kv, m_b_w_in[0], m_b_w_out[0]]))
    big_v = dict(zip(names, [v_a_w_in[0], v_a_w_out[0], v_w_kv, v_b_w_in[0], v_b_w_out[0]]))
    shard = {nm: w.astype(BF16) for nm, w in big_w.items()}
    sh_w = _pack_sharded(a_norm, a_conv_w, a_conv_b, a_ln_g, a_ln_b)

    wa_in, sh_all = all_gather([shard["a_w_in"], sh_w], "gather_first")
    sh_full = sh_all.transpose(1, 0, 2).reshape(SMALL_SH_ROWS, D)
    an, cw, cb = sh_full[0:1], sh_full[16:16 + HALO], sh_full[10:11]
    lg, lb = sh_full[8:9], sh_full[9:10]

    (h0,) = rms_fwd(x0, [an], "rms_a")
    g_a_out = gather_chips_job(shard["a_w_out"])
    proj = matmul_nn(h0, wa_in, "a_in", jobs=[g_a_out])
    g_a_out2 = gather_sibling_job(g_a_out.results[0])
    g_kv = gather_chips_job(shard["w_kv"])
    c, y = conf_fwd(proj, cw, cb, lg, lb, "conf_fwd", jobs=[g_a_out2, g_kv])
    wa_out = g_a_out2.results[0].reshape(1, E, D)
    g_kv2 = gather_sibling_job(g_kv.results[0])
    x1 = matmul_nn(y, wa_out, "a_out", res=x0, jobs=[g_kv2])
    wkv = g_kv2.results[0]
    hk, hb = rms_fwd(x1, [kvn, bn], "rms_b")
    g_b_in = gather_chips_job(shard["b_w_in"])
    kv = matmul_nn(hk, wkv, "kv_proj", out_dtype=BF16, kv_split=True, jobs=[g_b_in])

    bt = jnp.asarray(bucket_tables())
    onehot = (bt[None, :] == jnp.arange(N_BUCKETS, dtype=jnp.int32)[:, None]).astype(F32)
    g_b_in2 = gather_sibling_job(g_b_in.results[0])
    bias = small_dot(rel_bias.T, onehot, "nn", "bias_table", jobs=[g_b_in2])
    bias = bias.reshape(H, N_GROUPS, BLOCK, 2 * BLOCK)
    bias = [bias[:, g] for g in range(N_GROUPS)]
    wb_in = g_b_in2.results[0]
    g_b_out = gather_chips_job(shard["b_w_out"])
    qz = matmul_nn(hb, wb_in, "b_in", jobs=[g_b_out])
    g_b_out2 = gather_sibling_job(g_b_out.results[0])

    os_, lses = [], []
    for g, (_, dil) in enumerate(GROUPS):
        o_g, l_g = attn_fwd(qz, kv, bias[g], g, dil, "attn_fwd%d" % g,
                            jobs=[g_b_out2] if g == 0 else ())
        os_.append(o_g)
        lses.append(l_g)
    wb_out = g_b_out2.results[0].reshape(1, D, D)
    o, lse, y2 = merge_fwd(os_, lses, qz, "merge_fwd")
    x2 = matmul_nn(y2, wb_out, "b_out", res=x1)
    dx2, dx2b, fin_acc = final_loss(x2, fn, target, "final_loss")

    dy2 = matmul_nt(dx2b, wb_out, "b_out_dx")
    dwb_out = matmul_tn(y2, dx2b, 1, "b_out_dw").reshape(N_DEV, D // N_DEV, D)
    r_b_out = reduce_sibling_job(dwb_out)
    do, delta, dqz = merge_bwd(dy2, o, qz, "merge_bwd", jobs=[r_b_out])
    p_b_out = add_pairs(dwb_out, r_b_out.results[0], "pairs_b_out")
    r_b_out2 = reduce_chips_job(p_b_out)
    dkv = None
    ds_tabs = []
    for g, (_, dil) in enumerate(GROUPS):
        dqz, dkv, ds_tab = attn_bwd(qz, kv, do, lse, delta, bias[g], dqz, dkv, g, dil,
                                    "attn_bwd%d" % g, jobs=[r_b_out2] if g == 0 else ())
        ds_tabs.append(ds_tab.reshape(H, 2 * BLOCK * BLOCK))
    d_rel = small_dot(onehot, jnp.concatenate(ds_tabs, axis=1), "nt", "bias_grad")
    dhb = matmul_nt(dqz, wb_in, "b_in_dx")
    dwb_in = matmul_tn(hb, dqz, N_DEV, "b_in_dw")
    r_b_in = reduce_sibling_job(dwb_in)
    dhk = matmul_nt(dkv, wkv, "kv_dx", kv_split=True, jobs=[r_b_in])
    p_b_in = add_pairs(dwb_in, r_b_in.results[0], "pairs_b_in")
    r_b_in2 = reduce_chips_job(p_b_in)
    dw_kv = matmul_tn(hk, dkv, N_DEV, "kv_dw", kv_split=True, jobs=[r_b_in2])
    r_kv = reduce_sibling_job(dw_kv)
    dx1, dx1b, norm_acc = rms_bwd(x1, [dhk, dhb], [kvn, bn], dx2, "rms_b_bwd", True, jobs=[r_kv])
    p_kv = add_pairs(dw_kv, r_kv.results[0], "pairs_kv")
    r_kv2 = reduce_chips_job(p_kv)

    dy = matmul_nt(dx1b, wa_out, "a_out_dx")
    dwa_out = matmul_tn(y, dx1b, 1, "a_out_dw").reshape(N_DEV, E // N_DEV, D)
    r_a_out = reduce_sibling_job(dwa_out)
    dc, dz, ln_acc = conf_bwd_ln(c, dy, proj, lg, lb, "conf_bwd_ln", jobs=[r_a_out])
    p_a_out = add_pairs(dwa_out, r_a_out.results[0], "pairs_a_out")
    r_a_out2 = reduce_chips_job(p_a_out)
    dproj, dcw = conf_bwd_conv(proj, dc, dz, cw, "conf_bwd_conv", jobs=[r_kv2, r_a_out2])
    share = share_small_job(jnp.concatenate([ln_acc, dcw, norm_acc, fin_acc, _pack_rel(d_rel, D)], axis=0))
    dwa_in = matmul_tn(h0, dproj, N_DEV, "a_in_dw", jobs=[share])
    r_a_in = reduce_sibling_job(dwa_in)
    dh0 = matmul_nt(dproj, wa_in, "a_in_dx", jobs=[r_a_in])
    p_a_in = add_pairs(dwa_in, r_a_in.results[0], "pairs_a_in")
    r_a_in2 = reduce_chips_job(p_a_in)
    grad_x, a_acc = rms_bwd(x0, [dh0], [an], dx1, "rms_a_bwd", False, jobs=[r_a_in2])

    share_a = share_small_job(a_acc)
    sums = {"w_kv": (p_kv, r_kv2), "b_w_in": (p_b_in, r_b_in2), "b_w_out": (p_b_out, r_b_out2),
            "a_w_out": (p_a_out, r_a_out2), "a_w_in": (p_a_in, r_a_in2)}
    big_out = {}
    for nm, (part, job) in sums.items():
        big_out[nm] = adamw_reduce(part, job.results[0], big_w[nm], big_m[nm], big_v[nm], "adamw_" + nm,
                                   jobs=[share_a] if nm == "a_w_out" else ())

    gsum = jnp.concatenate([sum_parts(share_a.results[0], "sum_small_a"),
                            sum_parts(share.results[0], "sum_small")], axis=0)
    me = 4 * lax.axis_index("x") + 2 * lax.axis_index("y") + lax.axis_index("c")
    g_sh = lax.dynamic_slice(gsum, (0, me * nsh), (SMALL_SH_ROWS, nsh))
    g_rep = gsum[SMALL_SH_ROWS:]
    loss = g_rep[9, 0]
    sh_m = _pack_sharded(m_a_norm, m_a_conv_w, m_a_conv_b, m_a_ln_g, m_a_ln_b)
    sh_v = _pack_sharded(v_a_norm, v_a_conv_w, v_a_conv_b, v_a_ln_g, v_a_ln_b)
    sh_d, sh_nm, sh_nv = adamw_small(sh_w, g_sh, sh_m, sh_v, "adamw_sharded")
    rep_w = _pack_replicated(kv_norm, b_norm, final_norm, rel_bias, D)
    rep_m = _pack_replicated(m_kv_norm, m_b_norm, m_final_norm, m_rel_bias, D)
    rep_v = _pack_replicated(v_kv_norm, v_b_norm, v_final_norm, v_rel_bias, D)
    rep_d, rep_nm, rep_nv = adamw_small(rep_w, g_rep, rep_m, rep_v, "adamw_replicated")

    def unpack(kind):
        sh = (g_sh, sh_d, sh_nm, sh_nv)[kind]
        rep = (g_rep, rep_d, rep_nm, rep_nv)[kind]
        big = {nm: big_out[nm][kind] for nm in names}
        nb = rel_bias.size
        return [
            sh[0:1],
            big["a_w_in"][None],
            sh[16:16 + CONV_WIDTH][None],
            sh[10:11], sh[8:9], sh[9:10],
            big["a_w_out"][None],
            rep[0],
            big["w_kv"],
            rep[1:2],
            big["b_w_in"][None],
            big["b_w_out"][None],
            rep[16, :nb].reshape(rel_bias.shape),
            rep[8],
        ]

    return (loss, grad_x.reshape(1, S, D), *unpack(0), *unpack(1), *unpack(2), *unpack(3))
```

```python
import functools
import math

import numpy as np
import jax
import jax.numpy as jnp
from jax import lax
from jax.experimental import pallas as pl
from jax.experimental.pallas import tpu as pltpu

F32 = jnp.float32
BF16 = jnp.bfloat16

N_DEV = 8
N_CHIPS = 4
EPS = 1e-6
HEAD_DIM = 128
BLOCK = 128
GROUPS = ((128, 1), (512, 4), (2048, 16))
N_GROUPS = len(GROUPS)
CONV_WIDTH = 31
HALO = 32
N_BUCKETS = 32
MAX_EXACT = N_BUCKETS // 2
MAX_DISTANCE = 2048
V7X_VMEM_BYTES = 64 * 1024 * 1024
VMEM_LIMIT = (V7X_VMEM_BYTES * 7) // 8
MATMUL_VMEM_BUDGET = (V7X_VMEM_BYTES * 11) // 16
LANE = 128

ADAM_LR = 0.001
ADAM_B1 = 0.9
ADAM_B2 = 0.999
ADAM_EPS = 1e-08
ADAM_WD = 0.01
ADAM_STEP = 10

SMALL_SH_ROWS = 48
SMALL_REP_ROWS = 24
MESH = pl.DeviceIdType.MESH
ANY_SPEC = pl.BlockSpec(memory_space=pl.ANY)
VMEM_SPEC = pl.BlockSpec(memory_space=pltpu.VMEM)


def _tile(dim, pref, unit=LANE):
    if dim <= pref:
        return dim
    t = (pref // unit) * unit
    while dim % t:
        t -= unit
    assert t > 0
    return t


def _sigmoid(v):
    return jax.nn.sigmoid(v)


def _place():
    return lax.axis_index("x"), lax.axis_index("y"), lax.axis_index("c")


def _flip(v, bit):
    return 1 - v if bit else v


class Job:
    def __init__(self, srcs, dsts, n_sems, build):
        self.srcs, self.dsts, self.n_sems, self.build = list(srcs), list(dsts), n_sems, build
        self.results = None


def _remote(src, dst, send_sems, recv_sems, k, peer):
    return pltpu.make_async_remote_copy(src_ref=src, dst_ref=dst, send_sem=send_sems.at[k],
                                        recv_sem=recv_sems.at[k], device_id=peer, device_id_type=MESH)


def _call(body, *, name, in_specs, out_specs, out_shape, args, grid=(), scratch_shapes=(), sem=(),
          aliases=None, jobs=()):
    n_in, n_out, n_scr = len(in_specs), len(out_specs), len(scratch_shapes)
    aliases = dict(aliases or {})
    x_in, x_out, x_scr = [], [], []
    for job in jobs:
        job.in_at = n_in + len(x_in)
        x_in += job.srcs
        job.out_at = n_out + len(x_out)
        for d in job.dsts:
            if not isinstance(d, jax.ShapeDtypeStruct):
                aliases[n_in + len(x_in)] = n_out + len(x_out)
                x_in.append(d)
            x_out.append(jax.ShapeDtypeStruct(d.shape, d.dtype))
        job.scr_at = n_scr + len(x_scr)
        x_scr += [pltpu.SemaphoreType.DMA((job.n_sems,))] * 3

    def wrapped(*refs):
        ins = refs[:n_in + len(x_in)]
        outs = refs[len(ins):len(ins) + n_out + len(x_out)]
        scr = refs[len(ins) + len(outs):]
        core = ins[:n_in] + outs[:n_out] + scr[:n_scr]
        if not jobs:
            body(*core)
            return
        copies = []
        for job in jobs:
            copies += job.build(ins[job.in_at:job.in_at + len(job.srcs)],
                                outs[job.out_at:job.out_at + len(job.dsts)],
                                *scr[job.scr_at:job.scr_at + 3])
        if grid:
            pids = [pl.program_id(d) for d in range(len(grid))]
            first = functools.reduce(jnp.logical_and, [p == 0 for p in pids])
            last = functools.reduce(jnp.logical_and, [p == g - 1 for p, g in zip(pids, grid)])

            @pl.when(first)
            def _():
                for cp in copies:
                    cp.start()

            body(*core)

            @pl.when(last)
            def _():
                for cp in copies:
                    cp.wait()
        else:
            for cp in copies:
                cp.start()
            body(*core)
            for cp in copies:
                cp.wait()

    if jobs:
        sem = ("arbitrary",) * len(grid)
    kwargs = dict(grid=grid) if grid else {}
    if aliases:
        kwargs["input_output_aliases"] = aliases
    outs = pl.pallas_call(
        wrapped, name=name,
        in_specs=list(in_specs) + [ANY_SPEC] * len(x_in),
        out_specs=list(out_specs) + [ANY_SPEC] * len(x_out),
        out_shape=list(out_shape) + x_out,
        scratch_shapes=list(scratch_shapes) + x_scr,
        compiler_params=pltpu.CompilerParams(dimension_semantics=sem if sem else None,
                                             vmem_limit_bytes=VMEM_LIMIT),
        **kwargs,
    )(*args, *x_in)
    for job in jobs:
        job.results = list(outs[job.out_at:job.out_at + len(job.dsts)])
    return list(outs[:n_out])


def place_shard(w, name):
    K, n = w.shape
    tk = _tile(K, 512, 16)

    def body(p_ref, w_ref, o_ref):
        o_ref[...] = w_ref[...].astype(BF16)

    grid_spec = pltpu.PrefetchScalarGridSpec(
        num_scalar_prefetch=1, grid=(K // tk,),
        in_specs=[pl.BlockSpec((tk, n), lambda i, p: (i, 0))],
        out_specs=pl.BlockSpec((None, tk, n), lambda i, p: (4 * p[0] + 2 * p[1] + p[2], i, 0)))
    return pl.pallas_call(body, name=name, grid_spec=grid_spec,
                          out_shape=jax.ShapeDtypeStruct((N_DEV, K, n), BF16))(_place_vector(), w)


def _place_vector():
    return jnp.stack(_place()).astype(jnp.int32)


def gather_job(buf, chips=None, sibling=None):
    def build(srcs, dsts, send, recv, loc):
        (out,) = dsts
        x, y, c = _place()
        copies = []
        if chips is not None:
            mine = out.at[4 * x + 2 * y + c, pl.ds(chips[0], chips[1] - chips[0])]
            peers = [(x, y, 1 - c), (1 - x, y, c), (x, 1 - y, c), (1 - x, 1 - y, c)]
            copies += [_remote(mine, mine, send, recv, k, p) for k, p in enumerate(peers)]
        if sibling is not None:
            for k, (cx, cy) in enumerate([(1 - x, y), (x, 1 - y), (1 - x, 1 - y)]):
                blk = out.at[4 * cx + 2 * cy + c, pl.ds(sibling[0], sibling[1] - sibling[0])]
                copies.append(_remote(blk, blk, send, recv, 4 + k, (x, y, 1 - c)))
        return copies

    return Job([], [buf], 7, build)


def reduce_sibling_job(dw):
    def build(srcs, dsts, send, recv, loc):
        (src,), (got,) = srcs, dsts
        x, y, c = _place()
        return [_remote(src.at[4 * _flip(x, r & 2) + 2 * _flip(y, r & 1) + 1 - c], got.at[r], send, recv, r,
                        (x, y, 1 - c)) for r in range(N_CHIPS)]

    return Job([dw], [jax.ShapeDtypeStruct((N_CHIPS,) + dw.shape[1:], dw.dtype)], N_CHIPS, build)


def reduce_chips_job(part):
    def build(srcs, dsts, send, recv, loc):
        (src,), (got,) = srcs, dsts
        x, y, c = _place()
        return [_remote(src.at[r], got.at[r - 1], send, recv, r - 1, (_flip(x, r & 2), _flip(y, r & 1), c))
                for r in range(1, N_CHIPS)]

    return Job([part], [jax.ShapeDtypeStruct((N_CHIPS - 1,) + part.shape[1:], part.dtype)], N_CHIPS - 1, build)


def share_small_job(small):
    def build(srcs, dsts, send, recv, loc):
        (src,), (out,) = srcs, dsts
        x, y, c = _place()
        mine = out.at[4 * x + 2 * y + c]
        copies = [pltpu.make_async_copy(src, mine, loc.at[0])]
        for rel in range(1, N_DEV):
            peer = (_flip(x, rel & 4), _flip(y, rel & 2), _flip(c, rel & 1))
            copies.append(_remote(src, mine, send, recv, rel - 1, peer))
        return copies

    return Job([small], [jax.ShapeDtypeStruct((N_DEV,) + small.shape, small.dtype)], N_DEV - 1, build)


def all_gather(shards, name):
    n = len(shards)

    def body(*refs):
        ins, outs = refs[:n], refs[n:2 * n]
        send_sems, recv_sems, local_sems = refs[2 * n:]
        x, y, c = _place()
        me, sibling = (x, y, c), (x, y, 1 - c)
        chips = [(1 - x, y), (x, 1 - y), (1 - x, 1 - y)]

        def slot(a, dev):
            return outs[a].at[4 * dev[0] + 2 * dev[1] + dev[2]]

        def copy(a, k, block, to, src=None):
            return pltpu.make_async_remote_copy(
                src_ref=slot(a, block) if src is None else src, dst_ref=slot(a, block),
                send_sem=send_sems.at[a, k], recv_sem=recv_sems.at[a, k],
                device_id=to, device_id_type=MESH)

        mine, first, passed = [], [], []
        for a in range(n):
            mine.append(pltpu.make_async_copy(ins[a], slot(a, me), local_sems.at[a]))
            mine[a].start()
            first.append([copy(a, 0, me, sibling, src=ins[a])]
                         + [copy(a, 1 + j, me, (*chip, c), src=ins[a]) for j, chip in enumerate(chips)])
            for cp in first[a]:
                cp.start()
        for a in range(n):
            passed.append([copy(a, 4 + j, (*chip, c), sibling) for j, chip in enumerate(chips)])
            for j, chip in enumerate(chips):
                copy(a, 1 + j, (*chip, c), me).wait_recv()
                passed[a][j].start()
        for a in range(n):
            copy(a, 0, sibling, me).wait_recv()
            for j, chip in enumerate(chips):
                copy(a, 4 + j, (*chip, 1 - c), me).wait_recv()
            for cp in first[a] + passed[a]:
                cp.wait_send()
            mine[a].wait()

    return pl.pallas_call(
        body, name=name,
        in_specs=[ANY_SPEC] * n, out_specs=[ANY_SPEC] * n,
        out_shape=[jax.ShapeDtypeStruct((N_DEV,) + s.shape, s.dtype) for s in shards],
        scratch_shapes=[pltpu.SemaphoreType.DMA((n, 7)), pltpu.SemaphoreType.DMA((n, 7)),
                        pltpu.SemaphoreType.DMA((n,))],
    )(*shards)


def _kv_split_index(tw, D):
    pd = D // tw

    def index(j):
        return (j // pd) % 2, (j // (2 * pd)) * pd + j % pd

    return index


def _col_tile(n, also, pref):
    t = (min(pref, n) // LANE) * LANE
    while n % t or (also is not None and also % t):
        t -= LANE
    assert t > 0
    return t


def matmul_nn(a, w, name, res=None, out_dtype=F32, kv_split=False, jobs=()):
    M, K = a.shape
    nb, _, n = w.shape
    D = nb * n // (2 * N_GROUPS)
    tn = _col_tile(n, D if kv_split else None, 1024)
    out_bytes = jnp.dtype(out_dtype).itemsize + (4 if res is not None else 0)
    tm = _tile(M, 2048)
    if 2 * (tm * K * 2 + K * tn * 2 + tm * tn * out_bytes) > MATMUL_VMEM_BUDGET:
        tm = _tile(M, 1024)
    per = n // tn

    def body(*refs):
        if res is None:
            a_ref, w_ref, o_ref = refs
        else:
            a_ref, w_ref, r_ref, o_ref = refs
        acc = jnp.dot(a_ref[...], w_ref[...], preferred_element_type=F32)
        if res is not None:
            acc = r_ref[...] + acc
        o_ref[...] = acc.astype(o_ref.dtype)

    in_specs = [pl.BlockSpec((tm, K), lambda i, j: (i, 0)),
                pl.BlockSpec((None, K, tn), lambda i, j: (j // per, 0, j % per))]
    args = [a, w]
    if res is not None:
        in_specs.append(pl.BlockSpec((tm, tn), lambda i, j: (i, j)))
        args.append(res)
    if kv_split:
        split = _kv_split_index(tn, D)
        out_spec = pl.BlockSpec((None, tm, tn), lambda i, j: (split(j)[0], i, split(j)[1]))
        out_shape = jax.ShapeDtypeStruct((2, M, N_GROUPS * D), out_dtype)
    else:
        out_spec = pl.BlockSpec((tm, tn), lambda i, j: (i, j))
        out_shape = jax.ShapeDtypeStruct((M, nb * n), out_dtype)
    return _call(body, name=name, grid=(M // tm, nb * per), in_specs=in_specs, out_specs=[out_spec],
                 out_shape=[out_shape], args=args, sem=("parallel", "parallel"), jobs=jobs)[0]


def matmul_nt(dy, w, name, kv_split=False, jobs=()):
    M = dy.shape[-2]
    nb, K, n = w.shape
    D = nb * n // (2 * N_GROUPS)
    tm = _tile(M, 1024)
    tc = _col_tile(n, D if kv_split else None, 1024)
    per = n // tc
    pair = 2 if (nb * per) % 2 == 0 else 1

    def body(*refs):
        o_ref = refs[-1]
        j = pl.program_id(1)
        part = None
        for u in range(pair):
            d = lax.dot_general(refs[u][...], refs[pair + u][...], (((1,), (1,)), ((), ())),
                                preferred_element_type=F32)
            part = d if part is None else part + d

        @pl.when(j == 0)
        def _():
            o_ref[...] = part

        @pl.when(j > 0)
        def _():
            o_ref[...] += part

    def dy_spec(u):
        if kv_split:
            split = _kv_split_index(tc, D)
            return pl.BlockSpec((None, tm, tc),
                                lambda i, j: (split(pair * j + u)[0], i, split(pair * j + u)[1]))
        return pl.BlockSpec((tm, tc), lambda i, j: (i, pair * j + u))

    def w_spec(u):
        return pl.BlockSpec((None, K, tc), lambda i, j: ((pair * j + u) // per, 0, (pair * j + u) % per))

    return _call(body, name=name, grid=(M // tm, nb * per // pair),
                 in_specs=[dy_spec(u) for u in range(pair)] + [w_spec(u) for u in range(pair)],
                 out_specs=[pl.BlockSpec((tm, K), lambda i, j: (i, 0))],
                 out_shape=[jax.ShapeDtypeStruct((M, K), F32)], args=[dy] * pair + [w] * pair,
                 sem=("parallel", "arbitrary"), jobs=jobs)[0]


def matmul_tn(a, dy, nb, name, out_dtype=BF16, kv_split=False, k_tiles=None, jobs=()):
    M, K = a.shape
    N = 2 * dy.shape[-1] if kv_split else dy.shape[-1]
    n = N // nb
    D = N // (2 * N_GROUPS)
    tn = _col_tile(n, D if kv_split else None, 1024)
    per = n // tn
    tk = _tile(K, 1024)

    def body(a_ref, dy_ref, o_ref):
        o_ref[...] = lax.dot_general(a_ref[...], dy_ref[...], (((0,), (0,)), ((), ())),
                                     preferred_element_type=F32).astype(o_ref.dtype)

    if kv_split:
        split = _kv_split_index(tn, D)
        dy_spec = pl.BlockSpec((None, M, tn), lambda k, j: (split(j)[0], 0, split(j)[1]))
    else:
        dy_spec = pl.BlockSpec((M, tn), lambda k, j: (0, j))
    k0, k1 = k_tiles or (0, K // tk)
    return _call(body, name=name, grid=(k1 - k0, nb * per),
                 in_specs=[pl.BlockSpec((M, tk), lambda k, j: (0, k0 + k)), dy_spec],
                 out_specs=[pl.BlockSpec((None, tk, tn), lambda k, j: (j // per, k, j % per))],
                 out_shape=[jax.ShapeDtypeStruct((nb, (k1 - k0) * tk, n), out_dtype)], args=[a, dy],
                 sem=("parallel", "parallel"), jobs=jobs)[0]


def small_dot(a, b, contract, name, jobs=()):
    if contract == "nn":
        dims = (((1,), (0,)), ((), ()))
        out = (a.shape[0], b.shape[1])
    else:
        dims = (((1,), (1,)), ((), ()))
        out = (a.shape[0], b.shape[0])

    def body(a_ref, b_ref, o_ref):
        o_ref[...] = lax.dot_general(a_ref[...], b_ref[...], dims, precision=lax.Precision.HIGHEST,
                                     preferred_element_type=F32)

    return _call(body, name=name, in_specs=[VMEM_SPEC, VMEM_SPEC], out_specs=[VMEM_SPEC],
                 out_shape=[jax.ShapeDtypeStruct(out, F32)], args=[a, b], jobs=jobs)[0]


def rms_fwd(x, gains, name, jobs=()):
    S, D = x.shape
    T = _tile(S, 512, 8)
    n = len(gains)

    def body(x_ref, *refs):
        xv = x_ref[...]
        xn = xv * lax.rsqrt(jnp.mean(xv * xv, axis=-1, keepdims=True) + EPS)
        for g_ref, o_ref in zip(refs[:n], refs[n:]):
            o_ref[...] = (xn * g_ref[...]).astype(o_ref.dtype)

    row = pl.BlockSpec((T, D), lambda i: (i, 0))
    vec = pl.BlockSpec((1, D), lambda i: (0, 0))
    return _call(body, name=name, grid=(S // T,), in_specs=[row] + [vec] * n, out_specs=[row] * n,
                 out_shape=[jax.ShapeDtypeStruct((S, D), BF16)] * n, args=[x, *gains],
                 sem=("parallel",), jobs=jobs)


def rms_bwd(x, dhs, gains, dres, name, want_bf16, jobs=()):
    S, D = x.shape
    T = _tile(S, 256, 8)
    n = len(gains)

    def body(x_ref, *refs):
        dh_refs = refs[:n]
        g_refs = refs[n:2 * n]
        dres_ref = refs[2 * n]
        outs = refs[2 * n + 1:]
        dx_ref, dg_ref = outs[0], outs[-1]
        i = pl.program_id(0)

        @pl.when(i == 0)
        def _():
            dg_ref[...] = jnp.zeros_like(dg_ref)

        xv = x_ref[...]
        r = lax.rsqrt(jnp.mean(xv * xv, axis=-1, keepdims=True) + EPS)
        xn = xv * r
        dxn = jnp.zeros_like(xv)
        for k in range(n):
            dh = dh_refs[k][...]
            dg_ref[k:k + 1, :] += jnp.sum(dh * xn, axis=0, keepdims=True)
            dxn = dxn + dh * g_refs[k][...]
        dx = dres_ref[...] + r * (dxn - xn * jnp.mean(dxn * xn, axis=-1, keepdims=True))
        dx_ref[...] = dx
        if want_bf16:
            outs[1][...] = dx.astype(BF16)

    row = pl.BlockSpec((T, D), lambda i: (i, 0))
    vec = pl.BlockSpec((1, D), lambda i: (0, 0))
    acc = pl.BlockSpec((8, D), lambda i: (0, 0))
    out_specs = [row] + ([row] if want_bf16 else []) + [acc]
    out_shape = ([jax.ShapeDtypeStruct((S, D), F32)]
                 + ([jax.ShapeDtypeStruct((S, D), BF16)] if want_bf16 else [])
                 + [jax.ShapeDtypeStruct((8, D), F32)])
    return _call(body, name=name, grid=(S // T,), in_specs=[row] + [row] * n + [vec] * n + [row],
                 out_specs=out_specs, out_shape=out_shape, args=[x, *dhs, *gains, dres],
                 sem=("arbitrary",), jobs=jobs)


def final_loss(x2, gain, target, name, jobs=()):
    S, D = x2.shape
    T = _tile(S, 256, 8)

    def body(x_ref, g_ref, t_ref, dx_ref, dxb_ref, acc_ref):
        i = pl.program_id(0)

        @pl.when(i == 0)
        def _():
            acc_ref[...] = jnp.zeros_like(acc_ref)

        xv = x_ref[...]
        g = g_ref[...]
        r = lax.rsqrt(jnp.mean(xv * xv, axis=-1, keepdims=True) + EPS)
        xn = xv * r
        err = xn * g - t_ref[...]
        dy = err * (1.0 / D)
        acc_ref[0:1, :] += jnp.sum(dy * xn, axis=0, keepdims=True)
        acc_ref[1:2, :] += jnp.full((1, D), 0.5 / D, F32) * jnp.sum(err * err)
        dxn = dy * g
        dx = r * (dxn - xn * jnp.mean(dxn * xn, axis=-1, keepdims=True))
        dx_ref[...] = dx
        dxb_ref[...] = dx.astype(BF16)

    row = pl.BlockSpec((T, D), lambda i: (i, 0))
    return _call(body, name=name, grid=(S // T,),
                 in_specs=[row, pl.BlockSpec((1, D), lambda i: (0, 0)), row],
                 out_specs=[row, row, pl.BlockSpec((8, D), lambda i: (0, 0))],
                 out_shape=[jax.ShapeDtypeStruct((S, D), F32), jax.ShapeDtypeStruct((S, D), BF16),
                            jax.ShapeDtypeStruct((8, D), F32)],
                 args=[x2, gain, target], sem=("arbitrary",), jobs=jobs)


ROW_CHUNK = 64
LANE_CHUNK = 512
SUBLANES = 8


def _shifted_copies(buf, sh_scr, l0, lc):
    n = buf.shape[0] - SUBLANES
    for b in range(1, SUBLANES):
        sh_scr[b - 1, 0:n, :] = buf[b:b + n, l0:l0 + lc]


def _shifted(buf, sh_scr, start, rows, l0, lc):
    a8, b = (start // SUBLANES) * SUBLANES, start % SUBLANES
    if b == 0:
        return buf[a8:a8 + rows, l0:l0 + lc]
    return sh_scr[b - 1, a8:a8 + rows, :]


def conf_fwd(proj, cw, cb, lg, lb, name, jobs=()):
    S, E3 = proj.shape
    E = E3 // 3
    T = _tile(S, 256, HALO)
    R = T // HALO
    lc = _tile(E, LANE_CHUNK)
    rc = min(ROW_CHUNK, T)

    def body(a_ref, b_ref, z_ref, ap_ref, bp_ref, cw_ref, cb_ref, lg_ref, lb_ref, c_ref, y_ref, u_scr, sh_scr):
        i = pl.program_id(0)
        up = ap_ref[...] * _sigmoid(bp_ref[...])
        u_scr[0:HALO, :] = jnp.where(i > 0, up, 0.0)
        u_scr[HALO:HALO + T, :] = a_ref[...] * _sigmoid(b_ref[...])
        off = HALO - (CONV_WIDTH - 1)
        for l0 in range(0, E, lc):
            _shifted_copies(u_scr, sh_scr, l0, lc)
            for r0 in range(0, T, rc):
                acc = jnp.broadcast_to(cb_ref[:, l0:l0 + lc], (rc, lc))
                for k in range(CONV_WIDTH):
                    acc = acc + _shifted(u_scr, sh_scr, r0 + off + k, rc, l0, lc) * cw_ref[k:k + 1, l0:l0 + lc]
                c_ref[r0:r0 + rc, l0:l0 + lc] = acc
        c = c_ref[...]
        mu = jnp.mean(c, axis=-1, keepdims=True)
        d = c - mu
        var = jnp.mean(d * d, axis=-1, keepdims=True)
        cn = d * lax.rsqrt(var + EPS) * lg_ref[...] + lb_ref[...]
        z = z_ref[...]
        y_ref[...] = ((cn * _sigmoid(cn)) * (z * _sigmoid(z))).astype(BF16)

    def col(j):
        return pl.BlockSpec((T, E), lambda i: (i, j))

    def prev(j):
        return pl.BlockSpec((HALO, E), lambda i: (jnp.maximum(i * R - 1, 0), j))

    vec = pl.BlockSpec((1, E), lambda i: (0, 0))
    return _call(body, name=name, grid=(S // T,),
                 in_specs=[col(0), col(1), col(2), prev(0), prev(1),
                           pl.BlockSpec((HALO, E), lambda i: (0, 0)), vec, vec, vec],
                 out_specs=[pl.BlockSpec((T, E), lambda i: (i, 0))] * 2,
                 out_shape=[jax.ShapeDtypeStruct((S, E), F32), jax.ShapeDtypeStruct((S, E), BF16)],
                 scratch_shapes=[pltpu.VMEM((HALO + T, E), F32), pltpu.VMEM((SUBLANES - 1, HALO + T, lc), F32)],
                 args=[proj, proj, proj, proj, proj, cw, cb, lg, lb], sem=("parallel",), jobs=jobs)


def conf_bwd_ln(c, dy, proj, lg, lb, name, jobs=()):
    S, E = c.shape
    T = _tile(S, 256, 8)

    def body(c_ref, dy_ref, z_ref, lg_ref, lb_ref, dc_ref, dz_ref, acc_ref):
        i = pl.program_id(0)

        @pl.when(i == 0)
        def _():
            acc_ref[...] = jnp.zeros_like(acc_ref)

        cv = c_ref[...]
        mu = jnp.mean(cv, axis=-1, keepdims=True)
        d = cv - mu
        var = jnp.mean(d * d, axis=-1, keepdims=True)
        rstd = lax.rsqrt(var + EPS)
        xh = d * rstd
        lgv = lg_ref[...]
        cn = xh * lgv + lb_ref[...]
        z = z_ref[...]
        dy = dy_ref[...]
        sc = _sigmoid(cn)
        sz = _sigmoid(z)
        dcn = dy * (z * sz) * (sc * (1.0 + cn * (1.0 - sc)))
        dz_ref[...] = (dy * (cn * sc) * (sz * (1.0 + z * (1.0 - sz)))).astype(BF16)
        acc_ref[0:1, :] += jnp.sum(dcn * xh, axis=0, keepdims=True)
        acc_ref[1:2, :] += jnp.sum(dcn, axis=0, keepdims=True)
        dxh = dcn * lgv
        dc = rstd * (dxh - jnp.mean(dxh, axis=-1, keepdims=True)
                     - xh * jnp.mean(dxh * xh, axis=-1, keepdims=True))
        acc_ref[2:3, :] += jnp.sum(dc, axis=0, keepdims=True)
        dc_ref[...] = dc

    row = pl.BlockSpec((T, E), lambda i: (i, 0))
    vec = pl.BlockSpec((1, E), lambda i: (0, 0))
    return _call(body, name=name, grid=(S // T,),
                 in_specs=[row, row, pl.BlockSpec((T, E), lambda i: (i, 2)), vec, vec],
                 out_specs=[row, row, pl.BlockSpec((8, E), lambda i: (0, 0))],
                 out_shape=[jax.ShapeDtypeStruct((S, E), F32), jax.ShapeDtypeStruct((S, E), BF16),
                            jax.ShapeDtypeStruct((8, E), F32)],
                 args=[c, dy, proj, lg, lb], sem=("arbitrary",), jobs=jobs)


def conf_bwd_conv(proj, dc, dz, cw, name, jobs=()):
    S, E3 = proj.shape
    E = E3 // 3
    T = _tile(S, 256, HALO)
    R = T // HALO
    nt = S // T
    lc = _tile(E, LANE_CHUNK)
    rc = min(ROW_CHUNK, T)

    def body(a_ref, b_ref, ap_ref, bp_ref, dc_ref, dcn_ref, dz_ref, cw_ref, o_ref, dw_ref, u_scr, dc_scr, sh_scr):
        i = pl.program_id(0)

        @pl.when(i == 0)
        def _():
            dw_ref[...] = jnp.zeros_like(dw_ref)

        a = a_ref[...]
        sb = _sigmoid(b_ref[...])
        up = ap_ref[...] * _sigmoid(bp_ref[...])
        u_scr[0:HALO, :] = jnp.where(i > 0, up, 0.0)
        u_scr[HALO:HALO + T, :] = a * sb
        dc_scr[0:T, :] = dc_ref[...]
        dc_scr[T:T + HALO, :] = jnp.where(i < nt - 1, dcn_ref[...], 0.0)
        off = HALO - (CONV_WIDTH - 1)
        for l0 in range(0, E, lc):
            _shifted_copies(u_scr, sh_scr, l0, lc)
            for k in range(CONV_WIDTH):
                prod = _shifted(u_scr, sh_scr, off + k, T, l0, lc) * dc_scr[0:T, l0:l0 + lc]
                dw_ref[k:k + 1, l0:l0 + lc] += jnp.sum(prod, axis=0, keepdims=True)
            _shifted_copies(dc_scr, sh_scr, l0, lc)
            for r0 in range(0, T, rc):
                acc = jnp.zeros((rc, lc), F32)
                for k in range(CONV_WIDTH):
                    s0 = r0 + (CONV_WIDTH - 1) - k
                    acc = acc + _shifted(dc_scr, sh_scr, s0, rc, l0, lc) * cw_ref[k:k + 1, l0:l0 + lc]
                av = a[r0:r0 + rc, l0:l0 + lc]
                sv = sb[r0:r0 + rc, l0:l0 + lc]
                o_ref[r0:r0 + rc, l0:l0 + lc] = (acc * sv).astype(BF16)
                o_ref[r0:r0 + rc, E + l0:E + l0 + lc] = (acc * av * sv * (1.0 - sv)).astype(BF16)
        o_ref[:, 2 * E:3 * E] = dz_ref[...]

    def col(j):
        return pl.BlockSpec((T, E), lambda i: (i, j))

    def prev(j):
        return pl.BlockSpec((HALO, E), lambda i: (jnp.maximum(i * R - 1, 0), j))

    row = pl.BlockSpec((T, E), lambda i: (i, 0))
    nxt = pl.BlockSpec((HALO, E), lambda i: (jnp.minimum((i + 1) * R, S // HALO - 1), 0))
    return _call(body, name=name, grid=(nt,),
                 in_specs=[col(0), col(1), prev(0), prev(1), row, nxt, row,
                           pl.BlockSpec((HALO, E), lambda i: (0, 0))],
                 out_specs=[pl.BlockSpec((T, E3), lambda i: (i, 0)), pl.BlockSpec((HALO, E), lambda i: (0, 0))],
                 out_shape=[jax.ShapeDtypeStruct((S, E3), BF16), jax.ShapeDtypeStruct((HALO, E), F32)],
                 scratch_shapes=[pltpu.VMEM((HALO + T, E), F32), pltpu.VMEM((T + HALO, E), F32),
                                 pltpu.VMEM((SUBLANES - 1, HALO + T, lc), F32)],
                 args=[proj, proj, proj, proj, dc, dc, dz, cw], sem=("arbitrary",), jobs=jobs)


def bucket_tables():
    q = np.arange(BLOCK)[:, None]
    k = np.arange(2 * BLOCK)[None, :]
    out = []
    for window, dil in GROUPS:
        delta = q + BLOCK - k
        valid = (delta >= 0) & (delta <= window // dil)
        dist = np.clip(delta, 0, None) * dil
        large = MAX_EXACT + (np.log(np.maximum(dist, 1).astype(np.float32) / MAX_EXACT)
                             / math.log(MAX_DISTANCE / MAX_EXACT)
                             * (N_BUCKETS - MAX_EXACT)).astype(np.int32)
        large = np.minimum(large, N_BUCKETS - 1)
        bucket = np.where(dist < MAX_EXACT, dist, large)
        out.append(np.where(valid, bucket, -1).reshape(-1))
    return np.concatenate(out).astype(np.int32)


def _band_masks(has_previous):
    ql = lax.broadcasted_iota(jnp.int32, (BLOCK, 2 * BLOCK), 0)
    kk = lax.broadcasted_iota(jnp.int32, (BLOCK, 2 * BLOCK), 1)
    band = (kk >= ql) & (kk <= ql + BLOCK)
    return band, band & ((kk >= BLOCK) | has_previous)


def _dot_nt(a, b):
    return lax.dot_general(a, b, (((1,), (1,)), ((), ())), preferred_element_type=F32)


def _dot_tn(a, b):
    return lax.dot_general(a, b, (((0,), (0,)), ((), ())), preferred_element_type=F32)


ATTN_ROWS = 2048


def _sub(start, size, dil):
    return pl.ds(start, size) if dil == 1 else pl.ds(start, size, stride=dil)


def _attn_geometry(S, dil):
    halo = BLOCK * dil
    rows = max(min(S, ATTN_ROWS), halo)
    return halo, rows, S // rows, rows // halo


MAX_ROW_STRIDE = 8


def _split(dst, src, n, dil, tmp):
    if dil <= MAX_ROW_STRIDE:
        for r in range(dil):
            dst[r] = src[_sub(r, n, dil), :].astype(dst.dtype)
        return
    f, g = 4, dil // 4
    for r1 in range(f):
        tmp[0:n * g, :] = src[_sub(r1, n * g, f), :]
        for r2 in range(g):
            dst[r2 * f + r1] = tmp[_sub(r2, n, g), :].astype(dst.dtype)


def _merge(dst, src, n, dil, tmp):
    if dil <= MAX_ROW_STRIDE:
        for r in range(dil):
            dst[_sub(r, n, dil), :] = src[r]
        return
    f, g = 4, dil // 4
    for r1 in range(f):
        for r2 in range(g):
            tmp[_sub(r2, n, g), :] = src[r2 * f + r1]
        dst[_sub(r1, n * g, f), :] = tmp[0:n * g, :]


def _split_tmp(sub, dil):
    rows = (sub + BLOCK) * (dil // 4) if dil > MAX_ROW_STRIDE else SUBLANES
    return pltpu.VMEM((rows, HEAD_DIM), F32)


def attn_fwd(qz, kv, bias, g, dil, name, jobs=()):
    S = qz.shape[0]
    D = qz.shape[1] // (N_GROUPS + 1)
    H = D // HEAD_DIM
    halo, rows, nsb, nblk = _attn_geometry(S, dil)
    sub = nblk * BLOCK
    scale = HEAD_DIM ** -0.5

    def body(q_ref, kvc_ref, kvp_ref, b_ref, o_ref, l_ref, ks, vs, qd, kd, vd, od, ld, tmp):
        sb = pl.program_id(1)
        ks[0:halo, :] = kvp_ref[0].astype(F32)
        ks[halo:, :] = kvc_ref[0].astype(F32)
        vs[0:halo, :] = kvp_ref[1].astype(F32)
        vs[halo:, :] = kvc_ref[1].astype(F32)
        _split(kd, ks, sub + BLOCK, dil, tmp)
        _split(vd, vs, sub + BLOCK, dil, tmp)
        _split(qd, q_ref, sub, dil, tmp)
        band, first = _band_masks(sb > 0)
        bias_t = b_ref[0]
        for r in range(dil):
            for jj in range(nblk):
                q = qd[r, jj * BLOCK:(jj + 1) * BLOCK, :]
                keys = slice(jj * BLOCK, (jj + 2) * BLOCK)
                s = _dot_nt(q, kd[r, keys, :]) * scale + bias_t
                s = jnp.where(band if jj > 0 else first, s, -jnp.inf)
                m = jnp.max(s, axis=-1, keepdims=True)
                p = jnp.exp(s - m)
                den = jnp.sum(p, axis=-1, keepdims=True)
                pv = jnp.dot(p.astype(BF16), vd[r, keys, :], preferred_element_type=F32)
                own = slice(jj * BLOCK, (jj + 1) * BLOCK)
                od[r, own, :] = pv / den
                ld[r, own, :] = jnp.broadcast_to(m + jnp.log(den), (BLOCK, HEAD_DIM))
        _merge(o_ref, od, sub, dil, tmp)
        _merge(l_ref, ld, sub, dil, tmp)

    per = rows // halo
    out_spec = pl.BlockSpec((rows, HEAD_DIM), lambda h, sb: (sb, h))
    return _call(body, name=name, grid=(H, nsb),
                 in_specs=[pl.BlockSpec((rows, HEAD_DIM), lambda h, sb: (sb, g * H + h)),
                           pl.BlockSpec((2, rows, HEAD_DIM), lambda h, sb: (0, sb, g * H + h)),
                           pl.BlockSpec((2, halo, HEAD_DIM),
                                        lambda h, sb: (0, jnp.maximum(sb * per - 1, 0), g * H + h)),
                           pl.BlockSpec((1, BLOCK, 2 * BLOCK), lambda h, sb: (h, 0, 0))],
                 out_specs=[out_spec, out_spec],
                 out_shape=[jax.ShapeDtypeStruct((S, D), F32)] * 2,
                 scratch_shapes=[pltpu.VMEM((halo + rows, HEAD_DIM), F32)] * 2
                 + [pltpu.VMEM((dil, sub, HEAD_DIM), BF16)]
                 + [pltpu.VMEM((dil, sub + BLOCK, HEAD_DIM), BF16)] * 2
                 + [pltpu.VMEM((dil, sub, HEAD_DIM), F32)] * 2 + [_split_tmp(sub, dil)],
                 args=[qz, kv, kv, bias], sem=("parallel", "arbitrary"), jobs=jobs)


def merge_fwd(os_, lses, qz, name, jobs=()):
    S, D = os_[0].shape
    T = _tile(S, 256, 8)

    def body(o1, o2, o3, l1, l2, l3, z_ref, o_ref, lse_ref, y_ref):
        la, lb_, lc_ = l1[...], l2[...], l3[...]
        m = jnp.maximum(jnp.maximum(la, lb_), lc_)
        ea, eb, ec = jnp.exp(la - m), jnp.exp(lb_ - m), jnp.exp(lc_ - m)
        den = ea + eb + ec
        o = (ea * o1[...] + eb * o2[...] + ec * o3[...]) / den
        z = z_ref[...]
        o_ref[...] = o
        lse_ref[...] = m + jnp.log(den)
        y_ref[...] = (o * (z * _sigmoid(z))).astype(BF16)

    row = pl.BlockSpec((T, D), lambda i: (i, 0))
    return _call(body, name=name, grid=(S // T,),
                 in_specs=[row] * 6 + [pl.BlockSpec((T, D), lambda i: (i, N_GROUPS))],
                 out_specs=[row] * 3,
                 out_shape=[jax.ShapeDtypeStruct((S, D), F32), jax.ShapeDtypeStruct((S, D), F32),
                            jax.ShapeDtypeStruct((S, D), BF16)],
                 args=[*os_, *lses, qz], sem=("parallel",), jobs=jobs)


def merge_bwd(dy2, o, qz, name, jobs=()):
    S, D = o.shape
    H = D // HEAD_DIM
    T = _tile(S, 256, 8)
    nq = N_GROUPS + 1

    def body(dy_ref, o_ref, z_ref, do_ref, dl_ref, dqz_ref):
        dy = dy_ref[...]
        ov = o_ref[...]
        z = z_ref[...]
        sz = _sigmoid(z)
        do = dy * (z * sz)
        do_ref[...] = do.astype(BF16)
        dqz_ref[...] = (dy * ov * (sz * (1.0 + z * (1.0 - sz)))).astype(BF16)
        prod = do * ov
        for h in range(H):
            hs = slice(h * HEAD_DIM, (h + 1) * HEAD_DIM)
            dl_ref[:, hs] = jnp.broadcast_to(jnp.sum(prod[:, hs], axis=-1, keepdims=True), (T, HEAD_DIM))

    row = pl.BlockSpec((T, D), lambda i: (i, 0))
    last = pl.BlockSpec((T, D), lambda i: (i, N_GROUPS))
    return _call(body, name=name, grid=(S // T,), in_specs=[row, row, last], out_specs=[row, row, last],
                 out_shape=[jax.ShapeDtypeStruct((S, D), BF16), jax.ShapeDtypeStruct((S, D), F32),
                            jax.ShapeDtypeStruct((S, nq * D), BF16)],
                 args=[dy2, o, qz], sem=("parallel",), jobs=jobs)


def attn_bwd(qz, kv, do, lse, delta, bias, dqz, dkv, g, dil, name, jobs=()):
    S = qz.shape[0]
    D = qz.shape[1] // (N_GROUPS + 1)
    H = D // HEAD_DIM
    halo, rows, nsb, nblk = _attn_geometry(S, dil)
    sub = nblk * BLOCK
    scale = HEAD_DIM ** -0.5
    have_dkv = dkv is not None

    def body(*refs):
        q_ref, do_ref, l_ref, d_ref, kvc_ref, kvp_ref, b_ref = refs[:7]
        n_in = 7 + 1 + (1 if have_dkv else 0)
        (dq_ref, dkv_ref, ds_ref, ks, vs, dks, dvs, dos, dqs, carry_k, carry_v,
         qd, dod, ld, dd, dqd, kd, vd, dkd, dvd, tmp) = refs[n_in:]
        i = pl.program_id(1)
        sb = nsb - 1 - i

        @pl.when(i == 0)
        def _():
            ds_ref[...] = jnp.zeros_like(ds_ref)

        ks[0:halo, :] = kvp_ref[0].astype(F32)
        ks[halo:, :] = kvc_ref[0].astype(F32)
        vs[0:halo, :] = kvp_ref[1].astype(F32)
        vs[halo:, :] = kvc_ref[1].astype(F32)
        dos[...] = do_ref[...].astype(F32)
        _split(kd, ks, sub + BLOCK, dil, tmp)
        _split(vd, vs, sub + BLOCK, dil, tmp)
        for dst, src in ((qd, q_ref), (dod, dos), (ld, l_ref), (dd, d_ref)):
            _split(dst, src, sub, dil, tmp)
        dkd[...] = jnp.zeros_like(dkd)
        dvd[...] = jnp.zeros_like(dvd)
        band, first = _band_masks(sb > 0)
        bias_t = b_ref[0]
        for r in range(dil):
            for jj in range(nblk):
                own = slice(jj * BLOCK, (jj + 1) * BLOCK)
                keys = slice(jj * BLOCK, (jj + 2) * BLOCK)
                q = qd[r, own, :]
                do = dod[r, own, :]
                k = kd[r, keys, :]
                v = vd[r, keys, :]
                lse = ld[r, own, :]
                dlt = dd[r, own, :]
                s = _dot_nt(q, k) * scale + bias_t - jnp.concatenate([lse, lse], axis=-1)
                p = jnp.where(band if jj > 0 else first, jnp.exp(s), 0.0)
                ds = p * (_dot_nt(do, v) - jnp.concatenate([dlt, dlt], axis=-1))
                ds_ref[0] += ds
                dsb = ds.astype(BF16)
                dqd[r, own, :] = jnp.dot(dsb, k, preferred_element_type=F32) * scale
                dkd[r, keys, :] += _dot_tn(dsb, q) * scale
                dvd[r, keys, :] += _dot_tn(p.astype(BF16), do)
        _merge(dqs, dqd, sub, dil, tmp)
        _merge(dks, dkd, sub + BLOCK, dil, tmp)
        _merge(dvs, dvd, sub + BLOCK, dil, tmp)

        @pl.when(i > 0)
        def _():
            dks[rows:rows + halo, :] += carry_k[...]
            dvs[rows:rows + halo, :] += carry_v[...]

        dkv_ref[0] = dks[halo:, :].astype(BF16)
        dkv_ref[1] = dvs[halo:, :].astype(BF16)
        carry_k[...] = dks[0:halo, :]
        carry_v[...] = dvs[0:halo, :]
        dq_ref[...] = dqs[...].astype(BF16)

    per = rows // halo

    def rev(i):
        return nsb - 1 - i

    bias_spec = pl.BlockSpec((1, BLOCK, 2 * BLOCK), lambda h, i: (h, 0, 0))
    row_spec = pl.BlockSpec((rows, HEAD_DIM), lambda h, i: (rev(i), h))
    in_specs = [pl.BlockSpec((rows, HEAD_DIM), lambda h, i: (rev(i), g * H + h)),
                row_spec, row_spec, row_spec,
                pl.BlockSpec((2, rows, HEAD_DIM), lambda h, i: (0, rev(i), g * H + h)),
                pl.BlockSpec((2, halo, HEAD_DIM),
                             lambda h, i: (0, jnp.maximum(rev(i) * per - 1, 0), g * H + h)),
                bias_spec, ANY_SPEC]
    args = [qz, do, lse, delta, kv, kv, bias, dqz]
    aliases = {7: 0}
    if have_dkv:
        in_specs.append(ANY_SPEC)
        args.append(dkv)
        aliases[8] = 1
    blk = (halo + rows, HEAD_DIM)
    own, keys = (dil, sub, HEAD_DIM), (dil, sub + BLOCK, HEAD_DIM)
    return _call(body, name=name, grid=(H, nsb), in_specs=in_specs,
                 out_specs=[pl.BlockSpec((rows, HEAD_DIM), lambda h, i: (rev(i), g * H + h)),
                            pl.BlockSpec((2, rows, HEAD_DIM), lambda h, i: (0, rev(i), g * H + h)),
                            bias_spec],
                 out_shape=[jax.ShapeDtypeStruct(qz.shape, BF16),
                            jax.ShapeDtypeStruct((2, S, N_GROUPS * D), BF16),
                            jax.ShapeDtypeStruct((H, BLOCK, 2 * BLOCK), F32)],
                 scratch_shapes=[pltpu.VMEM(blk, F32), pltpu.VMEM(blk, F32), pltpu.VMEM(blk, F32),
                                 pltpu.VMEM(blk, F32),
                                 pltpu.VMEM((rows, HEAD_DIM), F32), pltpu.VMEM((rows, HEAD_DIM), F32),
                                 pltpu.VMEM((halo, HEAD_DIM), F32), pltpu.VMEM((halo, HEAD_DIM), F32),
                                 pltpu.VMEM(own, BF16), pltpu.VMEM(own, BF16), pltpu.VMEM(own, F32),
                                 pltpu.VMEM(own, F32), pltpu.VMEM(own, F32),
                                 pltpu.VMEM(keys, BF16), pltpu.VMEM(keys, BF16),
                                 pltpu.VMEM(keys, F32), pltpu.VMEM(keys, F32), _split_tmp(sub, dil)],
                 aliases=aliases, args=args, sem=("parallel", "arbitrary"), jobs=jobs)


def _adamw(w, g, m, v):
    m = ADAM_B1 * m + (1.0 - ADAM_B1) * g
    v = ADAM_B2 * v + (1.0 - ADAM_B2) * (g * g)
    m_hat = m / (1.0 - ADAM_B1 ** ADAM_STEP)
    v_hat = v / (1.0 - ADAM_B2 ** ADAM_STEP)
    delta = -ADAM_LR * (m_hat / (jnp.sqrt(v_hat) + ADAM_EPS) + ADAM_WD * w)
    return delta, m, v


def add_pairs(dw, got, name, jobs=()):
    _, K, n = got.shape
    tk = _tile(K, 512, 16)

    def own_block(r, i, p):
        return 4 * ((p[0] + r // 2) % 2) + 2 * ((p[1] + r % 2) % 2) + p[2], i, 0

    def body(p_ref, a_ref, b_ref, o_ref):
        o_ref[...] = (a_ref[...].astype(F32) + b_ref[...].astype(F32)).astype(o_ref.dtype)

    blk = pl.BlockSpec((None, tk, n), lambda r, i, p: (r, i, 0))
    grid_spec = pltpu.PrefetchScalarGridSpec(
        num_scalar_prefetch=1, grid=(N_CHIPS, K // tk),
        in_specs=[pl.BlockSpec((None, tk, n), own_block), blk], out_specs=blk)
    return pl.pallas_call(body, name=name, grid_spec=grid_spec,
                          out_shape=jax.ShapeDtypeStruct(got.shape, got.dtype))(_place_vector(), dw, got)


def adamw_reduce(pieces, w, m, v, name, jobs=()):
    K, n = w.shape
    kp = K // len(pieces)
    tk = _tile(kp, 256, 8)
    sp = kp // tk

    def body(*refs):
        w_ref, m_ref, v_ref, g_ref, d_ref, nm_ref, nv_ref = refs[2 * len(pieces):]
        i = pl.program_id(0)
        g = None
        for q in range(len(pieces)):
            p_ref, r_ref = refs[2 * q], refs[2 * q + 1]
            gq = p_ref[...].astype(F32)
            for r in range(N_CHIPS - 1):
                gq = gq + r_ref[r].astype(F32)
            g = gq if g is None else jnp.where(i >= q * sp, gq, g)
        d, nm, nv = _adamw(w_ref[...], g, m_ref[...], v_ref[...])
        g_ref[...] = g
        d_ref[...] = d
        nm_ref[...] = nm
        nv_ref[...] = nv

    def piece_specs(q):
        def at(i):
            return jnp.clip(i - q * sp, 0, sp - 1)
        return [pl.BlockSpec((None, tk, n), lambda i: (0, at(i), 0)),
                pl.BlockSpec((N_CHIPS - 1, tk, n), lambda i: (0, at(i), 0))]

    blk = pl.BlockSpec((tk, n), lambda i: (i, 0))
    in_specs, args = [], []
    for q, (part, got) in enumerate(pieces):
        in_specs += piece_specs(q)
        args += [part, got]
    return _call(body, name=name, grid=(K // tk,), in_specs=in_specs + [blk, blk, blk],
                 out_specs=[blk] * 4, out_shape=[jax.ShapeDtypeStruct((K, n), F32)] * 4,
                 args=args + [w, m, v], sem=("parallel",), jobs=jobs)


def sum_parts(parts, name):
    _, R, D = parts.shape

    def body(p_ref, o_ref):
        g = p_ref[0]
        for r in range(1, N_DEV):
            g = g + p_ref[r]
        o_ref[...] = g

    return _call(body, name=name, in_specs=[VMEM_SPEC], out_specs=[VMEM_SPEC],
                 out_shape=[jax.ShapeDtypeStruct((R, D), F32)], args=[parts])[0]


def adamw_small(w, g, m, v, name):
    def body(w_ref, g_ref, m_ref, v_ref, d_ref, nm_ref, nv_ref):
        d, nm, nv = _adamw(w_ref[...], g_ref[...], m_ref[...], v_ref[...])
        d_ref[...] = d
        nm_ref[...] = nm
        nv_ref[...] = nv

    return _call(body, name=name, in_specs=[VMEM_SPEC] * 4, out_specs=[VMEM_SPEC] * 3,
                 out_shape=[jax.ShapeDtypeStruct(w.shape, F32)] * 3, args=[w, g, m, v])


def _kv_row_pieces(K):
    a, b = (K * 5 // 16) // 16 * 16, (K * 3 // 4) // 16 * 16
    return (0, a), (a, b), (b, K)


def _row(v, at):
    return jnp.pad(v.reshape(1, -1), ((at, 7 - at), (0, 0)))


def _pack_sharded(norm, conv_w, conv_b, ln_g, ln_b):
    n = norm.shape[-1]
    taps = jnp.pad(conv_w.reshape(CONV_WIDTH, n), ((0, HALO - CONV_WIDTH), (0, 0)))
    return jnp.concatenate([_row(norm, 0), _row(ln_g, 0) + _row(ln_b, 1) + _row(conv_b, 2), taps], axis=0)


def _pack_rel(rel_bias, D):
    return jnp.pad(rel_bias.reshape(1, -1), ((0, 7), (0, D - rel_bias.size)))


def _pack_replicated(kv_norm, b_norm, final_norm, rel_bias, D):
    return jnp.concatenate([_row(kv_norm, 0) + _row(b_norm, 1), _row(final_norm, 0), _pack_rel(rel_bias, D)], axis=0)


def kernel(x, a_norm, a_w_in, a_conv_w, a_conv_b, a_ln_g, a_ln_b, a_w_out, kv_norm, w_kv, b_norm, b_w_in, b_w_out, rel_bias, final_norm, loss_target, m_a_norm, m_a_w_in, m_a_conv_w, m_a_conv_b, m_a_ln_g, m_a_ln_b, m_a_w_out, m_kv_norm, m_w_kv, m_b_norm, m_b_w_in, m_b_w_out, m_rel_bias, m_final_norm, v_a_norm, v_a_w_in, v_a_conv_w, v_a_conv_b, v_a_ln_g, v_a_ln_b, v_a_w_out, v_kv_norm, v_w_kv, v_b_norm, v_b_w_in, v_b_w_out, v_rel_bias, v_final_norm):
    _, S, D = x.shape
    E = D
    H = D // HEAD_DIM
    nsh = D // N_DEV
    x0 = x.reshape(S, D)
    target = loss_target.reshape(S, D)
    kvn, bn, fn = kv_norm.reshape(1, D), b_norm.reshape(1, D), final_norm.reshape(1, D)

    names = ["a_w_in", "a_w_out", "w_kv", "b_w_in", "b_w_out"]
    big_w = dict(zip(names, [a_w_in[0], a_w_out[0], w_kv, b_w_in[0], b_w_out[0]]))
    big_m = dict(zip(names, [m_a_w_in[0], m_a_w_out[0], m_w_kv, m_b_w_in[0], m_b_w_out[0]]))
    big_v = dict(zip(names, [v_a_w_in[0], v_a_w_out[0], v_w_kv, v_b_w_in[0], v_b_w_out[0]]))
    sh_w = _pack_sharded(a_norm, a_conv_w, a_conv_b, a_ln_g, a_ln_b)

    wa_in, sh_all = all_gather([big_w["a_w_in"].astype(BF16), sh_w], "gather_first")
    sh_full = sh_all.transpose(1, 0, 2).reshape(SMALL_SH_ROWS, D)
    an, cw, cb = sh_full[0:1], sh_full[16:16 + HALO], sh_full[10:11]
    lg, lb = sh_full[8:9], sh_full[9:10]

    buf = {nm: place_shard(big_w[nm], "place_" + nm) for nm in names[1:]}
    kv_cut = _kv_row_pieces(D)

    riding = []

    def ride(nm, **kw):
        riding.append((nm, gather_job(buf[nm], **kw)))
        return riding[-1][1]

    def landed():
        while riding:
            nm, job = riding.pop()
            buf[nm] = job.results[0]

    (h0,) = rms_fwd(x0, [an], "rms_a")
    proj = matmul_nn(h0, wa_in, "a_in",
                     jobs=[ride("a_w_out", chips=(0, E // N_DEV)), ride("w_kv", chips=kv_cut[0])])
    landed()
    c, y = conf_fwd(proj, cw, cb, lg, lb, "conf_fwd",
                    jobs=[ride("a_w_out", sibling=(0, E // N_DEV)),
                          ride("w_kv", sibling=kv_cut[0], chips=kv_cut[1])])
    landed()
    wa_out = buf["a_w_out"].reshape(1, E, D)
    x1 = matmul_nn(y, wa_out, "a_out", res=x0, jobs=[ride("w_kv", sibling=kv_cut[1], chips=kv_cut[2])])
    landed()
    hk, hb = rms_fwd(x1, [kvn, bn], "rms_b", jobs=[ride("w_kv", sibling=kv_cut[2])])
    landed()
    wkv = buf["w_kv"]
    kv = matmul_nn(hk, wkv, "kv_proj", out_dtype=BF16, kv_split=True, jobs=[ride("b_w_in", chips=(0, D))])
    landed()

    bt = jnp.asarray(bucket_tables())
    onehot = (bt[None, :] == jnp.arange(N_BUCKETS, dtype=jnp.int32)[:, None]).astype(F32)
    bias = small_dot(rel_bias.T, onehot, "nn", "bias_table", jobs=[ride("b_w_in", sibling=(0, D))])
    landed()
    bias = bias.reshape(H, N_GROUPS, BLOCK, 2 * BLOCK)
    bias = [bias[:, g] for g in range(N_GROUPS)]
    wb_in = buf["b_w_in"]
    qz = matmul_nn(hb, wb_in, "b_in", jobs=[ride("b_w_out", chips=(0, D // N_DEV))])
    landed()

    os_, lses = [], []
    for g, (_, dil) in enumerate(GROUPS):
        o_g, l_g = attn_fwd(qz, kv, bias[g], g, dil, "attn_fwd%d" % g,
                            jobs=[ride("b_w_out", sibling=(0, D // N_DEV))] if g == 0 else ())
        landed()
        os_.append(o_g)
        lses.append(l_g)
    wb_out = buf["b_w_out"].reshape(1, D, D)
    o, lse, y2 = merge_fwd(os_, lses, qz, "merge_fwd")
    x2 = matmul_nn(y2, wb_out, "b_out", res=x1)
    dx2, dx2b, fin_acc = final_loss(x2, fn, target, "final_loss")

    dy2 = matmul_nt(dx2b, wb_out, "b_out_dx")
    dwb_out = matmul_tn(y2, dx2b, 1, "b_out_dw").reshape(N_DEV, D // N_DEV, D)
    r_b_out = reduce_sibling_job(dwb_out)
    do, delta, dqz = merge_bwd(dy2, o, qz, "merge_bwd", jobs=[r_b_out])
    p_b_out = add_pairs(dwb_out, r_b_out.results[0], "pairs_b_out")
    r_b_out2 = reduce_chips_job(p_b_out)
    dkv = None
    ds_tabs = []
    for g, (_, dil) in enumerate(GROUPS):
        dqz, dkv, ds_tab = attn_bwd(qz, kv, do, lse, delta, bias[g], dqz, dkv, g, dil,
                                    "attn_bwd%d" % g, jobs=[r_b_out2] if g == 0 else ())
        ds_tabs.append(ds_tab.reshape(H, 2 * BLOCK * BLOCK))
    d_rel = small_dot(onehot, jnp.concatenate(ds_tabs, axis=1), "nt", "bias_grad")
    dhb = matmul_nt(dqz, wb_in, "b_in_dx")
    dwb_in = matmul_tn(hb, dqz, N_DEV, "b_in_dw")
    r_b_in = reduce_sibling_job(dwb_in)
    dhk = matmul_nt(dkv, wkv, "kv_dx", kv_split=True, jobs=[r_b_in])
    p_b_in = add_pairs(dwb_in, r_b_in.results[0], "pairs_b_in")
    r_b_in2 = reduce_chips_job(p_b_in)
    dw_kv = matmul_tn(hk, dkv, N_DEV, "kv_dw", kv_split=True, jobs=[r_b_in2])
    r_kv = reduce_sibling_job(dw_kv)
    dx1, dx1b, norm_acc = rms_bwd(x1, [dhk, dhb], [kvn, bn], dx2, "rms_b_bwd", True, jobs=[r_kv])
    p_kv = add_pairs(dw_kv, r_kv.results[0], "pairs_kv")
    r_kv2 = reduce_chips_job(p_kv)

    dy = matmul_nt(dx1b, wa_out, "a_out_dx")
    dwa_out = matmul_tn(y, dx1b, 1, "a_out_dw").reshape(N_DEV, E // N_DEV, D)
    r_a_out = reduce_sibling_job(dwa_out)
    dc, dz, ln_acc = conf_bwd_ln(c, dy, proj, lg, lb, "conf_bwd_ln", jobs=[r_a_out])
    p_a_out = add_pairs(dwa_out, r_a_out.results[0], "pairs_a_out")
    r_a_out2 = reduce_chips_job(p_a_out)
    dproj, dcw = conf_bwd_conv(proj, dc, dz, cw, "conf_bwd_conv", jobs=[r_kv2, r_a_out2])
    share = share_small_job(jnp.concatenate([ln_acc, dcw, norm_acc, fin_acc, _pack_rel(d_rel, D)], axis=0))
    dwa_lo = matmul_tn(h0, dproj, N_DEV, "a_in_dw_lo", k_tiles=(0, 1), jobs=[share])
    r_lo = reduce_sibling_job(dwa_lo)
    dwa_hi = matmul_tn(h0, dproj, N_DEV, "a_in_dw_hi", k_tiles=(1, 2), jobs=[r_lo])
    p_lo = add_pairs(dwa_lo, r_lo.results[0], "pairs_a_in_lo")
    r_lo2, r_hi = reduce_chips_job(p_lo), reduce_sibling_job(dwa_hi)
    dh0 = matmul_nt(dproj, wa_in, "a_in_dx", jobs=[r_lo2, r_hi])
    p_hi = add_pairs(dwa_hi, r_hi.results[0], "pairs_a_in_hi")
    r_hi2 = reduce_chips_job(p_hi)
    grad_x, a_acc = rms_bwd(x0, [dh0], [an], dx1, "rms_a_bwd", False, jobs=[r_hi2])

    share_a = share_small_job(a_acc)
    sums = {"w_kv": [(p_kv, r_kv2)], "b_w_in": [(p_b_in, r_b_in2)], "b_w_out": [(p_b_out, r_b_out2)],
            "a_w_out": [(p_a_out, r_a_out2)], "a_w_in": [(p_lo, r_lo2), (p_hi, r_hi2)]}
    big_out = {}
    for nm, pieces in sums.items():
        big_out[nm] = adamw_reduce([(part, job.results[0]) for part, job in pieces],
                                   big_w[nm], big_m[nm], big_v[nm], "adamw_" + nm,
                                   jobs=[share_a] if nm == "a_w_out" else ())

    gsum = jnp.concatenate([sum_parts(share_a.results[0], "sum_small_a"),
                            sum_parts(share.results[0], "sum_small")], axis=0)
    me = 4 * lax.axis_index("x") + 2 * lax.axis_index("y") + lax.axis_index("c")
    g_sh = lax.dynamic_slice(gsum, (0, me * nsh), (SMALL_SH_ROWS, nsh))
    g_rep = gsum[SMALL_SH_ROWS:]
    loss = g_rep[9, 0]
    sh_m = _pack_sharded(m_a_norm, m_a_conv_w, m_a_conv_b, m_a_ln_g, m_a_ln_b)
    sh_v = _pack_sharded(v_a_norm, v_a_conv_w, v_a_conv_b, v_a_ln_g, v_a_ln_b)
    sh_d, sh_nm, sh_nv = adamw_small(sh_w, g_sh, sh_m, sh_v, "adamw_sharded")
    rep_w = _pack_replicated(kv_norm, b_norm, final_norm, rel_bias, D)
    rep_m = _pack_replicated(m_kv_norm, m_b_norm, m_final_norm, m_rel_bias, D)
    rep_v = _pack_replicated(v_kv_norm, v_b_norm, v_final_norm, v_rel_bias, D)
    rep_d, rep_nm, rep_nv = adamw_small(rep_w, g_rep, rep_m, rep_v, "adamw_replicated")

    def unpack(kind):
        sh = (g_sh, sh_d, sh_nm, sh_nv)[kind]
        rep = (g_rep, rep_d, rep_nm, rep_nv)[kind]
        big = {nm: big_out[nm][kind] for nm in names}
        nb = rel_bias.size
        return [
            sh[0:1],
            big["a_w_in"][None],
            sh[16:16 + CONV_WIDTH][None],
            sh[10:11], sh[8:9], sh[9:10],
            big["a_w_out"][None],
            rep[0],
            big["w_kv"],
            rep[1:2],
            big["b_w_in"][None],
            big["b_w_out"][None],
            rep[16, :nb].reshape(rel_bias.shape),
            rep[8],
        ]

    return (loss, grad_x.reshape(1, S, D), *unpack(0), *unpack(1), *unpack(2), *unpack(3))
```

```python
import functools
import math

import numpy as np
import jax
import jax.numpy as jnp
from jax import lax
from jax.experimental import pallas as pl
from jax.experimental.pallas import tpu as pltpu

F32 = jnp.float32
BF16 = jnp.bfloat16

N_DEV = 8
N_CHIPS = 4
EPS = 1e-6
HEAD_DIM = 128
BLOCK = 128
GROUPS = ((128, 1), (512, 4), (2048, 16))
N_GROUPS = len(GROUPS)
CONV_WIDTH = 31
HALO = 32
N_BUCKETS = 32
MAX_EXACT = N_BUCKETS // 2
MAX_DISTANCE = 2048
V7X_VMEM_BYTES = 64 * 1024 * 1024
VMEM_LIMIT = (V7X_VMEM_BYTES * 7) // 8
MATMUL_VMEM_BUDGET = (V7X_VMEM_BYTES * 11) // 16
LANE = 128

ADAM_LR = 0.001
ADAM_B1 = 0.9
ADAM_B2 = 0.999
ADAM_EPS = 1e-08
ADAM_WD = 0.01
ADAM_STEP = 10

SMALL_SH_ROWS = 48
SMALL_REP_ROWS = 24
MESH = pl.DeviceIdType.MESH
ANY_SPEC = pl.BlockSpec(memory_space=pl.ANY)
VMEM_SPEC = pl.BlockSpec(memory_space=pltpu.VMEM)


def _tile(dim, pref, unit=LANE):
    if dim <= pref:
        return dim
    t = (pref // unit) * unit
    while dim % t:
        t -= unit
    assert t > 0
    return t


def _sigmoid(v):
    return jax.nn.sigmoid(v)


def _place():
    return lax.axis_index("x"), lax.axis_index("y"), lax.axis_index("c")


def _flip(v, bit):
    return 1 - v if bit else v


class Job:
    def __init__(self, srcs, dsts, n_sems, build):
        self.srcs, self.dsts, self.n_sems, self.build = list(srcs), list(dsts), n_sems, build
        self.results = None


def _remote(src, dst, send_sems, recv_sems, k, peer):
    return pltpu.make_async_remote_copy(src_ref=src, dst_ref=dst, send_sem=send_sems.at[k],
                                        recv_sem=recv_sems.at[k], device_id=peer, device_id_type=MESH)


def _call(body, *, name, in_specs, out_specs, out_shape, args, grid=(), scratch_shapes=(), sem=(),
          aliases=None, jobs=()):
    n_in, n_out, n_scr = len(in_specs), len(out_specs), len(scratch_shapes)
    aliases = dict(aliases or {})
    x_in, x_out, x_scr = [], [], []
    for job in jobs:
        job.in_at = n_in + len(x_in)
        x_in += job.srcs
        job.out_at = n_out + len(x_out)
        for d in job.dsts:
            if not isinstance(d, jax.ShapeDtypeStruct):
                aliases[n_in + len(x_in)] = n_out + len(x_out)
                x_in.append(d)
            x_out.append(jax.ShapeDtypeStruct(d.shape, d.dtype))
        job.scr_at = n_scr + len(x_scr)
        x_scr += [pltpu.SemaphoreType.DMA((job.n_sems,))] * 3

    def wrapped(*refs):
        ins = refs[:n_in + len(x_in)]
        outs = refs[len(ins):len(ins) + n_out + len(x_out)]
        scr = refs[len(ins) + len(outs):]
        core = ins[:n_in] + outs[:n_out] + scr[:n_scr]
        if not jobs:
            body(*core)
            return
        copies = []
        for job in jobs:
            copies += job.build(ins[job.in_at:job.in_at + len(job.srcs)],
                                outs[job.out_at:job.out_at + len(job.dsts)],
                                *scr[job.scr_at:job.scr_at + 3])
        if grid:
            pids = [pl.program_id(d) for d in range(len(grid))]
            first = functools.reduce(jnp.logical_and, [p == 0 for p in pids])
            last = functools.reduce(jnp.logical_and, [p == g - 1 for p, g in zip(pids, grid)])

            @pl.when(first)
            def _():
                for cp in copies:
                    cp.start()

            body(*core)

            @pl.when(last)
            def _():
                for cp in copies:
                    cp.wait()
        else:
            for cp in copies:
                cp.start()
            body(*core)
            for cp in copies:
                cp.wait()

    if jobs:
        sem = ("arbitrary",) * len(grid)
    kwargs = dict(grid=grid) if grid else {}
    if aliases:
        kwargs["input_output_aliases"] = aliases
    outs = pl.pallas_call(
        wrapped, name=name,
        in_specs=list(in_specs) + [ANY_SPEC] * len(x_in),
        out_specs=list(out_specs) + [ANY_SPEC] * len(x_out),
        out_shape=list(out_shape) + x_out,
        scratch_shapes=list(scratch_shapes) + x_scr,
        compiler_params=pltpu.CompilerParams(dimension_semantics=sem if sem else None,
                                             vmem_limit_bytes=VMEM_LIMIT),
        **kwargs,
    )(*args, *x_in)
    for job in jobs:
        job.results = list(outs[job.out_at:job.out_at + len(job.dsts)])
    return list(outs[:n_out])


def place_shard(w, name):
    K, n = w.shape
    tk = _tile(K, 512, 16)

    def body(p_ref, w_ref, o_ref):
        o_ref[...] = w_ref[...].astype(BF16)

    grid_spec = pltpu.PrefetchScalarGridSpec(
        num_scalar_prefetch=1, grid=(K // tk,),
        in_specs=[pl.BlockSpec((tk, n), lambda i, p: (i, 0))],
        out_specs=pl.BlockSpec((None, tk, n), lambda i, p: (4 * p[0] + 2 * p[1] + p[2], i, 0)))
    return pl.pallas_call(body, name=name, grid_spec=grid_spec,
                          out_shape=jax.ShapeDtypeStruct((N_DEV, K, n), BF16))(_place_vector(), w)


def _place_vector():
    return jnp.stack(_place()).astype(jnp.int32)


def gather_job(buf, chips=None, sibling=None):
    def build(srcs, dsts, send, recv, loc):
        (out,) = dsts
        x, y, c = _place()
        copies = []
        if chips is not None:
            mine = out.at[4 * x + 2 * y + c, pl.ds(chips[0], chips[1] - chips[0])]
            peers = [(x, y, 1 - c), (1 - x, y, c), (x, 1 - y, c), (1 - x, 1 - y, c)]
            copies += [_remote(mine, mine, send, recv, k, p) for k, p in enumerate(peers)]
        if sibling is not None:
            for k, (cx, cy) in enumerate([(1 - x, y), (x, 1 - y), (1 - x, 1 - y)]):
                blk = out.at[4 * cx + 2 * cy + c, pl.ds(sibling[0], sibling[1] - sibling[0])]
                copies.append(_remote(blk, blk, send, recv, 4 + k, (x, y, 1 - c)))
        return copies

    return Job([], [buf], 7, build)


def reduce_sibling_job(dw):
    def build(srcs, dsts, send, recv, loc):
        (src,), (got,) = srcs, dsts
        x, y, c = _place()
        return [_remote(src.at[4 * _flip(x, r & 2) + 2 * _flip(y, r & 1) + 1 - c], got.at[r], send, recv, r,
                        (x, y, 1 - c)) for r in range(N_CHIPS)]

    return Job([dw], [jax.ShapeDtypeStruct((N_CHIPS,) + dw.shape[1:], dw.dtype)], N_CHIPS, build)


def reduce_chips_job(part):
    def build(srcs, dsts, send, recv, loc):
        (src,), (got,) = srcs, dsts
        x, y, c = _place()
        return [_remote(src.at[r], got.at[r - 1], send, recv, r - 1, (_flip(x, r & 2), _flip(y, r & 1), c))
                for r in range(1, N_CHIPS)]

    return Job([part], [jax.ShapeDtypeStruct((N_CHIPS - 1,) + part.shape[1:], part.dtype)], N_CHIPS - 1, build)


def share_small_job(small):
    def build(srcs, dsts, send, recv, loc):
        (src,), (out,) = srcs, dsts
        x, y, c = _place()
        mine = out.at[4 * x + 2 * y + c]
        copies = [pltpu.make_async_copy(src, mine, loc.at[0])]
        for rel in range(1, N_DEV):
            peer = (_flip(x, rel & 4), _flip(y, rel & 2), _flip(c, rel & 1))
            copies.append(_remote(src, mine, send, recv, rel - 1, peer))
        return copies

    return Job([small], [jax.ShapeDtypeStruct((N_DEV,) + small.shape, small.dtype)], N_DEV - 1, build)


def all_gather(shards, name):
    n = len(shards)

    def body(*refs):
        ins, outs = refs[:n], refs[n:2 * n]
        send_sems, recv_sems, local_sems = refs[2 * n:]
        x, y, c = _place()
        me, sibling = (x, y, c), (x, y, 1 - c)
        chips = [(1 - x, y), (x, 1 - y), (1 - x, 1 - y)]

        def slot(a, dev):
            return outs[a].at[4 * dev[0] + 2 * dev[1] + dev[2]]

        def copy(a, k, block, to, src=None):
            return pltpu.make_async_remote_copy(
                src_ref=slot(a, block) if src is None else src, dst_ref=slot(a, block),
                send_sem=send_sems.at[a, k], recv_sem=recv_sems.at[a, k],
                device_id=to, device_id_type=MESH)

        mine, first, passed = [], [], []
        for a in range(n):
            mine.append(pltpu.make_async_copy(ins[a], slot(a, me), local_sems.at[a]))
            mine[a].start()
            first.append([copy(a, 0, me, sibling, src=ins[a])]
                         + [copy(a, 1 + j, me, (*chip, c), src=ins[a]) for j, chip in enumerate(chips)])
            for cp in first[a]:
                cp.start()
        for a in range(n):
            passed.append([copy(a, 4 + j, (*chip, c), sibling) for j, chip in enumerate(chips)])
            for j, chip in enumerate(chips):
                copy(a, 1 + j, (*chip, c), me).wait_recv()
                passed[a][j].start()
        for a in range(n):
            copy(a, 0, sibling, me).wait_recv()
            for j, chip in enumerate(chips):
                copy(a, 4 + j, (*chip, 1 - c), me).wait_recv()
            for cp in first[a] + passed[a]:
                cp.wait_send()
            mine[a].wait()

    return pl.pallas_call(
        body, name=name,
        in_specs=[ANY_SPEC] * n, out_specs=[ANY_SPEC] * n,
        out_shape=[jax.ShapeDtypeStruct((N_DEV,) + s.shape, s.dtype) for s in shards],
        scratch_shapes=[pltpu.SemaphoreType.DMA((n, 7)), pltpu.SemaphoreType.DMA((n, 7)),
                        pltpu.SemaphoreType.DMA((n,))],
    )(*shards)


def _kv_split_index(tw, D):
    pd = D // tw

    def index(j):
        return (j // pd) % 2, (j // (2 * pd)) * pd + j % pd

    return index


def _col_tile(n, also, pref):
    t = (min(pref, n) // LANE) * LANE
    while n % t or (also is not None and also % t):
        t -= LANE
    assert t > 0
    return t


def matmul_nn(a, w, name, res=None, out_dtype=F32, kv_split=False, jobs=()):
    M, K = a.shape
    nb, _, n = w.shape
    D = nb * n // (2 * N_GROUPS)
    tn = _col_tile(n, D if kv_split else None, 1024)
    out_bytes = jnp.dtype(out_dtype).itemsize + (4 if res is not None else 0)
    tm = _tile(M, 2048)
    if 2 * (tm * K * 2 + K * tn * 2 + tm * tn * out_bytes) > MATMUL_VMEM_BUDGET:
        tm = _tile(M, 1024)
    per = n // tn

    def body(*refs):
        if res is None:
            a_ref, w_ref, o_ref = refs
        else:
            a_ref, w_ref, r_ref, o_ref = refs
        acc = jnp.dot(a_ref[...], w_ref[...], preferred_element_type=F32)
        if res is not None:
            acc = r_ref[...] + acc
        o_ref[...] = acc.astype(o_ref.dtype)

    in_specs = [pl.BlockSpec((tm, K), lambda i, j: (i, 0)),
                pl.BlockSpec((None, K, tn), lambda i, j: (j // per, 0, j % per))]
    args = [a, w]
    if res is not None:
        in_specs.append(pl.BlockSpec((tm, tn), lambda i, j: (i, j)))
        args.append(res)
    if kv_split:
        split = _kv_split_index(tn, D)
        out_spec = pl.BlockSpec((None, tm, tn), lambda i, j: (split(j)[0], i, split(j)[1]))
        out_shape = jax.ShapeDtypeStruct((2, M, N_GROUPS * D), out_dtype)
    else:
        out_spec = pl.BlockSpec((tm, tn), lambda i, j: (i, j))
        out_shape = jax.ShapeDtypeStruct((M, nb * n), out_dtype)
    return _call(body, name=name, grid=(M // tm, nb * per), in_specs=in_specs, out_specs=[out_spec],
                 out_shape=[out_shape], args=args, sem=("parallel", "parallel"), jobs=jobs)[0]


def matmul_nt(dy, w, name, kv_split=False, jobs=()):
    M = dy.shape[-2]
    nb, K, n = w.shape
    D = nb * n // (2 * N_GROUPS)
    tm = _tile(M, 1024)
    tc = _col_tile(n, D if kv_split else None, 1024)
    per = n // tc
    pair = max(u for u in (1, 2, 4) if (nb * per) % u == 0 and u * tc <= 2048)

    def body(*refs):
        o_ref = refs[-1]
        j = pl.program_id(1)
        part = None
        for u in range(pair):
            d = lax.dot_general(refs[u][...], refs[pair + u][...], (((1,), (1,)), ((), ())),
                                preferred_element_type=F32)
            part = d if part is None else part + d

        @pl.when(j == 0)
        def _():
            o_ref[...] = part

        @pl.when(j > 0)
        def _():
            o_ref[...] += part

    def dy_spec(u):
        if kv_split:
            split = _kv_split_index(tc, D)
            return pl.BlockSpec((None, tm, tc),
                                lambda i, j: (split(pair * j + u)[0], i, split(pair * j + u)[1]))
        return pl.BlockSpec((tm, tc), lambda i, j: (i, pair * j + u))

    def w_spec(u):
        return pl.BlockSpec((None, K, tc), lambda i, j: ((pair * j + u) // per, 0, (pair * j + u) % per))

    return _call(body, name=name, grid=(M // tm, nb * per // pair),
                 in_specs=[dy_spec(u) for u in range(pair)] + [w_spec(u) for u in range(pair)],
                 out_specs=[pl.BlockSpec((tm, K), lambda i, j: (i, 0))],
                 out_shape=[jax.ShapeDtypeStruct((M, K), F32)], args=[dy] * pair + [w] * pair,
                 sem=("parallel", "arbitrary"), jobs=jobs)[0]


def matmul_tn(a, dy, nb, name, out_dtype=BF16, kv_split=False, k_tiles=None, jobs=()):
    M, K = a.shape
    N = 2 * dy.shape[-1] if kv_split else dy.shape[-1]
    n = N // nb
    D = N // (2 * N_GROUPS)
    tn = _col_tile(n, D if kv_split else None, 1024)
    per = n // tn
    tk = _tile(K, 1024)

    def body(a_ref, dy_ref, o_ref):
        o_ref[...] = lax.dot_general(a_ref[...], dy_ref[...], (((0,), (0,)), ((), ())),
                                     preferred_element_type=F32).astype(o_ref.dtype)

    if kv_split:
        split = _kv_split_index(tn, D)
        dy_spec = pl.BlockSpec((None, M, tn), lambda k, j: (split(j)[0], 0, split(j)[1]))
    else:
        dy_spec = pl.BlockSpec((M, tn), lambda k, j: (0, j))
    k0, k1 = k_tiles or (0, K // tk)
    return _call(body, name=name, grid=(k1 - k0, nb * per),
                 in_specs=[pl.BlockSpec((M, tk), lambda k, j: (0, k0 + k)), dy_spec],
                 out_specs=[pl.BlockSpec((None, tk, tn), lambda k, j: (j // per, k, j % per))],
                 out_shape=[jax.ShapeDtypeStruct((nb, (k1 - k0) * tk, n), out_dtype)], args=[a, dy],
                 sem=("parallel", "parallel"), jobs=jobs)[0]


def small_dot(a, b, contract, name, jobs=()):
    if contract == "nn":
        dims = (((1,), (0,)), ((), ()))
        out = (a.shape[0], b.shape[1])
    else:
        dims = (((1,), (1,)), ((), ()))
        out = (a.shape[0], b.shape[0])

    def body(a_ref, b_ref, o_ref):
        o_ref[...] = lax.dot_general(a_ref[...], b_ref[...], dims, precision=lax.Precision.HIGHEST,
                                     preferred_element_type=F32)

    return _call(body, name=name, in_specs=[VMEM_SPEC, VMEM_SPEC], out_specs=[VMEM_SPEC],
                 out_shape=[jax.ShapeDtypeStruct(out, F32)], args=[a, b], jobs=jobs)[0]


def rms_fwd(x, gains, name, jobs=()):
    S, D = x.shape
    T = _tile(S, 512, 8)
    n = len(gains)

    def body(x_ref, *refs):
        xv = x_ref[...]
        xn = xv * lax.rsqrt(jnp.mean(xv * xv, axis=-1, keepdims=True) + EPS)
        for g_ref, o_ref in zip(refs[:n], refs[n:]):
            o_ref[...] = (xn * g_ref[...]).astype(o_ref.dtype)

    row = pl.BlockSpec((T, D), lambda i: (i, 0))
    vec = pl.BlockSpec((1, D), lambda i: (0, 0))
    return _call(body, name=name, grid=(S // T,), in_specs=[row] + [vec] * n, out_specs=[row] * n,
                 out_shape=[jax.ShapeDtypeStruct((S, D), BF16)] * n, args=[x, *gains],
                 sem=("parallel",), jobs=jobs)


def rms_bwd(x, dhs, gains, dres, name, want_bf16, jobs=()):
    S, D = x.shape
    T = _tile(S, 256, 8)
    n = len(gains)

    def body(x_ref, *refs):
        dh_refs = refs[:n]
        g_refs = refs[n:2 * n]
        dres_ref = refs[2 * n]
        outs = refs[2 * n + 1:]
        dx_ref, dg_ref = outs[0], outs[-1]
        i = pl.program_id(0)

        @pl.when(i == 0)
        def _():
            dg_ref[...] = jnp.zeros_like(dg_ref)

        xv = x_ref[...]
        r = lax.rsqrt(jnp.mean(xv * xv, axis=-1, keepdims=True) + EPS)
        xn = xv * r
        dxn = jnp.zeros_like(xv)
        for k in range(n):
            dh = dh_refs[k][...]
            dg_ref[k:k + 1, :] += jnp.sum(dh * xn, axis=0, keepdims=True)
            dxn = dxn + dh * g_refs[k][...]
        dx = dres_ref[...] + r * (dxn - xn * jnp.mean(dxn * xn, axis=-1, keepdims=True))
        dx_ref[...] = dx
        if want_bf16:
            outs[1][...] = dx.astype(BF16)

    row = pl.BlockSpec((T, D), lambda i: (i, 0))
    vec = pl.BlockSpec((1, D), lambda i: (0, 0))
    acc = pl.BlockSpec((8, D), lambda i: (0, 0))
    out_specs = [row] + ([row] if want_bf16 else []) + [acc]
    out_shape = ([jax.ShapeDtypeStruct((S, D), F32)]
                 + ([jax.ShapeDtypeStruct((S, D), BF16)] if want_bf16 else [])
                 + [jax.ShapeDtypeStruct((8, D), F32)])
    return _call(body, name=name, grid=(S // T,), in_specs=[row] + [row] * n + [vec] * n + [row],
                 out_specs=out_specs, out_shape=out_shape, args=[x, *dhs, *gains, dres],
                 sem=("arbitrary",), jobs=jobs)


def final_loss(x2, gain, target, name, jobs=()):
    S, D = x2.shape
    T = _tile(S, 256, 8)

    def body(x_ref, g_ref, t_ref, dx_ref, dxb_ref, acc_ref):
        i = pl.program_id(0)

        @pl.when(i == 0)
        def _():
            acc_ref[...] = jnp.zeros_like(acc_ref)

        xv = x_ref[...]
        g = g_ref[...]
        r = lax.rsqrt(jnp.mean(xv * xv, axis=-1, keepdims=True) + EPS)
        xn = xv * r
        err = xn * g - t_ref[...]
        dy = err * (1.0 / D)
        acc_ref[0:1, :] += jnp.sum(dy * xn, axis=0, keepdims=True)
        acc_ref[1:2, :] += jnp.full((1, D), 0.5 / D, F32) * jnp.sum(err * err)
        dxn = dy * g
        dx = r * (dxn - xn * jnp.mean(dxn * xn, axis=-1, keepdims=True))
        dx_ref[...] = dx
        dxb_ref[...] = dx.astype(BF16)

    row = pl.BlockSpec((T, D), lambda i: (i, 0))
    return _call(body, name=name, grid=(S // T,),
                 in_specs=[row, pl.BlockSpec((1, D), lambda i: (0, 0)), row],
                 out_specs=[row, row, pl.BlockSpec((8, D), lambda i: (0, 0))],
                 out_shape=[jax.ShapeDtypeStruct((S, D), F32), jax.ShapeDtypeStruct((S, D), BF16),
                            jax.ShapeDtypeStruct((8, D), F32)],
                 args=[x2, gain, target], sem=("arbitrary",), jobs=jobs)


ROW_CHUNK = 64
LANE_CHUNK = 512
SUBLANES = 8


def _shifted_copies(buf, sh_scr, l0, lc):
    n = buf.shape[0] - SUBLANES
    for b in range(1, SUBLANES):
        sh_scr[b - 1, 0:n, :] = buf[b:b + n, l0:l0 + lc]


def _shifted(buf, sh_scr, start, rows, l0, lc):
    a8, b = (start // SUBLANES) * SUBLANES, start % SUBLANES
    if b == 0:
        return buf[a8:a8 + rows, l0:l0 + lc]
    return sh_scr[b - 1, a8:a8 + rows, :]


def conf_fwd(proj, cw, cb, lg, lb, name, jobs=()):
    S, E3 = proj.shape
    E = E3 // 3
    T = _tile(S, 256, HALO)
    R = T // HALO
    lc = _tile(E, LANE_CHUNK)
    rc = min(ROW_CHUNK, T)

    def body(a_ref, b_ref, z_ref, ap_ref, bp_ref, cw_ref, cb_ref, lg_ref, lb_ref, c_ref, y_ref, u_scr, sh_scr):
        i = pl.program_id(0)
        up = ap_ref[...] * _sigmoid(bp_ref[...])
        u_scr[0:HALO, :] = jnp.where(i > 0, up, 0.0)
        u_scr[HALO:HALO + T, :] = a_ref[...] * _sigmoid(b_ref[...])
        off = HALO - (CONV_WIDTH - 1)
        for l0 in range(0, E, lc):
            _shifted_copies(u_scr, sh_scr, l0, lc)
            for r0 in range(0, T, rc):
                acc = jnp.broadcast_to(cb_ref[:, l0:l0 + lc], (rc, lc))
                for k in range(CONV_WIDTH):
                    acc = acc + _shifted(u_scr, sh_scr, r0 + off + k, rc, l0, lc) * cw_ref[k:k + 1, l0:l0 + lc]
                c_ref[r0:r0 + rc, l0:l0 + lc] = acc
        c = c_ref[...]
        mu = jnp.mean(c, axis=-1, keepdims=True)
        d = c - mu
        var = jnp.mean(d * d, axis=-1, keepdims=True)
        cn = d * lax.rsqrt(var + EPS) * lg_ref[...] + lb_ref[...]
        z = z_ref[...]
        y_ref[...] = ((cn * _sigmoid(cn)) * (z * _sigmoid(z))).astype(BF16)

    def col(j):
        return pl.BlockSpec((T, E), lambda i: (i, j))

    def prev(j):
        return pl.BlockSpec((HALO, E), lambda i: (jnp.maximum(i * R - 1, 0), j))

    vec = pl.BlockSpec((1, E), lambda i: (0, 0))
    return _call(body, name=name, grid=(S // T,),
                 in_specs=[col(0), col(1), col(2), prev(0), prev(1),
                           pl.BlockSpec((HALO, E), lambda i: (0, 0)), vec, vec, vec],
                 out_specs=[pl.BlockSpec((T, E), lambda i: (i, 0))] * 2,
                 out_shape=[jax.ShapeDtypeStruct((S, E), F32), jax.ShapeDtypeStruct((S, E), BF16)],
                 scratch_shapes=[pltpu.VMEM((HALO + T, E), F32), pltpu.VMEM((SUBLANES - 1, HALO + T, lc), F32)],
                 args=[proj, proj, proj, proj, proj, cw, cb, lg, lb], sem=("parallel",), jobs=jobs)


def conf_bwd_ln(c, dy, proj, lg, lb, name, jobs=()):
    S, E = c.shape
    T = _tile(S, 256, 8)

    def body(c_ref, dy_ref, z_ref, lg_ref, lb_ref, dc_ref, dz_ref, acc_ref):
        i = pl.program_id(0)

        @pl.when(i == 0)
        def _():
            acc_ref[...] = jnp.zeros_like(acc_ref)

        cv = c_ref[...]
        mu = jnp.mean(cv, axis=-1, keepdims=True)
        d = cv - mu
        var = jnp.mean(d * d, axis=-1, keepdims=True)
        rstd = lax.rsqrt(var + EPS)
        xh = d * rstd
        lgv = lg_ref[...]
        cn = xh * lgv + lb_ref[...]
        z = z_ref[...]
        dy = dy_ref[...]
        sc = _sigmoid(cn)
        sz = _sigmoid(z)
        dcn = dy * (z * sz) * (sc * (1.0 + cn * (1.0 - sc)))
        dz_ref[...] = (dy * (cn * sc) * (sz * (1.0 + z * (1.0 - sz)))).astype(BF16)
        acc_ref[0:1, :] += jnp.sum(dcn * xh, axis=0, keepdims=True)
        acc_ref[1:2, :] += jnp.sum(dcn, axis=0, keepdims=True)
        dxh = dcn * lgv
        dc = rstd * (dxh - jnp.mean(dxh, axis=-1, keepdims=True)
                     - xh * jnp.mean(dxh * xh, axis=-1, keepdims=True))
        acc_ref[2:3, :] += jnp.sum(dc, axis=0, keepdims=True)
        dc_ref[...] = dc

    row = pl.BlockSpec((T, E), lambda i: (i, 0))
    vec = pl.BlockSpec((1, E), lambda i: (0, 0))
    return _call(body, name=name, grid=(S // T,),
                 in_specs=[row, row, pl.BlockSpec((T, E), lambda i: (i, 2)), vec, vec],
                 out_specs=[row, row, pl.BlockSpec((8, E), lambda i: (0, 0))],
                 out_shape=[jax.ShapeDtypeStruct((S, E), F32), jax.ShapeDtypeStruct((S, E), BF16),
                            jax.ShapeDtypeStruct((8, E), F32)],
                 args=[c, dy, proj, lg, lb], sem=("arbitrary",), jobs=jobs)


def conf_bwd_conv(proj, dc, dz, cw, name, jobs=()):
    S, E3 = proj.shape
    E = E3 // 3
    T = _tile(S, 256, HALO)
    R = T // HALO
    nt = S // T
    lc = _tile(E, LANE_CHUNK)
    rc = min(ROW_CHUNK, T)
    rd = min(ROW_CHUNK // 2, T)

    def body(a_ref, b_ref, ap_ref, bp_ref, dc_ref, dcn_ref, dz_ref, cw_ref, o_ref, dw_ref, u_scr, dc_scr, sh_scr,
             dw_scr):
        i = pl.program_id(0)

        @pl.when(i == 0)
        def _():
            dw_scr[...] = jnp.zeros_like(dw_scr)

        a = a_ref[...]
        sb = _sigmoid(b_ref[...])
        up = ap_ref[...] * _sigmoid(bp_ref[...])
        u_scr[0:HALO, :] = jnp.where(i > 0, up, 0.0)
        u_scr[HALO:HALO + T, :] = a * sb
        dc_scr[0:T, :] = dc_ref[...]
        dc_scr[T:T + HALO, :] = jnp.where(i < nt - 1, dcn_ref[...], 0.0)
        off = HALO - (CONV_WIDTH - 1)
        for l0 in range(0, E, lc):
            _shifted_copies(u_scr, sh_scr, l0, lc)
            for r0 in range(0, T, rd):
                dcc = dc_scr[r0:r0 + rd, l0:l0 + lc]
                for k in range(CONV_WIDTH):
                    prod = _shifted(u_scr, sh_scr, r0 + off + k, rd, l0, lc) * dcc
                    dw_scr[k, :, l0:l0 + lc] += jnp.sum(prod.reshape(rd // SUBLANES, SUBLANES, lc), axis=0)
            _shifted_copies(dc_scr, sh_scr, l0, lc)
            for r0 in range(0, T, rc):
                acc = jnp.zeros((rc, lc), F32)
                for k in range(CONV_WIDTH):
                    s0 = r0 + (CONV_WIDTH - 1) - k
                    acc = acc + _shifted(dc_scr, sh_scr, s0, rc, l0, lc) * cw_ref[k:k + 1, l0:l0 + lc]
                av = a[r0:r0 + rc, l0:l0 + lc]
                sv = sb[r0:r0 + rc, l0:l0 + lc]
                o_ref[r0:r0 + rc, l0:l0 + lc] = (acc * sv).astype(BF16)
                o_ref[r0:r0 + rc, E + l0:E + l0 + lc] = (acc * av * sv * (1.0 - sv)).astype(BF16)
        o_ref[:, 2 * E:3 * E] = dz_ref[...]

        @pl.when(i == nt - 1)
        def _():
            dw_ref[...] = jnp.sum(dw_scr[...], axis=1)

    def col(j):
        return pl.BlockSpec((T, E), lambda i: (i, j))

    def prev(j):
        return pl.BlockSpec((HALO, E), lambda i: (jnp.maximum(i * R - 1, 0), j))

    row = pl.BlockSpec((T, E), lambda i: (i, 0))
    nxt = pl.BlockSpec((HALO, E), lambda i: (jnp.minimum((i + 1) * R, S // HALO - 1), 0))
    return _call(body, name=name, grid=(nt,),
                 in_specs=[col(0), col(1), prev(0), prev(1), row, nxt, row,
                           pl.BlockSpec((HALO, E), lambda i: (0, 0))],
                 out_specs=[pl.BlockSpec((T, E3), lambda i: (i, 0)), pl.BlockSpec((HALO, E), lambda i: (0, 0))],
                 out_shape=[jax.ShapeDtypeStruct((S, E3), BF16), jax.ShapeDtypeStruct((HALO, E), F32)],
                 scratch_shapes=[pltpu.VMEM((HALO + T, E), F32), pltpu.VMEM((T + HALO, E), F32),
                                 pltpu.VMEM((SUBLANES - 1, HALO + T, lc), F32),
                                 pltpu.VMEM((HALO, SUBLANES, E), F32)],
                 args=[proj, proj, proj, proj, dc, dc, dz, cw], sem=("arbitrary",), jobs=jobs)


def bucket_tables():
    q = np.arange(BLOCK)[:, None]
    k = np.arange(2 * BLOCK)[None, :]
    out = []
    for window, dil in GROUPS:
        delta = q + BLOCK - k
        valid = (delta >= 0) & (delta <= window // dil)
        dist = np.clip(delta, 0, None) * dil
        large = MAX_EXACT + (np.log(np.maximum(dist, 1).astype(np.float32) / MAX_EXACT)
                             / math.log(MAX_DISTANCE / MAX_EXACT)
                             * (N_BUCKETS - MAX_EXACT)).astype(np.int32)
        large = np.minimum(large, N_BUCKETS - 1)
        bucket = np.where(dist < MAX_EXACT, dist, large)
        out.append(np.where(valid, bucket, -1).reshape(-1))
    return np.concatenate(out).astype(np.int32)


def _band_masks(has_previous):
    ql = lax.broadcasted_iota(jnp.int32, (BLOCK, 2 * BLOCK), 0)
    kk = lax.broadcasted_iota(jnp.int32, (BLOCK, 2 * BLOCK), 1)
    band = (kk >= ql) & (kk <= ql + BLOCK)
    return band, band & ((kk >= BLOCK) | has_previous)


def _dot_nt(a, b):
    return lax.dot_general(a, b, (((1,), (1,)), ((), ())), preferred_element_type=F32)


def _dot_tn(a, b):
    return lax.dot_general(a, b, (((0,), (0,)), ((), ())), preferred_element_type=F32)


ATTN_ROWS = 2048


def _sub(start, size, dil):
    return pl.ds(start, size) if dil == 1 else pl.ds(start, size, stride=dil)


def _attn_geometry(S, dil):
    halo = BLOCK * dil
    rows = max(min(S, ATTN_ROWS), halo)
    return halo, rows, S // rows, rows // halo


MAX_ROW_STRIDE = 8


def _split(dst, src, n, dil, tmp):
    if dil <= MAX_ROW_STRIDE:
        for r in range(dil):
            dst[r] = src[_sub(r, n, dil), :].astype(dst.dtype)
        return
    f, g = 4, dil // 4
    for r1 in range(f):
        tmp[0:n * g, :] = src[_sub(r1, n * g, f), :]
        for r2 in range(g):
            dst[r2 * f + r1] = tmp[_sub(r2, n, g), :].astype(dst.dtype)


def _merge(dst, src, n, dil, tmp):
    if dil <= MAX_ROW_STRIDE:
        for r in range(dil):
            dst[_sub(r, n, dil), :] = src[r]
        return
    f, g = 4, dil // 4
    for r1 in range(f):
        for r2 in range(g):
            tmp[_sub(r2, n, g), :] = src[r2 * f + r1]
        dst[_sub(r1, n * g, f), :] = tmp[0:n * g, :]


def _split_tmp(sub, dil):
    rows = (sub + BLOCK) * (dil // 4) if dil > MAX_ROW_STRIDE else SUBLANES
    return pltpu.VMEM((rows, HEAD_DIM), F32)


def attn_fwd(qz, kv, bias, g, dil, name, jobs=()):
    S = qz.shape[0]
    D = qz.shape[1] // (N_GROUPS + 1)
    H = D // HEAD_DIM
    halo, rows, nsb, nblk = _attn_geometry(S, dil)
    sub = nblk * BLOCK
    scale = HEAD_DIM ** -0.5

    def body(q_ref, kvc_ref, kvp_ref, b_ref, o_ref, l_ref, ks, vs, qd, kd, vd, od, ld, tmp):
        sb = pl.program_id(1)
        ks[0:halo, :] = kvp_ref[0].astype(F32)
        ks[halo:, :] = kvc_ref[0].astype(F32)
        vs[0:halo, :] = kvp_ref[1].astype(F32)
        vs[halo:, :] = kvc_ref[1].astype(F32)
        _split(kd, ks, sub + BLOCK, dil, tmp)
        _split(vd, vs, sub + BLOCK, dil, tmp)
        _split(qd, q_ref, sub, dil, tmp)
        band, first = _band_masks(sb > 0)
        bias_t = b_ref[0]
        for r in range(dil):
            for jj in range(nblk):
                q = qd[r, jj * BLOCK:(jj + 1) * BLOCK, :]
                keys = slice(jj * BLOCK, (jj + 2) * BLOCK)
                s = _dot_nt(q, kd[r, keys, :]) * scale + bias_t
                s = jnp.where(band if jj > 0 else first, s, -jnp.inf)
                m = jnp.max(s, axis=-1, keepdims=True)
                p = jnp.exp(s - m)
                den = jnp.sum(p, axis=-1, keepdims=True)
                pv = jnp.dot(p.astype(BF16), vd[r, keys, :], preferred_element_type=F32)
                own = slice(jj * BLOCK, (jj + 1) * BLOCK)
                od[r, own, :] = pv / den
                ld[r, own, :] = jnp.broadcast_to(m + jnp.log(den), (BLOCK, HEAD_DIM))
        _merge(o_ref, od, sub, dil, tmp)
        _merge(l_ref, ld, sub, dil, tmp)

    per = rows // halo
    out_spec = pl.BlockSpec((rows, HEAD_DIM), lambda h, sb: (sb, h))
    return _call(body, name=name, grid=(H, nsb),
                 in_specs=[pl.BlockSpec((rows, HEAD_DIM), lambda h, sb: (sb, g * H + h)),
                           pl.BlockSpec((2, rows, HEAD_DIM), lambda h, sb: (0, sb, g * H + h)),
                           pl.BlockSpec((2, halo, HEAD_DIM),
                                        lambda h, sb: (0, jnp.maximum(sb * per - 1, 0), g * H + h)),
                           pl.BlockSpec((1, BLOCK, 2 * BLOCK), lambda h, sb: (h, 0, 0))],
                 out_specs=[out_spec, out_spec],
                 out_shape=[jax.ShapeDtypeStruct((S, D), F32)] * 2,
                 scratch_shapes=[pltpu.VMEM((halo + rows, HEAD_DIM), F32)] * 2
                 + [pltpu.VMEM((dil, sub, HEAD_DIM), BF16)]
                 + [pltpu.VMEM((dil, sub + BLOCK, HEAD_DIM), BF16)] * 2
                 + [pltpu.VMEM((dil, sub, HEAD_DIM), F32)] * 2 + [_split_tmp(sub, dil)],
                 args=[qz, kv, kv, bias], sem=("parallel", "arbitrary"), jobs=jobs)


def merge_fwd(os_, lses, qz, name, jobs=()):
    S, D = os_[0].shape
    T = _tile(S, 256, 8)

    def body(o1, o2, o3, l1, l2, l3, z_ref, o_ref, lse_ref, y_ref):
        la, lb_, lc_ = l1[...], l2[...], l3[...]
        m = jnp.maximum(jnp.maximum(la, lb_), lc_)
        ea, eb, ec = jnp.exp(la - m), jnp.exp(lb_ - m), jnp.exp(lc_ - m)
        den = ea + eb + ec
        o = (ea * o1[...] + eb * o2[...] + ec * o3[...]) / den
        z = z_ref[...]
        o_ref[...] = o
        lse_ref[...] = m + jnp.log(den)
        y_ref[...] = (o * (z * _sigmoid(z))).astype(BF16)

    row = pl.BlockSpec((T, D), lambda i: (i, 0))
    return _call(body, name=name, grid=(S // T,),
                 in_specs=[row] * 6 + [pl.BlockSpec((T, D), lambda i: (i, N_GROUPS))],
                 out_specs=[row] * 3,
                 out_shape=[jax.ShapeDtypeStruct((S, D), F32), jax.ShapeDtypeStruct((S, D), F32),
                            jax.ShapeDtypeStruct((S, D), BF16)],
                 args=[*os_, *lses, qz], sem=("parallel",), jobs=jobs)


def merge_bwd(dy2, o, qz, name, jobs=()):
    S, D = o.shape
    H = D // HEAD_DIM
    T = _tile(S, 256, 8)
    nq = N_GROUPS + 1

    def body(dy_ref, o_ref, z_ref, do_ref, dl_ref, dqz_ref):
        dy = dy_ref[...]
        ov = o_ref[...]
        z = z_ref[...]
        sz = _sigmoid(z)
        do = dy * (z * sz)
        do_ref[...] = do.astype(BF16)
        dqz_ref[...] = (dy * ov * (sz * (1.0 + z * (1.0 - sz)))).astype(BF16)
        prod = do * ov
        for h in range(H):
            hs = slice(h * HEAD_DIM, (h + 1) * HEAD_DIM)
            dl_ref[:, hs] = jnp.broadcast_to(jnp.sum(prod[:, hs], axis=-1, keepdims=True), (T, HEAD_DIM))

    row = pl.BlockSpec((T, D), lambda i: (i, 0))
    last = pl.BlockSpec((T, D), lambda i: (i, N_GROUPS))
    return _call(body, name=name, grid=(S // T,), in_specs=[row, row, last], out_specs=[row, row, last],
                 out_shape=[jax.ShapeDtypeStruct((S, D), BF16), jax.ShapeDtypeStruct((S, D), F32),
                            jax.ShapeDtypeStruct((S, nq * D), BF16)],
                 args=[dy2, o, qz], sem=("parallel",), jobs=jobs)


def attn_bwd(qz, kv, do, lse, delta, bias, dqz, dkv, g, dil, name, jobs=()):
    S = qz.shape[0]
    D = qz.shape[1] // (N_GROUPS + 1)
    H = D // HEAD_DIM
    halo, rows, nsb, nblk = _attn_geometry(S, dil)
    sub = nblk * BLOCK
    scale = HEAD_DIM ** -0.5
    have_dkv = dkv is not None

    def body(*refs):
        q_ref, do_ref, l_ref, d_ref, kvc_ref, kvp_ref, b_ref = refs[:7]
        n_in = 7 + 1 + (1 if have_dkv else 0)
        (dq_ref, dkv_ref, ds_ref, ks, vs, dks, dvs, dos, dqs, carry_k, carry_v,
         qd, dod, ld, dd, dqd, kd, vd, dkd, dvd, tmp) = refs[n_in:]
        i = pl.program_id(1)
        sb = nsb - 1 - i

        @pl.when(i == 0)
        def _():
            ds_ref[...] = jnp.zeros_like(ds_ref)

        ks[0:halo, :] = kvp_ref[0].astype(F32)
        ks[halo:, :] = kvc_ref[0].astype(F32)
        vs[0:halo, :] = kvp_ref[1].astype(F32)
        vs[halo:, :] = kvc_ref[1].astype(F32)
        dos[...] = do_ref[...].astype(F32)
        _split(kd, ks, sub + BLOCK, dil, tmp)
        _split(vd, vs, sub + BLOCK, dil, tmp)
        for dst, src in ((qd, q_ref), (dod, dos), (ld, l_ref), (dd, d_ref)):
            _split(dst, src, sub, dil, tmp)
        dkd[...] = jnp.zeros_like(dkd)
        dvd[...] = jnp.zeros_like(dvd)
        band, first = _band_masks(sb > 0)
        bias_t = b_ref[0]
        for r in range(dil):
            for jj in range(nblk):
                own = slice(jj * BLOCK, (jj + 1) * BLOCK)
                keys = slice(jj * BLOCK, (jj + 2) * BLOCK)
                q = qd[r, own, :]
                do = dod[r, own, :]
                k = kd[r, keys, :]
                v = vd[r, keys, :]
                lse = ld[r, own, :]
                dlt = dd[r, own, :]
                s = _dot_nt(q, k) * scale + bias_t - jnp.concatenate([lse, lse], axis=-1)
                p = jnp.where(band if jj > 0 else first, jnp.exp(s), 0.0)
                ds = p * (_dot_nt(do, v) - jnp.concatenate([dlt, dlt], axis=-1))
                ds_ref[0] += ds
                dsb = ds.astype(BF16)
                dqd[r, own, :] = jnp.dot(dsb, k, preferred_element_type=F32) * scale
                dkd[r, keys, :] += _dot_tn(dsb, q) * scale
                dvd[r, keys, :] += _dot_tn(p.astype(BF16), do)
        _merge(dqs, dqd, sub, dil, tmp)
        _merge(dks, dkd, sub + BLOCK, dil, tmp)
        _merge(dvs, dvd, sub + BLOCK, dil, tmp)

        @pl.when(i > 0)
        def _():
            dks[rows:rows + halo, :] += carry_k[...]
            dvs[rows:rows + halo, :] += carry_v[...]

        dkv_ref[0] = dks[halo:, :].astype(BF16)
        dkv_ref[1] = dvs[halo:, :].astype(BF16)
        carry_k[...] = dks[0:halo, :]
        carry_v[...] = dvs[0:halo, :]
        dq_ref[...] = dqs[...].astype(BF16)

    per = rows // halo

    def rev(i):
        return nsb - 1 - i

    bias_spec = pl.BlockSpec((1, BLOCK, 2 * BLOCK), lambda h, i: (h, 0, 0))
    row_spec = pl.BlockSpec((rows, HEAD_DIM), lambda h, i: (rev(i), h))
    in_specs = [pl.BlockSpec((rows, HEAD_DIM), lambda h, i: (rev(i), g * H + h)),
                row_spec, row_spec, row_spec,
                pl.BlockSpec((2, rows, HEAD_DIM), lambda h, i: (0, rev(i), g * H + h)),
                pl.BlockSpec((2, halo, HEAD_DIM),
                             lambda h, i: (0, jnp.maximum(rev(i) * per - 1, 0), g * H + h)),
                bias_spec, ANY_SPEC]
    args = [qz, do, lse, delta, kv, kv, bias, dqz]
    aliases = {7: 0}
    if have_dkv:
        in_specs.append(ANY_SPEC)
        args.append(dkv)
        aliases[8] = 1
    blk = (halo + rows, HEAD_DIM)
    own, keys = (dil, sub, HEAD_DIM), (dil, sub + BLOCK, HEAD_DIM)
    return _call(body, name=name, grid=(H, nsb), in_specs=in_specs,
                 out_specs=[pl.BlockSpec((rows, HEAD_DIM), lambda h, i: (rev(i), g * H + h)),
                            pl.BlockSpec((2, rows, HEAD_DIM), lambda h, i: (0, rev(i), g * H + h)),
                            bias_spec],
                 out_shape=[jax.ShapeDtypeStruct(qz.shape, BF16),
                            jax.ShapeDtypeStruct((2, S, N_GROUPS * D), BF16),
                            jax.ShapeDtypeStruct((H, BLOCK, 2 * BLOCK), F32)],
                 scratch_shapes=[pltpu.VMEM(blk, F32), pltpu.VMEM(blk, F32), pltpu.VMEM(blk, F32),
                                 pltpu.VMEM(blk, F32),
                                 pltpu.VMEM((rows, HEAD_DIM), F32), pltpu.VMEM((rows, HEAD_DIM), F32),
                                 pltpu.VMEM((halo, HEAD_DIM), F32), pltpu.VMEM((halo, HEAD_DIM), F32),
                                 pltpu.VMEM(own, BF16), pltpu.VMEM(own, BF16), pltpu.VMEM(own, F32),
                                 pltpu.VMEM(own, F32), pltpu.VMEM(own, F32),
                                 pltpu.VMEM(keys, BF16), pltpu.VMEM(keys, BF16),
                                 pltpu.VMEM(keys, F32), pltpu.VMEM(keys, F32), _split_tmp(sub, dil)],
                 aliases=aliases, args=args, sem=("parallel", "arbitrary"), jobs=jobs)


def _adamw(w, g, m, v):
    m = ADAM_B1 * m + (1.0 - ADAM_B1) * g
    v = ADAM_B2 * v + (1.0 - ADAM_B2) * (g * g)
    m_hat = m / (1.0 - ADAM_B1 ** ADAM_STEP)
    v_hat = v / (1.0 - ADAM_B2 ** ADAM_STEP)
    delta = -ADAM_LR * (m_hat / (jnp.sqrt(v_hat) + ADAM_EPS) + ADAM_WD * w)
    return delta, m, v


def add_pairs(dw, got, name, jobs=()):
    _, K, n = got.shape
    tk = _tile(K, 512, 16)

    def own_block(r, i, p):
        return 4 * ((p[0] + r // 2) % 2) + 2 * ((p[1] + r % 2) % 2) + p[2], i, 0

    def body(p_ref, a_ref, b_ref, o_ref):
        o_ref[...] = (a_ref[...].astype(F32) + b_ref[...].astype(F32)).astype(o_ref.dtype)

    blk = pl.BlockSpec((None, tk, n), lambda r, i, p: (r, i, 0))
    grid_spec = pltpu.PrefetchScalarGridSpec(
        num_scalar_prefetch=1, grid=(N_CHIPS, K // tk),
        in_specs=[pl.BlockSpec((None, tk, n), own_block), blk], out_specs=blk)
    return pl.pallas_call(body, name=name, grid_spec=grid_spec,
                          out_shape=jax.ShapeDtypeStruct(got.shape, got.dtype))(_place_vector(), dw, got)


def adamw_reduce(pieces, w, m, v, name, jobs=()):
    K, n = w.shape
    kp = K // len(pieces)
    tk = _tile(kp, 256, 8)
    sp = kp // tk

    def body(*refs):
        w_ref, m_ref, v_ref, g_ref, d_ref, nm_ref, nv_ref = refs[2 * len(pieces):]
        i = pl.program_id(0)
        g = None
        for q in range(len(pieces)):
            p_ref, r_ref = refs[2 * q], refs[2 * q + 1]
            gq = p_ref[...].astype(F32)
            for r in range(N_CHIPS - 1):
                gq = gq + r_ref[r].astype(F32)
            g = gq if g is None else jnp.where(i >= q * sp, gq, g)
        d, nm, nv = _adamw(w_ref[...], g, m_ref[...], v_ref[...])
        g_ref[...] = g
        d_ref[...] = d
        nm_ref[...] = nm
        nv_ref[...] = nv

    def piece_specs(q):
        def at(i):
            return jnp.clip(i - q * sp, 0, sp - 1)
        return [pl.BlockSpec((None, tk, n), lambda i: (0, at(i), 0)),
                pl.BlockSpec((N_CHIPS - 1, tk, n), lambda i: (0, at(i), 0))]

    blk = pl.BlockSpec((tk, n), lambda i: (i, 0))
    in_specs, args = [], []
    for q, (part, got) in enumerate(pieces):
        in_specs += piece_specs(q)
        args += [part, got]
    return _call(body, name=name, grid=(K // tk,), in_specs=in_specs + [blk, blk, blk],
                 out_specs=[blk] * 4, out_shape=[jax.ShapeDtypeStruct((K, n), F32)] * 4,
                 args=args + [w, m, v], sem=("parallel",), jobs=jobs)


def sum_parts(parts, name):
    _, R, D = parts.shape

    def body(p_ref, o_ref):
        g = p_ref[0]
        for r in range(1, N_DEV):
            g = g + p_ref[r]
        o_ref[...] = g

    return _call(body, name=name, in_specs=[VMEM_SPEC], out_specs=[VMEM_SPEC],
                 out_shape=[jax.ShapeDtypeStruct((R, D), F32)], args=[parts])[0]


def adamw_small(w, g, m, v, name):
    def body(w_ref, g_ref, m_ref, v_ref, d_ref, nm_ref, nv_ref):
        d, nm, nv = _adamw(w_ref[...], g_ref[...], m_ref[...], v_ref[...])
        d_ref[...] = d
        nm_ref[...] = nm
        nv_ref[...] = nv

    return _call(body, name=name, in_specs=[VMEM_SPEC] * 4, out_specs=[VMEM_SPEC] * 3,
                 out_shape=[jax.ShapeDtypeStruct(w.shape, F32)] * 3, args=[w, g, m, v])


def _kv_row_pieces(K):
    a, b = (K * 5 // 16) // 16 * 16, (K * 3 // 4) // 16 * 16
    return (0, a), (a, b), (b, K)


def _row(v, at):
    return jnp.pad(v.reshape(1, -1), ((at, 7 - at), (0, 0)))


def _pack_sharded(norm, conv_w, conv_b, ln_g, ln_b):
    n = norm.shape[-1]
    taps = jnp.pad(conv_w.reshape(CONV_WIDTH, n), ((0, HALO - CONV_WIDTH), (0, 0)))
    return jnp.concatenate([_row(norm, 0), _row(ln_g, 0) + _row(ln_b, 1) + _row(conv_b, 2), taps], axis=0)


def _pack_rel(rel_bias, D):
    return jnp.pad(rel_bias.reshape(1, -1), ((0, 7), (0, D - rel_bias.size)))


def _pack_replicated(kv_norm, b_norm, final_norm, rel_bias, D):
    return jnp.concatenate([_row(kv_norm, 0) + _row(b_norm, 1), _row(final_norm, 0), _pack_rel(rel_bias, D)], axis=0)


def kernel(x, a_norm, a_w_in, a_conv_w, a_conv_b, a_ln_g, a_ln_b, a_w_out, kv_norm, w_kv, b_norm, b_w_in, b_w_out, rel_bias, final_norm, loss_target, m_a_norm, m_a_w_in, m_a_conv_w, m_a_conv_b, m_a_ln_g, m_a_ln_b, m_a_w_out, m_kv_norm, m_w_kv, m_b_norm, m_b_w_in, m_b_w_out, m_rel_bias, m_final_norm, v_a_norm, v_a_w_in, v_a_conv_w, v_a_conv_b, v_a_ln_g, v_a_ln_b, v_a_w_out, v_kv_norm, v_w_kv, v_b_norm, v_b_w_in, v_b_w_out, v_rel_bias, v_final_norm):
    _, S, D = x.shape
    E = D
    H = D // HEAD_DIM
    nsh = D // N_DEV
    x0 = x.reshape(S, D)
    target = loss_target.reshape(S, D)
    kvn, bn, fn = kv_norm.reshape(1, D), b_norm.reshape(1, D), final_norm.reshape(1, D)

    names = ["a_w_in", "a_w_out", "w_kv", "b_w_in", "b_w_out"]
    big_w = dict(zip(names, [a_w_in[0], a_w_out[0], w_kv, b_w_in[0], b_w_out[0]]))
    big_m = dict(zip(names, [m_a_w_in[0], m_a_w_out[0], m_w_kv, m_b_w_in[0], m_b_w_out[0]]))
    big_v = dict(zip(names, [v_a_w_in[0], v_a_w_out[0], v_w_kv, v_b_w_in[0], v_b_w_out[0]]))
    sh_w = _pack_sharded(a_norm, a_conv_w, a_conv_b, a_ln_g, a_ln_b)

    wa_in, sh_all = all_gather([big_w["a_w_in"].astype(BF16), sh_w], "gather_first")
    sh_full = sh_all.transpose(1, 0, 2).reshape(SMALL_SH_ROWS, D)
    an, cw, cb = sh_full[0:1], sh_full[16:16 + HALO], sh_full[10:11]
    lg, lb = sh_full[8:9], sh_full[9:10]

    buf = {nm: place_shard(big_w[nm], "place_" + nm) for nm in names[1:]}
    kv_cut = _kv_row_pieces(D)

    riding = []

    def ride(nm, **kw):
        riding.append((nm, gather_job(buf[nm], **kw)))
        return riding[-1][1]

    def landed():
        while riding:
            nm, job = riding.pop()
            buf[nm] = job.results[0]

    (h0,) = rms_fwd(x0, [an], "rms_a")
    proj = matmul_nn(h0, wa_in, "a_in",
                     jobs=[ride("a_w_out", chips=(0, E // N_DEV)), ride("w_kv", chips=kv_cut[0])])
    landed()
    c, y = conf_fwd(proj, cw, cb, lg, lb, "conf_fwd",
                    jobs=[ride("a_w_out", sibling=(0, E // N_DEV)),
                          ride("w_kv", sibling=kv_cut[0], chips=kv_cut[1])])
    landed()
    wa_out = buf["a_w_out"].reshape(1, E, D)
    x1 = matmul_nn(y, wa_out, "a_out", res=x0, jobs=[ride("w_kv", sibling=kv_cut[1], chips=kv_cut[2])])
    landed()
    hk, hb = rms_fwd(x1, [kvn, bn], "rms_b", jobs=[ride("w_kv", sibling=kv_cut[2])])
    landed()
    wkv = buf["w_kv"]
    kv = matmul_nn(hk, wkv, "kv_proj", out_dtype=BF16, kv_split=True, jobs=[ride("b_w_in", chips=(0, D))])
    landed()

    bt = jnp.asarray(bucket_tables())
    onehot = (bt[None, :] == jnp.arange(N_BUCKETS, dtype=jnp.int32)[:, None]).astype(F32)
    bias = small_dot(rel_bias.T, onehot, "nn", "bias_table", jobs=[ride("b_w_in", sibling=(0, D))])
    landed()
    bias = bias.reshape(H, N_GROUPS, BLOCK, 2 * BLOCK)
    bias = [bias[:, g] for g in range(N_GROUPS)]
    wb_in = buf["b_w_in"]
    qz = matmul_nn(hb, wb_in, "b_in", jobs=[ride("b_w_out", chips=(0, D // N_DEV))])
    landed()

    os_, lses = [], []
    for g, (_, dil) in enumerate(GROUPS):
        o_g, l_g = attn_fwd(qz, kv, bias[g], g, dil, "attn_fwd%d" % g,
                            jobs=[ride("b_w_out", sibling=(0, D // N_DEV))] if g == 0 else ())
        landed()
        os_.append(o_g)
        lses.append(l_g)
    wb_out = buf["b_w_out"].reshape(1, D, D)
    o, lse, y2 = merge_fwd(os_, lses, qz, "merge_fwd")
    x2 = matmul_nn(y2, wb_out, "b_out", res=x1)
    dx2, dx2b, fin_acc = final_loss(x2, fn, target, "final_loss")

    dy2 = matmul_nt(dx2b, wb_out, "b_out_dx")
    dwb_out = matmul_tn(y2, dx2b, 1, "b_out_dw").reshape(N_DEV, D // N_DEV, D)
    r_b_out = reduce_sibling_job(dwb_out)
    do, delta, dqz = merge_bwd(dy2, o, qz, "merge_bwd", jobs=[r_b_out])
    p_b_out = add_pairs(dwb_out, r_b_out.results[0], "pairs_b_out")
    r_b_out2 = reduce_chips_job(p_b_out)
    dkv = None
    ds_tabs = []
    for g, (_, dil) in enumerate(GROUPS):
        dqz, dkv, ds_tab = attn_bwd(qz, kv, do, lse, delta, bias[g], dqz, dkv, g, dil,
                                    "attn_bwd%d" % g, jobs=[r_b_out2] if g == 0 else ())
        ds_tabs.append(ds_tab.reshape(H, 2 * BLOCK * BLOCK))
    d_rel = small_dot(onehot, jnp.concatenate(ds_tabs, axis=1), "nt", "bias_grad")
    dw_kv_lo = matmul_tn(hk, dkv, N_DEV, "kv_dw_lo", kv_split=True, k_tiles=(0, 1))
    r_kv_lo = reduce_sibling_job(dw_kv_lo)
    dw_kv_hi = matmul_tn(hk, dkv, N_DEV, "kv_dw_hi", kv_split=True, k_tiles=(1, 2), jobs=[r_kv_lo])
    p_kv_lo = add_pairs(dw_kv_lo, r_kv_lo.results[0], "pairs_kv_lo")
    r_kv_lo2, r_kv_hi = reduce_chips_job(p_kv_lo), reduce_sibling_job(dw_kv_hi)
    dhk = matmul_nt(dkv, wkv, "kv_dx", kv_split=True, jobs=[r_kv_lo2, r_kv_hi])
    p_kv_hi = add_pairs(dw_kv_hi, r_kv_hi.results[0], "pairs_kv_hi")
    r_kv_hi2 = reduce_chips_job(p_kv_hi)
    dwb_in = matmul_tn(hb, dqz, N_DEV, "b_in_dw", jobs=[r_kv_hi2])
    r_b_in = reduce_sibling_job(dwb_in)
    dhb = matmul_nt(dqz, wb_in, "b_in_dx", jobs=[r_b_in])
    p_b_in = add_pairs(dwb_in, r_b_in.results[0], "pairs_b_in")
    r_b_in2 = reduce_chips_job(p_b_in)
    dx1, dx1b, norm_acc = rms_bwd(x1, [dhk, dhb], [kvn, bn], dx2, "rms_b_bwd", True)

    dy = matmul_nt(dx1b, wa_out, "a_out_dx")
    dwa_out = matmul_tn(y, dx1b, 1, "a_out_dw").reshape(N_DEV, E // N_DEV, D)
    r_a_out = reduce_sibling_job(dwa_out)
    dc, dz, ln_acc = conf_bwd_ln(c, dy, proj, lg, lb, "conf_bwd_ln", jobs=[r_a_out])
    p_a_out = add_pairs(dwa_out, r_a_out.results[0], "pairs_a_out")
    r_a_out2 = reduce_chips_job(p_a_out)
    dproj, dcw = conf_bwd_conv(proj, dc, dz, cw, "conf_bwd_conv", jobs=[r_b_in2, r_a_out2])
    share = share_small_job(jnp.concatenate([ln_acc, dcw, norm_acc, fin_acc, _pack_rel(d_rel, D)], axis=0))
    dwa_lo = matmul_tn(h0, dproj, N_DEV, "a_in_dw_lo", k_tiles=(0, 1), jobs=[share])
    r_lo = reduce_sibling_job(dwa_lo)
    dwa_hi = matmul_tn(h0, dproj, N_DEV, "a_in_dw_hi", k_tiles=(1, 2), jobs=[r_lo])
    p_lo = add_pairs(dwa_lo, r_lo.results[0], "pairs_a_in_lo")
    r_lo2, r_hi = reduce_chips_job(p_lo), reduce_sibling_job(dwa_hi)
    dh0 = matmul_nt(dproj, wa_in, "a_in_dx", jobs=[r_lo2, r_hi])
    p_hi = add_pairs(dwa_hi, r_hi.results[0], "pairs_a_in_hi")
    r_hi2 = reduce_chips_job(p_hi)
    grad_x, a_acc = rms_bwd(x0, [dh0], [an], dx1, "rms_a_bwd", False, jobs=[r_hi2])

    share_a = share_small_job(a_acc)
    sums = {"w_kv": [(p_kv_lo, r_kv_lo2), (p_kv_hi, r_kv_hi2)], "b_w_in": [(p_b_in, r_b_in2)], "b_w_out": [(p_b_out, r_b_out2)],
            "a_w_out": [(p_a_out, r_a_out2)], "a_w_in": [(p_lo, r_lo2), (p_hi, r_hi2)]}
    big_out = {}
    for nm, pieces in sums.items():
        big_out[nm] = adamw_reduce([(part, job.results[0]) for part, job in pieces],
                                   big_w[nm], big_m[nm], big_v[nm], "adamw_" + nm,
                                   jobs=[share_a] if nm == "a_w_out" else ())

    gsum = jnp.concatenate([sum_parts(share_a.results[0], "sum_small_a"),
                            sum_parts(share.results[0], "sum_small")], axis=0)
    me = 4 * lax.axis_index("x") + 2 * lax.axis_index("y") + lax.axis_index("c")
    g_sh = lax.dynamic_slice(gsum, (0, me * nsh), (SMALL_SH_ROWS, nsh))
    g_rep = gsum[SMALL_SH_ROWS:]
    loss = g_rep[9, 0]
    sh_m = _pack_sharded(m_a_norm, m_a_conv_w, m_a_conv_b, m_a_ln_g, m_a_ln_b)
    sh_v = _pack_sharded(v_a_norm, v_a_conv_w, v_a_conv_b, v_a_ln_g, v_a_ln_b)
    sh_d, sh_nm, sh_nv = adamw_small(sh_w, g_sh, sh_m, sh_v, "adamw_sharded")
    rep_w = _pack_replicated(kv_norm, b_norm, final_norm, rel_bias, D)
    rep_m = _pack_replicated(m_kv_norm, m_b_norm, m_final_norm, m_rel_bias, D)
    rep_v = _pack_replicated(v_kv_norm, v_b_norm, v_final_norm, v_rel_bias, D)
    rep_d, rep_nm, rep_nv = adamw_small(rep_w, g_rep, rep_m, rep_v, "adamw_replicated")

    def unpack(kind):
        sh = (g_sh, sh_d, sh_nm, sh_nv)[kind]
        rep = (g_rep, rep_d, rep_nm, rep_nv)[kind]
        big = {nm: big_out[nm][kind] for nm in names}
        nb = rel_bias.size
        return [
            sh[0:1],
            big["a_w_in"][None],
            sh[16:16 + CONV_WIDTH][None],
            sh[10:11], sh[8:9], sh[9:10],
            big["a_w_out"][None],
            rep[0],
            big["w_kv"],
            rep[1:2],
            big["b_w_in"][None],
            big["b_w_out"][None],
            rep[16, :nb].reshape(rel_bias.shape),
            rep[8],
        ]

    return (loss, grad_x.reshape(1, S, D), *unpack(0), *unpack(1), *unpack(2), *unpack(3))
```

```python
import functools
import math

import numpy as np
import jax
import jax.numpy as jnp
from jax import lax
from jax.experimental import pallas as pl
from jax.experimental.pallas import tpu as pltpu

F32 = jnp.float32
BF16 = jnp.bfloat16

N_DEV = 8
N_CHIPS = 4
EPS = 1e-6
HEAD_DIM = 128
BLOCK = 128
GROUPS = ((128, 1), (512, 4), (2048, 16))
N_GROUPS = len(GROUPS)
CONV_WIDTH = 31
HALO = 32
N_BUCKETS = 32
MAX_EXACT = N_BUCKETS // 2
MAX_DISTANCE = 2048
V7X_VMEM_BYTES = 64 * 1024 * 1024
VMEM_LIMIT = (V7X_VMEM_BYTES * 7) // 8
MATMUL_VMEM_BUDGET = (V7X_VMEM_BYTES * 11) // 16
LANE = 128

ADAM_LR = 0.001
ADAM_B1 = 0.9
ADAM_B2 = 0.999
ADAM_EPS = 1e-08
ADAM_WD = 0.01
ADAM_STEP = 10

SMALL_SH_ROWS = 48
SMALL_REP_ROWS = 24
MESH = pl.DeviceIdType.MESH
ANY_SPEC = pl.BlockSpec(memory_space=pl.ANY)
VMEM_SPEC = pl.BlockSpec(memory_space=pltpu.VMEM)


def _tile(dim, pref, unit=LANE):
    if dim <= pref:
        return dim
    t = (pref // unit) * unit
    while dim % t:
        t -= unit
    assert t > 0
    return t


def _sigmoid(v):
    return jax.nn.sigmoid(v)


def _place():
    return lax.axis_index("x"), lax.axis_index("y"), lax.axis_index("c")


def _flip(v, bit):
    return 1 - v if bit else v


class Job:
    def __init__(self, srcs, dsts, n_sems, build):
        self.srcs, self.dsts, self.n_sems, self.build = list(srcs), list(dsts), n_sems, build
        self.results = None


def _remote(src, dst, send_sems, recv_sems, k, peer):
    return pltpu.make_async_remote_copy(src_ref=src, dst_ref=dst, send_sem=send_sems.at[k],
                                        recv_sem=recv_sems.at[k], device_id=peer, device_id_type=MESH)


def _call(body, *, name, in_specs, out_specs, out_shape, args, grid=(), scratch_shapes=(), sem=(),
          aliases=None, jobs=()):
    n_in, n_out, n_scr = len(in_specs), len(out_specs), len(scratch_shapes)
    aliases = dict(aliases or {})
    x_in, x_out, x_scr = [], [], []
    for job in jobs:
        job.in_at = n_in + len(x_in)
        x_in += job.srcs
        job.out_at = n_out + len(x_out)
        for d in job.dsts:
            if not isinstance(d, jax.ShapeDtypeStruct):
                aliases[n_in + len(x_in)] = n_out + len(x_out)
                x_in.append(d)
            x_out.append(jax.ShapeDtypeStruct(d.shape, d.dtype))
        job.scr_at = n_scr + len(x_scr)
        x_scr += [pltpu.SemaphoreType.DMA((job.n_sems,))] * 3

    def wrapped(*refs):
        ins = refs[:n_in + len(x_in)]
        outs = refs[len(ins):len(ins) + n_out + len(x_out)]
        scr = refs[len(ins) + len(outs):]
        core = ins[:n_in] + outs[:n_out] + scr[:n_scr]
        if not jobs:
            body(*core)
            return
        copies = []
        for job in jobs:
            copies += job.build(ins[job.in_at:job.in_at + len(job.srcs)],
                                outs[job.out_at:job.out_at + len(job.dsts)],
                                *scr[job.scr_at:job.scr_at + 3])
        if grid:
            pids = [pl.program_id(d) for d in range(len(grid))]
            first = functools.reduce(jnp.logical_and, [p == 0 for p in pids])
            last = functools.reduce(jnp.logical_and, [p == g - 1 for p, g in zip(pids, grid)])

            @pl.when(first)
            def _():
                for cp in copies:
                    cp.start()

            body(*core)

            @pl.when(last)
            def _():
                for cp in copies:
                    cp.wait()
        else:
            for cp in copies:
                cp.start()
            body(*core)
            for cp in copies:
                cp.wait()

    if jobs:
        sem = ("arbitrary",) * len(grid)
    kwargs = dict(grid=grid) if grid else {}
    if aliases:
        kwargs["input_output_aliases"] = aliases
    outs = pl.pallas_call(
        wrapped, name=name,
        in_specs=list(in_specs) + [ANY_SPEC] * len(x_in),
        out_specs=list(out_specs) + [ANY_SPEC] * len(x_out),
        out_shape=list(out_shape) + x_out,
        scratch_shapes=list(scratch_shapes) + x_scr,
        compiler_params=pltpu.CompilerParams(dimension_semantics=sem if sem else None,
                                             vmem_limit_bytes=VMEM_LIMIT),
        **kwargs,
    )(*args, *x_in)
    for job in jobs:
        job.results = list(outs[job.out_at:job.out_at + len(job.dsts)])
    return list(outs[:n_out])


def place_shard(w, name):
    K, n = w.shape
    tk = _tile(K, 512, 16)

    def body(p_ref, w_ref, o_ref):
        o_ref[...] = w_ref[...].astype(BF16)

    grid_spec = pltpu.PrefetchScalarGridSpec(
        num_scalar_prefetch=1, grid=(K // tk,),
        in_specs=[pl.BlockSpec((tk, n), lambda i, p: (i, 0))],
        out_specs=pl.BlockSpec((None, tk, n), lambda i, p: (4 * p[0] + 2 * p[1] + p[2], i, 0)))
    return pl.pallas_call(body, name=name, grid_spec=grid_spec,
                          out_shape=jax.ShapeDtypeStruct((N_DEV, K, n), BF16))(_place_vector(), w)


def _place_vector():
    return jnp.stack(_place()).astype(jnp.int32)


def gather_job(buf, chips=None, sibling=None):
    def build(srcs, dsts, send, recv, loc):
        (out,) = dsts
        x, y, c = _place()
        copies = []
        if chips is not None:
            mine = out.at[4 * x + 2 * y + c, pl.ds(chips[0], chips[1] - chips[0])]
            peers = [(x, y, 1 - c), (1 - x, y, c), (x, 1 - y, c), (1 - x, 1 - y, c)]
            copies += [_remote(mine, mine, send, recv, k, p) for k, p in enumerate(peers)]
        if sibling is not None:
            for k, (cx, cy) in enumerate([(1 - x, y), (x, 1 - y), (1 - x, 1 - y)]):
                blk = out.at[4 * cx + 2 * cy + c, pl.ds(sibling[0], sibling[1] - sibling[0])]
                copies.append(_remote(blk, blk, send, recv, 4 + k, (x, y, 1 - c)))
        return copies

    return Job([], [buf], 7, build)


def _relay(x, y, c):
    return ((x + 1 - c) % 2, (y + c) % 2), ((x + c) % 2, (y + 1 - c) % 2)


def gather_relay_job(buf, near=None, far=None, last=None):
    def rows(slot, rng):
        return slot.at[pl.ds(rng[0], rng[1] - rng[0])]

    def build(srcs, dsts, send, recv, loc):
        (out,) = dsts
        x, y, c = _place()
        sib = (x, y, 1 - c)
        copies = []
        if near is not None:
            mine = rows(out.at[4 * x + 2 * y + c], near)
            for k, p in enumerate([sib, (1 - x, y, c), (x, 1 - y, c)]):
                copies.append(_remote(mine, mine, send, recv, k, p))
        if far is not None:
            (fx, fy), (tx, ty) = _relay(x, y, c)
            blk = rows(out.at[4 * fx + 2 * fy + c], far)
            copies.append(_remote(blk, blk, send, recv, 3, (tx, ty, c)))
            for k, (cx, cy) in enumerate([(1 - x, y), (x, 1 - y)]):
                blk = rows(out.at[4 * cx + 2 * cy + c], far)
                copies.append(_remote(blk, blk, send, recv, 4 + k, sib))
        if last is not None:
            blk = rows(out.at[4 * (1 - x) + 2 * (1 - y) + c], last)
            copies.append(_remote(blk, blk, send, recv, 6, sib))
        return copies

    return Job([], [buf], 7, build)


def reduce_sibling_job(dw):
    def build(srcs, dsts, send, recv, loc):
        (src,), (got,) = srcs, dsts
        x, y, c = _place()
        return [_remote(src.at[4 * _flip(x, r & 2) + 2 * _flip(y, r & 1) + 1 - c], got.at[r], send, recv, r,
                        (x, y, 1 - c)) for r in range(N_CHIPS)]

    return Job([dw], [jax.ShapeDtypeStruct((N_CHIPS,) + dw.shape[1:], dw.dtype)], N_CHIPS, build)


def reduce_chips_job(part):
    def build(srcs, dsts, send, recv, loc):
        (src,), (got,) = srcs, dsts
        x, y, c = _place()
        return [_remote(src.at[r], got.at[r - 1], send, recv, r - 1, (_flip(x, r & 2), _flip(y, r & 1), c))
                for r in range(1, N_CHIPS)]

    return Job([part], [jax.ShapeDtypeStruct((N_CHIPS - 1,) + part.shape[1:], part.dtype)], N_CHIPS - 1, build)


def share_small_job(small):
    def build(srcs, dsts, send, recv, loc):
        (src,), (out,) = srcs, dsts
        x, y, c = _place()
        mine = out.at[4 * x + 2 * y + c]
        copies = [pltpu.make_async_copy(src, mine, loc.at[0])]
        for rel in range(1, N_DEV):
            peer = (_flip(x, rel & 4), _flip(y, rel & 2), _flip(c, rel & 1))
            copies.append(_remote(src, mine, send, recv, rel - 1, peer))
        return copies

    return Job([small], [jax.ShapeDtypeStruct((N_DEV,) + small.shape, small.dtype)], N_DEV - 1, build)


def all_gather(shards, name):
    n = len(shards)

    def body(*refs):
        ins, outs = refs[:n], refs[n:2 * n]
        send_sems, recv_sems, local_sems = refs[2 * n:]
        x, y, c = _place()
        me, sibling = (x, y, c), (x, y, 1 - c)
        near = [(1 - x, y), (x, 1 - y)]
        diag = (1 - x, 1 - y)
        frm, to = _relay(x, y, c)

        def slot(a, dev):
            return outs[a].at[4 * dev[0] + 2 * dev[1] + dev[2]]

        def copy(a, k, block, to_dev, src=None):
            return pltpu.make_async_remote_copy(
                src_ref=slot(a, block) if src is None else src, dst_ref=slot(a, block),
                send_sem=send_sems.at[a, k], recv_sem=recv_sems.at[a, k],
                device_id=to_dev, device_id_type=MESH)

        mine, sent = [], []
        for a in range(n):
            mine.append(pltpu.make_async_copy(ins[a], slot(a, me), local_sems.at[a]))
            mine[a].start()
            sent.append([copy(a, 0, me, sibling, src=ins[a])]
                        + [copy(a, 1 + j, me, (*chip, c), src=ins[a]) for j, chip in enumerate(near)])
            for cp in sent[a]:
                cp.start()
        for a in range(n):
            for j, chip in enumerate(near):
                copy(a, 1 + j, (*chip, c), me).wait_recv()
            more = [copy(a, 3, (*frm, c), (*to, c))] + [copy(a, 4 + j, (*chip, c), sibling)
                                                       for j, chip in enumerate(near)]
            for cp in more:
                cp.start()
            sent[a] += more
        for a in range(n):
            copy(a, 3, (*diag, c), me).wait_recv()
            sent[a].append(copy(a, 6, (*diag, c), sibling))
            sent[a][-1].start()
        for a in range(n):
            for k, block in [(0, sibling), (4, (*near[0], 1 - c)), (5, (*near[1], 1 - c)), (6, (*diag, 1 - c))]:
                copy(a, k, block, me).wait_recv()
            for cp in sent[a]:
                cp.wait_send()
            mine[a].wait()

    return pl.pallas_call(
        body, name=name,
        in_specs=[ANY_SPEC] * n, out_specs=[ANY_SPEC] * n,
        out_shape=[jax.ShapeDtypeStruct((N_DEV,) + s.shape, s.dtype) for s in shards],
        scratch_shapes=[pltpu.SemaphoreType.DMA((n, 7)), pltpu.SemaphoreType.DMA((n, 7)),
                        pltpu.SemaphoreType.DMA((n,))],
    )(*shards)


def _kv_split_index(tw, D):
    pd = D // tw

    def index(j):
        return (j // pd) % 2, (j // (2 * pd)) * pd + j % pd

    return index


def _col_tile(n, also, pref):
    t = (min(pref, n) // LANE) * LANE
    while n % t or (also is not None and also % t):
        t -= LANE
    assert t > 0
    return t


def matmul_nn(a, w, name, res=None, out_dtype=F32, kv_split=False, jobs=()):
    M, K = a.shape
    nb, _, n = w.shape
    D = nb * n // (2 * N_GROUPS)
    tn = _col_tile(n, D if kv_split else None, 1024)
    out_bytes = jnp.dtype(out_dtype).itemsize + (4 if res is not None else 0)
    tm = _tile(M, 2048)
    if 2 * (tm * K * 2 + K * tn * 2 + tm * tn * out_bytes) > MATMUL_VMEM_BUDGET:
        tm = _tile(M, 1024)
    per = n // tn

    def body(*refs):
        if res is None:
            a_ref, w_ref, o_ref = refs
        else:
            a_ref, w_ref, r_ref, o_ref = refs
        acc = jnp.dot(a_ref[...], w_ref[...], preferred_element_type=F32)
        if res is not None:
            acc = r_ref[...] + acc
        o_ref[...] = acc.astype(o_ref.dtype)

    in_specs = [pl.BlockSpec((tm, K), lambda i, j: (i, 0)),
                pl.BlockSpec((None, K, tn), lambda i, j: (j // per, 0, j % per))]
    args = [a, w]
    if res is not None:
        in_specs.append(pl.BlockSpec((tm, tn), lambda i, j: (i, j)))
        args.append(res)
    if kv_split:
        split = _kv_split_index(tn, D)
        out_spec = pl.BlockSpec((None, tm, tn), lambda i, j: (split(j)[0], i, split(j)[1]))
        out_shape = jax.ShapeDtypeStruct((2, M, N_GROUPS * D), out_dtype)
    else:
        out_spec = pl.BlockSpec((tm, tn), lambda i, j: (i, j))
        out_shape = jax.ShapeDtypeStruct((M, nb * n), out_dtype)
    return _call(body, name=name, grid=(M // tm, nb * per), in_specs=in_specs, out_specs=[out_spec],
                 out_shape=[out_shape], args=args, sem=("parallel", "parallel"), jobs=jobs)[0]


def matmul_nt(dy, w, name, kv_split=False, jobs=()):
    M = dy.shape[-2]
    nb, K, n = w.shape
    D = nb * n // (2 * N_GROUPS)
    tm = _tile(M, 1024)
    tc = _col_tile(n, D if kv_split else None, 1024)
    per = n // tc
    pair = max(u for u in (1, 2, 4) if (nb * per) % u == 0 and u * tc <= 2048)

    def body(*refs):
        o_ref = refs[-1]
        j = pl.program_id(1)
        part = None
        for u in range(pair):
            d = lax.dot_general(refs[u][...], refs[pair + u][...], (((1,), (1,)), ((), ())),
                                preferred_element_type=F32)
            part = d if part is None else part + d

        @pl.when(j == 0)
        def _():
            o_ref[...] = part

        @pl.when(j > 0)
        def _():
            o_ref[...] += part

    def dy_spec(u):
        if kv_split:
            split = _kv_split_index(tc, D)
            return pl.BlockSpec((None, tm, tc),
                                lambda i, j: (split(pair * j + u)[0], i, split(pair * j + u)[1]))
        return pl.BlockSpec((tm, tc), lambda i, j: (i, pair * j + u))

    def w_spec(u):
        return pl.BlockSpec((None, K, tc), lambda i, j: ((pair * j + u) // per, 0, (pair * j + u) % per))

    return _call(body, name=name, grid=(M // tm, nb * per // pair),
                 in_specs=[dy_spec(u) for u in range(pair)] + [w_spec(u) for u in range(pair)],
                 out_specs=[pl.BlockSpec((tm, K), lambda i, j: (i, 0))],
                 out_shape=[jax.ShapeDtypeStruct((M, K), F32)], args=[dy] * pair + [w] * pair,
                 sem=("parallel", "arbitrary"), jobs=jobs)[0]


def matmul_tn(a, dy, nb, name, out_dtype=BF16, kv_split=False, k_tiles=None, jobs=()):
    M, K = a.shape
    N = 2 * dy.shape[-1] if kv_split else dy.shape[-1]
    n = N // nb
    D = N // (2 * N_GROUPS)
    tn = _col_tile(n, D if kv_split else None, 1024)
    per = n // tn
    tk = _tile(K, 1024)

    def body(a_ref, dy_ref, o_ref):
        o_ref[...] = lax.dot_general(a_ref[...], dy_ref[...], (((0,), (0,)), ((), ())),
                                     preferred_element_type=F32).astype(o_ref.dtype)

    if kv_split:
        split = _kv_split_index(tn, D)
        dy_spec = pl.BlockSpec((None, M, tn), lambda k, j: (split(j)[0], 0, split(j)[1]))
    else:
        dy_spec = pl.BlockSpec((M, tn), lambda k, j: (0, j))
    k0, k1 = k_tiles or (0, K // tk)
    return _call(body, name=name, grid=(k1 - k0, nb * per),
                 in_specs=[pl.BlockSpec((M, tk), lambda k, j: (0, k0 + k)), dy_spec],
                 out_specs=[pl.BlockSpec((None, tk, tn), lambda k, j: (j // per, k, j % per))],
                 out_shape=[jax.ShapeDtypeStruct((nb, (k1 - k0) * tk, n), out_dtype)], args=[a, dy],
                 sem=("parallel", "parallel"), jobs=jobs)[0]


def small_dot(a, b, contract, name, jobs=()):
    if contract == "nn":
        dims = (((1,), (0,)), ((), ()))
        out = (a.shape[0], b.shape[1])
    else:
        dims = (((1,), (1,)), ((), ()))
        out = (a.shape[0], b.shape[0])

    def body(a_ref, b_ref, o_ref):
        o_ref[...] = lax.dot_general(a_ref[...], b_ref[...], dims, precision=lax.Precision.HIGHEST,
                                     preferred_element_type=F32)

    return _call(body, name=name, in_specs=[VMEM_SPEC, VMEM_SPEC], out_specs=[VMEM_SPEC],
                 out_shape=[jax.ShapeDtypeStruct(out, F32)], args=[a, b], jobs=jobs)[0]


def rms_fwd(x, gains, name, jobs=()):
    S, D = x.shape
    T = _tile(S, 512, 8)
    n = len(gains)

    def body(x_ref, *refs):
        xv = x_ref[...]
        xn = xv * lax.rsqrt(jnp.mean(xv * xv, axis=-1, keepdims=True) + EPS)
        for g_ref, o_ref in zip(refs[:n], refs[n:]):
            o_ref[...] = (xn * g_ref[...]).astype(o_ref.dtype)

    row = pl.BlockSpec((T, D), lambda i: (i, 0))
    vec = pl.BlockSpec((1, D), lambda i: (0, 0))
    return _call(body, name=name, grid=(S // T,), in_specs=[row] + [vec] * n, out_specs=[row] * n,
                 out_shape=[jax.ShapeDtypeStruct((S, D), BF16)] * n, args=[x, *gains],
                 sem=("parallel",), jobs=jobs)


def rms_bwd(x, dhs, gains, dres, name, want_bf16, jobs=()):
    S, D = x.shape
    T = _tile(S, 256, 8)
    n = len(gains)

    def body(x_ref, *refs):
        dh_refs = refs[:n]
        g_refs = refs[n:2 * n]
        dres_ref = refs[2 * n]
        outs = refs[2 * n + 1:]
        dx_ref, dg_ref = outs[0], outs[-1]
        i = pl.program_id(0)

        @pl.when(i == 0)
        def _():
            dg_ref[...] = jnp.zeros_like(dg_ref)

        xv = x_ref[...]
        r = lax.rsqrt(jnp.mean(xv * xv, axis=-1, keepdims=True) + EPS)
        xn = xv * r
        dxn = jnp.zeros_like(xv)
        for k in range(n):
            dh = dh_refs[k][...]
            dg_ref[k:k + 1, :] += jnp.sum(dh * xn, axis=0, keepdims=True)
            dxn = dxn + dh * g_refs[k][...]
        dx = dres_ref[...] + r * (dxn - xn * jnp.mean(dxn * xn, axis=-1, keepdims=True))
        dx_ref[...] = dx
        if want_bf16:
            outs[1][...] = dx.astype(BF16)

    row = pl.BlockSpec((T, D), lambda i: (i, 0))
    vec = pl.BlockSpec((1, D), lambda i: (0, 0))
    acc = pl.BlockSpec((8, D), lambda i: (0, 0))
    out_specs = [row] + ([row] if want_bf16 else []) + [acc]
    out_shape = ([jax.ShapeDtypeStruct((S, D), F32)]
                 + ([jax.ShapeDtypeStruct((S, D), BF16)] if want_bf16 else [])
                 + [jax.ShapeDtypeStruct((8, D), F32)])
    return _call(body, name=name, grid=(S // T,), in_specs=[row] + [row] * n + [vec] * n + [row],
                 out_specs=out_specs, out_shape=out_shape, args=[x, *dhs, *gains, dres],
                 sem=("arbitrary",), jobs=jobs)


def final_loss(x2, gain, target, name, jobs=()):
    S, D = x2.shape
    T = _tile(S, 256, 8)

    def body(x_ref, g_ref, t_ref, dx_ref, dxb_ref, acc_ref):
        i = pl.program_id(0)

        @pl.when(i == 0)
        def _():
            acc_ref[...] = jnp.zeros_like(acc_ref)

        xv = x_ref[...]
        g = g_ref[...]
        r = lax.rsqrt(jnp.mean(xv * xv, axis=-1, keepdims=True) + EPS)
        xn = xv * r
        err = xn * g - t_ref[...]
        dy = err * (1.0 / D)
        acc_ref[0:1, :] += jnp.sum(dy * xn, axis=0, keepdims=True)
        acc_ref[1:2, :] += jnp.full((1, D), 0.5 / D, F32) * jnp.sum(err * err)
        dxn = dy * g
        dx = r * (dxn - xn * jnp.mean(dxn * xn, axis=-1, keepdims=True))
        dx_ref[...] = dx
        dxb_ref[...] = dx.astype(BF16)

    row = pl.BlockSpec((T, D), lambda i: (i, 0))
    return _call(body, name=name, grid=(S // T,),
                 in_specs=[row, pl.BlockSpec((1, D), lambda i: (0, 0)), row],
                 out_specs=[row, row, pl.BlockSpec((8, D), lambda i: (0, 0))],
                 out_shape=[jax.ShapeDtypeStruct((S, D), F32), jax.ShapeDtypeStruct((S, D), BF16),
                            jax.ShapeDtypeStruct((8, D), F32)],
                 args=[x2, gain, target], sem=("arbitrary",), jobs=jobs)


ROW_CHUNK = 64
LANE_CHUNK = 512
SUBLANES = 8


def _shifted_copies(buf, sh_scr, l0, lc):
    n = buf.shape[0] - SUBLANES
    for b in range(1, SUBLANES):
        sh_scr[b - 1, 0:n, :] = buf[b:b + n, l0:l0 + lc]


def _shifted(buf, sh_scr, start, rows, l0, lc):
    a8, b = (start // SUBLANES) * SUBLANES, start % SUBLANES
    if b == 0:
        return buf[a8:a8 + rows, l0:l0 + lc]
    return sh_scr[b - 1, a8:a8 + rows, :]


def conf_fwd(proj, cw, cb, lg, lb, name, jobs=()):
    S, E3 = proj.shape
    E = E3 // 3
    T = _tile(S, 256, HALO)
    R = T // HALO
    lc = _tile(E, LANE_CHUNK)
    rc = min(ROW_CHUNK, T)

    def body(a_ref, b_ref, z_ref, ap_ref, bp_ref, cw_ref, cb_ref, lg_ref, lb_ref, c_ref, y_ref, u_scr, sh_scr):
        i = pl.program_id(0)
        up = ap_ref[...] * _sigmoid(bp_ref[...])
        u_scr[0:HALO, :] = jnp.where(i > 0, up, 0.0)
        u_scr[HALO:HALO + T, :] = a_ref[...] * _sigmoid(b_ref[...])
        off = HALO - (CONV_WIDTH - 1)
        for l0 in range(0, E, lc):
            _shifted_copies(u_scr, sh_scr, l0, lc)
            for r0 in range(0, T, rc):
                acc = jnp.broadcast_to(cb_ref[:, l0:l0 + lc], (rc, lc))
                for k in range(CONV_WIDTH):
                    acc = acc + _shifted(u_scr, sh_scr, r0 + off + k, rc, l0, lc) * cw_ref[k:k + 1, l0:l0 + lc]
                c_ref[r0:r0 + rc, l0:l0 + lc] = acc
        c = c_ref[...]
        mu = jnp.mean(c, axis=-1, keepdims=True)
        d = c - mu
        var = jnp.mean(d * d, axis=-1, keepdims=True)
        cn = d * lax.rsqrt(var + EPS) * lg_ref[...] + lb_ref[...]
        z = z_ref[...]
        y_ref[...] = ((cn * _sigmoid(cn)) * (z * _sigmoid(z))).astype(BF16)

    def col(j):
        return pl.BlockSpec((T, E), lambda i: (i, j))

    def prev(j):
        return pl.BlockSpec((HALO, E), lambda i: (jnp.maximum(i * R - 1, 0), j))

    vec = pl.BlockSpec((1, E), lambda i: (0, 0))
    return _call(body, name=name, grid=(S // T,),
                 in_specs=[col(0), col(1), col(2), prev(0), prev(1),
                           pl.BlockSpec((HALO, E), lambda i: (0, 0)), vec, vec, vec],
                 out_specs=[pl.BlockSpec((T, E), lambda i: (i, 0))] * 2,
                 out_shape=[jax.ShapeDtypeStruct((S, E), F32), jax.ShapeDtypeStruct((S, E), BF16)],
                 scratch_shapes=[pltpu.VMEM((HALO + T, E), F32), pltpu.VMEM((SUBLANES - 1, HALO + T, lc), F32)],
                 args=[proj, proj, proj, proj, proj, cw, cb, lg, lb], sem=("parallel",), jobs=jobs)


def conf_bwd_ln(c, dy, proj, lg, lb, name, jobs=()):
    S, E = c.shape
    T = _tile(S, 256, 8)

    def body(c_ref, dy_ref, z_ref, lg_ref, lb_ref, dc_ref, dz_ref, acc_ref):
        i = pl.program_id(0)

        @pl.when(i == 0)
        def _():
            acc_ref[...] = jnp.zeros_like(acc_ref)

        cv = c_ref[...]
        mu = jnp.mean(cv, axis=-1, keepdims=True)
        d = cv - mu
        var = jnp.mean(d * d, axis=-1, keepdims=True)
        rstd = lax.rsqrt(var + EPS)
        xh = d * rstd
        lgv = lg_ref[...]
        cn = xh * lgv + lb_ref[...]
        z = z_ref[...]
        dy = dy_ref[...]
        sc = _sigmoid(cn)
        sz = _sigmoid(z)
        dcn = dy * (z * sz) * (sc * (1.0 + cn * (1.0 - sc)))
        dz_ref[...] = (dy * (cn * sc) * (sz * (1.0 + z * (1.0 - sz)))).astype(BF16)
        acc_ref[0:1, :] += jnp.sum(dcn * xh, axis=0, keepdims=True)
        acc_ref[1:2, :] += jnp.sum(dcn, axis=0, keepdims=True)
        dxh = dcn * lgv
        dc = rstd * (dxh - jnp.mean(dxh, axis=-1, keepdims=True)
                     - xh * jnp.mean(dxh * xh, axis=-1, keepdims=True))
        acc_ref[2:3, :] += jnp.sum(dc, axis=0, keepdims=True)
        dc_ref[...] = dc

    row = pl.BlockSpec((T, E), lambda i: (i, 0))
    vec = pl.BlockSpec((1, E), lambda i: (0, 0))
    return _call(body, name=name, grid=(S // T,),
                 in_specs=[row, row, pl.BlockSpec((T, E), lambda i: (i, 2)), vec, vec],
                 out_specs=[row, row, pl.BlockSpec((8, E), lambda i: (0, 0))],
                 out_shape=[jax.ShapeDtypeStruct((S, E), F32), jax.ShapeDtypeStruct((S, E), BF16),
                            jax.ShapeDtypeStruct((8, E), F32)],
                 args=[c, dy, proj, lg, lb], sem=("arbitrary",), jobs=jobs)


def conf_bwd_conv(proj, dc, dz, cw, name, jobs=()):
    S, E3 = proj.shape
    E = E3 // 3
    T = _tile(S, 256, HALO)
    R = T // HALO
    nt = S // T
    lc = _tile(E, LANE_CHUNK)
    rc = min(ROW_CHUNK, T)
    rd = min(ROW_CHUNK // 2, T)

    def body(a_ref, b_ref, ap_ref, bp_ref, dc_ref, dcn_ref, dz_ref, cw_ref, o_ref, dw_ref, u_scr, dc_scr, sh_scr,
             dw_scr):
        i = pl.program_id(0)

        @pl.when(i == 0)
        def _():
            dw_scr[...] = jnp.zeros_like(dw_scr)

        a = a_ref[...]
        sb = _sigmoid(b_ref[...])
        up = ap_ref[...] * _sigmoid(bp_ref[...])
        u_scr[0:HALO, :] = jnp.where(i > 0, up, 0.0)
        u_scr[HALO:HALO + T, :] = a * sb
        dc_scr[0:T, :] = dc_ref[...]
        dc_scr[T:T + HALO, :] = jnp.where(i < nt - 1, dcn_ref[...], 0.0)
        off = HALO - (CONV_WIDTH - 1)
        for l0 in range(0, E, lc):
            _shifted_copies(u_scr, sh_scr, l0, lc)
            for r0 in range(0, T, rd):
                dcc = dc_scr[r0:r0 + rd, l0:l0 + lc]
                for k in range(CONV_WIDTH):
                    prod = _shifted(u_scr, sh_scr, r0 + off + k, rd, l0, lc) * dcc
                    dw_scr[k, :, l0:l0 + lc] += jnp.sum(prod.reshape(rd // SUBLANES, SUBLANES, lc), axis=0)
            _shifted_copies(dc_scr, sh_scr, l0, lc)
            for r0 in range(0, T, rc):
                acc = jnp.zeros((rc, lc), F32)
                for k in range(CONV_WIDTH):
                    s0 = r0 + (CONV_WIDTH - 1) - k
                    acc = acc + _shifted(dc_scr, sh_scr, s0, rc, l0, lc) * cw_ref[k:k + 1, l0:l0 + lc]
                av = a[r0:r0 + rc, l0:l0 + lc]
                sv = sb[r0:r0 + rc, l0:l0 + lc]
                o_ref[r0:r0 + rc, l0:l0 + lc] = (acc * sv).astype(BF16)
                o_ref[r0:r0 + rc, E + l0:E + l0 + lc] = (acc * av * sv * (1.0 - sv)).astype(BF16)
        o_ref[:, 2 * E:3 * E] = dz_ref[...]

        @pl.when(i == nt - 1)
        def _():
            dw_ref[...] = jnp.sum(dw_scr[...], axis=1)

    def col(j):
        return pl.BlockSpec((T, E), lambda i: (i, j))

    def prev(j):
        return pl.BlockSpec((HALO, E), lambda i: (jnp.maximum(i * R - 1, 0), j))

    row = pl.BlockSpec((T, E), lambda i: (i, 0))
    nxt = pl.BlockSpec((HALO, E), lambda i: (jnp.minimum((i + 1) * R, S // HALO - 1), 0))
    return _call(body, name=name, grid=(nt,),
                 in_specs=[col(0), col(1), prev(0), prev(1), row, nxt, row,
                           pl.BlockSpec((HALO, E), lambda i: (0, 0))],
                 out_specs=[pl.BlockSpec((T, E3), lambda i: (i, 0)), pl.BlockSpec((HALO, E), lambda i: (0, 0))],
                 out_shape=[jax.ShapeDtypeStruct((S, E3), BF16), jax.ShapeDtypeStruct((HALO, E), F32)],
                 scratch_shapes=[pltpu.VMEM((HALO + T, E), F32), pltpu.VMEM((T + HALO, E), F32),
                                 pltpu.VMEM((SUBLANES - 1, HALO + T, lc), F32),
                                 pltpu.VMEM((HALO, SUBLANES, E), F32)],
                 args=[proj, proj, proj, proj, dc, dc, dz, cw], sem=("arbitrary",), jobs=jobs)


def bucket_tables():
    q = np.arange(BLOCK)[:, None]
    k = np.arange(2 * BLOCK)[None, :]
    out = []
    for window, dil in GROUPS:
        delta = q + BLOCK - k
        valid = (delta >= 0) & (delta <= window // dil)
        dist = np.clip(delta, 0, None) * dil
        large = MAX_EXACT + (np.log(np.maximum(dist, 1).astype(np.float32) / MAX_EXACT)
                             / math.log(MAX_DISTANCE / MAX_EXACT)
                             * (N_BUCKETS - MAX_EXACT)).astype(np.int32)
        large = np.minimum(large, N_BUCKETS - 1)
        bucket = np.where(dist < MAX_EXACT, dist, large)
        out.append(np.where(valid, bucket, -1).reshape(-1))
    return np.concatenate(out).astype(np.int32)


def _band_masks(has_previous):
    ql = lax.broadcasted_iota(jnp.int32, (BLOCK, 2 * BLOCK), 0)
    kk = lax.broadcasted_iota(jnp.int32, (BLOCK, 2 * BLOCK), 1)
    band = (kk >= ql) & (kk <= ql + BLOCK)
    return band, band & ((kk >= BLOCK) | has_previous)


def _dot_nt(a, b):
    return lax.dot_general(a, b, (((1,), (1,)), ((), ())), preferred_element_type=F32)


def _dot_tn(a, b):
    return lax.dot_general(a, b, (((0,), (0,)), ((), ())), preferred_element_type=F32)


ATTN_ROWS = 2048


def _sub(start, size, dil):
    return pl.ds(start, size) if dil == 1 else pl.ds(start, size, stride=dil)


def _attn_geometry(S, dil):
    halo = BLOCK * dil
    rows = max(min(S, ATTN_ROWS), halo)
    return halo, rows, S // rows, rows // halo


MAX_ROW_STRIDE = 8


def _split(dst, src, n, dil, tmp):
    if dil <= MAX_ROW_STRIDE:
        for r in range(dil):
            dst[r] = src[_sub(r, n, dil), :].astype(dst.dtype)
        return
    f, g = 4, dil // 4
    for r1 in range(f):
        tmp[0:n * g, :] = src[_sub(r1, n * g, f), :]
        for r2 in range(g):
            dst[r2 * f + r1] = tmp[_sub(r2, n, g), :].astype(dst.dtype)


def _merge(dst, src, n, dil, tmp):
    if dil <= MAX_ROW_STRIDE:
        for r in range(dil):
            dst[_sub(r, n, dil), :] = src[r]
        return
    f, g = 4, dil // 4
    for r1 in range(f):
        for r2 in range(g):
            tmp[_sub(r2, n, g), :] = src[r2 * f + r1]
        dst[_sub(r1, n * g, f), :] = tmp[0:n * g, :]


def _split_tmp(sub, dil):
    rows = (sub + BLOCK) * (dil // 4) if dil > MAX_ROW_STRIDE else SUBLANES
    return pltpu.VMEM((rows, HEAD_DIM), F32)


def attn_fwd(qz, kv, bias, g, dil, name, jobs=()):
    S = qz.shape[0]
    D = qz.shape[1] // (N_GROUPS + 1)
    H = D // HEAD_DIM
    halo, rows, nsb, nblk = _attn_geometry(S, dil)
    sub = nblk * BLOCK
    scale = HEAD_DIM ** -0.5

    def body(q_ref, kvc_ref, kvp_ref, b_ref, o_ref, l_ref, ks, vs, qd, kd, vd, od, ld, tmp):
        sb = pl.program_id(1)
        ks[0:halo, :] = kvp_ref[0].astype(F32)
        ks[halo:, :] = kvc_ref[0].astype(F32)
        vs[0:halo, :] = kvp_ref[1].astype(F32)
        vs[halo:, :] = kvc_ref[1].astype(F32)
        _split(kd, ks, sub + BLOCK, dil, tmp)
        _split(vd, vs, sub + BLOCK, dil, tmp)
        _split(qd, q_ref, sub, dil, tmp)
        band, first = _band_masks(sb > 0)
        bias_t = b_ref[0]
        for r in range(dil):
            for jj in range(nblk):
                q = qd[r, jj * BLOCK:(jj + 1) * BLOCK, :]
                keys = slice(jj * BLOCK, (jj + 2) * BLOCK)
                s = _dot_nt(q, kd[r, keys, :]) * scale + bias_t
                s = jnp.where(band if jj > 0 else first, s, -jnp.inf)
                m = jnp.max(s, axis=-1, keepdims=True)
                p = jnp.exp(s - m)
                den = jnp.sum(p, axis=-1, keepdims=True)
                pv = jnp.dot(p.astype(BF16), vd[r, keys, :], preferred_element_type=F32)
                own = slice(jj * BLOCK, (jj + 1) * BLOCK)
                od[r, own, :] = pv / den
                ld[r, own, :] = jnp.broadcast_to(m + jnp.log(den), (BLOCK, HEAD_DIM))
        _merge(o_ref, od, sub, dil, tmp)
        _merge(l_ref, ld, sub, dil, tmp)

    per = rows // halo
    out_spec = pl.BlockSpec((rows, HEAD_DIM), lambda h, sb: (sb, h))
    return _call(body, name=name, grid=(H, nsb),
                 in_specs=[pl.BlockSpec((rows, HEAD_DIM), lambda h, sb: (sb, g * H + h)),
                           pl.BlockSpec((2, rows, HEAD_DIM), lambda h, sb: (0, sb, g * H + h)),
                           pl.BlockSpec((2, halo, HEAD_DIM),
                                        lambda h, sb: (0, jnp.maximum(sb * per - 1, 0), g * H + h)),
                           pl.BlockSpec((1, BLOCK, 2 * BLOCK), lambda h, sb: (h, 0, 0))],
                 out_specs=[out_spec, out_spec],
                 out_shape=[jax.ShapeDtypeStruct((S, D), F32)] * 2,
                 scratch_shapes=[pltpu.VMEM((halo + rows, HEAD_DIM), F32)] * 2
                 + [pltpu.VMEM((dil, sub, HEAD_DIM), BF16)]
                 + [pltpu.VMEM((dil, sub + BLOCK, HEAD_DIM), BF16)] * 2
                 + [pltpu.VMEM((dil, sub, HEAD_DIM), F32)] * 2 + [_split_tmp(sub, dil)],
                 args=[qz, kv, kv, bias], sem=("parallel", "arbitrary"), jobs=jobs)


def merge_fwd(os_, lses, qz, name, jobs=()):
    S, D = os_[0].shape
    T = _tile(S, 256, 8)

    def body(o1, o2, o3, l1, l2, l3, z_ref, o_ref, lse_ref, y_ref):
        la, lb_, lc_ = l1[...], l2[...], l3[...]
        m = jnp.maximum(jnp.maximum(la, lb_), lc_)
        ea, eb, ec = jnp.exp(la - m), jnp.exp(lb_ - m), jnp.exp(lc_ - m)
        den = ea + eb + ec
        o = (ea * o1[...] + eb * o2[...] + ec * o3[...]) / den
        z = z_ref[...]
        o_ref[...] = o
        lse_ref[...] = m + jnp.log(den)
        y_ref[...] = (o * (z * _sigmoid(z))).astype(BF16)

    row = pl.BlockSpec((T, D), lambda i: (i, 0))
    return _call(body, name=name, grid=(S // T,),
                 in_specs=[row] * 6 + [pl.BlockSpec((T, D), lambda i: (i, N_GROUPS))],
                 out_specs=[row] * 3,
                 out_shape=[jax.ShapeDtypeStruct((S, D), F32), jax.ShapeDtypeStruct((S, D), F32),
                            jax.ShapeDtypeStruct((S, D), BF16)],
                 args=[*os_, *lses, qz], sem=("parallel",), jobs=jobs)


def merge_bwd(dy2, o, qz, name, jobs=()):
    S, D = o.shape
    H = D // HEAD_DIM
    T = _tile(S, 256, 8)
    nq = N_GROUPS + 1

    def body(dy_ref, o_ref, z_ref, do_ref, dl_ref, dqz_ref):
        dy = dy_ref[...]
        ov = o_ref[...]
        z = z_ref[...]
        sz = _sigmoid(z)
        do = dy * (z * sz)
        do_ref[...] = do.astype(BF16)
        dqz_ref[...] = (dy * ov * (sz * (1.0 + z * (1.0 - sz)))).astype(BF16)
        prod = do * ov
        for h in range(H):
            hs = slice(h * HEAD_DIM, (h + 1) * HEAD_DIM)
            dl_ref[:, hs] = jnp.broadcast_to(jnp.sum(prod[:, hs], axis=-1, keepdims=True), (T, HEAD_DIM))

    row = pl.BlockSpec((T, D), lambda i: (i, 0))
    last = pl.BlockSpec((T, D), lambda i: (i, N_GROUPS))
    return _call(body, name=name, grid=(S // T,), in_specs=[row, row, last], out_specs=[row, row, last],
                 out_shape=[jax.ShapeDtypeStruct((S, D), BF16), jax.ShapeDtypeStruct((S, D), F32),
                            jax.ShapeDtypeStruct((S, nq * D), BF16)],
                 args=[dy2, o, qz], sem=("parallel",), jobs=jobs)


def attn_bwd(qz, kv, do, lse, delta, bias, dqz, dkv, g, dil, name, jobs=()):
    S = qz.shape[0]
    D = qz.shape[1] // (N_GROUPS + 1)
    H = D // HEAD_DIM
    halo, rows, nsb, nblk = _attn_geometry(S, dil)
    sub = nblk * BLOCK
    scale = HEAD_DIM ** -0.5
    have_dkv = dkv is not None

    def body(*refs):
        q_ref, do_ref, l_ref, d_ref, kvc_ref, kvp_ref, b_ref = refs[:7]
        n_in = 7 + 1 + (1 if have_dkv else 0)
        (dq_ref, dkv_ref, ds_ref, ks, vs, dks, dvs, dos, dqs, carry_k, carry_v,
         qd, dod, ld, dd, dqd, kd, vd, dkd, dvd, tmp) = refs[n_in:]
        i = pl.program_id(1)
        sb = nsb - 1 - i

        @pl.when(i == 0)
        def _():
            ds_ref[...] = jnp.zeros_like(ds_ref)

        ks[0:halo, :] = kvp_ref[0].astype(F32)
        ks[halo:, :] = kvc_ref[0].astype(F32)
        vs[0:halo, :] = kvp_ref[1].astype(F32)
        vs[halo:, :] = kvc_ref[1].astype(F32)
        dos[...] = do_ref[...].astype(F32)
        _split(kd, ks, sub + BLOCK, dil, tmp)
        _split(vd, vs, sub + BLOCK, dil, tmp)
        for dst, src in ((qd, q_ref), (dod, dos), (ld, l_ref), (dd, d_ref)):
            _split(dst, src, sub, dil, tmp)
        dkd[...] = jnp.zeros_like(dkd)
        dvd[...] = jnp.zeros_like(dvd)
        band, first = _band_masks(sb > 0)
        bias_t = b_ref[0]
        for r in range(dil):
            for jj in range(nblk):
                own = slice(jj * BLOCK, (jj + 1) * BLOCK)
                keys = slice(jj * BLOCK, (jj + 2) * BLOCK)
                q = qd[r, own, :]
                do = dod[r, own, :]
                k = kd[r, keys, :]
                v = vd[r, keys, :]
                lse = ld[r, own, :]
                dlt = dd[r, own, :]
                s = _dot_nt(q, k) * scale + bias_t - jnp.concatenate([lse, lse], axis=-1)
                p = jnp.where(band if jj > 0 else first, jnp.exp(s), 0.0)
                ds = p * (_dot_nt(do, v) - jnp.concatenate([dlt, dlt], axis=-1))
                ds_ref[0] += ds
                dsb = ds.astype(BF16)
                dqd[r, own, :] = jnp.dot(dsb, k, preferred_element_type=F32) * scale
                dkd[r, keys, :] += _dot_tn(dsb, q) * scale
                dvd[r, keys, :] += _dot_tn(p.astype(BF16), do)
        _merge(dqs, dqd, sub, dil, tmp)
        _merge(dks, dkd, sub + BLOCK, dil, tmp)
        _merge(dvs, dvd, sub + BLOCK, dil, tmp)

        @pl.when(i > 0)
        def _():
            dks[rows:rows + halo, :] += carry_k[...]
            dvs[rows:rows + halo, :] += carry_v[...]

        dkv_ref[0] = dks[halo:, :].astype(BF16)
        dkv_ref[1] = dvs[halo:, :].astype(BF16)
        carry_k[...] = dks[0:halo, :]
        carry_v[...] = dvs[0:halo, :]
        dq_ref[...] = dqs[...].astype(BF16)

    per = rows // halo

    def rev(i):
        return nsb - 1 - i

    bias_spec = pl.BlockSpec((1, BLOCK, 2 * BLOCK), lambda h, i: (h, 0, 0))
    row_spec = pl.BlockSpec((rows, HEAD_DIM), lambda h, i: (rev(i), h))
    in_specs = [pl.BlockSpec((rows, HEAD_DIM), lambda h, i: (rev(i), g * H + h)),
                row_spec, row_spec, row_spec,
                pl.BlockSpec((2, rows, HEAD_DIM), lambda h, i: (0, rev(i), g * H + h)),
                pl.BlockSpec((2, halo, HEAD_DIM),
                             lambda h, i: (0, jnp.maximum(rev(i) * per - 1, 0), g * H + h)),
                bias_spec, ANY_SPEC]
    args = [qz, do, lse, delta, kv, kv, bias, dqz]
    aliases = {7: 0}
    if have_dkv:
        in_specs.append(ANY_SPEC)
        args.append(dkv)
        aliases[8] = 1
    blk = (halo + rows, HEAD_DIM)
    own, keys = (dil, sub, HEAD_DIM), (dil, sub + BLOCK, HEAD_DIM)
    return _call(body, name=name, grid=(H, nsb), in_specs=in_specs,
                 out_specs=[pl.BlockSpec((rows, HEAD_DIM), lambda h, i: (rev(i), g * H + h)),
                            pl.BlockSpec((2, rows, HEAD_DIM), lambda h, i: (0, rev(i), g * H + h)),
                            bias_spec],
                 out_shape=[jax.ShapeDtypeStruct(qz.shape, BF16),
                            jax.ShapeDtypeStruct((2, S, N_GROUPS * D), BF16),
                            jax.ShapeDtypeStruct((H, BLOCK, 2 * BLOCK), F32)],
                 scratch_shapes=[pltpu.VMEM(blk, F32), pltpu.VMEM(blk, F32), pltpu.VMEM(blk, F32),
                                 pltpu.VMEM(blk, F32),
                                 pltpu.VMEM((rows, HEAD_DIM), F32), pltpu.VMEM((rows, HEAD_DIM), F32),
                                 pltpu.VMEM((halo, HEAD_DIM), F32), pltpu.VMEM((halo, HEAD_DIM), F32),
                                 pltpu.VMEM(own, BF16), pltpu.VMEM(own, BF16), pltpu.VMEM(own, F32),
                                 pltpu.VMEM(own, F32), pltpu.VMEM(own, F32),
                                 pltpu.VMEM(keys, BF16), pltpu.VMEM(keys, BF16),
                                 pltpu.VMEM(keys, F32), pltpu.VMEM(keys, F32), _split_tmp(sub, dil)],
                 aliases=aliases, args=args, sem=("parallel", "arbitrary"), jobs=jobs)


def _adamw(w, g, m, v):
    m = ADAM_B1 * m + (1.0 - ADAM_B1) * g
    v = ADAM_B2 * v + (1.0 - ADAM_B2) * (g * g)
    m_hat = m / (1.0 - ADAM_B1 ** ADAM_STEP)
    v_hat = v / (1.0 - ADAM_B2 ** ADAM_STEP)
    delta = -ADAM_LR * (m_hat / (jnp.sqrt(v_hat) + ADAM_EPS) + ADAM_WD * w)
    return delta, m, v


def add_pairs(dw, got, name, jobs=()):
    _, K, n = got.shape
    tk = _tile(K, 512, 16)

    def own_block(r, i, p):
        return 4 * ((p[0] + r // 2) % 2) + 2 * ((p[1] + r % 2) % 2) + p[2], i, 0

    def body(p_ref, a_ref, b_ref, o_ref):
        o_ref[...] = (a_ref[...].astype(F32) + b_ref[...].astype(F32)).astype(o_ref.dtype)

    blk = pl.BlockSpec((None, tk, n), lambda r, i, p: (r, i, 0))
    grid_spec = pltpu.PrefetchScalarGridSpec(
        num_scalar_prefetch=1, grid=(N_CHIPS, K // tk),
        in_specs=[pl.BlockSpec((None, tk, n), own_block), blk], out_specs=blk)
    return pl.pallas_call(body, name=name, grid_spec=grid_spec,
                          out_shape=jax.ShapeDtypeStruct(got.shape, got.dtype))(_place_vector(), dw, got)


def adamw_reduce(pieces, w, m, v, name, jobs=()):
    K, n = w.shape
    kp = K // len(pieces)
    tk = _tile(kp, 256, 8)
    sp = kp // tk

    def body(*refs):
        w_ref, m_ref, v_ref, g_ref, d_ref, nm_ref, nv_ref = refs[2 * len(pieces):]
        i = pl.program_id(0)
        g = None
        for q in range(len(pieces)):
            p_ref, r_ref = refs[2 * q], refs[2 * q + 1]
            gq = p_ref[...].astype(F32)
            for r in range(N_CHIPS - 1):
                gq = gq + r_ref[r].astype(F32)
            g = gq if g is None else jnp.where(i >= q * sp, gq, g)
        d, nm, nv = _adamw(w_ref[...], g, m_ref[...], v_ref[...])
        g_ref[...] = g
        d_ref[...] = d
        nm_ref[...] = nm
        nv_ref[...] = nv

    def piece_specs(q):
        def at(i):
            return jnp.clip(i - q * sp, 0, sp - 1)
        return [pl.BlockSpec((None, tk, n), lambda i: (0, at(i), 0)),
                pl.BlockSpec((N_CHIPS - 1, tk, n), lambda i: (0, at(i), 0))]

    blk = pl.BlockSpec((tk, n), lambda i: (i, 0))
    in_specs, args = [], []
    for q, (part, got) in enumerate(pieces):
        in_specs += piece_specs(q)
        args += [part, got]
    return _call(body, name=name, grid=(K // tk,), in_specs=in_specs + [blk, blk, blk],
                 out_specs=[blk] * 4, out_shape=[jax.ShapeDtypeStruct((K, n), F32)] * 4,
                 args=args + [w, m, v], sem=("parallel",), jobs=jobs)


def sum_parts(parts, name):
    _, R, D = parts.shape

    def body(p_ref, o_ref):
        g = p_ref[0]
        for r in range(1, N_DEV):
            g = g + p_ref[r]
        o_ref[...] = g

    return _call(body, name=name, in_specs=[VMEM_SPEC], out_specs=[VMEM_SPEC],
                 out_shape=[jax.ShapeDtypeStruct((R, D), F32)], args=[parts])[0]


def adamw_small(w, g, m, v, name):
    def body(w_ref, g_ref, m_ref, v_ref, d_ref, nm_ref, nv_ref):
        d, nm, nv = _adamw(w_ref[...], g_ref[...], m_ref[...], v_ref[...])
        d_ref[...] = d
        nm_ref[...] = nm
        nv_ref[...] = nv

    return _call(body, name=name, in_specs=[VMEM_SPEC] * 4, out_specs=[VMEM_SPEC] * 3,
                 out_shape=[jax.ShapeDtypeStruct(w.shape, F32)] * 3, args=[w, g, m, v])


def _kv_row_pieces(K):
    return (0, K // 2), (K // 2, K)


def _row(v, at):
    return jnp.pad(v.reshape(1, -1), ((at, 7 - at), (0, 0)))


def _pack_sharded(norm, conv_w, conv_b, ln_g, ln_b):
    n = norm.shape[-1]
    taps = jnp.pad(conv_w.reshape(CONV_WIDTH, n), ((0, HALO - CONV_WIDTH), (0, 0)))
    return jnp.concatenate([_row(norm, 0), _row(ln_g, 0) + _row(ln_b, 1) + _row(conv_b, 2), taps], axis=0)


def _pack_rel(rel_bias, D):
    return jnp.pad(rel_bias.reshape(1, -1), ((0, 7), (0, D - rel_bias.size)))


def _pack_replicated(kv_norm, b_norm, final_norm, rel_bias, D):
    return jnp.concatenate([_row(kv_norm, 0) + _row(b_norm, 1), _row(final_norm, 0), _pack_rel(rel_bias, D)], axis=0)


def kernel(x, a_norm, a_w_in, a_conv_w, a_conv_b, a_ln_g, a_ln_b, a_w_out, kv_norm, w_kv, b_norm, b_w_in, b_w_out, rel_bias, final_norm, loss_target, m_a_norm, m_a_w_in, m_a_conv_w, m_a_conv_b, m_a_ln_g, m_a_ln_b, m_a_w_out, m_kv_norm, m_w_kv, m_b_norm, m_b_w_in, m_b_w_out, m_rel_bias, m_final_norm, v_a_norm, v_a_w_in, v_a_conv_w, v_a_conv_b, v_a_ln_g, v_a_ln_b, v_a_w_out, v_kv_norm, v_w_kv, v_b_norm, v_b_w_in, v_b_w_out, v_rel_bias, v_final_norm):
    _, S, D = x.shape
    E = D
    H = D // HEAD_DIM
    nsh = D // N_DEV
    x0 = x.reshape(S, D)
    target = loss_target.reshape(S, D)
    kvn, bn, fn = kv_norm.reshape(1, D), b_norm.reshape(1, D), final_norm.reshape(1, D)

    names = ["a_w_in", "a_w_out", "w_kv", "b_w_in", "b_w_out"]
    big_w = dict(zip(names, [a_w_in[0], a_w_out[0], w_kv, b_w_in[0], b_w_out[0]]))
    big_m = dict(zip(names, [m_a_w_in[0], m_a_w_out[0], m_w_kv, m_b_w_in[0], m_b_w_out[0]]))
    big_v = dict(zip(names, [v_a_w_in[0], v_a_w_out[0], v_w_kv, v_b_w_in[0], v_b_w_out[0]]))
    sh_w = _pack_sharded(a_norm, a_conv_w, a_conv_b, a_ln_g, a_ln_b)

    wa_in, sh_all = all_gather([big_w["a_w_in"].astype(BF16), sh_w], "gather_first")
    sh_full = sh_all.transpose(1, 0, 2).reshape(SMALL_SH_ROWS, D)
    an, cw, cb = sh_full[0:1], sh_full[16:16 + HALO], sh_full[10:11]
    lg, lb = sh_full[8:9], sh_full[9:10]

    buf = {nm: place_shard(big_w[nm], "place_" + nm) for nm in names[1:]}
    kv_cut = _kv_row_pieces(D)

    riding = []

    def ride(nm, **kw):
        make = gather_relay_job if set(kw) & {"near", "far", "last"} else gather_job
        riding.append((nm, make(buf[nm], **kw)))
        return riding[-1][1]

    def landed():
        while riding:
            nm, job = riding.pop()
            buf[nm] = job.results[0]

    (h0,) = rms_fwd(x0, [an], "rms_a")
    proj = matmul_nn(h0, wa_in, "a_in",
                     jobs=[ride("a_w_out", chips=(0, E // N_DEV)), ride("w_kv", near=kv_cut[0])])
    landed()
    c, y = conf_fwd(proj, cw, cb, lg, lb, "conf_fwd",
                    jobs=[ride("a_w_out", sibling=(0, E // N_DEV)),
                          ride("w_kv", far=kv_cut[0], near=kv_cut[1])])
    landed()
    wa_out = buf["a_w_out"].reshape(1, E, D)
    x1 = matmul_nn(y, wa_out, "a_out", res=x0, jobs=[ride("w_kv", last=kv_cut[0], far=kv_cut[1])])
    landed()
    hk, hb = rms_fwd(x1, [kvn, bn], "rms_b", jobs=[ride("w_kv", last=kv_cut[1])])
    landed()
    wkv = buf["w_kv"]
    kv = matmul_nn(hk, wkv, "kv_proj", out_dtype=BF16, kv_split=True, jobs=[ride("b_w_in", chips=(0, D))])
    landed()

    bt = jnp.asarray(bucket_tables())
    onehot = (bt[None, :] == jnp.arange(N_BUCKETS, dtype=jnp.int32)[:, None]).astype(F32)
    bias = small_dot(rel_bias.T, onehot, "nn", "bias_table", jobs=[ride("b_w_in", sibling=(0, D))])
    landed()
    bias = bias.reshape(H, N_GROUPS, BLOCK, 2 * BLOCK)
    bias = [bias[:, g] for g in range(N_GROUPS)]
    wb_in = buf["b_w_in"]
    qz = matmul_nn(hb, wb_in, "b_in", jobs=[ride("b_w_out", chips=(0, D // N_DEV))])
    landed()

    os_, lses = [], []
    for g, (_, dil) in enumerate(GROUPS):
        o_g, l_g = attn_fwd(qz, kv, bias[g], g, dil, "attn_fwd%d" % g,
                            jobs=[ride("b_w_out", sibling=(0, D // N_DEV))] if g == 0 else ())
        landed()
        os_.append(o_g)
        lses.append(l_g)
    wb_out = buf["b_w_out"].reshape(1, D, D)
    o, lse, y2 = merge_fwd(os_, lses, qz, "merge_fwd")
    x2 = matmul_nn(y2, wb_out, "b_out", res=x1)
    dx2, dx2b, fin_acc = final_loss(x2, fn, target, "final_loss")

    dy2 = matmul_nt(dx2b, wb_out, "b_out_dx")
    dwb_out = matmul_tn(y2, dx2b, 1, "b_out_dw").reshape(N_DEV, D // N_DEV, D)
    r_b_out = reduce_sibling_job(dwb_out)
    do, delta, dqz = merge_bwd(dy2, o, qz, "merge_bwd", jobs=[r_b_out])
    p_b_out = add_pairs(dwb_out, r_b_out.results[0], "pairs_b_out")
    r_b_out2 = reduce_chips_job(p_b_out)
    dkv = None
    ds_tabs = []
    for g, (_, dil) in enumerate(GROUPS):
        dqz, dkv, ds_tab = attn_bwd(qz, kv, do, lse, delta, bias[g], dqz, dkv, g, dil,
                                    "attn_bwd%d" % g, jobs=[r_b_out2] if g == 0 else ())
        ds_tabs.append(ds_tab.reshape(H, 2 * BLOCK * BLOCK))
    d_rel = small_dot(onehot, jnp.concatenate(ds_tabs, axis=1), "nt", "bias_grad")
    dw_kv_lo = matmul_tn(hk, dkv, N_DEV, "kv_dw_lo", kv_split=True, k_tiles=(0, 1))
    r_kv_lo = reduce_sibling_job(dw_kv_lo)
    dw_kv_hi = matmul_tn(hk, dkv, N_DEV, "kv_dw_hi", kv_split=True, k_tiles=(1, 2), jobs=[r_kv_lo])
    p_kv_lo = add_pairs(dw_kv_lo, r_kv_lo.results[0], "pairs_kv_lo")
    r_kv_lo2, r_kv_hi = reduce_chips_job(p_kv_lo), reduce_sibling_job(dw_kv_hi)
    dhk = matmul_nt(dkv, wkv, "kv_dx", kv_split=True, jobs=[r_kv_lo2, r_kv_hi])
    p_kv_hi = add_pairs(dw_kv_hi, r_kv_hi.results[0], "pairs_kv_hi")
    r_kv_hi2 = reduce_chips_job(p_kv_hi)
    dwb_in = matmul_tn(hb, dqz, N_DEV, "b_in_dw", jobs=[r_kv_hi2])
    r_b_in = reduce_sibling_job(dwb_in)
    dhb = matmul_nt(dqz, wb_in, "b_in_dx", jobs=[r_b_in])
    p_b_in = add_pairs(dwb_in, r_b_in.results[0], "pairs_b_in")
    r_b_in2 = reduce_chips_job(p_b_in)
    dx1, dx1b, norm_acc = rms_bwd(x1, [dhk, dhb], [kvn, bn], dx2, "rms_b_bwd", True)

    dy = matmul_nt(dx1b, wa_out, "a_out_dx")
    dwa_out = matmul_tn(y, dx1b, 1, "a_out_dw").reshape(N_DEV, E // N_DEV, D)
    r_a_out = reduce_sibling_job(dwa_out)
    dc, dz, ln_acc = conf_bwd_ln(c, dy, proj, lg, lb, "conf_bwd_ln", jobs=[r_a_out])
    p_a_out = add_pairs(dwa_out, r_a_out.results[0], "pairs_a_out")
    r_a_out2 = reduce_chips_job(p_a_out)
    dproj, dcw = conf_bwd_conv(proj, dc, dz, cw, "conf_bwd_conv", jobs=[r_b_in2, r_a_out2])
    share = share_small_job(jnp.concatenate([ln_acc, dcw, norm_acc, fin_acc, _pack_rel(d_rel, D)], axis=0))
    dwa_lo = matmul_tn(h0, dproj, N_DEV, "a_in_dw_lo", k_tiles=(0, 1), jobs=[share])
    r_lo = reduce_sibling_job(dwa_lo)
    dwa_hi = matmul_tn(h0, dproj, N_DEV, "a_in_dw_hi", k_tiles=(1, 2), jobs=[r_lo])
    p_lo = add_pairs(dwa_lo, r_lo.results[0], "pairs_a_in_lo")
    r_lo2, r_hi = reduce_chips_job(p_lo), reduce_sibling_job(dwa_hi)
    dh0 = matmul_nt(dproj, wa_in, "a_in_dx", jobs=[r_lo2, r_hi])
    p_hi = add_pairs(dwa_hi, r_hi.results[0], "pairs_a_in_hi")
    r_hi2 = reduce_chips_job(p_hi)
    grad_x, a_acc = rms_bwd(x0, [dh0], [an], dx1, "rms_a_bwd", False, jobs=[r_hi2])

    share_a = share_small_job(a_acc)
    sums = {"w_kv": [(p_kv_lo, r_kv_lo2), (p_kv_hi, r_kv_hi2)], "b_w_in": [(p_b_in, r_b_in2)], "b_w_out": [(p_b_out, r_b_out2)],
            "a_w_out": [(p_a_out, r_a_out2)], "a_w_in": [(p_lo, r_lo2), (p_hi, r_hi2)]}
    big_out = {}
    for nm, pieces in sums.items():
        big_out[nm] = adamw_reduce([(part, job.results[0]) for part, job in pieces],
                                   big_w[nm], big_m[nm], big_v[nm], "adamw_" + nm,
                                   jobs=[share_a] if nm == "a_w_out" else ())

    gsum = jnp.concatenate([sum_parts(share_a.results[0], "sum_small_a"),
                            sum_parts(share.results[0], "sum_small")], axis=0)
    me = 4 * lax.axis_index("x") + 2 * lax.axis_index("y") + lax.axis_index("c")
    g_sh = lax.dynamic_slice(gsum, (0, me * nsh), (SMALL_SH_ROWS, nsh))
    g_rep = gsum[SMALL_SH_ROWS:]
    loss = g_rep[9, 0]
    sh_m = _pack_sharded(m_a_norm, m_a_conv_w, m_a_conv_b, m_a_ln_g, m_a_ln_b)
    sh_v = _pack_sharded(v_a_norm, v_a_conv_w, v_a_conv_b, v_a_ln_g, v_a_ln_b)
    sh_d, sh_nm, sh_nv = adamw_small(sh_w, g_sh, sh_m, sh_v, "adamw_sharded")
    rep_w = _pack_replicated(kv_norm, b_norm, final_norm, rel_bias, D)
    rep_m = _pack_replicated(m_kv_norm, m_b_norm, m_final_norm, m_rel_bias, D)
    rep_v = _pack_replicated(v_kv_norm, v_b_norm, v_final_norm, v_rel_bias, D)
    rep_d, rep_nm, rep_nv = adamw_small(rep_w, g_rep, rep_m, rep_v, "adamw_replicated")

    def unpack(kind):
        sh = (g_sh, sh_d, sh_nm, sh_nv)[kind]
        rep = (g_rep, rep_d, rep_nm, rep_nv)[kind]
        big = {nm: big_out[nm][kind] for nm in names}
        nb = rel_bias.size
        return [
            sh[0:1],
            big["a_w_in"][None],
            sh[16:16 + CONV_WIDTH][None],
            sh[10:11], sh[8:9], sh[9:10],
            big["a_w_out"][None],
            rep[0],
            big["w_kv"],
            rep[1:2],
            big["b_w_in"][None],
            big["b_w_out"][None],
            rep[16, :nb].reshape(rel_bias.shape),
            rep[8],
        ]

    return (loss, grad_x.reshape(1, S, D), *unpack(0), *unpack(1), *unpack(2), *unpack(3))
```

```python
import functools
import math

import numpy as np
import jax
import jax.numpy as jnp
from jax import lax
from jax.experimental import pallas as pl
from jax.experimental.pallas import tpu as pltpu

F32 = jnp.float32
BF16 = jnp.bfloat16

N_DEV = 8
N_CHIPS = 4
EPS = 1e-6
HEAD_DIM = 128
BLOCK = 128
GROUPS = ((128, 1), (512, 4), (2048, 16))
N_GROUPS = len(GROUPS)
CONV_WIDTH = 31
HALO = 32
N_BUCKETS = 32
MAX_EXACT = N_BUCKETS // 2
MAX_DISTANCE = 2048
V7X_VMEM_BYTES = 64 * 1024 * 1024
VMEM_LIMIT = (V7X_VMEM_BYTES * 7) // 8
MATMUL_VMEM_BUDGET = (V7X_VMEM_BYTES * 11) // 16
LANE = 128

ADAM_LR = 0.001
ADAM_B1 = 0.9
ADAM_B2 = 0.999
ADAM_EPS = 1e-08
ADAM_WD = 0.01
ADAM_STEP = 10

SMALL_SH_ROWS = 48
SMALL_REP_ROWS = 24
MESH = pl.DeviceIdType.MESH
ANY_SPEC = pl.BlockSpec(memory_space=pl.ANY)
VMEM_SPEC = pl.BlockSpec(memory_space=pltpu.VMEM)


def _tile(dim, pref, unit=LANE):
    if dim <= pref:
        return dim
    t = (pref // unit) * unit
    while dim % t:
        t -= unit
    assert t > 0
    return t


def _sigmoid(v):
    return jax.nn.sigmoid(v)


def _place():
    return lax.axis_index("x"), lax.axis_index("y"), lax.axis_index("c")


def _flip(v, bit):
    return 1 - v if bit else v


class Job:
    def __init__(self, srcs, dsts, n_sems, build):
        self.srcs, self.dsts, self.n_sems, self.build = list(srcs), list(dsts), n_sems, build
        self.results = None


def _remote(src, dst, send_sems, recv_sems, k, peer):
    return pltpu.make_async_remote_copy(src_ref=src, dst_ref=dst, send_sem=send_sems.at[k],
                                        recv_sem=recv_sems.at[k], device_id=peer, device_id_type=MESH)


def _call(body, *, name, in_specs, out_specs, out_shape, args, grid=(), scratch_shapes=(), sem=(),
          aliases=None, jobs=()):
    n_in, n_out, n_scr = len(in_specs), len(out_specs), len(scratch_shapes)
    aliases = dict(aliases or {})
    x_in, x_out, x_scr = [], [], []
    for job in jobs:
        job.in_at = n_in + len(x_in)
        x_in += job.srcs
        job.out_at = n_out + len(x_out)
        for d in job.dsts:
            if not isinstance(d, jax.ShapeDtypeStruct):
                aliases[n_in + len(x_in)] = n_out + len(x_out)
                x_in.append(d)
            x_out.append(jax.ShapeDtypeStruct(d.shape, d.dtype))
        job.scr_at = n_scr + len(x_scr)
        x_scr += [pltpu.SemaphoreType.DMA((job.n_sems,))] * 3

    def wrapped(*refs):
        ins = refs[:n_in + len(x_in)]
        outs = refs[len(ins):len(ins) + n_out + len(x_out)]
        scr = refs[len(ins) + len(outs):]
        core = ins[:n_in] + outs[:n_out] + scr[:n_scr]
        if not jobs:
            body(*core)
            return
        copies = []
        for job in jobs:
            copies += job.build(ins[job.in_at:job.in_at + len(job.srcs)],
                                outs[job.out_at:job.out_at + len(job.dsts)],
                                *scr[job.scr_at:job.scr_at + 3])
        if grid:
            pids = [pl.program_id(d) for d in range(len(grid))]
            first = functools.reduce(jnp.logical_and, [p == 0 for p in pids])
            last = functools.reduce(jnp.logical_and, [p == g - 1 for p, g in zip(pids, grid)])

            @pl.when(first)
            def _():
                for cp in copies:
                    cp.start()

            body(*core)

            @pl.when(last)
            def _():
                for cp in copies:
                    cp.wait()
        else:
            for cp in copies:
                cp.start()
            body(*core)
            for cp in copies:
                cp.wait()

    if jobs:
        sem = ("arbitrary",) * len(grid)
    kwargs = dict(grid=grid) if grid else {}
    if aliases:
        kwargs["input_output_aliases"] = aliases
    outs = pl.pallas_call(
        wrapped, name=name,
        in_specs=list(in_specs) + [ANY_SPEC] * len(x_in),
        out_specs=list(out_specs) + [ANY_SPEC] * len(x_out),
        out_shape=list(out_shape) + x_out,
        scratch_shapes=list(scratch_shapes) + x_scr,
        compiler_params=pltpu.CompilerParams(dimension_semantics=sem if sem else None,
                                             vmem_limit_bytes=VMEM_LIMIT),
        **kwargs,
    )(*args, *x_in)
    for job in jobs:
        job.results = list(outs[job.out_at:job.out_at + len(job.dsts)])
    return list(outs[:n_out])


def place_shard(w, name):
    K, n = w.shape
    tk = _tile(K, 512, 16)

    def body(p_ref, w_ref, o_ref):
        o_ref[...] = w_ref[...].astype(BF16)

    grid_spec = pltpu.PrefetchScalarGridSpec(
        num_scalar_prefetch=1, grid=(K // tk,),
        in_specs=[pl.BlockSpec((tk, n), lambda i, p: (i, 0))],
        out_specs=pl.BlockSpec((None, tk, n), lambda i, p: (4 * p[0] + 2 * p[1] + p[2], i, 0)))
    return pl.pallas_call(body, name=name, grid_spec=grid_spec,
                          out_shape=jax.ShapeDtypeStruct((N_DEV, K, n), BF16))(_place_vector(), w)


def _place_vector():
    return jnp.stack(_place()).astype(jnp.int32)


def gather_job(buf, chips=None, sibling=None):
    def build(srcs, dsts, send, recv, loc):
        (out,) = dsts
        x, y, c = _place()
        copies = []
        if chips is not None:
            mine = out.at[4 * x + 2 * y + c, pl.ds(chips[0], chips[1] - chips[0])]
            peers = [(x, y, 1 - c), (1 - x, y, c), (x, 1 - y, c), (1 - x, 1 - y, c)]
            copies += [_remote(mine, mine, send, recv, k, p) for k, p in enumerate(peers)]
        if sibling is not None:
            for k, (cx, cy) in enumerate([(1 - x, y), (x, 1 - y), (1 - x, 1 - y)]):
                blk = out.at[4 * cx + 2 * cy + c, pl.ds(sibling[0], sibling[1] - sibling[0])]
                copies.append(_remote(blk, blk, send, recv, 4 + k, (x, y, 1 - c)))
        return copies

    return Job([], [buf], 7, build)


def _relay(x, y, c):
    return ((x + 1 - c) % 2, (y + c) % 2), ((x + c) % 2, (y + 1 - c) % 2)


def gather_relay_job(buf, near=None, far=None, last=None):
    def rows(slot, rng):
        return slot.at[pl.ds(rng[0], rng[1] - rng[0])]

    def build(srcs, dsts, send, recv, loc):
        (out,) = dsts
        x, y, c = _place()
        sib = (x, y, 1 - c)
        copies = []
        if near is not None:
            mine = rows(out.at[4 * x + 2 * y + c], near)
            for k, p in enumerate([sib, (1 - x, y, c), (x, 1 - y, c)]):
                copies.append(_remote(mine, mine, send, recv, k, p))
        if far is not None:
            (fx, fy), (tx, ty) = _relay(x, y, c)
            blk = rows(out.at[4 * fx + 2 * fy + c], far)
            copies.append(_remote(blk, blk, send, recv, 3, (tx, ty, c)))
            for k, (cx, cy) in enumerate([(1 - x, y), (x, 1 - y)]):
                blk = rows(out.at[4 * cx + 2 * cy + c], far)
                copies.append(_remote(blk, blk, send, recv, 4 + k, sib))
        if last is not None:
            blk = rows(out.at[4 * (1 - x) + 2 * (1 - y) + c], last)
            copies.append(_remote(blk, blk, send, recv, 6, sib))
        return copies

    return Job([], [buf], 7, build)


def reduce_sibling_job(dw):
    def build(srcs, dsts, send, recv, loc):
        (src,), (got,) = srcs, dsts
        x, y, c = _place()
        return [_remote(src.at[4 * _flip(x, r & 2) + 2 * _flip(y, r & 1) + 1 - c], got.at[r], send, recv, r,
                        (x, y, 1 - c)) for r in range(N_CHIPS)]

    return Job([dw], [jax.ShapeDtypeStruct((N_CHIPS,) + dw.shape[1:], dw.dtype)], N_CHIPS, build)


def reduce_chips_job(part):
    def build(srcs, dsts, send, recv, loc):
        (src,), (got,) = srcs, dsts
        x, y, c = _place()
        return [_remote(src.at[r], got.at[r - 1], send, recv, r - 1, (_flip(x, r & 2), _flip(y, r & 1), c))
                for r in range(1, N_CHIPS)]

    return Job([part], [jax.ShapeDtypeStruct((N_CHIPS - 1,) + part.shape[1:], part.dtype)], N_CHIPS - 1, build)


def share_small_job(small):
    def build(srcs, dsts, send, recv, loc):
        (src,), (out,) = srcs, dsts
        x, y, c = _place()
        mine = out.at[4 * x + 2 * y + c]
        copies = [pltpu.make_async_copy(src, mine, loc.at[0])]
        for rel in range(1, N_DEV):
            peer = (_flip(x, rel & 4), _flip(y, rel & 2), _flip(c, rel & 1))
            copies.append(_remote(src, mine, send, recv, rel - 1, peer))
        return copies

    return Job([small], [jax.ShapeDtypeStruct((N_DEV,) + small.shape, small.dtype)], N_DEV - 1, build)


def all_gather(shards, name):
    n = len(shards)

    def body(*refs):
        ins, outs = refs[:n], refs[n:2 * n]
        send_sems, recv_sems, local_sems = refs[2 * n:]
        x, y, c = _place()
        me, sibling = (x, y, c), (x, y, 1 - c)
        near = [(1 - x, y), (x, 1 - y)]
        diag = (1 - x, 1 - y)
        frm, to = _relay(x, y, c)

        def slot(a, dev):
            return outs[a].at[4 * dev[0] + 2 * dev[1] + dev[2]]

        def copy(a, k, block, to_dev, src=None):
            return pltpu.make_async_remote_copy(
                src_ref=slot(a, block) if src is None else src, dst_ref=slot(a, block),
                send_sem=send_sems.at[a, k], recv_sem=recv_sems.at[a, k],
                device_id=to_dev, device_id_type=MESH)

        mine, sent = [], []
        for a in range(n):
            mine.append(pltpu.make_async_copy(ins[a], slot(a, me), local_sems.at[a]))
            mine[a].start()
            sent.append([copy(a, 0, me, sibling, src=ins[a])]
                        + [copy(a, 1 + j, me, (*chip, c), src=ins[a]) for j, chip in enumerate(near)])
            for cp in sent[a]:
                cp.start()
        for a in range(n):
            for j, chip in enumerate(near):
                copy(a, 1 + j, (*chip, c), me).wait_recv()
            more = [copy(a, 3, (*frm, c), (*to, c))] + [copy(a, 4 + j, (*chip, c), sibling)
                                                       for j, chip in enumerate(near)]
            for cp in more:
                cp.start()
            sent[a] += more
        for a in range(n):
            copy(a, 3, (*diag, c), me).wait_recv()
            sent[a].append(copy(a, 6, (*diag, c), sibling))
            sent[a][-1].start()
        for a in range(n):
            for k, block in [(0, sibling), (4, (*near[0], 1 - c)), (5, (*near[1], 1 - c)), (6, (*diag, 1 - c))]:
                copy(a, k, block, me).wait_recv()
            for cp in sent[a]:
                cp.wait_send()
            mine[a].wait()

    return pl.pallas_call(
        body, name=name,
        in_specs=[ANY_SPEC] * n, out_specs=[ANY_SPEC] * n,
        out_shape=[jax.ShapeDtypeStruct((N_DEV,) + s.shape, s.dtype) for s in shards],
        scratch_shapes=[pltpu.SemaphoreType.DMA((n, 7)), pltpu.SemaphoreType.DMA((n, 7)),
                        pltpu.SemaphoreType.DMA((n,))],
    )(*shards)


def _kv_split_index(tw, D):
    pd = D // tw

    def index(j):
        return (j // pd) % 2, (j // (2 * pd)) * pd + j % pd

    return index


def _col_tile(n, also, pref):
    t = (min(pref, n) // LANE) * LANE
    while n % t or (also is not None and also % t):
        t -= LANE
    assert t > 0
    return t


def matmul_nn(a, w, name, res=None, out_dtype=F32, kv_split=False, jobs=()):
    M, K = a.shape
    nb, _, n = w.shape
    D = nb * n // (2 * N_GROUPS)
    tn = _col_tile(n, D if kv_split else None, 1024)
    out_bytes = jnp.dtype(out_dtype).itemsize + (4 if res is not None else 0)
    tm = _tile(M, 2048)
    if 2 * (tm * K * 2 + K * tn * 2 + tm * tn * out_bytes) > MATMUL_VMEM_BUDGET:
        tm = _tile(M, 1024)
    per = n // tn

    def body(*refs):
        if res is None:
            a_ref, w_ref, o_ref = refs
        else:
            a_ref, w_ref, r_ref, o_ref = refs
        acc = jnp.dot(a_ref[...], w_ref[...], preferred_element_type=F32)
        if res is not None:
            acc = r_ref[...] + acc
        o_ref[...] = acc.astype(o_ref.dtype)

    in_specs = [pl.BlockSpec((tm, K), lambda i, j: (i, 0)),
                pl.BlockSpec((None, K, tn), lambda i, j: (j // per, 0, j % per))]
    args = [a, w]
    if res is not None:
        in_specs.append(pl.BlockSpec((tm, tn), lambda i, j: (i, j)))
        args.append(res)
    if kv_split:
        split = _kv_split_index(tn, D)
        out_spec = pl.BlockSpec((None, tm, tn), lambda i, j: (split(j)[0], i, split(j)[1]))
        out_shape = jax.ShapeDtypeStruct((2, M, N_GROUPS * D), out_dtype)
    else:
        out_spec = pl.BlockSpec((tm, tn), lambda i, j: (i, j))
        out_shape = jax.ShapeDtypeStruct((M, nb * n), out_dtype)
    return _call(body, name=name, grid=(M // tm, nb * per), in_specs=in_specs, out_specs=[out_spec],
                 out_shape=[out_shape], args=args, sem=("parallel", "parallel"), jobs=jobs)[0]


def matmul_nt(dy, w, name, kv_split=False, jobs=()):
    M = dy.shape[-2]
    nb, K, n = w.shape
    D = nb * n // (2 * N_GROUPS)
    tm = _tile(M, 1024)
    tc = _col_tile(n, D if kv_split else None, 1024)
    per = n // tc
    pair = max(u for u in (1, 2, 4) if (nb * per) % u == 0 and u * tc <= 2048)

    def body(*refs):
        o_ref = refs[-1]
        j = pl.program_id(1)
        lhs = jnp.concatenate([refs[u][...] for u in range(pair)], axis=1)
        rhs = jnp.concatenate([refs[pair + u][...] for u in range(pair)], axis=1)
        part = lax.dot_general(lhs, rhs, (((1,), (1,)), ((), ())), preferred_element_type=F32)

        @pl.when(j == 0)
        def _():
            o_ref[...] = part

        @pl.when(j > 0)
        def _():
            o_ref[...] += part

    def dy_spec(u):
        if kv_split:
            split = _kv_split_index(tc, D)
            return pl.BlockSpec((None, tm, tc),
                                lambda i, j: (split(pair * j + u)[0], i, split(pair * j + u)[1]))
        return pl.BlockSpec((tm, tc), lambda i, j: (i, pair * j + u))

    def w_spec(u):
        return pl.BlockSpec((None, K, tc), lambda i, j: ((pair * j + u) // per, 0, (pair * j + u) % per))

    return _call(body, name=name, grid=(M // tm, nb * per // pair),
                 in_specs=[dy_spec(u) for u in range(pair)] + [w_spec(u) for u in range(pair)],
                 out_specs=[pl.BlockSpec((tm, K), lambda i, j: (i, 0))],
                 out_shape=[jax.ShapeDtypeStruct((M, K), F32)], args=[dy] * pair + [w] * pair,
                 sem=("parallel", "arbitrary"), jobs=jobs)[0]


def matmul_tn(a, dy, nb, name, out_dtype=BF16, kv_split=False, k_tiles=None, jobs=()):
    M, K = a.shape
    N = 2 * dy.shape[-1] if kv_split else dy.shape[-1]
    n = N // nb
    D = N // (2 * N_GROUPS)
    tn = _col_tile(n, D if kv_split else None, 1024)
    per = n // tn
    tk = _tile(K, 1024)

    def body(a_ref, dy_ref, o_ref):
        o_ref[...] = lax.dot_general(a_ref[...], dy_ref[...], (((0,), (0,)), ((), ())),
                                     preferred_element_type=F32).astype(o_ref.dtype)

    if kv_split:
        split = _kv_split_index(tn, D)
        dy_spec = pl.BlockSpec((None, M, tn), lambda k, j: (split(j)[0], 0, split(j)[1]))
    else:
        dy_spec = pl.BlockSpec((M, tn), lambda k, j: (0, j))
    k0, k1 = k_tiles or (0, K // tk)
    return _call(body, name=name, grid=(k1 - k0, nb * per),
                 in_specs=[pl.BlockSpec((M, tk), lambda k, j: (0, k0 + k)), dy_spec],
                 out_specs=[pl.BlockSpec((None, tk, tn), lambda k, j: (j // per, k, j % per))],
                 out_shape=[jax.ShapeDtypeStruct((nb, (k1 - k0) * tk, n), out_dtype)], args=[a, dy],
                 sem=("parallel", "parallel"), jobs=jobs)[0]


def small_dot(a, b, contract, name, jobs=()):
    if contract == "nn":
        dims = (((1,), (0,)), ((), ()))
        out = (a.shape[0], b.shape[1])
    else:
        dims = (((1,), (1,)), ((), ()))
        out = (a.shape[0], b.shape[0])

    def body(a_ref, b_ref, o_ref):
        o_ref[...] = lax.dot_general(a_ref[...], b_ref[...], dims, precision=lax.Precision.HIGHEST,
                                     preferred_element_type=F32)

    return _call(body, name=name, in_specs=[VMEM_SPEC, VMEM_SPEC], out_specs=[VMEM_SPEC],
                 out_shape=[jax.ShapeDtypeStruct(out, F32)], args=[a, b], jobs=jobs)[0]


def rms_fwd(x, gains, name, jobs=()):
    S, D = x.shape
    T = _tile(S, 512, 8)
    n = len(gains)

    def body(x_ref, *refs):
        xv = x_ref[...]
        xn = xv * lax.rsqrt(jnp.mean(xv * xv, axis=-1, keepdims=True) + EPS)
        for g_ref, o_ref in zip(refs[:n], refs[n:]):
            o_ref[...] = (xn * g_ref[...]).astype(o_ref.dtype)

    row = pl.BlockSpec((T, D), lambda i: (i, 0))
    vec = pl.BlockSpec((1, D), lambda i: (0, 0))
    return _call(body, name=name, grid=(S // T,), in_specs=[row] + [vec] * n, out_specs=[row] * n,
                 out_shape=[jax.ShapeDtypeStruct((S, D), BF16)] * n, args=[x, *gains],
                 sem=("parallel",), jobs=jobs)


def rms_bwd(x, dhs, gains, dres, name, want_bf16, jobs=()):
    S, D = x.shape
    T = _tile(S, 256, 8)
    n = len(gains)

    def body(x_ref, *refs):
        dh_refs = refs[:n]
        g_refs = refs[n:2 * n]
        dres_ref = refs[2 * n]
        outs = refs[2 * n + 1:]
        dx_ref, dg_ref = outs[0], outs[-1]
        i = pl.program_id(0)

        @pl.when(i == 0)
        def _():
            dg_ref[...] = jnp.zeros_like(dg_ref)

        xv = x_ref[...]
        r = lax.rsqrt(jnp.mean(xv * xv, axis=-1, keepdims=True) + EPS)
        xn = xv * r
        dxn = jnp.zeros_like(xv)
        for k in range(n):
            dh = dh_refs[k][...]
            dg_ref[k:k + 1, :] += jnp.sum(dh * xn, axis=0, keepdims=True)
            dxn = dxn + dh * g_refs[k][...]
        dx = dres_ref[...] + r * (dxn - xn * jnp.mean(dxn * xn, axis=-1, keepdims=True))
        dx_ref[...] = dx
        if want_bf16:
            outs[1][...] = dx.astype(BF16)

    row = pl.BlockSpec((T, D), lambda i: (i, 0))
    vec = pl.BlockSpec((1, D), lambda i: (0, 0))
    acc = pl.BlockSpec((8, D), lambda i: (0, 0))
    out_specs = [row] + ([row] if want_bf16 else []) + [acc]
    out_shape = ([jax.ShapeDtypeStruct((S, D), F32)]
                 + ([jax.ShapeDtypeStruct((S, D), BF16)] if want_bf16 else [])
                 + [jax.ShapeDtypeStruct((8, D), F32)])
    return _call(body, name=name, grid=(S // T,), in_specs=[row] + [row] * n + [vec] * n + [row],
                 out_specs=out_specs, out_shape=out_shape, args=[x, *dhs, *gains, dres],
                 sem=("arbitrary",), jobs=jobs)


def final_loss(x2, gain, target, name, jobs=()):
    S, D = x2.shape
    T = _tile(S, 256, 8)

    def body(x_ref, g_ref, t_ref, dx_ref, dxb_ref, acc_ref):
        i = pl.program_id(0)

        @pl.when(i == 0)
        def _():
            acc_ref[...] = jnp.zeros_like(acc_ref)

        xv = x_ref[...]
        g = g_ref[...]
        r = lax.rsqrt(jnp.mean(xv * xv, axis=-1, keepdims=True) + EPS)
        xn = xv * r
        err = xn * g - t_ref[...]
        dy = err * (1.0 / D)
        acc_ref[0:1, :] += jnp.sum(dy * xn, axis=0, keepdims=True)
        acc_ref[1:2, :] += jnp.full((1, D), 0.5 / D, F32) * jnp.sum(err * err)
        dxn = dy * g
        dx = r * (dxn - xn * jnp.mean(dxn * xn, axis=-1, keepdims=True))
        dx_ref[...] = dx
        dxb_ref[...] = dx.astype(BF16)

    row = pl.BlockSpec((T, D), lambda i: (i, 0))
    return _call(body, name=name, grid=(S // T,),
                 in_specs=[row, pl.BlockSpec((1, D), lambda i: (0, 0)), row],
                 out_specs=[row, row, pl.BlockSpec((8, D), lambda i: (0, 0))],
                 out_shape=[jax.ShapeDtypeStruct((S, D), F32), jax.ShapeDtypeStruct((S, D), BF16),
                            jax.ShapeDtypeStruct((8, D), F32)],
                 args=[x2, gain, target], sem=("arbitrary",), jobs=jobs)


ROW_CHUNK = 64
LANE_CHUNK = 512
SUBLANES = 8


def _shifted_copies(buf, sh_scr, l0, lc):
    n = buf.shape[0] - SUBLANES
    for b in range(1, SUBLANES):
        sh_scr[b - 1, 0:n, :] = buf[b:b + n, l0:l0 + lc]


def _shifted(buf, sh_scr, start, rows, l0, lc):
    a8, b = (start // SUBLANES) * SUBLANES, start % SUBLANES
    if b == 0:
        return buf[a8:a8 + rows, l0:l0 + lc]
    return sh_scr[b - 1, a8:a8 + rows, :]


def conf_fwd(proj, cw, cb, lg, lb, name, jobs=()):
    S, E3 = proj.shape
    E = E3 // 3
    T = _tile(S, 256, HALO)
    R = T // HALO
    lc = _tile(E, LANE_CHUNK)
    rc = min(ROW_CHUNK, T)

    def body(a_ref, b_ref, z_ref, ap_ref, bp_ref, cw_ref, cb_ref, lg_ref, lb_ref, c_ref, y_ref, u_scr, sh_scr):
        i = pl.program_id(0)
        up = ap_ref[...] * _sigmoid(bp_ref[...])
        u_scr[0:HALO, :] = jnp.where(i > 0, up, 0.0)
        u_scr[HALO:HALO + T, :] = a_ref[...] * _sigmoid(b_ref[...])
        off = HALO - (CONV_WIDTH - 1)
        for l0 in range(0, E, lc):
            _shifted_copies(u_scr, sh_scr, l0, lc)
            for r0 in range(0, T, rc):
                acc = jnp.broadcast_to(cb_ref[:, l0:l0 + lc], (rc, lc))
                for k in range(CONV_WIDTH):
                    acc = acc + _shifted(u_scr, sh_scr, r0 + off + k, rc, l0, lc) * cw_ref[k:k + 1, l0:l0 + lc]
                c_ref[r0:r0 + rc, l0:l0 + lc] = acc
        c = c_ref[...]
        mu = jnp.mean(c, axis=-1, keepdims=True)
        d = c - mu
        var = jnp.mean(d * d, axis=-1, keepdims=True)
        cn = d * lax.rsqrt(var + EPS) * lg_ref[...] + lb_ref[...]
        z = z_ref[...]
        y_ref[...] = ((cn * _sigmoid(cn)) * (z * _sigmoid(z))).astype(BF16)

    def col(j):
        return pl.BlockSpec((T, E), lambda i: (i, j))

    def prev(j):
        return pl.BlockSpec((HALO, E), lambda i: (jnp.maximum(i * R - 1, 0), j))

    vec = pl.BlockSpec((1, E), lambda i: (0, 0))
    return _call(body, name=name, grid=(S // T,),
                 in_specs=[col(0), col(1), col(2), prev(0), prev(1),
                           pl.BlockSpec((HALO, E), lambda i: (0, 0)), vec, vec, vec],
                 out_specs=[pl.BlockSpec((T, E), lambda i: (i, 0))] * 2,
                 out_shape=[jax.ShapeDtypeStruct((S, E), F32), jax.ShapeDtypeStruct((S, E), BF16)],
                 scratch_shapes=[pltpu.VMEM((HALO + T, E), F32), pltpu.VMEM((SUBLANES - 1, HALO + T, lc), F32)],
                 args=[proj, proj, proj, proj, proj, cw, cb, lg, lb], sem=("parallel",), jobs=jobs)


def conf_bwd_ln(c, dy, proj, lg, lb, name, jobs=()):
    S, E = c.shape
    T = _tile(S, 256, 8)

    def body(c_ref, dy_ref, z_ref, lg_ref, lb_ref, dc_ref, dz_ref, acc_ref):
        i = pl.program_id(0)

        @pl.when(i == 0)
        def _():
            acc_ref[...] = jnp.zeros_like(acc_ref)

        cv = c_ref[...]
        mu = jnp.mean(cv, axis=-1, keepdims=True)
        d = cv - mu
        var = jnp.mean(d * d, axis=-1, keepdims=True)
        rstd = lax.rsqrt(var + EPS)
        xh = d * rstd
        lgv = lg_ref[...]
        cn = xh * lgv + lb_ref[...]
        z = z_ref[...]
        dy = dy_ref[...]
        sc = _sigmoid(cn)
        sz = _sigmoid(z)
        dcn = dy * (z * sz) * (sc * (1.0 + cn * (1.0 - sc)))
        dz_ref[...] = (dy * (cn * sc) * (sz * (1.0 + z * (1.0 - sz)))).astype(BF16)
        acc_ref[0:1, :] += jnp.sum(dcn * xh, axis=0, keepdims=True)
        acc_ref[1:2, :] += jnp.sum(dcn, axis=0, keepdims=True)
        dxh = dcn * lgv
        dc = rstd * (dxh - jnp.mean(dxh, axis=-1, keepdims=True)
                     - xh * jnp.mean(dxh * xh, axis=-1, keepdims=True))
        acc_ref[2:3, :] += jnp.sum(dc, axis=0, keepdims=True)
        dc_ref[...] = dc

    row = pl.BlockSpec((T, E), lambda i: (i, 0))
    vec = pl.BlockSpec((1, E), lambda i: (0, 0))
    return _call(body, name=name, grid=(S // T,),
                 in_specs=[row, row, pl.BlockSpec((T, E), lambda i: (i, 2)), vec, vec],
                 out_specs=[row, row, pl.BlockSpec((8, E), lambda i: (0, 0))],
                 out_shape=[jax.ShapeDtypeStruct((S, E), F32), jax.ShapeDtypeStruct((S, E), BF16),
                            jax.ShapeDtypeStruct((8, E), F32)],
                 args=[c, dy, proj, lg, lb], sem=("arbitrary",), jobs=jobs)


def conf_bwd_conv(proj, dc, dz, cw, name, jobs=()):
    S, E3 = proj.shape
    E = E3 // 3
    T = _tile(S, 256, HALO)
    R = T // HALO
    nt = S // T
    lc = _tile(E, LANE_CHUNK)
    rc = min(ROW_CHUNK, T)
    rd = min(ROW_CHUNK // 2, T)

    def body(a_ref, b_ref, ap_ref, bp_ref, dc_ref, dcn_ref, dz_ref, cw_ref, o_ref, dw_ref, u_scr, dc_scr, sh_scr,
             dw_scr):
        i = pl.program_id(0)

        @pl.when(i == 0)
        def _():
            dw_scr[...] = jnp.zeros_like(dw_scr)

        a = a_ref[...]
        sb = _sigmoid(b_ref[...])
        up = ap_ref[...] * _sigmoid(bp_ref[...])
        u_scr[0:HALO, :] = jnp.where(i > 0, up, 0.0)
        u_scr[HALO:HALO + T, :] = a * sb
        dc_scr[0:T, :] = dc_ref[...]
        dc_scr[T:T + HALO, :] = jnp.where(i < nt - 1, dcn_ref[...], 0.0)
        off = HALO - (CONV_WIDTH - 1)
        for l0 in range(0, E, lc):
            _shifted_copies(u_scr, sh_scr, l0, lc)
            for r0 in range(0, T, rd):
                dcc = dc_scr[r0:r0 + rd, l0:l0 + lc]
                for k in range(CONV_WIDTH):
                    prod = _shifted(u_scr, sh_scr, r0 + off + k, rd, l0, lc) * dcc
                    dw_scr[k, :, l0:l0 + lc] += jnp.sum(prod.reshape(rd // SUBLANES, SUBLANES, lc), axis=0)
            _shifted_copies(dc_scr, sh_scr, l0, lc)
            for r0 in range(0, T, rc):
                acc = jnp.zeros((rc, lc), F32)
                for k in range(CONV_WIDTH):
                    s0 = r0 + (CONV_WIDTH - 1) - k
                    acc = acc + _shifted(dc_scr, sh_scr, s0, rc, l0, lc) * cw_ref[k:k + 1, l0:l0 + lc]
                av = a[r0:r0 + rc, l0:l0 + lc]
                sv = sb[r0:r0 + rc, l0:l0 + lc]
                o_ref[r0:r0 + rc, l0:l0 + lc] = (acc * sv).astype(BF16)
                o_ref[r0:r0 + rc, E + l0:E + l0 + lc] = (acc * av * sv * (1.0 - sv)).astype(BF16)
        o_ref[:, 2 * E:3 * E] = dz_ref[...]

        @pl.when(i == nt - 1)
        def _():
            dw_ref[...] = jnp.sum(dw_scr[...], axis=1)

    def col(j):
        return pl.BlockSpec((T, E), lambda i: (i, j))

    def prev(j):
        return pl.BlockSpec((HALO, E), lambda i: (jnp.maximum(i * R - 1, 0), j))

    row = pl.BlockSpec((T, E), lambda i: (i, 0))
    nxt = pl.BlockSpec((HALO, E), lambda i: (jnp.minimum((i + 1) * R, S // HALO - 1), 0))
    return _call(body, name=name, grid=(nt,),
                 in_specs=[col(0), col(1), prev(0), prev(1), row, nxt, row,
                           pl.BlockSpec((HALO, E), lambda i: (0, 0))],
                 out_specs=[pl.BlockSpec((T, E3), lambda i: (i, 0)), pl.BlockSpec((HALO, E), lambda i: (0, 0))],
                 out_shape=[jax.ShapeDtypeStruct((S, E3), BF16), jax.ShapeDtypeStruct((HALO, E), F32)],
                 scratch_shapes=[pltpu.VMEM((HALO + T, E), F32), pltpu.VMEM((T + HALO, E), F32),
                                 pltpu.VMEM((SUBLANES - 1, HALO + T, lc), F32),
                                 pltpu.VMEM((HALO, SUBLANES, E), F32)],
                 args=[proj, proj, proj, proj, dc, dc, dz, cw], sem=("arbitrary",), jobs=jobs)


def bucket_tables():
    q = np.arange(BLOCK)[:, None]
    k = np.arange(2 * BLOCK)[None, :]
    out = []
    for window, dil in GROUPS:
        delta = q + BLOCK - k
        valid = (delta >= 0) & (delta <= window // dil)
        dist = np.clip(delta, 0, None) * dil
        large = MAX_EXACT + (np.log(np.maximum(dist, 1).astype(np.float32) / MAX_EXACT)
                             / math.log(MAX_DISTANCE / MAX_EXACT)
                             * (N_BUCKETS - MAX_EXACT)).astype(np.int32)
        large = np.minimum(large, N_BUCKETS - 1)
        bucket = np.where(dist < MAX_EXACT, dist, large)
        out.append(np.where(valid, bucket, -1).reshape(-1))
    return np.concatenate(out).astype(np.int32)


def _band_masks(has_previous):
    ql = lax.broadcasted_iota(jnp.int32, (BLOCK, 2 * BLOCK), 0)
    kk = lax.broadcasted_iota(jnp.int32, (BLOCK, 2 * BLOCK), 1)
    band = (kk >= ql) & (kk <= ql + BLOCK)
    return band, band & ((kk >= BLOCK) | has_previous)


def _dot_nt(a, b):
    return lax.dot_general(a, b, (((1,), (1,)), ((), ())), preferred_element_type=F32)


def _dot_tn(a, b):
    return lax.dot_general(a, b, (((0,), (0,)), ((), ())), preferred_element_type=F32)


ATTN_ROWS = 2048


def _sub(start, size, dil):
    return pl.ds(start, size) if dil == 1 else pl.ds(start, size, stride=dil)


def _attn_geometry(S, dil):
    halo = BLOCK * dil
    rows = max(min(S, ATTN_ROWS), halo)
    return halo, rows, S // rows, rows // halo


MAX_ROW_STRIDE = 8


def _split(dst, src, n, dil, tmp):
    if dil <= MAX_ROW_STRIDE:
        for r in range(dil):
            dst[r] = src[_sub(r, n, dil), :].astype(dst.dtype)
        return
    f, g = 4, dil // 4
    for r1 in range(f):
        tmp[0:n * g, :] = src[_sub(r1, n * g, f), :]
        for r2 in range(g):
            dst[r2 * f + r1] = tmp[_sub(r2, n, g), :].astype(dst.dtype)


def _merge(dst, src, n, dil, tmp):
    if dil <= MAX_ROW_STRIDE:
        for r in range(dil):
            dst[_sub(r, n, dil), :] = src[r]
        return
    f, g = 4, dil // 4
    for r1 in range(f):
        for r2 in range(g):
            tmp[_sub(r2, n, g), :] = src[r2 * f + r1]
        dst[_sub(r1, n * g, f), :] = tmp[0:n * g, :]


def _split_tmp(sub, dil):
    rows = (sub + BLOCK) * (dil // 4) if dil > MAX_ROW_STRIDE else SUBLANES
    return pltpu.VMEM((rows, HEAD_DIM), F32)


def attn_fwd(qz, kv, bias, g, dil, name, jobs=()):
    S = qz.shape[0]
    D = qz.shape[1] // (N_GROUPS + 1)
    H = D // HEAD_DIM
    halo, rows, nsb, nblk = _attn_geometry(S, dil)
    sub = nblk * BLOCK
    scale = HEAD_DIM ** -0.5

    def body(q_ref, kvc_ref, kvp_ref, b_ref, o_ref, l_ref, ks, vs, qd, kd, vd, od, ld, tmp):
        sb = pl.program_id(1)
        ks[0:halo, :] = kvp_ref[0].astype(F32)
        ks[halo:, :] = kvc_ref[0].astype(F32)
        vs[0:halo, :] = kvp_ref[1].astype(F32)
        vs[halo:, :] = kvc_ref[1].astype(F32)
        _split(kd, ks, sub + BLOCK, dil, tmp)
        _split(vd, vs, sub + BLOCK, dil, tmp)
        _split(qd, q_ref, sub, dil, tmp)
        band, first = _band_masks(sb > 0)
        bias_t = b_ref[0]
        for r in range(dil):
            for jj in range(nblk):
                q = qd[r, jj * BLOCK:(jj + 1) * BLOCK, :]
                keys = slice(jj * BLOCK, (jj + 2) * BLOCK)
                s = _dot_nt(q, kd[r, keys, :]) * scale + bias_t
                s = jnp.where(band if jj > 0 else first, s, -jnp.inf)
                m = jnp.max(s, axis=-1, keepdims=True)
                p = jnp.exp(s - m)
                den = jnp.sum(p, axis=-1, keepdims=True)
                pv = jnp.dot(p.astype(BF16), vd[r, keys, :], preferred_element_type=F32)
                own = slice(jj * BLOCK, (jj + 1) * BLOCK)
                od[r, own, :] = pv / den
                ld[r, own, :] = jnp.broadcast_to(m + jnp.log(den), (BLOCK, HEAD_DIM))
        _merge(o_ref, od, sub, dil, tmp)
        _merge(l_ref, ld, sub, dil, tmp)

    per = rows // halo
    out_spec = pl.BlockSpec((rows, HEAD_DIM), lambda h, sb: (sb, h))
    return _call(body, name=name, grid=(H, nsb),
                 in_specs=[pl.BlockSpec((rows, HEAD_DIM), lambda h, sb: (sb, g * H + h)),
                           pl.BlockSpec((2, rows, HEAD_DIM), lambda h, sb: (0, sb, g * H + h)),
                           pl.BlockSpec((2, halo, HEAD_DIM),
                                        lambda h, sb: (0, jnp.maximum(sb * per - 1, 0), g * H + h)),
                           pl.BlockSpec((1, BLOCK, 2 * BLOCK), lambda h, sb: (h, 0, 0))],
                 out_specs=[out_spec, out_spec],
                 out_shape=[jax.ShapeDtypeStruct((S, D), F32)] * 2,
                 scratch_shapes=[pltpu.VMEM((halo + rows, HEAD_DIM), F32)] * 2
                 + [pltpu.VMEM((dil, sub, HEAD_DIM), BF16)]
                 + [pltpu.VMEM((dil, sub + BLOCK, HEAD_DIM), BF16)] * 2
                 + [pltpu.VMEM((dil, sub, HEAD_DIM), F32)] * 2 + [_split_tmp(sub, dil)],
                 args=[qz, kv, kv, bias], sem=("parallel", "arbitrary"), jobs=jobs)


def merge_fwd(os_, lses, qz, name, jobs=()):
    S, D = os_[0].shape
    T = _tile(S, 256, 8)

    def body(o1, o2, o3, l1, l2, l3, z_ref, o_ref, lse_ref, y_ref):
        la, lb_, lc_ = l1[...], l2[...], l3[...]
        m = jnp.maximum(jnp.maximum(la, lb_), lc_)
        ea, eb, ec = jnp.exp(la - m), jnp.exp(lb_ - m), jnp.exp(lc_ - m)
        den = ea + eb + ec
        o = (ea * o1[...] + eb * o2[...] + ec * o3[...]) / den
        z = z_ref[...]
        o_ref[...] = o
        lse_ref[...] = m + jnp.log(den)
        y_ref[...] = (o * (z * _sigmoid(z))).astype(BF16)

    row = pl.BlockSpec((T, D), lambda i: (i, 0))
    return _call(body, name=name, grid=(S // T,),
                 in_specs=[row] * 6 + [pl.BlockSpec((T, D), lambda i: (i, N_GROUPS))],
                 out_specs=[row] * 3,
                 out_shape=[jax.ShapeDtypeStruct((S, D), F32), jax.ShapeDtypeStruct((S, D), F32),
                            jax.ShapeDtypeStruct((S, D), BF16)],
                 args=[*os_, *lses, qz], sem=("parallel",), jobs=jobs)


def merge_bwd(dy2, o, qz, name, jobs=()):
    S, D = o.shape
    H = D // HEAD_DIM
    T = _tile(S, 256, 8)
    nq = N_GROUPS + 1

    def body(dy_ref, o_ref, z_ref, do_ref, dl_ref, dqz_ref):
        dy = dy_ref[...]
        ov = o_ref[...]
        z = z_ref[...]
        sz = _sigmoid(z)
        do = dy * (z * sz)
        do_ref[...] = do.astype(BF16)
        dqz_ref[...] = (dy * ov * (sz * (1.0 + z * (1.0 - sz)))).astype(BF16)
        prod = do * ov
        for h in range(H):
            hs = slice(h * HEAD_DIM, (h + 1) * HEAD_DIM)
            dl_ref[:, hs] = jnp.broadcast_to(jnp.sum(prod[:, hs], axis=-1, keepdims=True), (T, HEAD_DIM))

    row = pl.BlockSpec((T, D), lambda i: (i, 0))
    last = pl.BlockSpec((T, D), lambda i: (i, N_GROUPS))
    return _call(body, name=name, grid=(S // T,), in_specs=[row, row, last], out_specs=[row, row, last],
                 out_shape=[jax.ShapeDtypeStruct((S, D), BF16), jax.ShapeDtypeStruct((S, D), F32),
                            jax.ShapeDtypeStruct((S, nq * D), BF16)],
                 args=[dy2, o, qz], sem=("parallel",), jobs=jobs)


def attn_bwd(qz, kv, do, lse, delta, bias, dqz, dkv, g, dil, name, jobs=()):
    S = qz.shape[0]
    D = qz.shape[1] // (N_GROUPS + 1)
    H = D // HEAD_DIM
    halo, rows, nsb, nblk = _attn_geometry(S, dil)
    sub = nblk * BLOCK
    scale = HEAD_DIM ** -0.5
    have_dkv = dkv is not None

    def body(*refs):
        q_ref, do_ref, l_ref, d_ref, kvc_ref, kvp_ref, b_ref = refs[:7]
        n_in = 7 + 1 + (1 if have_dkv else 0)
        (dq_ref, dkv_ref, ds_ref, ks, vs, dks, dvs, dos, dqs, carry_k, carry_v,
         qd, dod, ld, dd, dqd, kd, vd, dkd, dvd, tmp) = refs[n_in:]
        i = pl.program_id(1)
        sb = nsb - 1 - i

        @pl.when(i == 0)
        def _():
            ds_ref[...] = jnp.zeros_like(ds_ref)

        ks[0:halo, :] = kvp_ref[0].astype(F32)
        ks[halo:, :] = kvc_ref[0].astype(F32)
        vs[0:halo, :] = kvp_ref[1].astype(F32)
        vs[halo:, :] = kvc_ref[1].astype(F32)
        dos[...] = do_ref[...].astype(F32)
        _split(kd, ks, sub + BLOCK, dil, tmp)
        _split(vd, vs, sub + BLOCK, dil, tmp)
        for dst, src in ((qd, q_ref), (dod, dos), (ld, l_ref), (dd, d_ref)):
            _split(dst, src, sub, dil, tmp)
        dkd[...] = jnp.zeros_like(dkd)
        dvd[...] = jnp.zeros_like(dvd)
        band, first = _band_masks(sb > 0)
        bias_t = b_ref[0]
        for r in range(dil):
            for jj in range(nblk):
                own = slice(jj * BLOCK, (jj + 1) * BLOCK)
                keys = slice(jj * BLOCK, (jj + 2) * BLOCK)
                q = qd[r, own, :]
                do = dod[r, own, :]
                k = kd[r, keys, :]
                v = vd[r, keys, :]
                lse = ld[r, own, :]
                dlt = dd[r, own, :]
                s = _dot_nt(q, k) * scale + bias_t - jnp.concatenate([lse, lse], axis=-1)
                p = jnp.where(band if jj > 0 else first, jnp.exp(s), 0.0)
                ds = p * (_dot_nt(do, v) - jnp.concatenate([dlt, dlt], axis=-1))
                ds_ref[0] += ds
                dsb = ds.astype(BF16)
                dqd[r, own, :] = jnp.dot(dsb, k, preferred_element_type=F32) * scale
                dkd[r, keys, :] += _dot_tn(dsb, q) * scale
                dvd[r, keys, :] += _dot_tn(p.astype(BF16), do)
        _merge(dqs, dqd, sub, dil, tmp)
        _merge(dks, dkd, sub + BLOCK, dil, tmp)
        _merge(dvs, dvd, sub + BLOCK, dil, tmp)

        @pl.when(i > 0)
        def _():
            dks[rows:rows + halo, :] += carry_k[...]
            dvs[rows:rows + halo, :] += carry_v[...]

        dkv_ref[0] = dks[halo:, :].astype(BF16)
        dkv_ref[1] = dvs[halo:, :].astype(BF16)
        carry_k[...] = dks[0:halo, :]
        carry_v[...] = dvs[0:halo, :]
        dq_ref[...] = dqs[...].astype(BF16)

    per = rows // halo

    def rev(i):
        return nsb - 1 - i

    bias_spec = pl.BlockSpec((1, BLOCK, 2 * BLOCK), lambda h, i: (h, 0, 0))
    row_spec = pl.BlockSpec((rows, HEAD_DIM), lambda h, i: (rev(i), h))
    in_specs = [pl.BlockSpec((rows, HEAD_DIM), lambda h, i: (rev(i), g * H + h)),
                row_spec, row_spec, row_spec,
                pl.BlockSpec((2, rows, HEAD_DIM), lambda h, i: (0, rev(i), g * H + h)),
                pl.BlockSpec((2, halo, HEAD_DIM),
                             lambda h, i: (0, jnp.maximum(rev(i) * per - 1, 0), g * H + h)),
                bias_spec, ANY_SPEC]
    args = [qz, do, lse, delta, kv, kv, bias, dqz]
    aliases = {7: 0}
    if have_dkv:
        in_specs.append(ANY_SPEC)
        args.append(dkv)
        aliases[8] = 1
    blk = (halo + rows, HEAD_DIM)
    own, keys = (dil, sub, HEAD_DIM), (dil, sub + BLOCK, HEAD_DIM)
    return _call(body, name=name, grid=(H, nsb), in_specs=in_specs,
                 out_specs=[pl.BlockSpec((rows, HEAD_DIM), lambda h, i: (rev(i), g * H + h)),
                            pl.BlockSpec((2, rows, HEAD_DIM), lambda h, i: (0, rev(i), g * H + h)),
                            bias_spec],
                 out_shape=[jax.ShapeDtypeStruct(qz.shape, BF16),
                            jax.ShapeDtypeStruct((2, S, N_GROUPS * D), BF16),
                            jax.ShapeDtypeStruct((H, BLOCK, 2 * BLOCK), F32)],
                 scratch_shapes=[pltpu.VMEM(blk, F32), pltpu.VMEM(blk, F32), pltpu.VMEM(blk, F32),
                                 pltpu.VMEM(blk, F32),
                                 pltpu.VMEM((rows, HEAD_DIM), F32), pltpu.VMEM((rows, HEAD_DIM), F32),
                                 pltpu.VMEM((halo, HEAD_DIM), F32), pltpu.VMEM((halo, HEAD_DIM), F32),
                                 pltpu.VMEM(own, BF16), pltpu.VMEM(own, BF16), pltpu.VMEM(own, F32),
                                 pltpu.VMEM(own, F32), pltpu.VMEM(own, F32),
                                 pltpu.VMEM(keys, BF16), pltpu.VMEM(keys, BF16),
                                 pltpu.VMEM(keys, F32), pltpu.VMEM(keys, F32), _split_tmp(sub, dil)],
                 aliases=aliases, args=args, sem=("parallel", "arbitrary"), jobs=jobs)


def _adamw(w, g, m, v):
    m = ADAM_B1 * m + (1.0 - ADAM_B1) * g
    v = ADAM_B2 * v + (1.0 - ADAM_B2) * (g * g)
    m_hat = m / (1.0 - ADAM_B1 ** ADAM_STEP)
    v_hat = v / (1.0 - ADAM_B2 ** ADAM_STEP)
    delta = -ADAM_LR * (m_hat / (jnp.sqrt(v_hat) + ADAM_EPS) + ADAM_WD * w)
    return delta, m, v


def add_pairs(dw, got, name, jobs=()):
    _, K, n = got.shape
    tk = _tile(K, 512, 16)

    def own_block(r, i, p):
        return 4 * ((p[0] + r // 2) % 2) + 2 * ((p[1] + r % 2) % 2) + p[2], i, 0

    def body(p_ref, a_ref, b_ref, o_ref):
        o_ref[...] = (a_ref[...].astype(F32) + b_ref[...].astype(F32)).astype(o_ref.dtype)

    blk = pl.BlockSpec((None, tk, n), lambda r, i, p: (r, i, 0))
    grid_spec = pltpu.PrefetchScalarGridSpec(
        num_scalar_prefetch=1, grid=(N_CHIPS, K // tk),
        in_specs=[pl.BlockSpec((None, tk, n), own_block), blk], out_specs=blk)
    return pl.pallas_call(body, name=name, grid_spec=grid_spec,
                          out_shape=jax.ShapeDtypeStruct(got.shape, got.dtype))(_place_vector(), dw, got)


def adamw_reduce(pieces, w, m, v, name, jobs=()):
    K, n = w.shape
    kp = K // len(pieces)
    tk = _tile(kp, 256, 8)
    sp = kp // tk

    def body(*refs):
        w_ref, m_ref, v_ref, g_ref, d_ref, nm_ref, nv_ref = refs[2 * len(pieces):]
        i = pl.program_id(0)
        g = None
        for q in range(len(pieces)):
            p_ref, r_ref = refs[2 * q], refs[2 * q + 1]
            gq = p_ref[...].astype(F32)
            for r in range(N_CHIPS - 1):
                gq = gq + r_ref[r].astype(F32)
            g = gq if g is None else jnp.where(i >= q * sp, gq, g)
        d, nm, nv = _adamw(w_ref[...], g, m_ref[...], v_ref[...])
        g_ref[...] = g
        d_ref[...] = d
        nm_ref[...] = nm
        nv_ref[...] = nv

    def piece_specs(q):
        def at(i):
            return jnp.clip(i - q * sp, 0, sp - 1)
        return [pl.BlockSpec((None, tk, n), lambda i: (0, at(i), 0)),
                pl.BlockSpec((N_CHIPS - 1, tk, n), lambda i: (0, at(i), 0))]

    blk = pl.BlockSpec((tk, n), lambda i: (i, 0))
    in_specs, args = [], []
    for q, (part, got) in enumerate(pieces):
        in_specs += piece_specs(q)
        args += [part, got]
    return _call(body, name=name, grid=(K // tk,), in_specs=in_specs + [blk, blk, blk],
                 out_specs=[blk] * 4, out_shape=[jax.ShapeDtypeStruct((K, n), F32)] * 4,
                 args=args + [w, m, v], sem=("parallel",), jobs=jobs)


def sum_parts(parts, name):
    _, R, D = parts.shape

    def body(p_ref, o_ref):
        g = p_ref[0]
        for r in range(1, N_DEV):
            g = g + p_ref[r]
        o_ref[...] = g

    return _call(body, name=name, in_specs=[VMEM_SPEC], out_specs=[VMEM_SPEC],
                 out_shape=[jax.ShapeDtypeStruct((R, D), F32)], args=[parts])[0]


def adamw_small(w, g, m, v, name):
    def body(w_ref, g_ref, m_ref, v_ref, d_ref, nm_ref, nv_ref):
        d, nm, nv = _adamw(w_ref[...], g_ref[...], m_ref[...], v_ref[...])
        d_ref[...] = d
        nm_ref[...] = nm
        nv_ref[...] = nv

    return _call(body, name=name, in_specs=[VMEM_SPEC] * 4, out_specs=[VMEM_SPEC] * 3,
                 out_shape=[jax.ShapeDtypeStruct(w.shape, F32)] * 3, args=[w, g, m, v])


def _kv_row_pieces(K):
    return (0, K // 2), (K // 2, K)


def _row(v, at):
    return jnp.pad(v.reshape(1, -1), ((at, 7 - at), (0, 0)))


def _pack_sharded(norm, conv_w, conv_b, ln_g, ln_b):
    n = norm.shape[-1]
    taps = jnp.pad(conv_w.reshape(CONV_WIDTH, n), ((0, HALO - CONV_WIDTH), (0, 0)))
    return jnp.concatenate([_row(norm, 0), _row(ln_g, 0) + _row(ln_b, 1) + _row(conv_b, 2), taps], axis=0)


def _pack_rel(rel_bias, D):
    return jnp.pad(rel_bias.reshape(1, -1), ((0, 7), (0, D - rel_bias.size)))


def _pack_replicated(kv_norm, b_norm, final_norm, rel_bias, D):
    return jnp.concatenate([_row(kv_norm, 0) + _row(b_norm, 1), _row(final_norm, 0), _pack_rel(rel_bias, D)], axis=0)


def kernel(x, a_norm, a_w_in, a_conv_w, a_conv_b, a_ln_g, a_ln_b, a_w_out, kv_norm, w_kv, b_norm, b_w_in, b_w_out, rel_bias, final_norm, loss_target, m_a_norm, m_a_w_in, m_a_conv_w, m_a_conv_b, m_a_ln_g, m_a_ln_b, m_a_w_out, m_kv_norm, m_w_kv, m_b_norm, m_b_w_in, m_b_w_out, m_rel_bias, m_final_norm, v_a_norm, v_a_w_in, v_a_conv_w, v_a_conv_b, v_a_ln_g, v_a_ln_b, v_a_w_out, v_kv_norm, v_w_kv, v_b_norm, v_b_w_in, v_b_w_out, v_rel_bias, v_final_norm):
    _, S, D = x.shape
    E = D
    H = D // HEAD_DIM
    nsh = D // N_DEV
    x0 = x.reshape(S, D)
    target = loss_target.reshape(S, D)
    kvn, bn, fn = kv_norm.reshape(1, D), b_norm.reshape(1, D), final_norm.reshape(1, D)

    names = ["a_w_in", "a_w_out", "w_kv", "b_w_in", "b_w_out"]
    big_w = dict(zip(names, [a_w_in[0], a_w_out[0], w_kv, b_w_in[0], b_w_out[0]]))
    big_m = dict(zip(names, [m_a_w_in[0], m_a_w_out[0], m_w_kv, m_b_w_in[0], m_b_w_out[0]]))
    big_v = dict(zip(names, [v_a_w_in[0], v_a_w_out[0], v_w_kv, v_b_w_in[0], v_b_w_out[0]]))
    sh_w = _pack_sharded(a_norm, a_conv_w, a_conv_b, a_ln_g, a_ln_b)

    wa_in, sh_all = all_gather([big_w["a_w_in"].astype(BF16), sh_w], "gather_first")
    sh_full = sh_all.transpose(1, 0, 2).reshape(SMALL_SH_ROWS, D)
    an, cw, cb = sh_full[0:1], sh_full[16:16 + HALO], sh_full[10:11]
    lg, lb = sh_full[8:9], sh_full[9:10]

    buf = {nm: place_shard(big_w[nm], "place_" + nm) for nm in names[1:]}
    kv_cut = _kv_row_pieces(D)

    riding = []

    def ride(nm, **kw):
        make = gather_relay_job if set(kw) & {"near", "far", "last"} else gather_job
        riding.append((nm, make(buf[nm], **kw)))
        return riding[-1][1]

    def landed():
        while riding:
            nm, job = riding.pop()
            buf[nm] = job.results[0]

    (h0,) = rms_fwd(x0, [an], "rms_a")
    proj = matmul_nn(h0, wa_in, "a_in",
                     jobs=[ride("a_w_out", chips=(0, E // N_DEV)), ride("w_kv", near=kv_cut[0])])
    landed()
    c, y = conf_fwd(proj, cw, cb, lg, lb, "conf_fwd",
                    jobs=[ride("a_w_out", sibling=(0, E // N_DEV)),
                          ride("w_kv", far=kv_cut[0], near=kv_cut[1])])
    landed()
    wa_out = buf["a_w_out"].reshape(1, E, D)
    x1 = matmul_nn(y, wa_out, "a_out", res=x0, jobs=[ride("w_kv", last=kv_cut[0], far=kv_cut[1])])
    landed()
    hk, hb = rms_fwd(x1, [kvn, bn], "rms_b", jobs=[ride("w_kv", last=kv_cut[1])])
    landed()
    wkv = buf["w_kv"]
    kv = matmul_nn(hk, wkv, "kv_proj", out_dtype=BF16, kv_split=True, jobs=[ride("b_w_in", chips=(0, D))])
    landed()

    bt = jnp.asarray(bucket_tables())
    onehot = (bt[None, :] == jnp.arange(N_BUCKETS, dtype=jnp.int32)[:, None]).astype(F32)
    bias = small_dot(rel_bias.T, onehot, "nn", "bias_table", jobs=[ride("b_w_in", sibling=(0, D))])
    landed()
    bias = bias.reshape(H, N_GROUPS, BLOCK, 2 * BLOCK)
    bias = [bias[:, g] for g in range(N_GROUPS)]
    wb_in = buf["b_w_in"]
    qz = matmul_nn(hb, wb_in, "b_in", jobs=[ride("b_w_out", chips=(0, D // N_DEV))])
    landed()

    os_, lses = [], []
    for g, (_, dil) in enumerate(GROUPS):
        o_g, l_g = attn_fwd(qz, kv, bias[g], g, dil, "attn_fwd%d" % g,
                            jobs=[ride("b_w_out", sibling=(0, D // N_DEV))] if g == 0 else ())
        landed()
        os_.append(o_g)
        lses.append(l_g)
    wb_out = buf["b_w_out"].reshape(1, D, D)
    o, lse, y2 = merge_fwd(os_, lses, qz, "merge_fwd")
    x2 = matmul_nn(y2, wb_out, "b_out", res=x1)
    dx2, dx2b, fin_acc = final_loss(x2, fn, target, "final_loss")

    dy2 = matmul_nt(dx2b, wb_out, "b_out_dx")
    dwb_out = matmul_tn(y2, dx2b, 1, "b_out_dw").reshape(N_DEV, D // N_DEV, D)
    r_b_out = reduce_sibling_job(dwb_out)
    do, delta, dqz = merge_bwd(dy2, o, qz, "merge_bwd", jobs=[r_b_out])
    p_b_out = add_pairs(dwb_out, r_b_out.results[0], "pairs_b_out")
    r_b_out2 = reduce_chips_job(p_b_out)
    dkv = None
    ds_tabs = []
    for g, (_, dil) in enumerate(GROUPS):
        dqz, dkv, ds_tab = attn_bwd(qz, kv, do, lse, delta, bias[g], dqz, dkv, g, dil,
                                    "attn_bwd%d" % g, jobs=[r_b_out2] if g == 0 else ())
        ds_tabs.append(ds_tab.reshape(H, 2 * BLOCK * BLOCK))
    d_rel = small_dot(onehot, jnp.concatenate(ds_tabs, axis=1), "nt", "bias_grad")
    dw_kv_lo = matmul_tn(hk, dkv, N_DEV, "kv_dw_lo", kv_split=True, k_tiles=(0, 1))
    r_kv_lo = reduce_sibling_job(dw_kv_lo)
    dw_kv_hi = matmul_tn(hk, dkv, N_DEV, "kv_dw_hi", kv_split=True, k_tiles=(1, 2), jobs=[r_kv_lo])
    p_kv_lo = add_pairs(dw_kv_lo, r_kv_lo.results[0], "pairs_kv_lo")
    r_kv_lo2, r_kv_hi = reduce_chips_job(p_kv_lo), reduce_sibling_job(dw_kv_hi)
    dhk = matmul_nt(dkv, wkv, "kv_dx", kv_split=True, jobs=[r_kv_lo2, r_kv_hi])
    p_kv_hi = add_pairs(dw_kv_hi, r_kv_hi.results[0], "pairs_kv_hi")
    r_kv_hi2 = reduce_chips_job(p_kv_hi)
    dwb_in = matmul_tn(hb, dqz, N_DEV, "b_in_dw", jobs=[r_kv_hi2])
    r_b_in = reduce_sibling_job(dwb_in)
    dhb = matmul_nt(dqz, wb_in, "b_in_dx", jobs=[r_b_in])
    p_b_in = add_pairs(dwb_in, r_b_in.results[0], "pairs_b_in")
    r_b_in2 = reduce_chips_job(p_b_in)
    dx1, dx1b, norm_acc = rms_bwd(x1, [dhk, dhb], [kvn, bn], dx2, "rms_b_bwd", True)

    dy = matmul_nt(dx1b, wa_out, "a_out_dx")
    dwa_out = matmul_tn(y, dx1b, 1, "a_out_dw").reshape(N_DEV, E // N_DEV, D)
    r_a_out = reduce_sibling_job(dwa_out)
    dc, dz, ln_acc = conf_bwd_ln(c, dy, proj, lg, lb, "conf_bwd_ln", jobs=[r_a_out])
    p_a_out = add_pairs(dwa_out, r_a_out.results[0], "pairs_a_out")
    r_a_out2 = reduce_chips_job(p_a_out)
    dproj, dcw = conf_bwd_conv(proj, dc, dz, cw, "conf_bwd_conv", jobs=[r_b_in2, r_a_out2])
    share = share_small_job(jnp.concatenate([ln_acc, dcw, norm_acc, fin_acc, _pack_rel(d_rel, D)], axis=0))
    dwa_lo = matmul_tn(h0, dproj, N_DEV, "a_in_dw_lo", k_tiles=(0, 1), jobs=[share])
    r_lo = reduce_sibling_job(dwa_lo)
    dwa_hi = matmul_tn(h0, dproj, N_DEV, "a_in_dw_hi", k_tiles=(1, 2), jobs=[r_lo])
    p_lo = add_pairs(dwa_lo, r_lo.results[0], "pairs_a_in_lo")
    r_lo2, r_hi = reduce_chips_job(p_lo), reduce_sibling_job(dwa_hi)
    dh0 = matmul_nt(dproj, wa_in, "a_in_dx", jobs=[r_lo2, r_hi])
    p_hi = add_pairs(dwa_hi, r_hi.results[0], "pairs_a_in_hi")
    r_hi2 = reduce_chips_job(p_hi)
    grad_x, a_acc = rms_bwd(x0, [dh0], [an], dx1, "rms_a_bwd", False, jobs=[r_hi2])

    share_a = share_small_job(a_acc)
    sums = {"w_kv": [(p_kv_lo, r_kv_lo2), (p_kv_hi, r_kv_hi2)], "b_w_in": [(p_b_in, r_b_in2)], "b_w_out": [(p_b_out, r_b_out2)],
            "a_w_out": [(p_a_out, r_a_out2)], "a_w_in": [(p_lo, r_lo2), (p_hi, r_hi2)]}
    big_out = {}
    for nm, pieces in sums.items():
        big_out[nm] = adamw_reduce([(part, job.results[0]) for part, job in pieces],
                                   big_w[nm], big_m[nm], big_v[nm], "adamw_" + nm,
                                   jobs=[share_a] if nm == "a_w_out" else ())

    gsum = jnp.concatenate([sum_parts(share_a.results[0], "sum_small_a"),
                            sum_parts(share.results[0], "sum_small")], axis=0)
    me = 4 * lax.axis_index("x") + 2 * lax.axis_index("y") + lax.axis_index("c")
    g_sh = lax.dynamic_slice(gsum, (0, me * nsh), (SMALL_SH_ROWS, nsh))
    g_rep = gsum[SMALL_SH_ROWS:]
    loss = g_rep[9, 0]
    sh_m = _pack_sharded(m_a_norm, m_a_conv_w, m_a_conv_b, m_a_ln_g, m_a_ln_b)
    sh_v = _pack_sharded(v_a_norm, v_a_conv_w, v_a_conv_b, v_a_ln_g, v_a_ln_b)
    sh_d, sh_nm, sh_nv = adamw_small(sh_w, g_sh, sh_m, sh_v, "adamw_sharded")
    rep_w = _pack_replicated(kv_norm, b_norm, final_norm, rel_bias, D)
    rep_m = _pack_replicated(m_kv_norm, m_b_norm, m_final_norm, m_rel_bias, D)
    rep_v = _pack_replicated(v_kv_norm, v_b_norm, v_final_norm, v_rel_bias, D)
    rep_d, rep_nm, rep_nv = adamw_small(rep_w, g_rep, rep_m, rep_v, "adamw_replicated")

    def unpack(kind):
        sh = (g_sh, sh_d, sh_nm, sh_nv)[kind]
        rep = (g_rep, rep_d, rep_nm, rep_nv)[kind]
        big = {nm: big_out[nm][kind] for nm in names}
        nb = rel_bias.size
        return [
            sh[0:1],
            big["a_w_in"][None],
            sh[16:16 + CONV_WIDTH][None],
            sh[10:11], sh[8:9], sh[9:10],
            big["a_w_out"][None],
            rep[0],
            big["w_kv"],
            rep[1:2],
            big["b_w_in"][None],
            big["b_w_out"][None],
            rep[16, :nb].reshape(rel_bias.shape),
            rep[8],
        ]

    return (loss, grad_x.reshape(1, S, D), *unpack(0), *unpack(1), *unpack(2), *unpack(3))
```

```python
import functools
import math

import numpy as np
import jax
import jax.numpy as jnp
from jax import lax
from jax.experimental import pallas as pl
from jax.experimental.pallas import tpu as pltpu

F32 = jnp.float32
BF16 = jnp.bfloat16

N_DEV = 8
N_CHIPS = 4
EPS = 1e-6
HEAD_DIM = 128
BLOCK = 128
GROUPS = ((128, 1), (512, 4), (2048, 16))
N_GROUPS = len(GROUPS)
CONV_WIDTH = 31
HALO = 32
N_BUCKETS = 32
MAX_EXACT = N_BUCKETS // 2
MAX_DISTANCE = 2048
V7X_VMEM_BYTES = 64 * 1024 * 1024
VMEM_LIMIT = (V7X_VMEM_BYTES * 7) // 8
MATMUL_VMEM_BUDGET = (V7X_VMEM_BYTES * 11) // 16
LANE = 128

ADAM_LR = 0.001
ADAM_B1 = 0.9
ADAM_B2 = 0.999
ADAM_EPS = 1e-08
ADAM_WD = 0.01
ADAM_STEP = 10

SMALL_SH_ROWS = 48
SMALL_REP_ROWS = 24
MESH = pl.DeviceIdType.MESH
ANY_SPEC = pl.BlockSpec(memory_space=pl.ANY)
VMEM_SPEC = pl.BlockSpec(memory_space=pltpu.VMEM)


def _tile(dim, pref, unit=LANE):
    if dim <= pref:
        return dim
    t = (pref // unit) * unit
    while dim % t:
        t -= unit
    assert t > 0
    return t


def _sigmoid(v):
    return jax.nn.sigmoid(v)


def _place():
    return lax.axis_index("x"), lax.axis_index("y"), lax.axis_index("c")


def _flip(v, bit):
    return 1 - v if bit else v


class Job:
    def __init__(self, srcs, dsts, n_sems, build):
        self.srcs, self.dsts, self.n_sems, self.build = list(srcs), list(dsts), n_sems, build
        self.results = None


def _remote(src, dst, send_sems, recv_sems, k, peer):
    return pltpu.make_async_remote_copy(src_ref=src, dst_ref=dst, send_sem=send_sems.at[k],
                                        recv_sem=recv_sems.at[k], device_id=peer, device_id_type=MESH)


def _call(body, *, name, in_specs, out_specs, out_shape, args, grid=(), scratch_shapes=(), sem=(),
          aliases=None, jobs=()):
    n_in, n_out, n_scr = len(in_specs), len(out_specs), len(scratch_shapes)
    aliases = dict(aliases or {})
    x_in, x_out, x_scr = [], [], []
    for job in jobs:
        job.in_at = n_in + len(x_in)
        x_in += job.srcs
        job.out_at = n_out + len(x_out)
        for d in job.dsts:
            if not isinstance(d, jax.ShapeDtypeStruct):
                aliases[n_in + len(x_in)] = n_out + len(x_out)
                x_in.append(d)
            x_out.append(jax.ShapeDtypeStruct(d.shape, d.dtype))
        job.scr_at = n_scr + len(x_scr)
        x_scr += [pltpu.SemaphoreType.DMA((job.n_sems,))] * 3

    def wrapped(*refs):
        ins = refs[:n_in + len(x_in)]
        outs = refs[len(ins):len(ins) + n_out + len(x_out)]
        scr = refs[len(ins) + len(outs):]
        core = ins[:n_in] + outs[:n_out] + scr[:n_scr]
        if not jobs:
            body(*core)
            return
        copies = []
        for job in jobs:
            copies += job.build(ins[job.in_at:job.in_at + len(job.srcs)],
                                outs[job.out_at:job.out_at + len(job.dsts)],
                                *scr[job.scr_at:job.scr_at + 3])
        if grid:
            pids = [pl.program_id(d) for d in range(len(grid))]
            first = functools.reduce(jnp.logical_and, [p == 0 for p in pids])
            last = functools.reduce(jnp.logical_and, [p == g - 1 for p, g in zip(pids, grid)])

            @pl.when(first)
            def _():
                for cp in copies:
                    cp.start()

            body(*core)

            @pl.when(last)
            def _():
                for cp in copies:
                    cp.wait()
        else:
            for cp in copies:
                cp.start()
            body(*core)
            for cp in copies:
                cp.wait()

    if jobs:
        sem = ("arbitrary",) * len(grid)
    kwargs = dict(grid=grid) if grid else {}
    if aliases:
        kwargs["input_output_aliases"] = aliases
    outs = pl.pallas_call(
        wrapped, name=name,
        in_specs=list(in_specs) + [ANY_SPEC] * len(x_in),
        out_specs=list(out_specs) + [ANY_SPEC] * len(x_out),
        out_shape=list(out_shape) + x_out,
        scratch_shapes=list(scratch_shapes) + x_scr,
        compiler_params=pltpu.CompilerParams(dimension_semantics=sem if sem else None,
                                             vmem_limit_bytes=VMEM_LIMIT),
        **kwargs,
    )(*args, *x_in)
    for job in jobs:
        job.results = list(outs[job.out_at:job.out_at + len(job.dsts)])
    return list(outs[:n_out])


def place_shard(w, name):
    K, n = w.shape
    tk = _tile(K, 512, 16)

    def body(p_ref, w_ref, o_ref):
        o_ref[...] = w_ref[...].astype(BF16)

    grid_spec = pltpu.PrefetchScalarGridSpec(
        num_scalar_prefetch=1, grid=(K // tk,),
        in_specs=[pl.BlockSpec((tk, n), lambda i, p: (i, 0))],
        out_specs=pl.BlockSpec((None, tk, n), lambda i, p: (4 * p[0] + 2 * p[1] + p[2], i, 0)))
    return pl.pallas_call(body, name=name, grid_spec=grid_spec,
                          out_shape=jax.ShapeDtypeStruct((N_DEV, K, n), BF16))(_place_vector(), w)


def _place_vector():
    return jnp.stack(_place()).astype(jnp.int32)


def gather_job(buf, chips=None, sibling=None):
    def build(srcs, dsts, send, recv, loc):
        (out,) = dsts
        x, y, c = _place()
        copies = []
        if chips is not None:
            mine = out.at[4 * x + 2 * y + c, pl.ds(chips[0], chips[1] - chips[0])]
            peers = [(x, y, 1 - c), (1 - x, y, c), (x, 1 - y, c), (1 - x, 1 - y, c)]
            copies += [_remote(mine, mine, send, recv, k, p) for k, p in enumerate(peers)]
        if sibling is not None:
            for k, (cx, cy) in enumerate([(1 - x, y), (x, 1 - y), (1 - x, 1 - y)]):
                blk = out.at[4 * cx + 2 * cy + c, pl.ds(sibling[0], sibling[1] - sibling[0])]
                copies.append(_remote(blk, blk, send, recv, 4 + k, (x, y, 1 - c)))
        return copies

    return Job([], [buf], 7, build)


def _relay(x, y, c):
    return ((x + 1 - c) % 2, (y + c) % 2), ((x + c) % 2, (y + 1 - c) % 2)


def gather_relay_job(buf, near=None, far=None, last=None):
    def rows(slot, rng):
        return slot.at[pl.ds(rng[0], rng[1] - rng[0])]

    def build(srcs, dsts, send, recv, loc):
        (out,) = dsts
        x, y, c = _place()
        sib = (x, y, 1 - c)
        copies = []
        if near is not None:
            mine = rows(out.at[4 * x + 2 * y + c], near)
            for k, p in enumerate([sib, (1 - x, y, c), (x, 1 - y, c)]):
                copies.append(_remote(mine, mine, send, recv, k, p))
        if far is not None:
            (fx, fy), (tx, ty) = _relay(x, y, c)
            blk = rows(out.at[4 * fx + 2 * fy + c], far)
            copies.append(_remote(blk, blk, send, recv, 3, (tx, ty, c)))
            for k, (cx, cy) in enumerate([(1 - x, y), (x, 1 - y)]):
                blk = rows(out.at[4 * cx + 2 * cy + c], far)
                copies.append(_remote(blk, blk, send, recv, 4 + k, sib))
        if last is not None:
            blk = rows(out.at[4 * (1 - x) + 2 * (1 - y) + c], last)
            copies.append(_remote(blk, blk, send, recv, 6, sib))
        return copies

    return Job([], [buf], 7, build)


def reduce_sibling_job(dw):
    def build(srcs, dsts, send, recv, loc):
        (src,), (got,) = srcs, dsts
        x, y, c = _place()
        return [_remote(src.at[4 * _flip(x, r & 2) + 2 * _flip(y, r & 1) + 1 - c], got.at[r], send, recv, r,
                        (x, y, 1 - c)) for r in range(N_CHIPS)]

    return Job([dw], [jax.ShapeDtypeStruct((N_CHIPS,) + dw.shape[1:], dw.dtype)], N_CHIPS, build)


def reduce_chips_job(part):
    def build(srcs, dsts, send, recv, loc):
        (src,), (got,) = srcs, dsts
        x, y, c = _place()
        return [_remote(src.at[r], got.at[r - 1], send, recv, r - 1, (_flip(x, r & 2), _flip(y, r & 1), c))
                for r in range(1, N_CHIPS)]

    return Job([part], [jax.ShapeDtypeStruct((N_CHIPS - 1,) + part.shape[1:], part.dtype)], N_CHIPS - 1, build)


def share_small_job(small):
    def build(srcs, dsts, send, recv, loc):
        (src,), (out,) = srcs, dsts
        x, y, c = _place()
        mine = out.at[4 * x + 2 * y + c]
        copies = [pltpu.make_async_copy(src, mine, loc.at[0])]
        for rel in range(1, N_DEV):
            peer = (_flip(x, rel & 4), _flip(y, rel & 2), _flip(c, rel & 1))
            copies.append(_remote(src, mine, send, recv, rel - 1, peer))
        return copies

    return Job([small], [jax.ShapeDtypeStruct((N_DEV,) + small.shape, small.dtype)], N_DEV - 1, build)


def all_gather(shards, name):
    n = len(shards)

    def body(*refs):
        ins, outs = refs[:n], refs[n:2 * n]
        send_sems, recv_sems, local_sems = refs[2 * n:]
        x, y, c = _place()
        me, sibling = (x, y, c), (x, y, 1 - c)
        near = [(1 - x, y), (x, 1 - y)]
        diag = (1 - x, 1 - y)
        frm, to = _relay(x, y, c)

        def slot(a, dev):
            return outs[a].at[4 * dev[0] + 2 * dev[1] + dev[2]]

        def copy(a, k, block, to_dev, src=None):
            return pltpu.make_async_remote_copy(
                src_ref=slot(a, block) if src is None else src, dst_ref=slot(a, block),
                send_sem=send_sems.at[a, k], recv_sem=recv_sems.at[a, k],
                device_id=to_dev, device_id_type=MESH)

        mine, sent = [], []
        for a in range(n):
            mine.append(pltpu.make_async_copy(ins[a], slot(a, me), local_sems.at[a]))
            mine[a].start()
            sent.append([copy(a, 0, me, sibling, src=ins[a])]
                        + [copy(a, 1 + j, me, (*chip, c), src=ins[a]) for j, chip in enumerate(near)])
            for cp in sent[a]:
                cp.start()
        for a in range(n):
            for j, chip in enumerate(near):
                copy(a, 1 + j, (*chip, c), me).wait_recv()
            more = [copy(a, 3, (*frm, c), (*to, c))] + [copy(a, 4 + j, (*chip, c), sibling)
                                                       for j, chip in enumerate(near)]
            for cp in more:
                cp.start()
            sent[a] += more
        for a in range(n):
            copy(a, 3, (*diag, c), me).wait_recv()
            sent[a].append(copy(a, 6, (*diag, c), sibling))
            sent[a][-1].start()
        for a in range(n):
            for k, block in [(0, sibling), (4, (*near[0], 1 - c)), (5, (*near[1], 1 - c)), (6, (*diag, 1 - c))]:
                copy(a, k, block, me).wait_recv()
            for cp in sent[a]:
                cp.wait_send()
            mine[a].wait()

    return pl.pallas_call(
        body, name=name,
        in_specs=[ANY_SPEC] * n, out_specs=[ANY_SPEC] * n,
        out_shape=[jax.ShapeDtypeStruct((N_DEV,) + s.shape, s.dtype) for s in shards],
        scratch_shapes=[pltpu.SemaphoreType.DMA((n, 7)), pltpu.SemaphoreType.DMA((n, 7)),
                        pltpu.SemaphoreType.DMA((n,))],
    )(*shards)


def _kv_split_index(tw, D):
    pd = D // tw

    def index(j):
        return (j // pd) % 2, (j // (2 * pd)) * pd + j % pd

    return index


def _col_tile(n, also, pref):
    t = (min(pref, n) // LANE) * LANE
    while n % t or (also is not None and also % t):
        t -= LANE
    assert t > 0
    return t


def matmul_nn(a, w, name, res=None, out_dtype=F32, kv_split=False, jobs=()):
    M, K = a.shape
    nb, _, n = w.shape
    D = nb * n // (2 * N_GROUPS)
    tn = _col_tile(n, D if kv_split else None, 1024)
    out_bytes = jnp.dtype(out_dtype).itemsize + (4 if res is not None else 0)
    tm = _tile(M, 2048)
    if 2 * (tm * K * 2 + K * tn * 2 + tm * tn * out_bytes) > MATMUL_VMEM_BUDGET:
        tm = _tile(M, 1024)
    per = n // tn

    def body(*refs):
        if res is None:
            a_ref, w_ref, o_ref = refs
        else:
            a_ref, w_ref, r_ref, o_ref = refs
        acc = jnp.dot(a_ref[...], w_ref[...], preferred_element_type=F32)
        if res is not None:
            acc = r_ref[...] + acc
        o_ref[...] = acc.astype(o_ref.dtype)

    in_specs = [pl.BlockSpec((tm, K), lambda i, j: (i, 0)),
                pl.BlockSpec((None, K, tn), lambda i, j: (j // per, 0, j % per))]
    args = [a, w]
    if res is not None:
        in_specs.append(pl.BlockSpec((tm, tn), lambda i, j: (i, j)))
        args.append(res)
    if kv_split:
        split = _kv_split_index(tn, D)
        out_spec = pl.BlockSpec((None, tm, tn), lambda i, j: (split(j)[0], i, split(j)[1]))
        out_shape = jax.ShapeDtypeStruct((2, M, N_GROUPS * D), out_dtype)
    else:
        out_spec = pl.BlockSpec((tm, tn), lambda i, j: (i, j))
        out_shape = jax.ShapeDtypeStruct((M, nb * n), out_dtype)
    return _call(body, name=name, grid=(M // tm, nb * per), in_specs=in_specs, out_specs=[out_spec],
                 out_shape=[out_shape], args=args, sem=("parallel", "parallel"), jobs=jobs)[0]


def matmul_nt(dy, w, name, kv_split=False, rows=None, into=None, jobs=()):
    M = dy.shape[-2]
    nb, K, n = w.shape
    D = nb * n // (2 * N_GROUPS)
    tm = _tile(M, 1024)
    tc = _col_tile(n, D if kv_split else None, 1024)
    per = n // tc
    pair = max(u for u in (1, 2, 4) if (nb * per) % u == 0 and u * tc <= 2048)
    p, parts = rows or (0, 1)
    mt = M // tm // parts
    i0 = p * mt

    def body(*refs):
        o_ref = refs[-1]
        j = pl.program_id(1)
        lhs = jnp.concatenate([refs[u][...] for u in range(pair)], axis=1)
        rhs = jnp.concatenate([refs[pair + u][...] for u in range(pair)], axis=1)
        part = lax.dot_general(lhs, rhs, (((1,), (1,)), ((), ())), preferred_element_type=F32)

        @pl.when(j == 0)
        def _():
            o_ref[...] = part

        @pl.when(j > 0)
        def _():
            o_ref[...] += part

    def dy_spec(u):
        if kv_split:
            split = _kv_split_index(tc, D)
            return pl.BlockSpec((None, tm, tc),
                                lambda i, j: (split(pair * j + u)[0], i0 + i, split(pair * j + u)[1]))
        return pl.BlockSpec((tm, tc), lambda i, j: (i0 + i, pair * j + u))

    def w_spec(u):
        return pl.BlockSpec((None, K, tc), lambda i, j: ((pair * j + u) // per, 0, (pair * j + u) % per))

    in_specs = [dy_spec(u) for u in range(pair)] + [w_spec(u) for u in range(pair)]
    args = [dy] * pair + [w] * pair
    aliases = None
    if into is not None:
        aliases = {len(args): 0}
        in_specs.append(ANY_SPEC)
        args.append(into)
    return _call(body, name=name, grid=(mt, nb * per // pair), in_specs=in_specs,
                 out_specs=[pl.BlockSpec((tm, K), lambda i, j: (i0 + i, 0))],
                 out_shape=[jax.ShapeDtypeStruct((M, K), F32)], args=args, aliases=aliases,
                 sem=("parallel", "arbitrary"), jobs=jobs)[0]


def matmul_tn(a, dy, nb, name, out_dtype=BF16, kv_split=False, k_tiles=None, jobs=()):
    M, K = a.shape
    N = 2 * dy.shape[-1] if kv_split else dy.shape[-1]
    n = N // nb
    D = N // (2 * N_GROUPS)
    tn = _col_tile(n, D if kv_split else None, 1024)
    per = n // tn
    tk = _tile(K, 1024)

    def body(a_ref, dy_ref, o_ref):
        o_ref[...] = lax.dot_general(a_ref[...], dy_ref[...], (((0,), (0,)), ((), ())),
                                     preferred_element_type=F32).astype(o_ref.dtype)

    if kv_split:
        split = _kv_split_index(tn, D)
        dy_spec = pl.BlockSpec((None, M, tn), lambda k, j: (split(j)[0], 0, split(j)[1]))
    else:
        dy_spec = pl.BlockSpec((M, tn), lambda k, j: (0, j))
    k0, k1 = k_tiles or (0, K // tk)
    return _call(body, name=name, grid=(k1 - k0, nb * per),
                 in_specs=[pl.BlockSpec((M, tk), lambda k, j: (0, k0 + k)), dy_spec],
                 out_specs=[pl.BlockSpec((None, tk, tn), lambda k, j: (j // per, k, j % per))],
                 out_shape=[jax.ShapeDtypeStruct((nb, (k1 - k0) * tk, n), out_dtype)], args=[a, dy],
                 sem=("parallel", "parallel"), jobs=jobs)[0]


def small_dot(a, b, contract, name, jobs=()):
    if contract == "nn":
        dims = (((1,), (0,)), ((), ()))
        out = (a.shape[0], b.shape[1])
    else:
        dims = (((1,), (1,)), ((), ()))
        out = (a.shape[0], b.shape[0])

    def body(a_ref, b_ref, o_ref):
        o_ref[...] = lax.dot_general(a_ref[...], b_ref[...], dims, precision=lax.Precision.HIGHEST,
                                     preferred_element_type=F32)

    return _call(body, name=name, in_specs=[VMEM_SPEC, VMEM_SPEC], out_specs=[VMEM_SPEC],
                 out_shape=[jax.ShapeDtypeStruct(out, F32)], args=[a, b], jobs=jobs)[0]


def rms_fwd(x, gains, name, jobs=()):
    S, D = x.shape
    T = _tile(S, 512, 8)
    n = len(gains)

    def body(x_ref, *refs):
        xv = x_ref[...]
        xn = xv * lax.rsqrt(jnp.mean(xv * xv, axis=-1, keepdims=True) + EPS)
        for g_ref, o_ref in zip(refs[:n], refs[n:]):
            o_ref[...] = (xn * g_ref[...]).astype(o_ref.dtype)

    row = pl.BlockSpec((T, D), lambda i: (i, 0))
    vec = pl.BlockSpec((1, D), lambda i: (0, 0))
    return _call(body, name=name, grid=(S // T,), in_specs=[row] + [vec] * n, out_specs=[row] * n,
                 out_shape=[jax.ShapeDtypeStruct((S, D), BF16)] * n, args=[x, *gains],
                 sem=("parallel",), jobs=jobs)


def rms_bwd(x, dhs, gains, dres, name, want_bf16, jobs=()):
    S, D = x.shape
    T = _tile(S, 256, 8)
    n = len(gains)

    def body(x_ref, *refs):
        dh_refs = refs[:n]
        g_refs = refs[n:2 * n]
        dres_ref = refs[2 * n]
        outs = refs[2 * n + 1:]
        dx_ref, dg_ref = outs[0], outs[-1]
        i = pl.program_id(0)

        @pl.when(i == 0)
        def _():
            dg_ref[...] = jnp.zeros_like(dg_ref)

        xv = x_ref[...]
        r = lax.rsqrt(jnp.mean(xv * xv, axis=-1, keepdims=True) + EPS)
        xn = xv * r
        dxn = jnp.zeros_like(xv)
        for k in range(n):
            dh = dh_refs[k][...]
            dg_ref[k:k + 1, :] += jnp.sum(dh * xn, axis=0, keepdims=True)
            dxn = dxn + dh * g_refs[k][...]
        dx = dres_ref[...] + r * (dxn - xn * jnp.mean(dxn * xn, axis=-1, keepdims=True))
        dx_ref[...] = dx
        if want_bf16:
            outs[1][...] = dx.astype(BF16)

    row = pl.BlockSpec((T, D), lambda i: (i, 0))
    vec = pl.BlockSpec((1, D), lambda i: (0, 0))
    acc = pl.BlockSpec((8, D), lambda i: (0, 0))
    out_specs = [row] + ([row] if want_bf16 else []) + [acc]
    out_shape = ([jax.ShapeDtypeStruct((S, D), F32)]
                 + ([jax.ShapeDtypeStruct((S, D), BF16)] if want_bf16 else [])
                 + [jax.ShapeDtypeStruct((8, D), F32)])
    return _call(body, name=name, grid=(S // T,), in_specs=[row] + [row] * n + [vec] * n + [row],
                 out_specs=out_specs, out_shape=out_shape, args=[x, *dhs, *gains, dres],
                 sem=("arbitrary",), jobs=jobs)


def final_loss(x2, gain, target, name, jobs=()):
    S, D = x2.shape
    T = _tile(S, 256, 8)

    def body(x_ref, g_ref, t_ref, dx_ref, dxb_ref, acc_ref):
        i = pl.program_id(0)

        @pl.when(i == 0)
        def _():
            acc_ref[...] = jnp.zeros_like(acc_ref)

        xv = x_ref[...]
        g = g_ref[...]
        r = lax.rsqrt(jnp.mean(xv * xv, axis=-1, keepdims=True) + EPS)
        xn = xv * r
        err = xn * g - t_ref[...]
        dy = err * (1.0 / D)
        acc_ref[0:1, :] += jnp.sum(dy * xn, axis=0, keepdims=True)
        acc_ref[1:2, :] += jnp.full((1, D), 0.5 / D, F32) * jnp.sum(err * err)
        dxn = dy * g
        dx = r * (dxn - xn * jnp.mean(dxn * xn, axis=-1, keepdims=True))
        dx_ref[...] = dx
        dxb_ref[...] = dx.astype(BF16)

    row = pl.BlockSpec((T, D), lambda i: (i, 0))
    return _call(body, name=name, grid=(S // T,),
                 in_specs=[row, pl.BlockSpec((1, D), lambda i: (0, 0)), row],
                 out_specs=[row, row, pl.BlockSpec((8, D), lambda i: (0, 0))],
                 out_shape=[jax.ShapeDtypeStruct((S, D), F32), jax.ShapeDtypeStruct((S, D), BF16),
                            jax.ShapeDtypeStruct((8, D), F32)],
                 args=[x2, gain, target], sem=("arbitrary",), jobs=jobs)


ROW_CHUNK = 64
LANE_CHUNK = 512
SUBLANES = 8


def _shifted_copies(buf, sh_scr, l0, lc):
    n = buf.shape[0] - SUBLANES
    for b in range(1, SUBLANES):
        sh_scr[b - 1, 0:n, :] = buf[b:b + n, l0:l0 + lc]


def _shifted(buf, sh_scr, start, rows, l0, lc):
    a8, b = (start // SUBLANES) * SUBLANES, start % SUBLANES
    if b == 0:
        return buf[a8:a8 + rows, l0:l0 + lc]
    return sh_scr[b - 1, a8:a8 + rows, :]


def conf_fwd(proj, cw, cb, lg, lb, name, jobs=()):
    S, E3 = proj.shape
    E = E3 // 3
    T = _tile(S, 256, HALO)
    R = T // HALO
    lc = _tile(E, LANE_CHUNK)
    rc = min(ROW_CHUNK, T)

    def body(a_ref, b_ref, z_ref, ap_ref, bp_ref, cw_ref, cb_ref, lg_ref, lb_ref, c_ref, y_ref, u_scr, sh_scr):
        i = pl.program_id(0)
        up = ap_ref[...] * _sigmoid(bp_ref[...])
        u_scr[0:HALO, :] = jnp.where(i > 0, up, 0.0)
        u_scr[HALO:HALO + T, :] = a_ref[...] * _sigmoid(b_ref[...])
        off = HALO - (CONV_WIDTH - 1)
        for l0 in range(0, E, lc):
            _shifted_copies(u_scr, sh_scr, l0, lc)
            for r0 in range(0, T, rc):
                acc = jnp.broadcast_to(cb_ref[:, l0:l0 + lc], (rc, lc))
                for k in range(CONV_WIDTH):
                    acc = acc + _shifted(u_scr, sh_scr, r0 + off + k, rc, l0, lc) * cw_ref[k:k + 1, l0:l0 + lc]
                c_ref[r0:r0 + rc, l0:l0 + lc] = acc
        c = c_ref[...]
        mu = jnp.mean(c, axis=-1, keepdims=True)
        d = c - mu
        var = jnp.mean(d * d, axis=-1, keepdims=True)
        cn = d * lax.rsqrt(var + EPS) * lg_ref[...] + lb_ref[...]
        z = z_ref[...]
        y_ref[...] = ((cn * _sigmoid(cn)) * (z * _sigmoid(z))).astype(BF16)

    def col(j):
        return pl.BlockSpec((T, E), lambda i: (i, j))

    def prev(j):
        return pl.BlockSpec((HALO, E), lambda i: (jnp.maximum(i * R - 1, 0), j))

    vec = pl.BlockSpec((1, E), lambda i: (0, 0))
    return _call(body, name=name, grid=(S // T,),
                 in_specs=[col(0), col(1), col(2), prev(0), prev(1),
                           pl.BlockSpec((HALO, E), lambda i: (0, 0)), vec, vec, vec],
                 out_specs=[pl.BlockSpec((T, E), lambda i: (i, 0))] * 2,
                 out_shape=[jax.ShapeDtypeStruct((S, E), F32), jax.ShapeDtypeStruct((S, E), BF16)],
                 scratch_shapes=[pltpu.VMEM((HALO + T, E), F32), pltpu.VMEM((SUBLANES - 1, HALO + T, lc), F32)],
                 args=[proj, proj, proj, proj, proj, cw, cb, lg, lb], sem=("parallel",), jobs=jobs)


def conf_bwd_ln(c, dy, proj, lg, lb, name, jobs=()):
    S, E = c.shape
    T = _tile(S, 256, 8)

    def body(c_ref, dy_ref, z_ref, lg_ref, lb_ref, dc_ref, dz_ref, acc_ref):
        i = pl.program_id(0)

        @pl.when(i == 0)
        def _():
            acc_ref[...] = jnp.zeros_like(acc_ref)

        cv = c_ref[...]
        mu = jnp.mean(cv, axis=-1, keepdims=True)
        d = cv - mu
        var = jnp.mean(d * d, axis=-1, keepdims=True)
        rstd = lax.rsqrt(var + EPS)
        xh = d * rstd
        lgv = lg_ref[...]
        cn = xh * lgv + lb_ref[...]
        z = z_ref[...]
        dy = dy_ref[...]
        sc = _sigmoid(cn)
        sz = _sigmoid(z)
        dcn = dy * (z * sz) * (sc * (1.0 + cn * (1.0 - sc)))
        dz_ref[...] = (dy * (cn * sc) * (sz * (1.0 + z * (1.0 - sz)))).astype(BF16)
        acc_ref[0:1, :] += jnp.sum(dcn * xh, axis=0, keepdims=True)
        acc_ref[1:2, :] += jnp.sum(dcn, axis=0, keepdims=True)
        dxh = dcn * lgv
        dc = rstd * (dxh - jnp.mean(dxh, axis=-1, keepdims=True)
                     - xh * jnp.mean(dxh * xh, axis=-1, keepdims=True))
        acc_ref[2:3, :] += jnp.sum(dc, axis=0, keepdims=True)
        dc_ref[...] = dc

    row = pl.BlockSpec((T, E), lambda i: (i, 0))
    vec = pl.BlockSpec((1, E), lambda i: (0, 0))
    return _call(body, name=name, grid=(S // T,),
                 in_specs=[row, row, pl.BlockSpec((T, E), lambda i: (i, 2)), vec, vec],
                 out_specs=[row, row, pl.BlockSpec((8, E), lambda i: (0, 0))],
                 out_shape=[jax.ShapeDtypeStruct((S, E), F32), jax.ShapeDtypeStruct((S, E), BF16),
                            jax.ShapeDtypeStruct((8, E), F32)],
                 args=[c, dy, proj, lg, lb], sem=("arbitrary",), jobs=jobs)


def conf_bwd_conv(proj, dc, dz, cw, name, jobs=()):
    S, E3 = proj.shape
    E = E3 // 3
    T = _tile(S, 256, HALO)
    R = T // HALO
    nt = S // T
    lc = _tile(E, LANE_CHUNK)
    rc = min(ROW_CHUNK, T)
    rd = min(ROW_CHUNK // 2, T)

    def body(a_ref, b_ref, ap_ref, bp_ref, dc_ref, dcn_ref, dz_ref, cw_ref, o_ref, dw_ref, u_scr, dc_scr, sh_scr,
             dw_scr):
        i = pl.program_id(0)

        @pl.when(i == 0)
        def _():
            dw_scr[...] = jnp.zeros_like(dw_scr)

        a = a_ref[...]
        sb = _sigmoid(b_ref[...])
        up = ap_ref[...] * _sigmoid(bp_ref[...])
        u_scr[0:HALO, :] = jnp.where(i > 0, up, 0.0)
        u_scr[HALO:HALO + T, :] = a * sb
        dc_scr[0:T, :] = dc_ref[...]
        dc_scr[T:T + HALO, :] = jnp.where(i < nt - 1, dcn_ref[...], 0.0)
        off = HALO - (CONV_WIDTH - 1)
        for l0 in range(0, E, lc):
            _shifted_copies(u_scr, sh_scr, l0, lc)
            for r0 in range(0, T, rd):
                dcc = dc_scr[r0:r0 + rd, l0:l0 + lc]
                for k in range(CONV_WIDTH):
                    prod = _shifted(u_scr, sh_scr, r0 + off + k, rd, l0, lc) * dcc
                    dw_scr[k, :, l0:l0 + lc] += jnp.sum(prod.reshape(rd // SUBLANES, SUBLANES, lc), axis=0)
            _shifted_copies(dc_scr, sh_scr, l0, lc)
            for r0 in range(0, T, rc):
                acc = jnp.zeros((rc, lc), F32)
                for k in range(CONV_WIDTH):
                    s0 = r0 + (CONV_WIDTH - 1) - k
                    acc = acc + _shifted(dc_scr, sh_scr, s0, rc, l0, lc) * cw_ref[k:k + 1, l0:l0 + lc]
                av = a[r0:r0 + rc, l0:l0 + lc]
                sv = sb[r0:r0 + rc, l0:l0 + lc]
                o_ref[r0:r0 + rc, l0:l0 + lc] = (acc * sv).astype(BF16)
                o_ref[r0:r0 + rc, E + l0:E + l0 + lc] = (acc * av * sv * (1.0 - sv)).astype(BF16)
        o_ref[:, 2 * E:3 * E] = dz_ref[...]

        @pl.when(i == nt - 1)
        def _():
            dw_ref[...] = jnp.sum(dw_scr[...], axis=1)

    def col(j):
        return pl.BlockSpec((T, E), lambda i: (i, j))

    def prev(j):
        return pl.BlockSpec((HALO, E), lambda i: (jnp.maximum(i * R - 1, 0), j))

    row = pl.BlockSpec((T, E), lambda i: (i, 0))
    nxt = pl.BlockSpec((HALO, E), lambda i: (jnp.minimum((i + 1) * R, S // HALO - 1), 0))
    return _call(body, name=name, grid=(nt,),
                 in_specs=[col(0), col(1), prev(0), prev(1), row, nxt, row,
                           pl.BlockSpec((HALO, E), lambda i: (0, 0))],
                 out_specs=[pl.BlockSpec((T, E3), lambda i: (i, 0)), pl.BlockSpec((HALO, E), lambda i: (0, 0))],
                 out_shape=[jax.ShapeDtypeStruct((S, E3), BF16), jax.ShapeDtypeStruct((HALO, E), F32)],
                 scratch_shapes=[pltpu.VMEM((HALO + T, E), F32), pltpu.VMEM((T + HALO, E), F32),
                                 pltpu.VMEM((SUBLANES - 1, HALO + T, lc), F32),
                                 pltpu.VMEM((HALO, SUBLANES, E), F32)],
                 args=[proj, proj, proj, proj, dc, dc, dz, cw], sem=("arbitrary",), jobs=jobs)


def bucket_tables():
    q = np.arange(BLOCK)[:, None]
    k = np.arange(2 * BLOCK)[None, :]
    out = []
    for window, dil in GROUPS:
        delta = q + BLOCK - k
        valid = (delta >= 0) & (delta <= window // dil)
        dist = np.clip(delta, 0, None) * dil
        large = MAX_EXACT + (np.log(np.maximum(dist, 1).astype(np.float32) / MAX_EXACT)
                             / math.log(MAX_DISTANCE / MAX_EXACT)
                             * (N_BUCKETS - MAX_EXACT)).astype(np.int32)
        large = np.minimum(large, N_BUCKETS - 1)
        bucket = np.where(dist < MAX_EXACT, dist, large)
        out.append(np.where(valid, bucket, -1).reshape(-1))
    return np.concatenate(out).astype(np.int32)


def _band_masks(has_previous):
    ql = lax.broadcasted_iota(jnp.int32, (BLOCK, 2 * BLOCK), 0)
    kk = lax.broadcasted_iota(jnp.int32, (BLOCK, 2 * BLOCK), 1)
    band = (kk >= ql) & (kk <= ql + BLOCK)
    return band, band & ((kk >= BLOCK) | has_previous)


def _dot_nt(a, b):
    return lax.dot_general(a, b, (((1,), (1,)), ((), ())), preferred_element_type=F32)


def _dot_tn(a, b):
    return lax.dot_general(a, b, (((0,), (0,)), ((), ())), preferred_element_type=F32)


ATTN_ROWS = 2048


def _sub(start, size, dil):
    return pl.ds(start, size) if dil == 1 else pl.ds(start, size, stride=dil)


def _attn_geometry(S, dil):
    halo = BLOCK * dil
    rows = max(min(S, ATTN_ROWS), halo)
    return halo, rows, S // rows, rows // halo


MAX_ROW_STRIDE = 8


def _split(dst, src, n, dil, tmp):
    if dil <= MAX_ROW_STRIDE:
        for r in range(dil):
            dst[r] = src[_sub(r, n, dil), :].astype(dst.dtype)
        return
    f, g = 4, dil // 4
    for r1 in range(f):
        tmp[0:n * g, :] = src[_sub(r1, n * g, f), :]
        for r2 in range(g):
            dst[r2 * f + r1] = tmp[_sub(r2, n, g), :].astype(dst.dtype)


def _merge(dst, src, n, dil, tmp):
    if dil <= MAX_ROW_STRIDE:
        for r in range(dil):
            dst[_sub(r, n, dil), :] = src[r]
        return
    f, g = 4, dil // 4
    for r1 in range(f):
        for r2 in range(g):
            tmp[_sub(r2, n, g), :] = src[r2 * f + r1]
        dst[_sub(r1, n * g, f), :] = tmp[0:n * g, :]


def _split_tmp(sub, dil):
    rows = (sub + BLOCK) * (dil // 4) if dil > MAX_ROW_STRIDE else SUBLANES
    return pltpu.VMEM((rows, HEAD_DIM), F32)


def attn_fwd(qz, kv, bias, g, dil, name, jobs=()):
    S = qz.shape[0]
    D = qz.shape[1] // (N_GROUPS + 1)
    H = D // HEAD_DIM
    halo, rows, nsb, nblk = _attn_geometry(S, dil)
    sub = nblk * BLOCK
    scale = HEAD_DIM ** -0.5

    def body(q_ref, kvc_ref, kvp_ref, b_ref, o_ref, l_ref, ks, vs, qd, kd, vd, od, ld, tmp):
        sb = pl.program_id(1)
        ks[0:halo, :] = kvp_ref[0].astype(F32)
        ks[halo:, :] = kvc_ref[0].astype(F32)
        vs[0:halo, :] = kvp_ref[1].astype(F32)
        vs[halo:, :] = kvc_ref[1].astype(F32)
        _split(kd, ks, sub + BLOCK, dil, tmp)
        _split(vd, vs, sub + BLOCK, dil, tmp)
        _split(qd, q_ref, sub, dil, tmp)
        band, first = _band_masks(sb > 0)
        bias_t = b_ref[0]
        for r in range(dil):
            for jj in range(nblk):
                q = qd[r, jj * BLOCK:(jj + 1) * BLOCK, :]
                keys = slice(jj * BLOCK, (jj + 2) * BLOCK)
                s = _dot_nt(q, kd[r, keys, :]) * scale + bias_t
                s = jnp.where(band if jj > 0 else first, s, -jnp.inf)
                m = jnp.max(s, axis=-1, keepdims=True)
                p = jnp.exp(s - m)
                den = jnp.sum(p, axis=-1, keepdims=True)
                pv = jnp.dot(p.astype(BF16), vd[r, keys, :], preferred_element_type=F32)
                own = slice(jj * BLOCK, (jj + 1) * BLOCK)
                od[r, own, :] = pv / den
                ld[r, own, :] = jnp.broadcast_to(m + jnp.log(den), (BLOCK, HEAD_DIM))
        _merge(o_ref, od, sub, dil, tmp)
        _merge(l_ref, ld, sub, dil, tmp)

    per = rows // halo
    out_spec = pl.BlockSpec((rows, HEAD_DIM), lambda h, sb: (sb, h))
    return _call(body, name=name, grid=(H, nsb),
                 in_specs=[pl.BlockSpec((rows, HEAD_DIM), lambda h, sb: (sb, g * H + h)),
                           pl.BlockSpec((2, rows, HEAD_DIM), lambda h, sb: (0, sb, g * H + h)),
                           pl.BlockSpec((2, halo, HEAD_DIM),
                                        lambda h, sb: (0, jnp.maximum(sb * per - 1, 0), g * H + h)),
                           pl.BlockSpec((1, BLOCK, 2 * BLOCK), lambda h, sb: (h, 0, 0))],
                 out_specs=[out_spec, out_spec],
                 out_shape=[jax.ShapeDtypeStruct((S, D), F32)] * 2,
                 scratch_shapes=[pltpu.VMEM((halo + rows, HEAD_DIM), F32)] * 2
                 + [pltpu.VMEM((dil, sub, HEAD_DIM), BF16)]
                 + [pltpu.VMEM((dil, sub + BLOCK, HEAD_DIM), BF16)] * 2
                 + [pltpu.VMEM((dil, sub, HEAD_DIM), F32)] * 2 + [_split_tmp(sub, dil)],
                 args=[qz, kv, kv, bias], sem=("parallel", "arbitrary"), jobs=jobs)


def merge_fwd(os_, lses, qz, name, jobs=()):
    S, D = os_[0].shape
    T = _tile(S, 256, 8)

    def body(o1, o2, o3, l1, l2, l3, z_ref, o_ref, lse_ref, y_ref):
        la, lb_, lc_ = l1[...], l2[...], l3[...]
        m = jnp.maximum(jnp.maximum(la, lb_), lc_)
        ea, eb, ec = jnp.exp(la - m), jnp.exp(lb_ - m), jnp.exp(lc_ - m)
        den = ea + eb + ec
        o = (ea * o1[...] + eb * o2[...] + ec * o3[...]) / den
        z = z_ref[...]
        o_ref[...] = o
        lse_ref[...] = m + jnp.log(den)
        y_ref[...] = (o * (z * _sigmoid(z))).astype(BF16)

    row = pl.BlockSpec((T, D), lambda i: (i, 0))
    return _call(body, name=name, grid=(S // T,),
                 in_specs=[row] * 6 + [pl.BlockSpec((T, D), lambda i: (i, N_GROUPS))],
                 out_specs=[row] * 3,
                 out_shape=[jax.ShapeDtypeStruct((S, D), F32), jax.ShapeDtypeStruct((S, D), F32),
                            jax.ShapeDtypeStruct((S, D), BF16)],
                 args=[*os_, *lses, qz], sem=("parallel",), jobs=jobs)


def merge_bwd(dy2, o, qz, name, jobs=()):
    S, D = o.shape
    H = D // HEAD_DIM
    T = _tile(S, 256, 8)
    nq = N_GROUPS + 1

    def body(dy_ref, o_ref, z_ref, do_ref, dl_ref, dqz_ref):
        dy = dy_ref[...]
        ov = o_ref[...]
        z = z_ref[...]
        sz = _sigmoid(z)
        do = dy * (z * sz)
        do_ref[...] = do.astype(BF16)
        dqz_ref[...] = (dy * ov * (sz * (1.0 + z * (1.0 - sz)))).astype(BF16)
        prod = do * ov
        for h in range(H):
            hs = slice(h * HEAD_DIM, (h + 1) * HEAD_DIM)
            dl_ref[:, hs] = jnp.broadcast_to(jnp.sum(prod[:, hs], axis=-1, keepdims=True), (T, HEAD_DIM))

    row = pl.BlockSpec((T, D), lambda i: (i, 0))
    last = pl.BlockSpec((T, D), lambda i: (i, N_GROUPS))
    return _call(body, name=name, grid=(S // T,), in_specs=[row, row, last], out_specs=[row, row, last],
                 out_shape=[jax.ShapeDtypeStruct((S, D), BF16), jax.ShapeDtypeStruct((S, D), F32),
                            jax.ShapeDtypeStruct((S, nq * D), BF16)],
                 args=[dy2, o, qz], sem=("parallel",), jobs=jobs)


def attn_bwd(qz, kv, do, lse, delta, bias, dqz, dkv, g, dil, name, jobs=()):
    S = qz.shape[0]
    D = qz.shape[1] // (N_GROUPS + 1)
    H = D // HEAD_DIM
    halo, rows, nsb, nblk = _attn_geometry(S, dil)
    sub = nblk * BLOCK
    scale = HEAD_DIM ** -0.5
    have_dkv = dkv is not None

    def body(*refs):
        q_ref, do_ref, l_ref, d_ref, kvc_ref, kvp_ref, b_ref = refs[:7]
        n_in = 7 + 1 + (1 if have_dkv else 0)
        (dq_ref, dkv_ref, ds_ref, ks, vs, dks, dvs, dos, dqs, carry_k, carry_v,
         qd, dod, ld, dd, dqd, kd, vd, dkd, dvd, tmp) = refs[n_in:]
        i = pl.program_id(1)
        sb = nsb - 1 - i

        @pl.when(i == 0)
        def _():
            ds_ref[...] = jnp.zeros_like(ds_ref)

        ks[0:halo, :] = kvp_ref[0].astype(F32)
        ks[halo:, :] = kvc_ref[0].astype(F32)
        vs[0:halo, :] = kvp_ref[1].astype(F32)
        vs[halo:, :] = kvc_ref[1].astype(F32)
        dos[...] = do_ref[...].astype(F32)
        _split(kd, ks, sub + BLOCK, dil, tmp)
        _split(vd, vs, sub + BLOCK, dil, tmp)
        for dst, src in ((qd, q_ref), (dod, dos), (ld, l_ref), (dd, d_ref)):
            _split(dst, src, sub, dil, tmp)
        dkd[...] = jnp.zeros_like(dkd)
        dvd[...] = jnp.zeros_like(dvd)
        band, first = _band_masks(sb > 0)
        bias_t = b_ref[0]
        for r in range(dil):
            for jj in range(nblk):
                own = slice(jj * BLOCK, (jj + 1) * BLOCK)
                keys = slice(jj * BLOCK, (jj + 2) * BLOCK)
                q = qd[r, own, :]
                do = dod[r, own, :]
                k = kd[r, keys, :]
                v = vd[r, keys, :]
                lse = ld[r, own, :]
                dlt = dd[r, own, :]
                s = _dot_nt(q, k) * scale + bias_t - jnp.concatenate([lse, lse], axis=-1)
                p = jnp.where(band if jj > 0 else first, jnp.exp(s), 0.0)
                ds = p * (_dot_nt(do, v) - jnp.concatenate([dlt, dlt], axis=-1))
                ds_ref[0] += ds
                dsb = ds.astype(BF16)
                dqd[r, own, :] = jnp.dot(dsb, k, preferred_element_type=F32) * scale
                dkd[r, keys, :] += _dot_tn(dsb, q) * scale
                dvd[r, keys, :] += _dot_tn(p.astype(BF16), do)
        _merge(dqs, dqd, sub, dil, tmp)
        _merge(dks, dkd, sub + BLOCK, dil, tmp)
        _merge(dvs, dvd, sub + BLOCK, dil, tmp)

        @pl.when(i > 0)
        def _():
            dks[rows:rows + halo, :] += carry_k[...]
            dvs[rows:rows + halo, :] += carry_v[...]

        dkv_ref[0] = dks[halo:, :].astype(BF16)
        dkv_ref[1] = dvs[halo:, :].astype(BF16)
        carry_k[...] = dks[0:halo, :]
        carry_v[...] = dvs[0:halo, :]
        dq_ref[...] = dqs[...].astype(BF16)

    per = rows // halo

    def rev(i):
        return nsb - 1 - i

    bias_spec = pl.BlockSpec((1, BLOCK, 2 * BLOCK), lambda h, i: (h, 0, 0))
    row_spec = pl.BlockSpec((rows, HEAD_DIM), lambda h, i: (rev(i), h))
    in_specs = [pl.BlockSpec((rows, HEAD_DIM), lambda h, i: (rev(i), g * H + h)),
                row_spec, row_spec, row_spec,
                pl.BlockSpec((2, rows, HEAD_DIM), lambda h, i: (0, rev(i), g * H + h)),
                pl.BlockSpec((2, halo, HEAD_DIM),
                             lambda h, i: (0, jnp.maximum(rev(i) * per - 1, 0), g * H + h)),
                bias_spec, ANY_SPEC]
    args = [qz, do, lse, delta, kv, kv, bias, dqz]
    aliases = {7: 0}
    if have_dkv:
        in_specs.append(ANY_SPEC)
        args.append(dkv)
        aliases[8] = 1
    blk = (halo + rows, HEAD_DIM)
    own, keys = (dil, sub, HEAD_DIM), (dil, sub + BLOCK, HEAD_DIM)
    return _call(body, name=name, grid=(H, nsb), in_specs=in_specs,
                 out_specs=[pl.BlockSpec((rows, HEAD_DIM), lambda h, i: (rev(i), g * H + h)),
                            pl.BlockSpec((2, rows, HEAD_DIM), lambda h, i: (0, rev(i), g * H + h)),
                            bias_spec],
                 out_shape=[jax.ShapeDtypeStruct(qz.shape, BF16),
                            jax.ShapeDtypeStruct((2, S, N_GROUPS * D), BF16),
                            jax.ShapeDtypeStruct((H, BLOCK, 2 * BLOCK), F32)],
                 scratch_shapes=[pltpu.VMEM(blk, F32), pltpu.VMEM(blk, F32), pltpu.VMEM(blk, F32),
                                 pltpu.VMEM(blk, F32),
                                 pltpu.VMEM((rows, HEAD_DIM), F32), pltpu.VMEM((rows, HEAD_DIM), F32),
                                 pltpu.VMEM((halo, HEAD_DIM), F32), pltpu.VMEM((halo, HEAD_DIM), F32),
                                 pltpu.VMEM(own, BF16), pltpu.VMEM(own, BF16), pltpu.VMEM(own, F32),
                                 pltpu.VMEM(own, F32), pltpu.VMEM(own, F32),
                                 pltpu.VMEM(keys, BF16), pltpu.VMEM(keys, BF16),
                                 pltpu.VMEM(keys, F32), pltpu.VMEM(keys, F32), _split_tmp(sub, dil)],
                 aliases=aliases, args=args, sem=("parallel", "arbitrary"), jobs=jobs)


def _adamw(w, g, m, v):
    m = ADAM_B1 * m + (1.0 - ADAM_B1) * g
    v = ADAM_B2 * v + (1.0 - ADAM_B2) * (g * g)
    m_hat = m / (1.0 - ADAM_B1 ** ADAM_STEP)
    v_hat = v / (1.0 - ADAM_B2 ** ADAM_STEP)
    delta = -ADAM_LR * (m_hat / (jnp.sqrt(v_hat) + ADAM_EPS) + ADAM_WD * w)
    return delta, m, v


def add_pairs(dw, got, name, jobs=()):
    _, K, n = got.shape
    tk = _tile(K, 512, 16)

    def own_block(r, i, p):
        return 4 * ((p[0] + r // 2) % 2) + 2 * ((p[1] + r % 2) % 2) + p[2], i, 0

    def body(p_ref, a_ref, b_ref, o_ref):
        o_ref[...] = (a_ref[...].astype(F32) + b_ref[...].astype(F32)).astype(o_ref.dtype)

    blk = pl.BlockSpec((None, tk, n), lambda r, i, p: (r, i, 0))
    grid_spec = pltpu.PrefetchScalarGridSpec(
        num_scalar_prefetch=1, grid=(N_CHIPS, K // tk),
        in_specs=[pl.BlockSpec((None, tk, n), own_block), blk], out_specs=blk)
    return pl.pallas_call(body, name=name, grid_spec=grid_spec,
                          out_shape=jax.ShapeDtypeStruct(got.shape, got.dtype))(_place_vector(), dw, got)


def adamw_reduce(pieces, w, m, v, name, jobs=()):
    K, n = w.shape
    kp = K // len(pieces)
    tk = _tile(kp, 256, 8)
    sp = kp // tk

    def body(*refs):
        w_ref, m_ref, v_ref, g_ref, d_ref, nm_ref, nv_ref = refs[2 * len(pieces):]
        i = pl.program_id(0)
        g = None
        for q in range(len(pieces)):
            p_ref, r_ref = refs[2 * q], refs[2 * q + 1]
            gq = p_ref[...].astype(F32)
            for r in range(N_CHIPS - 1):
                gq = gq + r_ref[r].astype(F32)
            g = gq if g is None else jnp.where(i >= q * sp, gq, g)
        d, nm, nv = _adamw(w_ref[...], g, m_ref[...], v_ref[...])
        g_ref[...] = g
        d_ref[...] = d
        nm_ref[...] = nm
        nv_ref[...] = nv

    def piece_specs(q):
        def at(i):
            return jnp.clip(i - q * sp, 0, sp - 1)
        return [pl.BlockSpec((None, tk, n), lambda i: (0, at(i), 0)),
                pl.BlockSpec((N_CHIPS - 1, tk, n), lambda i: (0, at(i), 0))]

    blk = pl.BlockSpec((tk, n), lambda i: (i, 0))
    in_specs, args = [], []
    for q, (part, got) in enumerate(pieces):
        in_specs += piece_specs(q)
        args += [part, got]
    return _call(body, name=name, grid=(K // tk,), in_specs=in_specs + [blk, blk, blk],
                 out_specs=[blk] * 4, out_shape=[jax.ShapeDtypeStruct((K, n), F32)] * 4,
                 args=args + [w, m, v], sem=("parallel",), jobs=jobs)


def sum_parts(parts, name):
    _, R, D = parts.shape

    def body(p_ref, o_ref):
        g = p_ref[0]
        for r in range(1, N_DEV):
            g = g + p_ref[r]
        o_ref[...] = g

    return _call(body, name=name, in_specs=[VMEM_SPEC], out_specs=[VMEM_SPEC],
                 out_shape=[jax.ShapeDtypeStruct((R, D), F32)], args=[parts])[0]


def adamw_small(w, g, m, v, name):
    def body(w_ref, g_ref, m_ref, v_ref, d_ref, nm_ref, nv_ref):
        d, nm, nv = _adamw(w_ref[...], g_ref[...], m_ref[...], v_ref[...])
        d_ref[...] = d
        nm_ref[...] = nm
        nv_ref[...] = nv

    return _call(body, name=name, in_specs=[VMEM_SPEC] * 4, out_specs=[VMEM_SPEC] * 3,
                 out_shape=[jax.ShapeDtypeStruct(w.shape, F32)] * 3, args=[w, g, m, v])


def _kv_row_pieces(K):
    return (0, K // 2), (K // 2, K)


def _row(v, at):
    return jnp.pad(v.reshape(1, -1), ((at, 7 - at), (0, 0)))


def _pack_sharded(norm, conv_w, conv_b, ln_g, ln_b):
    n = norm.shape[-1]
    taps = jnp.pad(conv_w.reshape(CONV_WIDTH, n), ((0, HALO - CONV_WIDTH), (0, 0)))
    return jnp.concatenate([_row(norm, 0), _row(ln_g, 0) + _row(ln_b, 1) + _row(conv_b, 2), taps], axis=0)


def _pack_rel(rel_bias, D):
    return jnp.pad(rel_bias.reshape(1, -1), ((0, 7), (0, D - rel_bias.size)))


def _pack_replicated(kv_norm, b_norm, final_norm, rel_bias, D):
    return jnp.concatenate([_row(kv_norm, 0) + _row(b_norm, 1), _row(final_norm, 0), _pack_rel(rel_bias, D)], axis=0)


def kernel(x, a_norm, a_w_in, a_conv_w, a_conv_b, a_ln_g, a_ln_b, a_w_out, kv_norm, w_kv, b_norm, b_w_in, b_w_out, rel_bias, final_norm, loss_target, m_a_norm, m_a_w_in, m_a_conv_w, m_a_conv_b, m_a_ln_g, m_a_ln_b, m_a_w_out, m_kv_norm, m_w_kv, m_b_norm, m_b_w_in, m_b_w_out, m_rel_bias, m_final_norm, v_a_norm, v_a_w_in, v_a_conv_w, v_a_conv_b, v_a_ln_g, v_a_ln_b, v_a_w_out, v_kv_norm, v_w_kv, v_b_norm, v_b_w_in, v_b_w_out, v_rel_bias, v_final_norm):
    _, S, D = x.shape
    E = D
    H = D // HEAD_DIM
    nsh = D // N_DEV
    x0 = x.reshape(S, D)
    target = loss_target.reshape(S, D)
    kvn, bn, fn = kv_norm.reshape(1, D), b_norm.reshape(1, D), final_norm.reshape(1, D)

    names = ["a_w_in", "a_w_out", "w_kv", "b_w_in", "b_w_out"]
    big_w = dict(zip(names, [a_w_in[0], a_w_out[0], w_kv, b_w_in[0], b_w_out[0]]))
    big_m = dict(zip(names, [m_a_w_in[0], m_a_w_out[0], m_w_kv, m_b_w_in[0], m_b_w_out[0]]))
    big_v = dict(zip(names, [v_a_w_in[0], v_a_w_out[0], v_w_kv, v_b_w_in[0], v_b_w_out[0]]))
    sh_w = _pack_sharded(a_norm, a_conv_w, a_conv_b, a_ln_g, a_ln_b)

    wa_in, sh_all = all_gather([big_w["a_w_in"].astype(BF16), sh_w], "gather_first")
    sh_full = sh_all.transpose(1, 0, 2).reshape(SMALL_SH_ROWS, D)
    an, cw, cb = sh_full[0:1], sh_full[16:16 + HALO], sh_full[10:11]
    lg, lb = sh_full[8:9], sh_full[9:10]

    buf = {nm: place_shard(big_w[nm], "place_" + nm) for nm in names[1:]}
    kv_cut = _kv_row_pieces(D)

    riding = []

    def ride(nm, **kw):
        make = gather_relay_job if set(kw) & {"near", "far", "last"} else gather_job
        riding.append((nm, make(buf[nm], **kw)))
        return riding[-1][1]

    def landed():
        while riding:
            nm, job = riding.pop()
            buf[nm] = job.results[0]

    (h0,) = rms_fwd(x0, [an], "rms_a")
    proj = matmul_nn(h0, wa_in, "a_in",
                     jobs=[ride("a_w_out", chips=(0, E // N_DEV)), ride("w_kv", near=kv_cut[0])])
    landed()
    c, y = conf_fwd(proj, cw, cb, lg, lb, "conf_fwd",
                    jobs=[ride("a_w_out", sibling=(0, E // N_DEV)),
                          ride("w_kv", far=kv_cut[0], near=kv_cut[1])])
    landed()
    wa_out = buf["a_w_out"].reshape(1, E, D)
    x1 = matmul_nn(y, wa_out, "a_out", res=x0, jobs=[ride("w_kv", last=kv_cut[0], far=kv_cut[1])])
    landed()
    hk, hb = rms_fwd(x1, [kvn, bn], "rms_b", jobs=[ride("w_kv", last=kv_cut[1])])
    landed()
    wkv = buf["w_kv"]
    kv = matmul_nn(hk, wkv, "kv_proj", out_dtype=BF16, kv_split=True, jobs=[ride("b_w_in", chips=(0, D))])
    landed()

    bt = jnp.asarray(bucket_tables())
    onehot = (bt[None, :] == jnp.arange(N_BUCKETS, dtype=jnp.int32)[:, None]).astype(F32)
    bias = small_dot(rel_bias.T, onehot, "nn", "bias_table", jobs=[ride("b_w_in", sibling=(0, D))])
    landed()
    bias = bias.reshape(H, N_GROUPS, BLOCK, 2 * BLOCK)
    bias = [bias[:, g] for g in range(N_GROUPS)]
    wb_in = buf["b_w_in"]
    qz = matmul_nn(hb, wb_in, "b_in", jobs=[ride("b_w_out", chips=(0, D // N_DEV))])
    landed()

    os_, lses = [], []
    for g, (_, dil) in enumerate(GROUPS):
        o_g, l_g = attn_fwd(qz, kv, bias[g], g, dil, "attn_fwd%d" % g,
                            jobs=[ride("b_w_out", sibling=(0, D // N_DEV))] if g == 0 else ())
        landed()
        os_.append(o_g)
        lses.append(l_g)
    wb_out = buf["b_w_out"].reshape(1, D, D)
    o, lse, y2 = merge_fwd(os_, lses, qz, "merge_fwd")
    x2 = matmul_nn(y2, wb_out, "b_out", res=x1)
    dx2, dx2b, fin_acc = final_loss(x2, fn, target, "final_loss")

    dy2 = matmul_nt(dx2b, wb_out, "b_out_dx")
    dwb_out = matmul_tn(y2, dx2b, 1, "b_out_dw").reshape(N_DEV, D // N_DEV, D)
    r_b_out = reduce_sibling_job(dwb_out)
    do, delta, dqz = merge_bwd(dy2, o, qz, "merge_bwd", jobs=[r_b_out])
    p_b_out = add_pairs(dwb_out, r_b_out.results[0], "pairs_b_out")
    r_b_out2 = reduce_chips_job(p_b_out)
    dkv = None
    ds_tabs = []
    for g, (_, dil) in enumerate(GROUPS):
        dqz, dkv, ds_tab = attn_bwd(qz, kv, do, lse, delta, bias[g], dqz, dkv, g, dil,
                                    "attn_bwd%d" % g, jobs=[r_b_out2] if g == 0 else ())
        ds_tabs.append(ds_tab.reshape(H, 2 * BLOCK * BLOCK))
    d_rel = small_dot(onehot, jnp.concatenate(ds_tabs, axis=1), "nt", "bias_grad")
    dw_kv_lo = matmul_tn(hk, dkv, N_DEV, "kv_dw_lo", kv_split=True, k_tiles=(0, 1))
    r_kv_lo = reduce_sibling_job(dw_kv_lo)
    dw_kv_hi = matmul_tn(hk, dkv, N_DEV, "kv_dw_hi", kv_split=True, k_tiles=(1, 2), jobs=[r_kv_lo])
    p_kv_lo = add_pairs(dw_kv_lo, r_kv_lo.results[0], "pairs_kv_lo")
    r_kv_lo2, r_kv_hi = reduce_chips_job(p_kv_lo), reduce_sibling_job(dw_kv_hi)
    dhk = matmul_nt(dkv, wkv, "kv_dx", kv_split=True, jobs=[r_kv_lo2, r_kv_hi])
    p_kv_hi = add_pairs(dw_kv_hi, r_kv_hi.results[0], "pairs_kv_hi")
    r_kv_hi2 = reduce_chips_job(p_kv_hi)
    dwb_in = matmul_tn(hb, dqz, N_DEV, "b_in_dw", jobs=[r_kv_hi2])
    r_b_in = reduce_sibling_job(dwb_in)
    dhb = matmul_nt(dqz, wb_in, "b_in_dx", jobs=[r_b_in])
    p_b_in = add_pairs(dwb_in, r_b_in.results[0], "pairs_b_in")
    r_b_in2 = reduce_chips_job(p_b_in)
    dx1, dx1b, norm_acc = rms_bwd(x1, [dhk, dhb], [kvn, bn], dx2, "rms_b_bwd", True)

    dy = matmul_nt(dx1b, wa_out, "a_out_dx")
    dwa_out = matmul_tn(y, dx1b, 1, "a_out_dw").reshape(N_DEV, E // N_DEV, D)
    r_a_out = reduce_sibling_job(dwa_out)
    dc, dz, ln_acc = conf_bwd_ln(c, dy, proj, lg, lb, "conf_bwd_ln", jobs=[r_a_out])
    p_a_out = add_pairs(dwa_out, r_a_out.results[0], "pairs_a_out")
    r_a_out2 = reduce_chips_job(p_a_out)
    dproj, dcw = conf_bwd_conv(proj, dc, dz, cw, "conf_bwd_conv", jobs=[r_b_in2, r_a_out2])
    share = share_small_job(jnp.concatenate([ln_acc, dcw, norm_acc, fin_acc, _pack_rel(d_rel, D)], axis=0))
    dwa_lo = matmul_tn(h0, dproj, N_DEV, "a_in_dw_lo", k_tiles=(0, 1), jobs=[share])
    r_lo = reduce_sibling_job(dwa_lo)
    dwa_hi = matmul_tn(h0, dproj, N_DEV, "a_in_dw_hi", k_tiles=(1, 2), jobs=[r_lo])
    p_lo = add_pairs(dwa_lo, r_lo.results[0], "pairs_a_in_lo")
    r_lo2, r_hi = reduce_chips_job(p_lo), reduce_sibling_job(dwa_hi)
    dh0 = matmul_nt(dproj, wa_in, "a_in_dx_lo", rows=(0, 2), jobs=[r_lo2, r_hi])
    p_hi = add_pairs(dwa_hi, r_hi.results[0], "pairs_a_in_hi")
    r_hi2 = reduce_chips_job(p_hi)
    dh0 = matmul_nt(dproj, wa_in, "a_in_dx_hi", rows=(1, 2), into=dh0, jobs=[r_hi2])
    grad_x, a_acc = rms_bwd(x0, [dh0], [an], dx1, "rms_a_bwd", False)

    share_a = share_small_job(a_acc)
    sums = {"w_kv": [(p_kv_lo, r_kv_lo2), (p_kv_hi, r_kv_hi2)], "b_w_in": [(p_b_in, r_b_in2)], "b_w_out": [(p_b_out, r_b_out2)],
            "a_w_out": [(p_a_out, r_a_out2)], "a_w_in": [(p_lo, r_lo2), (p_hi, r_hi2)]}
    big_out = {}
    for nm, pieces in sums.items():
        big_out[nm] = adamw_reduce([(part, job.results[0]) for part, job in pieces],
                                   big_w[nm], big_m[nm], big_v[nm], "adamw_" + nm,
                                   jobs=[share_a] if nm == "a_w_out" else ())

    gsum = jnp.concatenate([sum_parts(share_a.results[0], "sum_small_a"),
                            sum_parts(share.results[0], "sum_small")], axis=0)
    me = 4 * lax.axis_index("x") + 2 * lax.axis_index("y") + lax.axis_index("c")
    g_sh = lax.dynamic_slice(gsum, (0, me * nsh), (SMALL_SH_ROWS, nsh))
    g_rep = gsum[SMALL_SH_ROWS:]
    loss = g_rep[9, 0]
    sh_m = _pack_sharded(m_a_norm, m_a_conv_w, m_a_conv_b, m_a_ln_g, m_a_ln_b)
    sh_v = _pack_sharded(v_a_norm, v_a_conv_w, v_a_conv_b, v_a_ln_g, v_a_ln_b)
    sh_d, sh_nm, sh_nv = adamw_small(sh_w, g_sh, sh_m, sh_v, "adamw_sharded")
    rep_w = _pack_replicated(kv_norm, b_norm, final_norm, rel_bias, D)
    rep_m = _pack_replicated(m_kv_norm, m_b_norm, m_final_norm, m_rel_bias, D)
    rep_v = _pack_replicated(v_kv_norm, v_b_norm, v_final_norm, v_rel_bias, D)
    rep_d, rep_nm, rep_nv = adamw_small(rep_w, g_rep, rep_m, rep_v, "adamw_replicated")

    def unpack(kind):
        sh = (g_sh, sh_d, sh_nm, sh_nv)[kind]
        rep = (g_rep, rep_d, rep_nm, rep_nv)[kind]
        big = {nm: big_out[nm][kind] for nm in names}
        nb = rel_bias.size
        return [
            sh[0:1],
            big["a_w_in"][None],
            sh[16:16 + CONV_WIDTH][None],
            sh[10:11], sh[8:9], sh[9:10],
            big["a_w_out"][None],
            rep[0],
            big["w_kv"],
            rep[1:2],
            big["b_w_in"][None],
            big["b_w_out"][None],
            rep[16, :nb].reshape(rel_bias.shape),
            rep[8],
        ]

    return (loss, grad_x.reshape(1, S, D), *unpack(0), *unpack(1), *unpack(2), *unpack(3))
```

```python
import functools
import math

import numpy as np
import jax
import jax.numpy as jnp
from jax import lax
from jax.experimental import pallas as pl
from jax.experimental.pallas import tpu as pltpu

F32 = jnp.float32
BF16 = jnp.bfloat16

N_DEV = 8
N_CHIPS = 4
EPS = 1e-6
HEAD_DIM = 128
BLOCK = 128
GROUPS = ((128, 1), (512, 4), (2048, 16))
N_GROUPS = len(GROUPS)
CONV_WIDTH = 31
HALO = 32
N_BUCKETS = 32
MAX_EXACT = N_BUCKETS // 2
MAX_DISTANCE = 2048
V7X_VMEM_BYTES = 64 * 1024 * 1024
VMEM_LIMIT = (V7X_VMEM_BYTES * 7) // 8
MATMUL_VMEM_BUDGET = (V7X_VMEM_BYTES * 11) // 16
LANE = 128

ADAM_LR = 0.001
ADAM_B1 = 0.9
ADAM_B2 = 0.999
ADAM_EPS = 1e-08
ADAM_WD = 0.01
ADAM_STEP = 10

SMALL_SH_ROWS = 48
SMALL_REP_ROWS = 24
MESH = pl.DeviceIdType.MESH
ANY_SPEC = pl.BlockSpec(memory_space=pl.ANY)
VMEM_SPEC = pl.BlockSpec(memory_space=pltpu.VMEM)


def _tile(dim, pref, unit=LANE):
    if dim <= pref:
        return dim
    t = (pref // unit) * unit
    while dim % t:
        t -= unit
    assert t > 0
    return t


def _sigmoid(v):
    return jax.nn.sigmoid(v)


def _place():
    return lax.axis_index("x"), lax.axis_index("y"), lax.axis_index("c")


def _flip(v, bit):
    return 1 - v if bit else v


class Job:
    def __init__(self, srcs, dsts, n_sems, build):
        self.srcs, self.dsts, self.n_sems, self.build = list(srcs), list(dsts), n_sems, build
        self.results = None


def _remote(src, dst, send_sems, recv_sems, k, peer):
    return pltpu.make_async_remote_copy(src_ref=src, dst_ref=dst, send_sem=send_sems.at[k],
                                        recv_sem=recv_sems.at[k], device_id=peer, device_id_type=MESH)


def _call(body, *, name, in_specs, out_specs, out_shape, args, grid=(), scratch_shapes=(), sem=(),
          aliases=None, jobs=()):
    n_in, n_out, n_scr = len(in_specs), len(out_specs), len(scratch_shapes)
    aliases = dict(aliases or {})
    x_in, x_out, x_scr = [], [], []
    for job in jobs:
        job.in_at = n_in + len(x_in)
        x_in += job.srcs
        job.out_at = n_out + len(x_out)
        for d in job.dsts:
            if not isinstance(d, jax.ShapeDtypeStruct):
                aliases[n_in + len(x_in)] = n_out + len(x_out)
                x_in.append(d)
            x_out.append(jax.ShapeDtypeStruct(d.shape, d.dtype))
        job.scr_at = n_scr + len(x_scr)
        x_scr += [pltpu.SemaphoreType.DMA((job.n_sems,))] * 3

    def wrapped(*refs):
        ins = refs[:n_in + len(x_in)]
        outs = refs[len(ins):len(ins) + n_out + len(x_out)]
        scr = refs[len(ins) + len(outs):]
        core = ins[:n_in] + outs[:n_out] + scr[:n_scr]
        if not jobs:
            body(*core)
            return
        copies = []
        for job in jobs:
            copies += job.build(ins[job.in_at:job.in_at + len(job.srcs)],
                                outs[job.out_at:job.out_at + len(job.dsts)],
                                *scr[job.scr_at:job.scr_at + 3])
        if grid:
            pids = [pl.program_id(d) for d in range(len(grid))]
            first = functools.reduce(jnp.logical_and, [p == 0 for p in pids])
            last = functools.reduce(jnp.logical_and, [p == g - 1 for p, g in zip(pids, grid)])

            @pl.when(first)
            def _():
                for cp in copies:
                    cp.start()

            body(*core)

            @pl.when(last)
            def _():
                for cp in copies:
                    cp.wait()
        else:
            for cp in copies:
                cp.start()
            body(*core)
            for cp in copies:
                cp.wait()

    if jobs:
        sem = ("arbitrary",) * len(grid)
    kwargs = dict(grid=grid) if grid else {}
    if aliases:
        kwargs["input_output_aliases"] = aliases
    outs = pl.pallas_call(
        wrapped, name=name,
        in_specs=list(in_specs) + [ANY_SPEC] * len(x_in),
        out_specs=list(out_specs) + [ANY_SPEC] * len(x_out),
        out_shape=list(out_shape) + x_out,
        scratch_shapes=list(scratch_shapes) + x_scr,
        compiler_params=pltpu.CompilerParams(dimension_semantics=sem if sem else None,
                                             vmem_limit_bytes=VMEM_LIMIT),
        **kwargs,
    )(*args, *x_in)
    for job in jobs:
        job.results = list(outs[job.out_at:job.out_at + len(job.dsts)])
    return list(outs[:n_out])


def place_shard(w, name):
    K, n = w.shape
    tk = _tile(K, 512, 16)

    def body(p_ref, w_ref, o_ref):
        o_ref[...] = w_ref[...].astype(BF16)

    grid_spec = pltpu.PrefetchScalarGridSpec(
        num_scalar_prefetch=1, grid=(K // tk,),
        in_specs=[pl.BlockSpec((tk, n), lambda i, p: (i, 0))],
        out_specs=pl.BlockSpec((None, tk, n), lambda i, p: (4 * p[0] + 2 * p[1] + p[2], i, 0)))
    return pl.pallas_call(body, name=name, grid_spec=grid_spec,
                          out_shape=jax.ShapeDtypeStruct((N_DEV, K, n), BF16))(_place_vector(), w)


def _place_vector():
    return jnp.stack(_place()).astype(jnp.int32)


def gather_job(buf, chips=None, sibling=None):
    def build(srcs, dsts, send, recv, loc):
        (out,) = dsts
        x, y, c = _place()
        copies = []
        if chips is not None:
            mine = out.at[4 * x + 2 * y + c, pl.ds(chips[0], chips[1] - chips[0])]
            peers = [(x, y, 1 - c), (1 - x, y, c), (x, 1 - y, c), (1 - x, 1 - y, c)]
            copies += [_remote(mine, mine, send, recv, k, p) for k, p in enumerate(peers)]
        if sibling is not None:
            for k, (cx, cy) in enumerate([(1 - x, y), (x, 1 - y), (1 - x, 1 - y)]):
                blk = out.at[4 * cx + 2 * cy + c, pl.ds(sibling[0], sibling[1] - sibling[0])]
                copies.append(_remote(blk, blk, send, recv, 4 + k, (x, y, 1 - c)))
        return copies

    return Job([], [buf], 7, build)


def _relay(x, y, c):
    return ((x + 1 - c) % 2, (y + c) % 2), ((x + c) % 2, (y + 1 - c) % 2)


def gather_relay_job(buf, near=None, far=None, last=None):
    def rows(slot, rng):
        return slot.at[pl.ds(rng[0], rng[1] - rng[0])]

    def build(srcs, dsts, send, recv, loc):
        (out,) = dsts
        x, y, c = _place()
        sib = (x, y, 1 - c)
        copies = []
        if near is not None:
            mine = rows(out.at[4 * x + 2 * y + c], near)
            for k, p in enumerate([sib, (1 - x, y, c), (x, 1 - y, c)]):
                copies.append(_remote(mine, mine, send, recv, k, p))
        if far is not None:
            (fx, fy), (tx, ty) = _relay(x, y, c)
            blk = rows(out.at[4 * fx + 2 * fy + c], far)
            copies.append(_remote(blk, blk, send, recv, 3, (tx, ty, c)))
            for k, (cx, cy) in enumerate([(1 - x, y), (x, 1 - y)]):
                blk = rows(out.at[4 * cx + 2 * cy + c], far)
                copies.append(_remote(blk, blk, send, recv, 4 + k, sib))
        if last is not None:
            blk = rows(out.at[4 * (1 - x) + 2 * (1 - y) + c], last)
            copies.append(_remote(blk, blk, send, recv, 6, sib))
        return copies

    return Job([], [buf], 7, build)


def reduce_sibling_job(dw):
    def build(srcs, dsts, send, recv, loc):
        (src,), (got,) = srcs, dsts
        x, y, c = _place()
        return [_remote(src.at[4 * _flip(x, r & 2) + 2 * _flip(y, r & 1) + 1 - c], got.at[r], send, recv, r,
                        (x, y, 1 - c)) for r in range(N_CHIPS)]

    return Job([dw], [jax.ShapeDtypeStruct((N_CHIPS,) + dw.shape[1:], dw.dtype)], N_CHIPS, build)


def reduce_chips_job(part):
    def build(srcs, dsts, send, recv, loc):
        (src,), (got,) = srcs, dsts
        x, y, c = _place()
        return [_remote(src.at[r], got.at[r - 1], send, recv, r - 1, (_flip(x, r & 2), _flip(y, r & 1), c))
                for r in range(1, N_CHIPS)]

    return Job([part], [jax.ShapeDtypeStruct((N_CHIPS - 1,) + part.shape[1:], part.dtype)], N_CHIPS - 1, build)


def share_small_job(small):
    def build(srcs, dsts, send, recv, loc):
        (src,), (out,) = srcs, dsts
        x, y, c = _place()
        mine = out.at[4 * x + 2 * y + c]
        copies = [pltpu.make_async_copy(src, mine, loc.at[0])]
        for rel in range(1, N_DEV):
            peer = (_flip(x, rel & 4), _flip(y, rel & 2), _flip(c, rel & 1))
            copies.append(_remote(src, mine, send, recv, rel - 1, peer))
        return copies

    return Job([small], [jax.ShapeDtypeStruct((N_DEV,) + small.shape, small.dtype)], N_DEV - 1, build)


def all_gather(shards, name):
    n = len(shards)

    def body(*refs):
        ins, outs = refs[:n], refs[n:2 * n]
        send_sems, recv_sems, local_sems = refs[2 * n:]
        x, y, c = _place()
        me, sibling = (x, y, c), (x, y, 1 - c)
        near = [(1 - x, y), (x, 1 - y)]
        diag = (1 - x, 1 - y)
        frm, to = _relay(x, y, c)

        def slot(a, dev):
            return outs[a].at[4 * dev[0] + 2 * dev[1] + dev[2]]

        def copy(a, k, block, to_dev, src=None):
            return pltpu.make_async_remote_copy(
                src_ref=slot(a, block) if src is None else src, dst_ref=slot(a, block),
                send_sem=send_sems.at[a, k], recv_sem=recv_sems.at[a, k],
                device_id=to_dev, device_id_type=MESH)

        mine, sent = [], []
        for a in range(n):
            mine.append(pltpu.make_async_copy(ins[a], slot(a, me), local_sems.at[a]))
            mine[a].start()
            sent.append([copy(a, 0, me, sibling, src=ins[a])]
                        + [copy(a, 1 + j, me, (*chip, c), src=ins[a]) for j, chip in enumerate(near)])
            for cp in sent[a]:
                cp.start()
        for a in range(n):
            for j, chip in enumerate(near):
                copy(a, 1 + j, (*chip, c), me).wait_recv()
            more = [copy(a, 3, (*frm, c), (*to, c))] + [copy(a, 4 + j, (*chip, c), sibling)
                                                       for j, chip in enumerate(near)]
            for cp in more:
                cp.start()
            sent[a] += more
        for a in range(n):
            copy(a, 3, (*diag, c), me).wait_recv()
            sent[a].append(copy(a, 6, (*diag, c), sibling))
            sent[a][-1].start()
        for a in range(n):
            for k, block in [(0, sibling), (4, (*near[0], 1 - c)), (5, (*near[1], 1 - c)), (6, (*diag, 1 - c))]:
                copy(a, k, block, me).wait_recv()
            for cp in sent[a]:
                cp.wait_send()
            mine[a].wait()

    return pl.pallas_call(
        body, name=name,
        in_specs=[ANY_SPEC] * n, out_specs=[ANY_SPEC] * n,
        out_shape=[jax.ShapeDtypeStruct((N_DEV,) + s.shape, s.dtype) for s in shards],
        scratch_shapes=[pltpu.SemaphoreType.DMA((n, 7)), pltpu.SemaphoreType.DMA((n, 7)),
                        pltpu.SemaphoreType.DMA((n,))],
    )(*shards)


def _kv_split_index(tw, D):
    pd = D // tw

    def index(j):
        return (j // pd) % 2, (j // (2 * pd)) * pd + j % pd

    return index


def _col_tile(n, also, pref):
    t = (min(pref, n) // LANE) * LANE
    while n % t or (also is not None and also % t):
        t -= LANE
    assert t > 0
    return t


def matmul_nn(a, w, name, res=None, out_dtype=F32, kv_split=False, jobs=()):
    M, K = a.shape
    nb, _, n = w.shape
    D = nb * n // (2 * N_GROUPS)
    tn = _col_tile(n, D if kv_split else None, 1024)
    out_bytes = jnp.dtype(out_dtype).itemsize + (4 if res is not None else 0)
    tm = _tile(M, 2048)
    if 2 * (tm * K * 2 + K * tn * 2 + tm * tn * out_bytes) > MATMUL_VMEM_BUDGET:
        tm = _tile(M, 1024)
    per = n // tn

    def body(*refs):
        if res is None:
            a_ref, w_ref, o_ref = refs
        else:
            a_ref, w_ref, r_ref, o_ref = refs
        acc = jnp.dot(a_ref[...], w_ref[...], preferred_element_type=F32)
        if res is not None:
            acc = r_ref[...] + acc
        o_ref[...] = acc.astype(o_ref.dtype)

    in_specs = [pl.BlockSpec((tm, K), lambda i, j: (i, 0)),
                pl.BlockSpec((None, K, tn), lambda i, j: (j // per, 0, j % per))]
    args = [a, w]
    if res is not None:
        in_specs.append(pl.BlockSpec((tm, tn), lambda i, j: (i, j)))
        args.append(res)
    if kv_split:
        split = _kv_split_index(tn, D)
        out_spec = pl.BlockSpec((None, tm, tn), lambda i, j: (split(j)[0], i, split(j)[1]))
        out_shape = jax.ShapeDtypeStruct((2, M, N_GROUPS * D), out_dtype)
    else:
        out_spec = pl.BlockSpec((tm, tn), lambda i, j: (i, j))
        out_shape = jax.ShapeDtypeStruct((M, nb * n), out_dtype)
    return _call(body, name=name, grid=(M // tm, nb * per), in_specs=in_specs, out_specs=[out_spec],
                 out_shape=[out_shape], args=args, sem=("parallel", "parallel"), jobs=jobs)[0]


def matmul_nt(dy, w, name, kv_split=False, rows=None, into=None, jobs=()):
    M = dy.shape[-2]
    nb, K, n = w.shape
    D = nb * n // (2 * N_GROUPS)
    tm = _tile(M, 1024)
    tc = _col_tile(n, D if kv_split else None, 1024)
    per = n // tc
    pair = max(u for u in (1, 2, 4) if (nb * per) % u == 0 and u * tc <= 2048)
    p, parts = rows or (0, 1)
    mt = M // tm // parts
    i0 = p * mt

    steps = nb * per // pair

    def body(*refs):
        o_ref, acc_ref = refs[-2], refs[-1]
        j = pl.program_id(1)
        lhs = jnp.concatenate([refs[u][...] for u in range(pair)], axis=1)
        rhs = jnp.concatenate([refs[pair + u][...] for u in range(pair)], axis=1)
        part = lax.dot_general(lhs, rhs, (((1,), (1,)), ((), ())), preferred_element_type=F32)

        @pl.when(j == 0)
        def _():
            acc_ref[...] = part

        @pl.when(j > 0)
        def _():
            acc_ref[...] += part

        @pl.when(j == steps - 1)
        def _():
            o_ref[...] = acc_ref[...].astype(o_ref.dtype)

    def dy_spec(u):
        if kv_split:
            split = _kv_split_index(tc, D)
            return pl.BlockSpec((None, tm, tc),
                                lambda i, j: (split(pair * j + u)[0], i0 + i, split(pair * j + u)[1]))
        return pl.BlockSpec((tm, tc), lambda i, j: (i0 + i, pair * j + u))

    def w_spec(u):
        return pl.BlockSpec((None, K, tc), lambda i, j: ((pair * j + u) // per, 0, (pair * j + u) % per))

    in_specs = [dy_spec(u) for u in range(pair)] + [w_spec(u) for u in range(pair)]
    args = [dy] * pair + [w] * pair
    aliases = None
    if into is not None:
        aliases = {len(args): 0}
        in_specs.append(ANY_SPEC)
        args.append(into)
    return _call(body, name=name, grid=(mt, steps), in_specs=in_specs,
                 out_specs=[pl.BlockSpec((tm, K), lambda i, j: (i0 + i, 0))],
                 out_shape=[jax.ShapeDtypeStruct((M, K), BF16)], args=args, aliases=aliases,
                 scratch_shapes=[pltpu.VMEM((tm, K), F32)],
                 sem=("parallel", "arbitrary"), jobs=jobs)[0]


def matmul_tn(a, dy, nb, name, out_dtype=BF16, kv_split=False, k_tiles=None, jobs=()):
    M, K = a.shape
    N = 2 * dy.shape[-1] if kv_split else dy.shape[-1]
    n = N // nb
    D = N // (2 * N_GROUPS)
    tn = _col_tile(n, D if kv_split else None, 1024)
    per = n // tn
    tk = _tile(K, 1024)

    def body(a_ref, dy_ref, o_ref):
        o_ref[...] = lax.dot_general(a_ref[...], dy_ref[...], (((0,), (0,)), ((), ())),
                                     preferred_element_type=F32).astype(o_ref.dtype)

    if kv_split:
        split = _kv_split_index(tn, D)
        dy_spec = pl.BlockSpec((None, M, tn), lambda k, j: (split(j)[0], 0, split(j)[1]))
    else:
        dy_spec = pl.BlockSpec((M, tn), lambda k, j: (0, j))
    k0, k1 = k_tiles or (0, K // tk)
    return _call(body, name=name, grid=(k1 - k0, nb * per),
                 in_specs=[pl.BlockSpec((M, tk), lambda k, j: (0, k0 + k)), dy_spec],
                 out_specs=[pl.BlockSpec((None, tk, tn), lambda k, j: (j // per, k, j % per))],
                 out_shape=[jax.ShapeDtypeStruct((nb, (k1 - k0) * tk, n), out_dtype)], args=[a, dy],
                 sem=("parallel", "parallel"), jobs=jobs)[0]


def small_dot(a, b, contract, name, jobs=()):
    if contract == "nn":
        dims = (((1,), (0,)), ((), ()))
        out = (a.shape[0], b.shape[1])
    else:
        dims = (((1,), (1,)), ((), ()))
        out = (a.shape[0], b.shape[0])

    def body(a_ref, b_ref, o_ref):
        o_ref[...] = lax.dot_general(a_ref[...], b_ref[...], dims, precision=lax.Precision.HIGHEST,
                                     preferred_element_type=F32)

    return _call(body, name=name, in_specs=[VMEM_SPEC, VMEM_SPEC], out_specs=[VMEM_SPEC],
                 out_shape=[jax.ShapeDtypeStruct(out, F32)], args=[a, b], jobs=jobs)[0]


def rms_fwd(x, gains, name, jobs=()):
    S, D = x.shape
    T = _tile(S, 512, 8)
    n = len(gains)

    def body(x_ref, *refs):
        xv = x_ref[...]
        xn = xv * lax.rsqrt(jnp.mean(xv * xv, axis=-1, keepdims=True) + EPS)
        for g_ref, o_ref in zip(refs[:n], refs[n:]):
            o_ref[...] = (xn * g_ref[...]).astype(o_ref.dtype)

    row = pl.BlockSpec((T, D), lambda i: (i, 0))
    vec = pl.BlockSpec((1, D), lambda i: (0, 0))
    return _call(body, name=name, grid=(S // T,), in_specs=[row] + [vec] * n, out_specs=[row] * n,
                 out_shape=[jax.ShapeDtypeStruct((S, D), BF16)] * n, args=[x, *gains],
                 sem=("parallel",), jobs=jobs)


def rms_bwd(x, dhs, gains, dres, name, want_bf16, jobs=()):
    S, D = x.shape
    T = _tile(S, 256, 8)
    n = len(gains)

    def body(x_ref, *refs):
        dh_refs = refs[:n]
        g_refs = refs[n:2 * n]
        dres_ref = refs[2 * n]
        outs = refs[2 * n + 1:]
        dx_ref, dg_ref = outs[0], outs[-1]
        i = pl.program_id(0)

        @pl.when(i == 0)
        def _():
            dg_ref[...] = jnp.zeros_like(dg_ref)

        xv = x_ref[...]
        r = lax.rsqrt(jnp.mean(xv * xv, axis=-1, keepdims=True) + EPS)
        xn = xv * r
        dxn = jnp.zeros_like(xv)
        for k in range(n):
            dh = dh_refs[k][...]
            dg_ref[k:k + 1, :] += jnp.sum(dh * xn, axis=0, keepdims=True)
            dxn = dxn + dh * g_refs[k][...]
        dx = dres_ref[...] + r * (dxn - xn * jnp.mean(dxn * xn, axis=-1, keepdims=True))
        dx_ref[...] = dx
        if want_bf16:
            outs[1][...] = dx.astype(BF16)

    row = pl.BlockSpec((T, D), lambda i: (i, 0))
    vec = pl.BlockSpec((1, D), lambda i: (0, 0))
    acc = pl.BlockSpec((8, D), lambda i: (0, 0))
    out_specs = [row] + ([row] if want_bf16 else []) + [acc]
    out_shape = ([jax.ShapeDtypeStruct((S, D), F32)]
                 + ([jax.ShapeDtypeStruct((S, D), BF16)] if want_bf16 else [])
                 + [jax.ShapeDtypeStruct((8, D), F32)])
    return _call(body, name=name, grid=(S // T,), in_specs=[row] + [row] * n + [vec] * n + [row],
                 out_specs=out_specs, out_shape=out_shape, args=[x, *dhs, *gains, dres],
                 sem=("arbitrary",), jobs=jobs)


def final_loss(x2, gain, target, name, jobs=()):
    S, D = x2.shape
    T = _tile(S, 256, 8)

    def body(x_ref, g_ref, t_ref, dx_ref, dxb_ref, acc_ref):
        i = pl.program_id(0)

        @pl.when(i == 0)
        def _():
            acc_ref[...] = jnp.zeros_like(acc_ref)

        xv = x_ref[...]
        g = g_ref[...]
        r = lax.rsqrt(jnp.mean(xv * xv, axis=-1, keepdims=True) + EPS)
        xn = xv * r
        err = xn * g - t_ref[...]
        dy = err * (1.0 / D)
        acc_ref[0:1, :] += jnp.sum(dy * xn, axis=0, keepdims=True)
        acc_ref[1:2, :] += jnp.full((1, D), 0.5 / D, F32) * jnp.sum(err * err)
        dxn = dy * g
        dx = r * (dxn - xn * jnp.mean(dxn * xn, axis=-1, keepdims=True))
        dx_ref[...] = dx
        dxb_ref[...] = dx.astype(BF16)

    row = pl.BlockSpec((T, D), lambda i: (i, 0))
    return _call(body, name=name, grid=(S // T,),
                 in_specs=[row, pl.BlockSpec((1, D), lambda i: (0, 0)), row],
                 out_specs=[row, row, pl.BlockSpec((8, D), lambda i: (0, 0))],
                 out_shape=[jax.ShapeDtypeStruct((S, D), F32), jax.ShapeDtypeStruct((S, D), BF16),
                            jax.ShapeDtypeStruct((8, D), F32)],
                 args=[x2, gain, target], sem=("arbitrary",), jobs=jobs)


ROW_CHUNK = 64
LANE_CHUNK = 512
SUBLANES = 8


def _shifted_copies(buf, sh_scr, l0, lc):
    n = buf.shape[0] - SUBLANES
    for b in range(1, SUBLANES):
        sh_scr[b - 1, 0:n, :] = buf[b:b + n, l0:l0 + lc]


def _shifted(buf, sh_scr, start, rows, l0, lc):
    a8, b = (start // SUBLANES) * SUBLANES, start % SUBLANES
    if b == 0:
        return buf[a8:a8 + rows, l0:l0 + lc]
    return sh_scr[b - 1, a8:a8 + rows, :]


def conf_fwd(proj, cw, cb, lg, lb, name, jobs=()):
    S, E3 = proj.shape
    E = E3 // 3
    T = _tile(S, 256, HALO)
    R = T // HALO
    lc = _tile(E, LANE_CHUNK)
    rc = min(ROW_CHUNK, T)

    def body(a_ref, b_ref, z_ref, ap_ref, bp_ref, cw_ref, cb_ref, lg_ref, lb_ref, c_ref, y_ref, u_scr, sh_scr):
        i = pl.program_id(0)
        up = ap_ref[...] * _sigmoid(bp_ref[...])
        u_scr[0:HALO, :] = jnp.where(i > 0, up, 0.0)
        u_scr[HALO:HALO + T, :] = a_ref[...] * _sigmoid(b_ref[...])
        off = HALO - (CONV_WIDTH - 1)
        for l0 in range(0, E, lc):
            _shifted_copies(u_scr, sh_scr, l0, lc)
            for r0 in range(0, T, rc):
                acc = jnp.broadcast_to(cb_ref[:, l0:l0 + lc], (rc, lc))
                for k in range(CONV_WIDTH):
                    acc = acc + _shifted(u_scr, sh_scr, r0 + off + k, rc, l0, lc) * cw_ref[k:k + 1, l0:l0 + lc]
                c_ref[r0:r0 + rc, l0:l0 + lc] = acc
        c = c_ref[...]
        mu = jnp.mean(c, axis=-1, keepdims=True)
        d = c - mu
        var = jnp.mean(d * d, axis=-1, keepdims=True)
        cn = d * lax.rsqrt(var + EPS) * lg_ref[...] + lb_ref[...]
        z = z_ref[...]
        y_ref[...] = ((cn * _sigmoid(cn)) * (z * _sigmoid(z))).astype(BF16)

    def col(j):
        return pl.BlockSpec((T, E), lambda i: (i, j))

    def prev(j):
        return pl.BlockSpec((HALO, E), lambda i: (jnp.maximum(i * R - 1, 0), j))

    vec = pl.BlockSpec((1, E), lambda i: (0, 0))
    return _call(body, name=name, grid=(S // T,),
                 in_specs=[col(0), col(1), col(2), prev(0), prev(1),
                           pl.BlockSpec((HALO, E), lambda i: (0, 0)), vec, vec, vec],
                 out_specs=[pl.BlockSpec((T, E), lambda i: (i, 0))] * 2,
                 out_shape=[jax.ShapeDtypeStruct((S, E), F32), jax.ShapeDtypeStruct((S, E), BF16)],
                 scratch_shapes=[pltpu.VMEM((HALO + T, E), F32), pltpu.VMEM((SUBLANES - 1, HALO + T, lc), F32)],
                 args=[proj, proj, proj, proj, proj, cw, cb, lg, lb], sem=("parallel",), jobs=jobs)


def conf_bwd_ln(c, dy, proj, lg, lb, name, jobs=()):
    S, E = c.shape
    T = _tile(S, 256, 8)

    def body(c_ref, dy_ref, z_ref, lg_ref, lb_ref, dc_ref, dz_ref, acc_ref):
        i = pl.program_id(0)

        @pl.when(i == 0)
        def _():
            acc_ref[...] = jnp.zeros_like(acc_ref)

        cv = c_ref[...]
        mu = jnp.mean(cv, axis=-1, keepdims=True)
        d = cv - mu
        var = jnp.mean(d * d, axis=-1, keepdims=True)
        rstd = lax.rsqrt(var + EPS)
        xh = d * rstd
        lgv = lg_ref[...]
        cn = xh * lgv + lb_ref[...]
        z = z_ref[...]
        dy = dy_ref[...]
        sc = _sigmoid(cn)
        sz = _sigmoid(z)
        dcn = dy * (z * sz) * (sc * (1.0 + cn * (1.0 - sc)))
        dz_ref[...] = (dy * (cn * sc) * (sz * (1.0 + z * (1.0 - sz)))).astype(BF16)
        acc_ref[0:1, :] += jnp.sum(dcn * xh, axis=0, keepdims=True)
        acc_ref[1:2, :] += jnp.sum(dcn, axis=0, keepdims=True)
        dxh = dcn * lgv
        dc = rstd * (dxh - jnp.mean(dxh, axis=-1, keepdims=True)
                     - xh * jnp.mean(dxh * xh, axis=-1, keepdims=True))
        acc_ref[2:3, :] += jnp.sum(dc, axis=0, keepdims=True)
        dc_ref[...] = dc

    row = pl.BlockSpec((T, E), lambda i: (i, 0))
    vec = pl.BlockSpec((1, E), lambda i: (0, 0))
    return _call(body, name=name, grid=(S // T,),
                 in_specs=[row, row, pl.BlockSpec((T, E), lambda i: (i, 2)), vec, vec],
                 out_specs=[row, row, pl.BlockSpec((8, E), lambda i: (0, 0))],
                 out_shape=[jax.ShapeDtypeStruct((S, E), F32), jax.ShapeDtypeStruct((S, E), BF16),
                            jax.ShapeDtypeStruct((8, E), F32)],
                 args=[c, dy, proj, lg, lb], sem=("arbitrary",), jobs=jobs)


def conf_bwd_conv(proj, dc, dz, cw, name, jobs=()):
    S, E3 = proj.shape
    E = E3 // 3
    T = _tile(S, 256, HALO)
    R = T // HALO
    nt = S // T
    lc = _tile(E, LANE_CHUNK)
    rc = min(ROW_CHUNK, T)
    rd = min(ROW_CHUNK // 2, T)

    def body(a_ref, b_ref, ap_ref, bp_ref, dc_ref, dcn_ref, dz_ref, cw_ref, o_ref, dw_ref, u_scr, dc_scr, sh_scr,
             dw_scr):
        i = pl.program_id(0)

        @pl.when(i == 0)
        def _():
            dw_scr[...] = jnp.zeros_like(dw_scr)

        a = a_ref[...]
        sb = _sigmoid(b_ref[...])
        up = ap_ref[...] * _sigmoid(bp_ref[...])
        u_scr[0:HALO, :] = jnp.where(i > 0, up, 0.0)
        u_scr[HALO:HALO + T, :] = a * sb
        dc_scr[0:T, :] = dc_ref[...]
        dc_scr[T:T + HALO, :] = jnp.where(i < nt - 1, dcn_ref[...], 0.0)
        off = HALO - (CONV_WIDTH - 1)
        for l0 in range(0, E, lc):
            _shifted_copies(u_scr, sh_scr, l0, lc)
            for r0 in range(0, T, rd):
                dcc = dc_scr[r0:r0 + rd, l0:l0 + lc]
                for k in range(CONV_WIDTH):
                    prod = _shifted(u_scr, sh_scr, r0 + off + k, rd, l0, lc) * dcc
                    dw_scr[k, :, l0:l0 + lc] += jnp.sum(prod.reshape(rd // SUBLANES, SUBLANES, lc), axis=0)
            _shifted_copies(dc_scr, sh_scr, l0, lc)
            for r0 in range(0, T, rc):
                acc = jnp.zeros((rc, lc), F32)
                for k in range(CONV_WIDTH):
                    s0 = r0 + (CONV_WIDTH - 1) - k
                    acc = acc + _shifted(dc_scr, sh_scr, s0, rc, l0, lc) * cw_ref[k:k + 1, l0:l0 + lc]
                av = a[r0:r0 + rc, l0:l0 + lc]
                sv = sb[r0:r0 + rc, l0:l0 + lc]
                o_ref[r0:r0 + rc, l0:l0 + lc] = (acc * sv).astype(BF16)
                o_ref[r0:r0 + rc, E + l0:E + l0 + lc] = (acc * av * sv * (1.0 - sv)).astype(BF16)
        o_ref[:, 2 * E:3 * E] = dz_ref[...]

        @pl.when(i == nt - 1)
        def _():
            dw_ref[...] = jnp.sum(dw_scr[...], axis=1)

    def col(j):
        return pl.BlockSpec((T, E), lambda i: (i, j))

    def prev(j):
        return pl.BlockSpec((HALO, E), lambda i: (jnp.maximum(i * R - 1, 0), j))

    row = pl.BlockSpec((T, E), lambda i: (i, 0))
    nxt = pl.BlockSpec((HALO, E), lambda i: (jnp.minimum((i + 1) * R, S // HALO - 1), 0))
    return _call(body, name=name, grid=(nt,),
                 in_specs=[col(0), col(1), prev(0), prev(1), row, nxt, row,
                           pl.BlockSpec((HALO, E), lambda i: (0, 0))],
                 out_specs=[pl.BlockSpec((T, E3), lambda i: (i, 0)), pl.BlockSpec((HALO, E), lambda i: (0, 0))],
                 out_shape=[jax.ShapeDtypeStruct((S, E3), BF16), jax.ShapeDtypeStruct((HALO, E), F32)],
                 scratch_shapes=[pltpu.VMEM((HALO + T, E), F32), pltpu.VMEM((T + HALO, E), F32),
                                 pltpu.VMEM((SUBLANES - 1, HALO + T, lc), F32),
                                 pltpu.VMEM((HALO, SUBLANES, E), F32)],
                 args=[proj, proj, proj, proj, dc, dc, dz, cw], sem=("arbitrary",), jobs=jobs)


def bucket_tables():
    q = np.arange(BLOCK)[:, None]
    k = np.arange(2 * BLOCK)[None, :]
    out = []
    for window, dil in GROUPS:
        delta = q + BLOCK - k
        valid = (delta >= 0) & (delta <= window // dil)
        dist = np.clip(delta, 0, None) * dil
        large = MAX_EXACT + (np.log(np.maximum(dist, 1).astype(np.float32) / MAX_EXACT)
                             / math.log(MAX_DISTANCE / MAX_EXACT)
                             * (N_BUCKETS - MAX_EXACT)).astype(np.int32)
        large = np.minimum(large, N_BUCKETS - 1)
        bucket = np.where(dist < MAX_EXACT, dist, large)
        out.append(np.where(valid, bucket, -1).reshape(-1))
    return np.concatenate(out).astype(np.int32)


def _band_masks(has_previous):
    ql = lax.broadcasted_iota(jnp.int32, (BLOCK, 2 * BLOCK), 0)
    kk = lax.broadcasted_iota(jnp.int32, (BLOCK, 2 * BLOCK), 1)
    band = (kk >= ql) & (kk <= ql + BLOCK)
    return band, band & ((kk >= BLOCK) | has_previous)


def _dot_nt(a, b):
    return lax.dot_general(a, b, (((1,), (1,)), ((), ())), preferred_element_type=F32)


def _dot_tn(a, b):
    return lax.dot_general(a, b, (((0,), (0,)), ((), ())), preferred_element_type=F32)


ATTN_ROWS = 2048


def _sub(start, size, dil):
    return pl.ds(start, size) if dil == 1 else pl.ds(start, size, stride=dil)


def _attn_geometry(S, dil):
    halo = BLOCK * dil
    rows = max(min(S, ATTN_ROWS), halo)
    return halo, rows, S // rows, rows // halo


MAX_ROW_STRIDE = 8


def _split(dst, src, n, dil, tmp):
    if dil <= MAX_ROW_STRIDE:
        for r in range(dil):
            dst[r] = src[_sub(r, n, dil), :].astype(dst.dtype)
        return
    f, g = 4, dil // 4
    for r1 in range(f):
        tmp[0:n * g, :] = src[_sub(r1, n * g, f), :]
        for r2 in range(g):
            dst[r2 * f + r1] = tmp[_sub(r2, n, g), :].astype(dst.dtype)


def _merge(dst, src, n, dil, tmp):
    if dil <= MAX_ROW_STRIDE:
        for r in range(dil):
            dst[_sub(r, n, dil), :] = src[r]
        return
    f, g = 4, dil // 4
    for r1 in range(f):
        for r2 in range(g):
            tmp[_sub(r2, n, g), :] = src[r2 * f + r1]
        dst[_sub(r1, n * g, f), :] = tmp[0:n * g, :]


def _split_tmp(sub, dil):
    rows = (sub + BLOCK) * (dil // 4) if dil > MAX_ROW_STRIDE else SUBLANES
    return pltpu.VMEM((rows, HEAD_DIM), F32)


def attn_fwd(qz, kv, bias, g, dil, name, jobs=()):
    S = qz.shape[0]
    D = qz.shape[1] // (N_GROUPS + 1)
    H = D // HEAD_DIM
    halo, rows, nsb, nblk = _attn_geometry(S, dil)
    sub = nblk * BLOCK
    scale = HEAD_DIM ** -0.5

    def body(q_ref, kvc_ref, kvp_ref, b_ref, o_ref, l_ref, ks, vs, qd, kd, vd, od, ld, tmp):
        sb = pl.program_id(1)
        ks[0:halo, :] = kvp_ref[0].astype(F32)
        ks[halo:, :] = kvc_ref[0].astype(F32)
        vs[0:halo, :] = kvp_ref[1].astype(F32)
        vs[halo:, :] = kvc_ref[1].astype(F32)
        _split(kd, ks, sub + BLOCK, dil, tmp)
        _split(vd, vs, sub + BLOCK, dil, tmp)
        _split(qd, q_ref, sub, dil, tmp)
        band, first = _band_masks(sb > 0)
        bias_t = b_ref[0]
        for r in range(dil):
            for jj in range(nblk):
                q = qd[r, jj * BLOCK:(jj + 1) * BLOCK, :]
                keys = slice(jj * BLOCK, (jj + 2) * BLOCK)
                s = _dot_nt(q, kd[r, keys, :]) * scale + bias_t
                s = jnp.where(band if jj > 0 else first, s, -jnp.inf)
                m = jnp.max(s, axis=-1, keepdims=True)
                p = jnp.exp(s - m)
                den = jnp.sum(p, axis=-1, keepdims=True)
                pv = jnp.dot(p.astype(BF16), vd[r, keys, :], preferred_element_type=F32)
                own = slice(jj * BLOCK, (jj + 1) * BLOCK)
                od[r, own, :] = pv / den
                ld[r, own, :] = jnp.broadcast_to(m + jnp.log(den), (BLOCK, HEAD_DIM))
        _merge(o_ref, od, sub, dil, tmp)
        _merge(l_ref, ld, sub, dil, tmp)

    per = rows // halo
    out_spec = pl.BlockSpec((rows, HEAD_DIM), lambda h, sb: (sb, h))
    return _call(body, name=name, grid=(H, nsb),
                 in_specs=[pl.BlockSpec((rows, HEAD_DIM), lambda h, sb: (sb, g * H + h)),
                           pl.BlockSpec((2, rows, HEAD_DIM), lambda h, sb: (0, sb, g * H + h)),
                           pl.BlockSpec((2, halo, HEAD_DIM),
                                        lambda h, sb: (0, jnp.maximum(sb * per - 1, 0), g * H + h)),
                           pl.BlockSpec((1, BLOCK, 2 * BLOCK), lambda h, sb: (h, 0, 0))],
                 out_specs=[out_spec, out_spec],
                 out_shape=[jax.ShapeDtypeStruct((S, D), F32)] * 2,
                 scratch_shapes=[pltpu.VMEM((halo + rows, HEAD_DIM), F32)] * 2
                 + [pltpu.VMEM((dil, sub, HEAD_DIM), BF16)]
                 + [pltpu.VMEM((dil, sub + BLOCK, HEAD_DIM), BF16)] * 2
                 + [pltpu.VMEM((dil, sub, HEAD_DIM), F32)] * 2 + [_split_tmp(sub, dil)],
                 args=[qz, kv, kv, bias], sem=("parallel", "arbitrary"), jobs=jobs)


def merge_fwd(os_, lses, qz, name, jobs=()):
    S, D = os_[0].shape
    T = _tile(S, 256, 8)

    def body(o1, o2, o3, l1, l2, l3, z_ref, o_ref, lse_ref, y_ref):
        la, lb_, lc_ = l1[...], l2[...], l3[...]
        m = jnp.maximum(jnp.maximum(la, lb_), lc_)
        ea, eb, ec = jnp.exp(la - m), jnp.exp(lb_ - m), jnp.exp(lc_ - m)
        den = ea + eb + ec
        o = (ea * o1[...] + eb * o2[...] + ec * o3[...]) / den
        z = z_ref[...]
        o_ref[...] = o
        lse_ref[...] = m + jnp.log(den)
        y_ref[...] = (o * (z * _sigmoid(z))).astype(BF16)

    row = pl.BlockSpec((T, D), lambda i: (i, 0))
    return _call(body, name=name, grid=(S // T,),
                 in_specs=[row] * 6 + [pl.BlockSpec((T, D), lambda i: (i, N_GROUPS))],
                 out_specs=[row] * 3,
                 out_shape=[jax.ShapeDtypeStruct((S, D), F32), jax.ShapeDtypeStruct((S, D), F32),
                            jax.ShapeDtypeStruct((S, D), BF16)],
                 args=[*os_, *lses, qz], sem=("parallel",), jobs=jobs)


def merge_bwd(dy2, o, qz, name, jobs=()):
    S, D = o.shape
    H = D // HEAD_DIM
    T = _tile(S, 256, 8)
    nq = N_GROUPS + 1

    def body(dy_ref, o_ref, z_ref, do_ref, dl_ref, dqz_ref):
        dy = dy_ref[...]
        ov = o_ref[...]
        z = z_ref[...]
        sz = _sigmoid(z)
        do = dy * (z * sz)
        do_ref[...] = do.astype(BF16)
        dqz_ref[...] = (dy * ov * (sz * (1.0 + z * (1.0 - sz)))).astype(BF16)
        prod = do * ov
        for h in range(H):
            hs = slice(h * HEAD_DIM, (h + 1) * HEAD_DIM)
            dl_ref[:, hs] = jnp.broadcast_to(jnp.sum(prod[:, hs], axis=-1, keepdims=True), (T, HEAD_DIM))

    row = pl.BlockSpec((T, D), lambda i: (i, 0))
    last = pl.BlockSpec((T, D), lambda i: (i, N_GROUPS))
    return _call(body, name=name, grid=(S // T,), in_specs=[row, row, last], out_specs=[row, row, last],
                 out_shape=[jax.ShapeDtypeStruct((S, D), BF16), jax.ShapeDtypeStruct((S, D), F32),
                            jax.ShapeDtypeStruct((S, nq * D), BF16)],
                 args=[dy2, o, qz], sem=("parallel",), jobs=jobs)


def attn_bwd(qz, kv, do, lse, delta, bias, dqz, dkv, g, dil, name, jobs=()):
    S = qz.shape[0]
    D = qz.shape[1] // (N_GROUPS + 1)
    H = D // HEAD_DIM
    halo, rows, nsb, nblk = _attn_geometry(S, dil)
    sub = nblk * BLOCK
    scale = HEAD_DIM ** -0.5
    have_dkv = dkv is not None

    def body(*refs):
        q_ref, do_ref, l_ref, d_ref, kvc_ref, kvp_ref, b_ref = refs[:7]
        n_in = 7 + 1 + (1 if have_dkv else 0)
        (dq_ref, dkv_ref, ds_ref, ks, vs, dks, dvs, dos, dqs, carry_k, carry_v,
         qd, dod, ld, dd, dqd, kd, vd, dkd, dvd, tmp) = refs[n_in:]
        i = pl.program_id(1)
        sb = nsb - 1 - i

        @pl.when(i == 0)
        def _():
            ds_ref[...] = jnp.zeros_like(ds_ref)

        ks[0:halo, :] = kvp_ref[0].astype(F32)
        ks[halo:, :] = kvc_ref[0].astype(F32)
        vs[0:halo, :] = kvp_ref[1].astype(F32)
        vs[halo:, :] = kvc_ref[1].astype(F32)
        dos[...] = do_ref[...].astype(F32)
        _split(kd, ks, sub + BLOCK, dil, tmp)
        _split(vd, vs, sub + BLOCK, dil, tmp)
        for dst, src in ((qd, q_ref), (dod, dos), (ld, l_ref), (dd, d_ref)):
            _split(dst, src, sub, dil, tmp)
        dkd[...] = jnp.zeros_like(dkd)
        dvd[...] = jnp.zeros_like(dvd)
        band, first = _band_masks(sb > 0)
        bias_t = b_ref[0]
        for r in range(dil):
            for jj in range(nblk):
                own = slice(jj * BLOCK, (jj + 1) * BLOCK)
                keys = slice(jj * BLOCK, (jj + 2) * BLOCK)
                q = qd[r, own, :]
                do = dod[r, own, :]
                k = kd[r, keys, :]
                v = vd[r, keys, :]
                lse = ld[r, own, :]
                dlt = dd[r, own, :]
                s = _dot_nt(q, k) * scale + bias_t - jnp.concatenate([lse, lse], axis=-1)
                p = jnp.where(band if jj > 0 else first, jnp.exp(s), 0.0)
                ds = p * (_dot_nt(do, v) - jnp.concatenate([dlt, dlt], axis=-1))
                ds_ref[0] += ds
                dsb = ds.astype(BF16)
                dqd[r, own, :] = jnp.dot(dsb, k, preferred_element_type=F32) * scale
                dkd[r, keys, :] += _dot_tn(dsb, q) * scale
                dvd[r, keys, :] += _dot_tn(p.astype(BF16), do)
        _merge(dqs, dqd, sub, dil, tmp)
        _merge(dks, dkd, sub + BLOCK, dil, tmp)
        _merge(dvs, dvd, sub + BLOCK, dil, tmp)

        @pl.when(i > 0)
        def _():
            dks[rows:rows + halo, :] += carry_k[...]
            dvs[rows:rows + halo, :] += carry_v[...]

        dkv_ref[0] = dks[halo:, :].astype(BF16)
        dkv_ref[1] = dvs[halo:, :].astype(BF16)
        carry_k[...] = dks[0:halo, :]
        carry_v[...] = dvs[0:halo, :]
        dq_ref[...] = dqs[...].astype(BF16)

    per = rows // halo

    def rev(i):
        return nsb - 1 - i

    bias_spec = pl.BlockSpec((1, BLOCK, 2 * BLOCK), lambda h, i: (h, 0, 0))
    row_spec = pl.BlockSpec((rows, HEAD_DIM), lambda h, i: (rev(i), h))
    in_specs = [pl.BlockSpec((rows, HEAD_DIM), lambda h, i: (rev(i), g * H + h)),
                row_spec, row_spec, row_spec,
                pl.BlockSpec((2, rows, HEAD_DIM), lambda h, i: (0, rev(i), g * H + h)),
                pl.BlockSpec((2, halo, HEAD_DIM),
                             lambda h, i: (0, jnp.maximum(rev(i) * per - 1, 0), g * H + h)),
                bias_spec, ANY_SPEC]
    args = [qz, do, lse, delta, kv, kv, bias, dqz]
    aliases = {7: 0}
    if have_dkv:
        in_specs.append(ANY_SPEC)
        args.append(dkv)
        aliases[8] = 1
    blk = (halo + rows, HEAD_DIM)
    own, keys = (dil, sub, HEAD_DIM), (dil, sub + BLOCK, HEAD_DIM)
    return _call(body, name=name, grid=(H, nsb), in_specs=in_specs,
                 out_specs=[pl.BlockSpec((rows, HEAD_DIM), lambda h, i: (rev(i), g * H + h)),
                            pl.BlockSpec((2, rows, HEAD_DIM), lambda h, i: (0, rev(i), g * H + h)),
                            bias_spec],
                 out_shape=[jax.ShapeDtypeStruct(qz.shape, BF16),
                            jax.ShapeDtypeStruct((2, S, N_GROUPS * D), BF16),
                            jax.ShapeDtypeStruct((H, BLOCK, 2 * BLOCK), F32)],
                 scratch_shapes=[pltpu.VMEM(blk, F32), pltpu.VMEM(blk, F32), pltpu.VMEM(blk, F32),
                                 pltpu.VMEM(blk, F32),
                                 pltpu.VMEM((rows, HEAD_DIM), F32), pltpu.VMEM((rows, HEAD_DIM), F32),
                                 pltpu.VMEM((halo, HEAD_DIM), F32), pltpu.VMEM((halo, HEAD_DIM), F32),
                                 pltpu.VMEM(own, BF16), pltpu.VMEM(own, BF16), pltpu.VMEM(own, F32),
                                 pltpu.VMEM(own, F32), pltpu.VMEM(own, F32),
                                 pltpu.VMEM(keys, BF16), pltpu.VMEM(keys, BF16),
                                 pltpu.VMEM(keys, F32), pltpu.VMEM(keys, F32), _split_tmp(sub, dil)],
                 aliases=aliases, args=args, sem=("parallel", "arbitrary"), jobs=jobs)


def _adamw(w, g, m, v):
    m = ADAM_B1 * m + (1.0 - ADAM_B1) * g
    v = ADAM_B2 * v + (1.0 - ADAM_B2) * (g * g)
    m_hat = m / (1.0 - ADAM_B1 ** ADAM_STEP)
    v_hat = v / (1.0 - ADAM_B2 ** ADAM_STEP)
    delta = -ADAM_LR * (m_hat / (jnp.sqrt(v_hat) + ADAM_EPS) + ADAM_WD * w)
    return delta, m, v


def add_pairs(dw, got, name, jobs=()):
    _, K, n = got.shape
    tk = _tile(K, 512, 16)

    def own_block(r, i, p):
        return 4 * ((p[0] + r // 2) % 2) + 2 * ((p[1] + r % 2) % 2) + p[2], i, 0

    def body(p_ref, a_ref, b_ref, o_ref):
        o_ref[...] = (a_ref[...].astype(F32) + b_ref[...].astype(F32)).astype(o_ref.dtype)

    blk = pl.BlockSpec((None, tk, n), lambda r, i, p: (r, i, 0))
    grid_spec = pltpu.PrefetchScalarGridSpec(
        num_scalar_prefetch=1, grid=(N_CHIPS, K // tk),
        in_specs=[pl.BlockSpec((None, tk, n), own_block), blk], out_specs=blk)
    return pl.pallas_call(body, name=name, grid_spec=grid_spec,
                          out_shape=jax.ShapeDtypeStruct(got.shape, got.dtype))(_place_vector(), dw, got)


def adamw_reduce(pieces, w, m, v, name, jobs=()):
    K, n = w.shape
    kp = K // len(pieces)
    tk = _tile(kp, 256, 8)
    sp = kp // tk

    def body(*refs):
        w_ref, m_ref, v_ref, g_ref, d_ref, nm_ref, nv_ref = refs[2 * len(pieces):]
        i = pl.program_id(0)
        g = None
        for q in range(len(pieces)):
            p_ref, r_ref = refs[2 * q], refs[2 * q + 1]
            gq = p_ref[...].astype(F32)
            for r in range(N_CHIPS - 1):
                gq = gq + r_ref[r].astype(F32)
            g = gq if g is None else jnp.where(i >= q * sp, gq, g)
        d, nm, nv = _adamw(w_ref[...], g, m_ref[...], v_ref[...])
        g_ref[...] = g
        d_ref[...] = d
        nm_ref[...] = nm
        nv_ref[...] = nv

    def piece_specs(q):
        def at(i):
            return jnp.clip(i - q * sp, 0, sp - 1)
        return [pl.BlockSpec((None, tk, n), lambda i: (0, at(i), 0)),
                pl.BlockSpec((N_CHIPS - 1, tk, n), lambda i: (0, at(i), 0))]

    blk = pl.BlockSpec((tk, n), lambda i: (i, 0))
    in_specs, args = [], []
    for q, (part, got) in enumerate(pieces):
        in_specs += piece_specs(q)
        args += [part, got]
    return _call(body, name=name, grid=(K // tk,), in_specs=in_specs + [blk, blk, blk],
                 out_specs=[blk] * 4, out_shape=[jax.ShapeDtypeStruct((K, n), F32)] * 4,
                 args=args + [w, m, v], sem=("parallel",), jobs=jobs)


def sum_parts(parts, name):
    _, R, D = parts.shape

    def body(p_ref, o_ref):
        g = p_ref[0]
        for r in range(1, N_DEV):
            g = g + p_ref[r]
        o_ref[...] = g

    return _call(body, name=name, in_specs=[VMEM_SPEC], out_specs=[VMEM_SPEC],
                 out_shape=[jax.ShapeDtypeStruct((R, D), F32)], args=[parts])[0]


def adamw_small(w, g, m, v, name):
    def body(w_ref, g_ref, m_ref, v_ref, d_ref, nm_ref, nv_ref):
        d, nm, nv = _adamw(w_ref[...], g_ref[...], m_ref[...], v_ref[...])
        d_ref[...] = d
        nm_ref[...] = nm
        nv_ref[...] = nv

    return _call(body, name=name, in_specs=[VMEM_SPEC] * 4, out_specs=[VMEM_SPEC] * 3,
                 out_shape=[jax.ShapeDtypeStruct(w.shape, F32)] * 3, args=[w, g, m, v])


def _kv_row_pieces(K):
    return (0, K // 2), (K // 2, K)


def _row(v, at):
    return jnp.pad(v.reshape(1, -1), ((at, 7 - at), (0, 0)))


def _pack_sharded(norm, conv_w, conv_b, ln_g, ln_b):
    n = norm.shape[-1]
    taps = jnp.pad(conv_w.reshape(CONV_WIDTH, n), ((0, HALO - CONV_WIDTH), (0, 0)))
    return jnp.concatenate([_row(norm, 0), _row(ln_g, 0) + _row(ln_b, 1) + _row(conv_b, 2), taps], axis=0)


def _pack_rel(rel_bias, D):
    return jnp.pad(rel_bias.reshape(1, -1), ((0, 7), (0, D - rel_bias.size)))


def _pack_replicated(kv_norm, b_norm, final_norm, rel_bias, D):
    return jnp.concatenate([_row(kv_norm, 0) + _row(b_norm, 1), _row(final_norm, 0), _pack_rel(rel_bias, D)], axis=0)


def kernel(x, a_norm, a_w_in, a_conv_w, a_conv_b, a_ln_g, a_ln_b, a_w_out, kv_norm, w_kv, b_norm, b_w_in, b_w_out, rel_bias, final_norm, loss_target, m_a_norm, m_a_w_in, m_a_conv_w, m_a_conv_b, m_a_ln_g, m_a_ln_b, m_a_w_out, m_kv_norm, m_w_kv, m_b_norm, m_b_w_in, m_b_w_out, m_rel_bias, m_final_norm, v_a_norm, v_a_w_in, v_a_conv_w, v_a_conv_b, v_a_ln_g, v_a_ln_b, v_a_w_out, v_kv_norm, v_w_kv, v_b_norm, v_b_w_in, v_b_w_out, v_rel_bias, v_final_norm):
    _, S, D = x.shape
    E = D
    H = D // HEAD_DIM
    nsh = D // N_DEV
    x0 = x.reshape(S, D)
    target = loss_target.reshape(S, D)
    kvn, bn, fn = kv_norm.reshape(1, D), b_norm.reshape(1, D), final_norm.reshape(1, D)

    names = ["a_w_in", "a_w_out", "w_kv", "b_w_in", "b_w_out"]
    big_w = dict(zip(names, [a_w_in[0], a_w_out[0], w_kv, b_w_in[0], b_w_out[0]]))
    big_m = dict(zip(names, [m_a_w_in[0], m_a_w_out[0], m_w_kv, m_b_w_in[0], m_b_w_out[0]]))
    big_v = dict(zip(names, [v_a_w_in[0], v_a_w_out[0], v_w_kv, v_b_w_in[0], v_b_w_out[0]]))
    sh_w = _pack_sharded(a_norm, a_conv_w, a_conv_b, a_ln_g, a_ln_b)

    wa_in, sh_all = all_gather([big_w["a_w_in"].astype(BF16), sh_w], "gather_first")
    sh_full = sh_all.transpose(1, 0, 2).reshape(SMALL_SH_ROWS, D)
    an, cw, cb = sh_full[0:1], sh_full[16:16 + HALO], sh_full[10:11]
    lg, lb = sh_full[8:9], sh_full[9:10]

    buf = {nm: place_shard(big_w[nm], "place_" + nm) for nm in names[1:]}
    kv_cut = _kv_row_pieces(D)

    riding = []

    def ride(nm, **kw):
        make = gather_relay_job if set(kw) & {"near", "far", "last"} else gather_job
        riding.append((nm, make(buf[nm], **kw)))
        return riding[-1][1]

    def landed():
        while riding:
            nm, job = riding.pop()
            buf[nm] = job.results[0]

    (h0,) = rms_fwd(x0, [an], "rms_a")
    proj = matmul_nn(h0, wa_in, "a_in",
                     jobs=[ride("a_w_out", chips=(0, E // N_DEV)), ride("w_kv", near=kv_cut[0])])
    landed()
    c, y = conf_fwd(proj, cw, cb, lg, lb, "conf_fwd",
                    jobs=[ride("a_w_out", sibling=(0, E // N_DEV)),
                          ride("w_kv", far=kv_cut[0], near=kv_cut[1])])
    landed()
    wa_out = buf["a_w_out"].reshape(1, E, D)
    x1 = matmul_nn(y, wa_out, "a_out", res=x0, jobs=[ride("w_kv", last=kv_cut[0], far=kv_cut[1])])
    landed()
    hk, hb = rms_fwd(x1, [kvn, bn], "rms_b", jobs=[ride("w_kv", last=kv_cut[1])])
    landed()
    wkv = buf["w_kv"]
    kv = matmul_nn(hk, wkv, "kv_proj", out_dtype=BF16, kv_split=True, jobs=[ride("b_w_in", chips=(0, D))])
    landed()

    bt = jnp.asarray(bucket_tables())
    onehot = (bt[None, :] == jnp.arange(N_BUCKETS, dtype=jnp.int32)[:, None]).astype(F32)
    bias = small_dot(rel_bias.T, onehot, "nn", "bias_table", jobs=[ride("b_w_in", sibling=(0, D))])
    landed()
    bias = bias.reshape(H, N_GROUPS, BLOCK, 2 * BLOCK)
    bias = [bias[:, g] for g in range(N_GROUPS)]
    wb_in = buf["b_w_in"]
    qz = matmul_nn(hb, wb_in, "b_in", jobs=[ride("b_w_out", chips=(0, D // N_DEV))])
    landed()

    os_, lses = [], []
    for g, (_, dil) in enumerate(GROUPS):
        o_g, l_g = attn_fwd(qz, kv, bias[g], g, dil, "attn_fwd%d" % g,
                            jobs=[ride("b_w_out", sibling=(0, D // N_DEV))] if g == 0 else ())
        landed()
        os_.append(o_g)
        lses.append(l_g)
    wb_out = buf["b_w_out"].reshape(1, D, D)
    o, lse, y2 = merge_fwd(os_, lses, qz, "merge_fwd")
    x2 = matmul_nn(y2, wb_out, "b_out", res=x1)
    dx2, dx2b, fin_acc = final_loss(x2, fn, target, "final_loss")

    dy2 = matmul_nt(dx2b, wb_out, "b_out_dx")
    dwb_out = matmul_tn(y2, dx2b, 1, "b_out_dw").reshape(N_DEV, D // N_DEV, D)
    r_b_out = reduce_sibling_job(dwb_out)
    do, delta, dqz = merge_bwd(dy2, o, qz, "merge_bwd", jobs=[r_b_out])
    p_b_out = add_pairs(dwb_out, r_b_out.results[0], "pairs_b_out")
    r_b_out2 = reduce_chips_job(p_b_out)
    dkv = None
    ds_tabs = []
    for g, (_, dil) in enumerate(GROUPS):
        dqz, dkv, ds_tab = attn_bwd(qz, kv, do, lse, delta, bias[g], dqz, dkv, g, dil,
                                    "attn_bwd%d" % g, jobs=[r_b_out2] if g == 0 else ())
        ds_tabs.append(ds_tab.reshape(H, 2 * BLOCK * BLOCK))
    d_rel = small_dot(onehot, jnp.concatenate(ds_tabs, axis=1), "nt", "bias_grad")
    dw_kv_lo = matmul_tn(hk, dkv, N_DEV, "kv_dw_lo", kv_split=True, k_tiles=(0, 1))
    r_kv_lo = reduce_sibling_job(dw_kv_lo)
    dw_kv_hi = matmul_tn(hk, dkv, N_DEV, "kv_dw_hi", kv_split=True, k_tiles=(1, 2), jobs=[r_kv_lo])
    p_kv_lo = add_pairs(dw_kv_lo, r_kv_lo.results[0], "pairs_kv_lo")
    r_kv_lo2, r_kv_hi = reduce_chips_job(p_kv_lo), reduce_sibling_job(dw_kv_hi)
    dhk = matmul_nt(dkv, wkv, "kv_dx", kv_split=True, jobs=[r_kv_lo2, r_kv_hi])
    p_kv_hi = add_pairs(dw_kv_hi, r_kv_hi.results[0], "pairs_kv_hi")
    r_kv_hi2 = reduce_chips_job(p_kv_hi)
    dwb_in = matmul_tn(hb, dqz, N_DEV, "b_in_dw", jobs=[r_kv_hi2])
    r_b_in = reduce_sibling_job(dwb_in)
    dhb = matmul_nt(dqz, wb_in, "b_in_dx", jobs=[r_b_in])
    p_b_in = add_pairs(dwb_in, r_b_in.results[0], "pairs_b_in")
    r_b_in2 = reduce_chips_job(p_b_in)
    dx1, dx1b, norm_acc = rms_bwd(x1, [dhk, dhb], [kvn, bn], dx2, "rms_b_bwd", True)

    dy = matmul_nt(dx1b, wa_out, "a_out_dx")
    dwa_out = matmul_tn(y, dx1b, 1, "a_out_dw").reshape(N_DEV, E // N_DEV, D)
    r_a_out = reduce_sibling_job(dwa_out)
    dc, dz, ln_acc = conf_bwd_ln(c, dy, proj, lg, lb, "conf_bwd_ln", jobs=[r_a_out])
    p_a_out = add_pairs(dwa_out, r_a_out.results[0], "pairs_a_out")
    r_a_out2 = reduce_chips_job(p_a_out)
    dproj, dcw = conf_bwd_conv(proj, dc, dz, cw, "conf_bwd_conv", jobs=[r_b_in2, r_a_out2])
    share = share_small_job(jnp.concatenate([ln_acc, dcw, norm_acc, fin_acc, _pack_rel(d_rel, D)], axis=0))
    dwa_lo = matmul_tn(h0, dproj, N_DEV, "a_in_dw_lo", k_tiles=(0, 1), jobs=[share])
    r_lo = reduce_sibling_job(dwa_lo)
    dwa_hi = matmul_tn(h0, dproj, N_DEV, "a_in_dw_hi", k_tiles=(1, 2), jobs=[r_lo])
    p_lo = add_pairs(dwa_lo, r_lo.results[0], "pairs_a_in_lo")
    r_lo2, r_hi = reduce_chips_job(p_lo), reduce_sibling_job(dwa_hi)
    dh0 = matmul_nt(dproj, wa_in, "a_in_dx_lo", rows=(0, 2), jobs=[r_lo2, r_hi])
    p_hi = add_pairs(dwa_hi, r_hi.results[0], "pairs_a_in_hi")
    r_hi2 = reduce_chips_job(p_hi)
    dh0 = matmul_nt(dproj, wa_in, "a_in_dx_hi", rows=(1, 2), into=dh0, jobs=[r_hi2])
    grad_x, a_acc = rms_bwd(x0, [dh0], [an], dx1, "rms_a_bwd", False)

    share_a = share_small_job(a_acc)
    sums = {"w_kv": [(p_kv_lo, r_kv_lo2), (p_kv_hi, r_kv_hi2)], "b_w_in": [(p_b_in, r_b_in2)], "b_w_out": [(p_b_out, r_b_out2)],
            "a_w_out": [(p_a_out, r_a_out2)], "a_w_in": [(p_lo, r_lo2), (p_hi, r_hi2)]}
    big_out = {}
    for nm, pieces in sums.items():
        big_out[nm] = adamw_reduce([(part, job.results[0]) for part, job in pieces],
                                   big_w[nm], big_m[nm], big_v[nm], "adamw_" + nm,
                                   jobs=[share_a] if nm == "a_w_out" else ())

    gsum = jnp.concatenate([sum_parts(share_a.results[0], "sum_small_a"),
                            sum_parts(share.results[0], "sum_small")], axis=0)
    me = 4 * lax.axis_index("x") + 2 * lax.axis_index("y") + lax.axis_index("c")
    g_sh = lax.dynamic_slice(gsum, (0, me * nsh), (SMALL_SH_ROWS, nsh))
    g_rep = gsum[SMALL_SH_ROWS:]
    loss = g_rep[9, 0]
    sh_m = _pack_sharded(m_a_norm, m_a_conv_w, m_a_conv_b, m_a_ln_g, m_a_ln_b)
    sh_v = _pack_sharded(v_a_norm, v_a_conv_w, v_a_conv_b, v_a_ln_g, v_a_ln_b)
    sh_d, sh_nm, sh_nv = adamw_small(sh_w, g_sh, sh_m, sh_v, "adamw_sharded")
    rep_w = _pack_replicated(kv_norm, b_norm, final_norm, rel_bias, D)
    rep_m = _pack_replicated(m_kv_norm, m_b_norm, m_final_norm, m_rel_bias, D)
    rep_v = _pack_replicated(v_kv_norm, v_b_norm, v_final_norm, v_rel_bias, D)
    rep_d, rep_nm, rep_nv = adamw_small(rep_w, g_rep, rep_m, rep_v, "adamw_replicated")

    def unpack(kind):
        sh = (g_sh, sh_d, sh_nm, sh_nv)[kind]
        rep = (g_rep, rep_d, rep_nm, rep_nv)[kind]
        big = {nm: big_out[nm][kind] for nm in names}
        nb = rel_bias.size
        return [
            sh[0:1],
            big["a_w_in"][None],
            sh[16:16 + CONV_WIDTH][None],
            sh[10:11], sh[8:9], sh[9:10],
            big["a_w_out"][None],
            rep[0],
            big["w_kv"],
            rep[1:2],
            big["b_w_in"][None],
            big["b_w_out"][None],
            rep[16, :nb].reshape(rel_bias.shape),
            rep[8],
        ]

    return (loss, grad_x.reshape(1, S, D), *unpack(0), *unpack(1), *unpack(2), *unpack(3))
```

```python
import functools
import math

import numpy as np
import jax
import jax.numpy as jnp
from jax import lax
from jax.experimental import pallas as pl
from jax.experimental.pallas import tpu as pltpu

F32 = jnp.float32
BF16 = jnp.bfloat16

N_DEV = 8
N_CHIPS = 4
EPS = 1e-6
HEAD_DIM = 128
BLOCK = 128
GROUPS = ((128, 1), (512, 4), (2048, 16))
N_GROUPS = len(GROUPS)
CONV_WIDTH = 31
HALO = 32
N_BUCKETS = 32
MAX_EXACT = N_BUCKETS // 2
MAX_DISTANCE = 2048
V7X_VMEM_BYTES = 64 * 1024 * 1024
VMEM_LIMIT = (V7X_VMEM_BYTES * 7) // 8
MATMUL_VMEM_BUDGET = (V7X_VMEM_BYTES * 11) // 16
LANE = 128

ADAM_LR = 0.001
ADAM_B1 = 0.9
ADAM_B2 = 0.999
ADAM_EPS = 1e-08
ADAM_WD = 0.01
ADAM_STEP = 10

SMALL_SH_ROWS = 48
SMALL_REP_ROWS = 24
MESH = pl.DeviceIdType.MESH
ANY_SPEC = pl.BlockSpec(memory_space=pl.ANY)
VMEM_SPEC = pl.BlockSpec(memory_space=pltpu.VMEM)


def _tile(dim, pref, unit=LANE):
    if dim <= pref:
        return dim
    t = (pref // unit) * unit
    while dim % t:
        t -= unit
    assert t > 0
    return t


def _sigmoid(v):
    return jax.nn.sigmoid(v)


def _place():
    return lax.axis_index("x"), lax.axis_index("y"), lax.axis_index("c")


def _flip(v, bit):
    return 1 - v if bit else v


class Job:
    def __init__(self, srcs, dsts, n_sems, build):
        self.srcs, self.dsts, self.n_sems, self.build = list(srcs), list(dsts), n_sems, build
        self.results = None


def _remote(src, dst, send_sems, recv_sems, k, peer):
    return pltpu.make_async_remote_copy(src_ref=src, dst_ref=dst, send_sem=send_sems.at[k],
                                        recv_sem=recv_sems.at[k], device_id=peer, device_id_type=MESH)


def _call(body, *, name, in_specs, out_specs, out_shape, args, grid=(), scratch_shapes=(), sem=(),
          aliases=None, jobs=()):
    n_in, n_out, n_scr = len(in_specs), len(out_specs), len(scratch_shapes)
    aliases = dict(aliases or {})
    x_in, x_out, x_scr = [], [], []
    for job in jobs:
        job.in_at = n_in + len(x_in)
        x_in += job.srcs
        job.out_at = n_out + len(x_out)
        for d in job.dsts:
            if not isinstance(d, jax.ShapeDtypeStruct):
                aliases[n_in + len(x_in)] = n_out + len(x_out)
                x_in.append(d)
            x_out.append(jax.ShapeDtypeStruct(d.shape, d.dtype))
        job.scr_at = n_scr + len(x_scr)
        x_scr += [pltpu.SemaphoreType.DMA((job.n_sems,))] * 3

    def wrapped(*refs):
        ins = refs[:n_in + len(x_in)]
        outs = refs[len(ins):len(ins) + n_out + len(x_out)]
        scr = refs[len(ins) + len(outs):]
        core = ins[:n_in] + outs[:n_out] + scr[:n_scr]
        if not jobs:
            body(*core)
            return
        copies = []
        for job in jobs:
            copies += job.build(ins[job.in_at:job.in_at + len(job.srcs)],
                                outs[job.out_at:job.out_at + len(job.dsts)],
                                *scr[job.scr_at:job.scr_at + 3])
        if grid:
            pids = [pl.program_id(d) for d in range(len(grid))]
            first = functools.reduce(jnp.logical_and, [p == 0 for p in pids])
            last = functools.reduce(jnp.logical_and, [p == g - 1 for p, g in zip(pids, grid)])

            @pl.when(first)
            def _():
                for cp in copies:
                    cp.start()

            body(*core)

            @pl.when(last)
            def _():
                for cp in copies:
                    cp.wait()
        else:
            for cp in copies:
                cp.start()
            body(*core)
            for cp in copies:
                cp.wait()

    if jobs:
        sem = ("arbitrary",) * len(grid)
    kwargs = dict(grid=grid) if grid else {}
    if aliases:
        kwargs["input_output_aliases"] = aliases
    outs = pl.pallas_call(
        wrapped, name=name,
        in_specs=list(in_specs) + [ANY_SPEC] * len(x_in),
        out_specs=list(out_specs) + [ANY_SPEC] * len(x_out),
        out_shape=list(out_shape) + x_out,
        scratch_shapes=list(scratch_shapes) + x_scr,
        compiler_params=pltpu.CompilerParams(dimension_semantics=sem if sem else None,
                                             vmem_limit_bytes=VMEM_LIMIT),
        **kwargs,
    )(*args, *x_in)
    for job in jobs:
        job.results = list(outs[job.out_at:job.out_at + len(job.dsts)])
    return list(outs[:n_out])


def place_shard(w, name):
    K, n = w.shape
    tk = _tile(K, 512, 16)

    def body(p_ref, w_ref, o_ref):
        o_ref[...] = w_ref[...].astype(BF16)

    grid_spec = pltpu.PrefetchScalarGridSpec(
        num_scalar_prefetch=1, grid=(K // tk,),
        in_specs=[pl.BlockSpec((tk, n), lambda i, p: (i, 0))],
        out_specs=pl.BlockSpec((None, tk, n), lambda i, p: (4 * p[0] + 2 * p[1] + p[2], i, 0)))
    return pl.pallas_call(body, name=name, grid_spec=grid_spec,
                          out_shape=jax.ShapeDtypeStruct((N_DEV, K, n), BF16))(_place_vector(), w)


def _place_vector():
    return jnp.stack(_place()).astype(jnp.int32)


def gather_job(buf, chips=None, sibling=None):
    def build(srcs, dsts, send, recv, loc):
        (out,) = dsts
        x, y, c = _place()
        copies = []
        if chips is not None:
            mine = out.at[4 * x + 2 * y + c, pl.ds(chips[0], chips[1] - chips[0])]
            peers = [(x, y, 1 - c), (1 - x, y, c), (x, 1 - y, c), (1 - x, 1 - y, c)]
            copies += [_remote(mine, mine, send, recv, k, p) for k, p in enumerate(peers)]
        if sibling is not None:
            for k, (cx, cy) in enumerate([(1 - x, y), (x, 1 - y), (1 - x, 1 - y)]):
                blk = out.at[4 * cx + 2 * cy + c, pl.ds(sibling[0], sibling[1] - sibling[0])]
                copies.append(_remote(blk, blk, send, recv, 4 + k, (x, y, 1 - c)))
        return copies

    return Job([], [buf], 7, build)


def _relay(x, y, c):
    return ((x + 1 - c) % 2, (y + c) % 2), ((x + c) % 2, (y + 1 - c) % 2)


def gather_relay_job(buf, near=None, far=None, last=None):
    def rows(slot, rng):
        return slot.at[pl.ds(rng[0], rng[1] - rng[0])]

    def build(srcs, dsts, send, recv, loc):
        (out,) = dsts
        x, y, c = _place()
        sib = (x, y, 1 - c)
        copies = []
        if near is not None:
            mine = rows(out.at[4 * x + 2 * y + c], near)
            for k, p in enumerate([sib, (1 - x, y, c), (x, 1 - y, c)]):
                copies.append(_remote(mine, mine, send, recv, k, p))
        if far is not None:
            (fx, fy), (tx, ty) = _relay(x, y, c)
            blk = rows(out.at[4 * fx + 2 * fy + c], far)
            copies.append(_remote(blk, blk, send, recv, 3, (tx, ty, c)))
            for k, (cx, cy) in enumerate([(1 - x, y), (x, 1 - y)]):
                blk = rows(out.at[4 * cx + 2 * cy + c], far)
                copies.append(_remote(blk, blk, send, recv, 4 + k, sib))
        if last is not None:
            blk = rows(out.at[4 * (1 - x) + 2 * (1 - y) + c], last)
            copies.append(_remote(blk, blk, send, recv, 6, sib))
        return copies

    return Job([], [buf], 7, build)


def reduce_sibling_job(dw):
    def build(srcs, dsts, send, recv, loc):
        (src,), (got,) = srcs, dsts
        x, y, c = _place()
        return [_remote(src.at[4 * _flip(x, r & 2) + 2 * _flip(y, r & 1) + 1 - c], got.at[r], send, recv, r,
                        (x, y, 1 - c)) for r in range(N_CHIPS)]

    return Job([dw], [jax.ShapeDtypeStruct((N_CHIPS,) + dw.shape[1:], dw.dtype)], N_CHIPS, build)


def reduce_chips_job(part):
    def build(srcs, dsts, send, recv, loc):
        (src,), (got,) = srcs, dsts
        x, y, c = _place()
        return [_remote(src.at[r], got.at[r - 1], send, recv, r - 1, (_flip(x, r & 2), _flip(y, r & 1), c))
                for r in range(1, N_CHIPS)]

    return Job([part], [jax.ShapeDtypeStruct((N_CHIPS - 1,) + part.shape[1:], part.dtype)], N_CHIPS - 1, build)


def share_small_job(small):
    def build(srcs, dsts, send, recv, loc):
        (src,), (out,) = srcs, dsts
        x, y, c = _place()
        mine = out.at[4 * x + 2 * y + c]
        copies = [pltpu.make_async_copy(src, mine, loc.at[0])]
        for rel in range(1, N_DEV):
            peer = (_flip(x, rel & 4), _flip(y, rel & 2), _flip(c, rel & 1))
            copies.append(_remote(src, mine, send, recv, rel - 1, peer))
        return copies

    return Job([small], [jax.ShapeDtypeStruct((N_DEV,) + small.shape, small.dtype)], N_DEV - 1, build)


def all_gather(shards, name):
    n = len(shards)

    def body(*refs):
        ins, outs = refs[:n], refs[n:2 * n]
        send_sems, recv_sems, local_sems = refs[2 * n:]
        x, y, c = _place()
        me, sibling = (x, y, c), (x, y, 1 - c)
        near = [(1 - x, y), (x, 1 - y)]
        diag = (1 - x, 1 - y)
        frm, to = _relay(x, y, c)

        def slot(a, dev):
            return outs[a].at[4 * dev[0] + 2 * dev[1] + dev[2]]

        def copy(a, k, block, to_dev, src=None):
            return pltpu.make_async_remote_copy(
                src_ref=slot(a, block) if src is None else src, dst_ref=slot(a, block),
                send_sem=send_sems.at[a, k], recv_sem=recv_sems.at[a, k],
                device_id=to_dev, device_id_type=MESH)

        mine, sent = [], []
        for a in range(n):
            mine.append(pltpu.make_async_copy(ins[a], slot(a, me), local_sems.at[a]))
            mine[a].start()
            sent.append([copy(a, 0, me, sibling, src=ins[a])]
                        + [copy(a, 1 + j, me, (*chip, c), src=ins[a]) for j, chip in enumerate(near)])
            for cp in sent[a]:
                cp.start()
        for a in range(n):
            for j, chip in enumerate(near):
                copy(a, 1 + j, (*chip, c), me).wait_recv()
            more = [copy(a, 3, (*frm, c), (*to, c))] + [copy(a, 4 + j, (*chip, c), sibling)
                                                       for j, chip in enumerate(near)]
            for cp in more:
                cp.start()
            sent[a] += more
        for a in range(n):
            copy(a, 3, (*diag, c), me).wait_recv()
            sent[a].append(copy(a, 6, (*diag, c), sibling))
            sent[a][-1].start()
        for a in range(n):
            for k, block in [(0, sibling), (4, (*near[0], 1 - c)), (5, (*near[1], 1 - c)), (6, (*diag, 1 - c))]:
                copy(a, k, block, me).wait_recv()
            for cp in sent[a]:
                cp.wait_send()
            mine[a].wait()

    return pl.pallas_call(
        body, name=name,
        in_specs=[ANY_SPEC] * n, out_specs=[ANY_SPEC] * n,
        out_shape=[jax.ShapeDtypeStruct((N_DEV,) + s.shape, s.dtype) for s in shards],
        scratch_shapes=[pltpu.SemaphoreType.DMA((n, 7)), pltpu.SemaphoreType.DMA((n, 7)),
                        pltpu.SemaphoreType.DMA((n,))],
    )(*shards)


def _kv_split_index(tw, D):
    pd = D // tw

    def index(j):
        return (j // pd) % 2, (j // (2 * pd)) * pd + j % pd

    return index


def _col_tile(n, also, pref):
    t = (min(pref, n) // LANE) * LANE
    while n % t or (also is not None and also % t):
        t -= LANE
    assert t > 0
    return t


def matmul_nn(a, w, name, res=None, out_dtype=F32, kv_split=False, jobs=()):
    M, K = a.shape
    nb, _, n = w.shape
    D = nb * n // (2 * N_GROUPS)
    tn = _col_tile(n, D if kv_split else None, 1024)
    out_bytes = jnp.dtype(out_dtype).itemsize + (4 if res is not None else 0)
    tm = _tile(M, 2048)
    if 2 * (tm * K * 2 + K * tn * 2 + tm * tn * out_bytes) > MATMUL_VMEM_BUDGET:
        tm = _tile(M, 1024)
    per = n // tn

    def body(*refs):
        if res is None:
            a_ref, w_ref, o_ref = refs
        else:
            a_ref, w_ref, r_ref, o_ref = refs
        acc = jnp.dot(a_ref[...], w_ref[...], preferred_element_type=F32)
        if res is not None:
            acc = r_ref[...] + acc
        o_ref[...] = acc.astype(o_ref.dtype)

    in_specs = [pl.BlockSpec((tm, K), lambda i, j: (i, 0)),
                pl.BlockSpec((None, K, tn), lambda i, j: (j // per, 0, j % per))]
    args = [a, w]
    if res is not None:
        in_specs.append(pl.BlockSpec((tm, tn), lambda i, j: (i, j)))
        args.append(res)
    if kv_split:
        split = _kv_split_index(tn, D)
        out_spec = pl.BlockSpec((None, tm, tn), lambda i, j: (split(j)[0], i, split(j)[1]))
        out_shape = jax.ShapeDtypeStruct((2, M, N_GROUPS * D), out_dtype)
    else:
        out_spec = pl.BlockSpec((tm, tn), lambda i, j: (i, j))
        out_shape = jax.ShapeDtypeStruct((M, nb * n), out_dtype)
    return _call(body, name=name, grid=(M // tm, nb * per), in_specs=in_specs, out_specs=[out_spec],
                 out_shape=[out_shape], args=args, sem=("parallel", "parallel"), jobs=jobs)[0]


def matmul_nt(dy, w, name, kv_split=False, rows=None, into=None, jobs=()):
    M = dy.shape[-2]
    nb, K, n = w.shape
    D = nb * n // (2 * N_GROUPS)
    tm = _tile(M, 1024)
    tc = _col_tile(n, D if kv_split else None, 1024)
    per = n // tc
    pair = max(u for u in (1, 2, 4) if (nb * per) % u == 0 and u * tc <= 2048)
    p, parts = rows or (0, 1)
    mt = M // tm // parts
    i0 = p * mt

    steps = nb * per // pair

    def body(*refs):
        o_ref, acc_ref = refs[-2], refs[-1]
        j = pl.program_id(1)
        lhs = jnp.concatenate([refs[u][...] for u in range(pair)], axis=1)
        rhs = jnp.concatenate([refs[pair + u][...] for u in range(pair)], axis=1)
        part = lax.dot_general(lhs, rhs, (((1,), (1,)), ((), ())), preferred_element_type=F32)

        @pl.when(j == 0)
        def _():
            acc_ref[...] = part

        @pl.when(j > 0)
        def _():
            acc_ref[...] += part

        @pl.when(j == steps - 1)
        def _():
            o_ref[...] = acc_ref[...].astype(o_ref.dtype)

    def dy_spec(u):
        if kv_split:
            split = _kv_split_index(tc, D)
            return pl.BlockSpec((None, tm, tc),
                                lambda i, j: (split(pair * j + u)[0], i0 + i, split(pair * j + u)[1]))
        return pl.BlockSpec((tm, tc), lambda i, j: (i0 + i, pair * j + u))

    def w_spec(u):
        return pl.BlockSpec((None, K, tc), lambda i, j: ((pair * j + u) // per, 0, (pair * j + u) % per))

    in_specs = [dy_spec(u) for u in range(pair)] + [w_spec(u) for u in range(pair)]
    args = [dy] * pair + [w] * pair
    aliases = None
    if into is not None:
        aliases = {len(args): 0}
        in_specs.append(ANY_SPEC)
        args.append(into)
    return _call(body, name=name, grid=(mt, steps), in_specs=in_specs,
                 out_specs=[pl.BlockSpec((tm, K), lambda i, j: (i0 + i, 0))],
                 out_shape=[jax.ShapeDtypeStruct((M, K), BF16)], args=args, aliases=aliases,
                 scratch_shapes=[pltpu.VMEM((tm, K), F32)],
                 sem=("parallel", "arbitrary"), jobs=jobs)[0]


def matmul_tn(a, dy, nb, name, out_dtype=BF16, kv_split=False, k_tiles=None, jobs=()):
    M, K = a.shape
    N = 2 * dy.shape[-1] if kv_split else dy.shape[-1]
    n = N // nb
    D = N // (2 * N_GROUPS)
    tn = _col_tile(n, D if kv_split else None, 1024)
    per = n // tn
    tk = _tile(K, 1024)

    def body(a_ref, dy_ref, o_ref):
        o_ref[...] = lax.dot_general(a_ref[...], dy_ref[...], (((0,), (0,)), ((), ())),
                                     preferred_element_type=F32).astype(o_ref.dtype)

    if kv_split:
        split = _kv_split_index(tn, D)
        dy_spec = pl.BlockSpec((None, M, tn), lambda k, j: (split(j)[0], 0, split(j)[1]))
    else:
        dy_spec = pl.BlockSpec((M, tn), lambda k, j: (0, j))
    k0, k1 = k_tiles or (0, K // tk)
    return _call(body, name=name, grid=(k1 - k0, nb * per),
                 in_specs=[pl.BlockSpec((M, tk), lambda k, j: (0, k0 + k)), dy_spec],
                 out_specs=[pl.BlockSpec((None, tk, tn), lambda k, j: (j // per, k, j % per))],
                 out_shape=[jax.ShapeDtypeStruct((nb, (k1 - k0) * tk, n), out_dtype)], args=[a, dy],
                 sem=("parallel", "parallel"), jobs=jobs)[0]


def small_dot(a, b, contract, name, jobs=()):
    if contract == "nn":
        dims = (((1,), (0,)), ((), ()))
        out = (a.shape[0], b.shape[1])
    else:
        dims = (((1,), (1,)), ((), ()))
        out = (a.shape[0], b.shape[0])

    def body(a_ref, b_ref, o_ref):
        o_ref[...] = lax.dot_general(a_ref[...], b_ref[...], dims, precision=lax.Precision.HIGHEST,
                                     preferred_element_type=F32)

    return _call(body, name=name, in_specs=[VMEM_SPEC, VMEM_SPEC], out_specs=[VMEM_SPEC],
                 out_shape=[jax.ShapeDtypeStruct(out, F32)], args=[a, b], jobs=jobs)[0]


def rms_fwd(x, gains, name, jobs=()):
    S, D = x.shape
    T = _tile(S, 512, 8)
    n = len(gains)

    def body(x_ref, *refs):
        xv = x_ref[...]
        xn = xv * lax.rsqrt(jnp.mean(xv * xv, axis=-1, keepdims=True) + EPS)
        for g_ref, o_ref in zip(refs[:n], refs[n:]):
            o_ref[...] = (xn * g_ref[...]).astype(o_ref.dtype)

    row = pl.BlockSpec((T, D), lambda i: (i, 0))
    vec = pl.BlockSpec((1, D), lambda i: (0, 0))
    return _call(body, name=name, grid=(S // T,), in_specs=[row] + [vec] * n, out_specs=[row] * n,
                 out_shape=[jax.ShapeDtypeStruct((S, D), BF16)] * n, args=[x, *gains],
                 sem=("parallel",), jobs=jobs)


def rms_bwd(x, dhs, gains, dres, name, want_bf16, jobs=()):
    S, D = x.shape
    T = _tile(S, 256, 8)
    n = len(gains)

    def body(x_ref, *refs):
        dh_refs = refs[:n]
        g_refs = refs[n:2 * n]
        dres_ref = refs[2 * n]
        outs = refs[2 * n + 1:]
        dx_ref, dg_ref = outs[0], outs[-1]
        i = pl.program_id(0)

        @pl.when(i == 0)
        def _():
            dg_ref[...] = jnp.zeros_like(dg_ref)

        xv = x_ref[...]
        r = lax.rsqrt(jnp.mean(xv * xv, axis=-1, keepdims=True) + EPS)
        xn = xv * r
        dxn = jnp.zeros_like(xv)
        for k in range(n):
            dh = dh_refs[k][...]
            dg_ref[k:k + 1, :] += jnp.sum(dh * xn, axis=0, keepdims=True)
            dxn = dxn + dh * g_refs[k][...]
        dx = dres_ref[...] + r * (dxn - xn * jnp.mean(dxn * xn, axis=-1, keepdims=True))
        dx_ref[...] = dx
        if want_bf16:
            outs[1][...] = dx.astype(BF16)

    row = pl.BlockSpec((T, D), lambda i: (i, 0))
    vec = pl.BlockSpec((1, D), lambda i: (0, 0))
    acc = pl.BlockSpec((8, D), lambda i: (0, 0))
    out_specs = [row] + ([row] if want_bf16 else []) + [acc]
    out_shape = ([jax.ShapeDtypeStruct((S, D), F32)]
                 + ([jax.ShapeDtypeStruct((S, D), BF16)] if want_bf16 else [])
                 + [jax.ShapeDtypeStruct((8, D), F32)])
    return _call(body, name=name, grid=(S // T,), in_specs=[row] + [row] * n + [vec] * n + [row],
                 out_specs=out_specs, out_shape=out_shape, args=[x, *dhs, *gains, dres],
                 sem=("arbitrary",), jobs=jobs)


def final_loss(x2, gain, target, name, jobs=()):
    S, D = x2.shape
    T = _tile(S, 256, 8)

    def body(x_ref, g_ref, t_ref, dx_ref, dxb_ref, acc_ref):
        i = pl.program_id(0)

        @pl.when(i == 0)
        def _():
            acc_ref[...] = jnp.zeros_like(acc_ref)

        xv = x_ref[...]
        g = g_ref[...]
        r = lax.rsqrt(jnp.mean(xv * xv, axis=-1, keepdims=True) + EPS)
        xn = xv * r
        err = xn * g - t_ref[...]
        dy = err * (1.0 / D)
        acc_ref[0:1, :] += jnp.sum(dy * xn, axis=0, keepdims=True)
        acc_ref[1:2, :] += jnp.full((1, D), 0.5 / D, F32) * jnp.sum(err * err)
        dxn = dy * g
        dx = r * (dxn - xn * jnp.mean(dxn * xn, axis=-1, keepdims=True))
        dx_ref[...] = dx
        dxb_ref[...] = dx.astype(BF16)

    row = pl.BlockSpec((T, D), lambda i: (i, 0))
    return _call(body, name=name, grid=(S // T,),
                 in_specs=[row, pl.BlockSpec((1, D), lambda i: (0, 0)), row],
                 out_specs=[row, row, pl.BlockSpec((8, D), lambda i: (0, 0))],
                 out_shape=[jax.ShapeDtypeStruct((S, D), F32), jax.ShapeDtypeStruct((S, D), BF16),
                            jax.ShapeDtypeStruct((8, D), F32)],
                 args=[x2, gain, target], sem=("arbitrary",), jobs=jobs)


ROW_CHUNK = 64
LANE_CHUNK = 512
SUBLANES = 8


def _shifted_copies(buf, sh_scr, l0, lc):
    n = buf.shape[0] - SUBLANES
    for b in range(1, SUBLANES):
        sh_scr[b - 1, 0:n, :] = buf[b:b + n, l0:l0 + lc]


def _shifted(buf, sh_scr, start, rows, l0, lc):
    a8, b = (start // SUBLANES) * SUBLANES, start % SUBLANES
    if b == 0:
        return buf[a8:a8 + rows, l0:l0 + lc]
    return sh_scr[b - 1, a8:a8 + rows, :]


def conf_fwd(proj, cw, cb, lg, lb, name, jobs=()):
    S, E3 = proj.shape
    E = E3 // 3
    T = _tile(S, 256, HALO)
    R = T // HALO
    lc = _tile(E, LANE_CHUNK)
    rc = min(ROW_CHUNK, T)

    def body(a_ref, b_ref, z_ref, ap_ref, bp_ref, cw_ref, cb_ref, lg_ref, lb_ref, c_ref, y_ref, u_scr, sh_scr):
        i = pl.program_id(0)
        up = ap_ref[...] * _sigmoid(bp_ref[...])
        u_scr[0:HALO, :] = jnp.where(i > 0, up, 0.0)
        u_scr[HALO:HALO + T, :] = a_ref[...] * _sigmoid(b_ref[...])
        off = HALO - (CONV_WIDTH - 1)
        for l0 in range(0, E, lc):
            _shifted_copies(u_scr, sh_scr, l0, lc)
            for r0 in range(0, T, rc):
                acc = jnp.broadcast_to(cb_ref[:, l0:l0 + lc], (rc, lc))
                for k in range(CONV_WIDTH):
                    acc = acc + _shifted(u_scr, sh_scr, r0 + off + k, rc, l0, lc) * cw_ref[k:k + 1, l0:l0 + lc]
                c_ref[r0:r0 + rc, l0:l0 + lc] = acc
        c = c_ref[...]
        mu = jnp.mean(c, axis=-1, keepdims=True)
        d = c - mu
        var = jnp.mean(d * d, axis=-1, keepdims=True)
        cn = d * lax.rsqrt(var + EPS) * lg_ref[...] + lb_ref[...]
        z = z_ref[...]
        y_ref[...] = ((cn * _sigmoid(cn)) * (z * _sigmoid(z))).astype(BF16)

    def col(j):
        return pl.BlockSpec((T, E), lambda i: (i, j))

    def prev(j):
        return pl.BlockSpec((HALO, E), lambda i: (jnp.maximum(i * R - 1, 0), j))

    vec = pl.BlockSpec((1, E), lambda i: (0, 0))
    return _call(body, name=name, grid=(S // T,),
                 in_specs=[col(0), col(1), col(2), prev(0), prev(1),
                           pl.BlockSpec((HALO, E), lambda i: (0, 0)), vec, vec, vec],
                 out_specs=[pl.BlockSpec((T, E), lambda i: (i, 0))] * 2,
                 out_shape=[jax.ShapeDtypeStruct((S, E), F32), jax.ShapeDtypeStruct((S, E), BF16)],
                 scratch_shapes=[pltpu.VMEM((HALO + T, E), F32), pltpu.VMEM((SUBLANES - 1, HALO + T, lc), F32)],
                 args=[proj, proj, proj, proj, proj, cw, cb, lg, lb], sem=("parallel",), jobs=jobs)


def conf_bwd_ln(c, dy, proj, lg, lb, name, jobs=()):
    S, E = c.shape
    T = _tile(S, 256, 8)

    def body(c_ref, dy_ref, z_ref, lg_ref, lb_ref, dc_ref, dz_ref, acc_ref):
        i = pl.program_id(0)

        @pl.when(i == 0)
        def _():
            acc_ref[...] = jnp.zeros_like(acc_ref)

        cv = c_ref[...]
        mu = jnp.mean(cv, axis=-1, keepdims=True)
        d = cv - mu
        var = jnp.mean(d * d, axis=-1, keepdims=True)
        rstd = lax.rsqrt(var + EPS)
        xh = d * rstd
        lgv = lg_ref[...]
        cn = xh * lgv + lb_ref[...]
        z = z_ref[...]
        dy = dy_ref[...]
        sc = _sigmoid(cn)
        sz = _sigmoid(z)
        dcn = dy * (z * sz) * (sc * (1.0 + cn * (1.0 - sc)))
        dz_ref[...] = (dy * (cn * sc) * (sz * (1.0 + z * (1.0 - sz)))).astype(BF16)
        acc_ref[0:1, :] += jnp.sum(dcn * xh, axis=0, keepdims=True)
        acc_ref[1:2, :] += jnp.sum(dcn, axis=0, keepdims=True)
        dxh = dcn * lgv
        dc = rstd * (dxh - jnp.mean(dxh, axis=-1, keepdims=True)
                     - xh * jnp.mean(dxh * xh, axis=-1, keepdims=True))
        acc_ref[2:3, :] += jnp.sum(dc, axis=0, keepdims=True)
        dc_ref[...] = dc

    row = pl.BlockSpec((T, E), lambda i: (i, 0))
    vec = pl.BlockSpec((1, E), lambda i: (0, 0))
    return _call(body, name=name, grid=(S // T,),
                 in_specs=[row, row, pl.BlockSpec((T, E), lambda i: (i, 2)), vec, vec],
                 out_specs=[row, row, pl.BlockSpec((8, E), lambda i: (0, 0))],
                 out_shape=[jax.ShapeDtypeStruct((S, E), F32), jax.ShapeDtypeStruct((S, E), BF16),
                            jax.ShapeDtypeStruct((8, E), F32)],
                 args=[c, dy, proj, lg, lb], sem=("arbitrary",), jobs=jobs)


def conf_bwd_conv(proj, dc, dz, cw, name, jobs=()):
    S, E3 = proj.shape
    E = E3 // 3
    T = _tile(S, 256, HALO)
    R = T // HALO
    nt = S // T
    lc = _tile(E, LANE_CHUNK)
    rc = min(ROW_CHUNK, T)
    rd = min(ROW_CHUNK // 2, T)

    def body(a_ref, b_ref, ap_ref, bp_ref, dc_ref, dcn_ref, dz_ref, cw_ref, o_ref, dw_ref, u_scr, dc_scr, sh_scr,
             dw_scr):
        i = pl.program_id(0)

        @pl.when(i == 0)
        def _():
            dw_scr[...] = jnp.zeros_like(dw_scr)

        a = a_ref[...]
        sb = _sigmoid(b_ref[...])
        up = ap_ref[...] * _sigmoid(bp_ref[...])
        u_scr[0:HALO, :] = jnp.where(i > 0, up, 0.0)
        u_scr[HALO:HALO + T, :] = a * sb
        dc_scr[0:T, :] = dc_ref[...]
        dc_scr[T:T + HALO, :] = jnp.where(i < nt - 1, dcn_ref[...], 0.0)
        off = HALO - (CONV_WIDTH - 1)
        for l0 in range(0, E, lc):
            _shifted_copies(u_scr, sh_scr, l0, lc)
            for r0 in range(0, T, rd):
                dcc = dc_scr[r0:r0 + rd, l0:l0 + lc]
                for k in range(CONV_WIDTH):
                    prod = _shifted(u_scr, sh_scr, r0 + off + k, rd, l0, lc) * dcc
                    dw_scr[k, :, l0:l0 + lc] += jnp.sum(prod.reshape(rd // SUBLANES, SUBLANES, lc), axis=0)
            _shifted_copies(dc_scr, sh_scr, l0, lc)
            for r0 in range(0, T, rc):
                acc = jnp.zeros((rc, lc), F32)
                for k in range(CONV_WIDTH):
                    s0 = r0 + (CONV_WIDTH - 1) - k
                    acc = acc + _shifted(dc_scr, sh_scr, s0, rc, l0, lc) * cw_ref[k:k + 1, l0:l0 + lc]
                av = a[r0:r0 + rc, l0:l0 + lc]
                sv = sb[r0:r0 + rc, l0:l0 + lc]
                o_ref[r0:r0 + rc, l0:l0 + lc] = (acc * sv).astype(BF16)
                o_ref[r0:r0 + rc, E + l0:E + l0 + lc] = (acc * av * sv * (1.0 - sv)).astype(BF16)
        o_ref[:, 2 * E:3 * E] = dz_ref[...]

        @pl.when(i == nt - 1)
        def _():
            dw_ref[...] = jnp.sum(dw_scr[...], axis=1)

    def col(j):
        return pl.BlockSpec((T, E), lambda i: (i, j))

    def prev(j):
        return pl.BlockSpec((HALO, E), lambda i: (jnp.maximum(i * R - 1, 0), j))

    row = pl.BlockSpec((T, E), lambda i: (i, 0))
    nxt = pl.BlockSpec((HALO, E), lambda i: (jnp.minimum((i + 1) * R, S // HALO - 1), 0))
    return _call(body, name=name, grid=(nt,),
                 in_specs=[col(0), col(1), prev(0), prev(1), row, nxt, row,
                           pl.BlockSpec((HALO, E), lambda i: (0, 0))],
                 out_specs=[pl.BlockSpec((T, E3), lambda i: (i, 0)), pl.BlockSpec((HALO, E), lambda i: (0, 0))],
                 out_shape=[jax.ShapeDtypeStruct((S, E3), BF16), jax.ShapeDtypeStruct((HALO, E), F32)],
                 scratch_shapes=[pltpu.VMEM((HALO + T, E), F32), pltpu.VMEM((T + HALO, E), F32),
                                 pltpu.VMEM((SUBLANES - 1, HALO + T, lc), F32),
                                 pltpu.VMEM((HALO, SUBLANES, E), F32)],
                 args=[proj, proj, proj, proj, dc, dc, dz, cw], sem=("arbitrary",), jobs=jobs)


def bucket_tables():
    q = np.arange(BLOCK)[:, None]
    k = np.arange(2 * BLOCK)[None, :]
    out = []
    for window, dil in GROUPS:
        delta = q + BLOCK - k
        valid = (delta >= 0) & (delta <= window // dil)
        dist = np.clip(delta, 0, None) * dil
        large = MAX_EXACT + (np.log(np.maximum(dist, 1).astype(np.float32) / MAX_EXACT)
                             / math.log(MAX_DISTANCE / MAX_EXACT)
                             * (N_BUCKETS - MAX_EXACT)).astype(np.int32)
        large = np.minimum(large, N_BUCKETS - 1)
        bucket = np.where(dist < MAX_EXACT, dist, large)
        out.append(np.where(valid, bucket, -1).reshape(-1))
    return np.concatenate(out).astype(np.int32)


def _band_masks(has_previous):
    ql = lax.broadcasted_iota(jnp.int32, (BLOCK, 2 * BLOCK), 0)
    kk = lax.broadcasted_iota(jnp.int32, (BLOCK, 2 * BLOCK), 1)
    band = (kk >= ql) & (kk <= ql + BLOCK)
    return band, band & ((kk >= BLOCK) | has_previous)


def _dot_nt(a, b):
    return lax.dot_general(a, b, (((1,), (1,)), ((), ())), preferred_element_type=F32)


def _dot_tn(a, b):
    return lax.dot_general(a, b, (((0,), (0,)), ((), ())), preferred_element_type=F32)


ATTN_ROWS = 2048


def _sub(start, size, dil):
    return pl.ds(start, size) if dil == 1 else pl.ds(start, size, stride=dil)


def _attn_geometry(S, dil):
    halo = BLOCK * dil
    rows = max(min(S, ATTN_ROWS), halo)
    return halo, rows, S // rows, rows // halo


MAX_ROW_STRIDE = 8


def _split(dst, src, n, dil, tmp):
    if dil <= MAX_ROW_STRIDE:
        for r in range(dil):
            dst[r] = src[_sub(r, n, dil), :].astype(dst.dtype)
        return
    f, g = 4, dil // 4
    for r1 in range(f):
        tmp[0:n * g, :] = src[_sub(r1, n * g, f), :]
        for r2 in range(g):
            dst[r2 * f + r1] = tmp[_sub(r2, n, g), :].astype(dst.dtype)


def _merge(dst, src, n, dil, tmp):
    if dil <= MAX_ROW_STRIDE:
        for r in range(dil):
            dst[_sub(r, n, dil), :] = src[r]
        return
    f, g = 4, dil // 4
    for r1 in range(f):
        for r2 in range(g):
            tmp[_sub(r2, n, g), :] = src[r2 * f + r1]
        dst[_sub(r1, n * g, f), :] = tmp[0:n * g, :]


def _split_tmp(sub, dil):
    rows = (sub + BLOCK) * (dil // 4) if dil > MAX_ROW_STRIDE else SUBLANES
    return pltpu.VMEM((rows, HEAD_DIM), F32)


def attn_fwd(qz, kv, bias, g, dil, name, jobs=()):
    S = qz.shape[0]
    D = qz.shape[1] // (N_GROUPS + 1)
    H = D // HEAD_DIM
    halo, rows, nsb, nblk = _attn_geometry(S, dil)
    sub = nblk * BLOCK
    scale = HEAD_DIM ** -0.5

    def body(q_ref, kvc_ref, kvp_ref, b_ref, o_ref, l_ref, ks, vs, qd, kd, vd, od, ld, tmp):
        sb = pl.program_id(1)
        ks[0:halo, :] = kvp_ref[0].astype(F32)
        ks[halo:, :] = kvc_ref[0].astype(F32)
        vs[0:halo, :] = kvp_ref[1].astype(F32)
        vs[halo:, :] = kvc_ref[1].astype(F32)
        _split(kd, ks, sub + BLOCK, dil, tmp)
        _split(vd, vs, sub + BLOCK, dil, tmp)
        _split(qd, q_ref, sub, dil, tmp)
        band, first = _band_masks(sb > 0)
        bias_t = b_ref[0]
        for r in range(dil):
            for jj in range(nblk):
                q = qd[r, jj * BLOCK:(jj + 1) * BLOCK, :]
                keys = slice(jj * BLOCK, (jj + 2) * BLOCK)
                s = _dot_nt(q, kd[r, keys, :]) * scale + bias_t
                s = jnp.where(band if jj > 0 else first, s, -jnp.inf)
                m = jnp.max(s, axis=-1, keepdims=True)
                p = jnp.exp(s - m)
                den = jnp.sum(p, axis=-1, keepdims=True)
                pv = jnp.dot(p.astype(BF16), vd[r, keys, :], preferred_element_type=F32)
                own = slice(jj * BLOCK, (jj + 1) * BLOCK)
                od[r, own, :] = pv / den
                ld[r, own, :] = jnp.broadcast_to(m + jnp.log(den), (BLOCK, HEAD_DIM))
        _merge(o_ref, od, sub, dil, tmp)
        _merge(l_ref, ld, sub, dil, tmp)

    per = rows // halo
    out_spec = pl.BlockSpec((rows, HEAD_DIM), lambda h, sb: (sb, h))
    return _call(body, name=name, grid=(H, nsb),
                 in_specs=[pl.BlockSpec((rows, HEAD_DIM), lambda h, sb: (sb, g * H + h)),
                           pl.BlockSpec((2, rows, HEAD_DIM), lambda h, sb: (0, sb, g * H + h)),
                           pl.BlockSpec((2, halo, HEAD_DIM),
                                        lambda h, sb: (0, jnp.maximum(sb * per - 1, 0), g * H + h)),
                           pl.BlockSpec((1, BLOCK, 2 * BLOCK), lambda h, sb: (h, 0, 0))],
                 out_specs=[out_spec, out_spec],
                 out_shape=[jax.ShapeDtypeStruct((S, D), F32)] * 2,
                 scratch_shapes=[pltpu.VMEM((halo + rows, HEAD_DIM), F32)] * 2
                 + [pltpu.VMEM((dil, sub, HEAD_DIM), BF16)]
                 + [pltpu.VMEM((dil, sub + BLOCK, HEAD_DIM), BF16)] * 2
                 + [pltpu.VMEM((dil, sub, HEAD_DIM), F32)] * 2 + [_split_tmp(sub, dil)],
                 args=[qz, kv, kv, bias], sem=("parallel", "arbitrary"), jobs=jobs)


def merge_fwd(os_, lses, qz, name, jobs=()):
    S, D = os_[0].shape
    T = _tile(S, 256, 8)

    def body(o1, o2, o3, l1, l2, l3, z_ref, o_ref, lse_ref, y_ref):
        la, lb_, lc_ = l1[...], l2[...], l3[...]
        m = jnp.maximum(jnp.maximum(la, lb_), lc_)
        ea, eb, ec = jnp.exp(la - m), jnp.exp(lb_ - m), jnp.exp(lc_ - m)
        den = ea + eb + ec
        o = (ea * o1[...] + eb * o2[...] + ec * o3[...]) / den
        z = z_ref[...]
        o_ref[...] = o.astype(o_ref.dtype)
        lse_ref[...] = m + jnp.log(den)
        y_ref[...] = (o * (z * _sigmoid(z))).astype(BF16)

    row = pl.BlockSpec((T, D), lambda i: (i, 0))
    return _call(body, name=name, grid=(S // T,),
                 in_specs=[row] * 6 + [pl.BlockSpec((T, D), lambda i: (i, N_GROUPS))],
                 out_specs=[row] * 3,
                 out_shape=[jax.ShapeDtypeStruct((S, D), BF16), jax.ShapeDtypeStruct((S, D), F32),
                            jax.ShapeDtypeStruct((S, D), BF16)],
                 args=[*os_, *lses, qz], sem=("parallel",), jobs=jobs)


def merge_bwd(dy2, o, qz, name, jobs=()):
    S, D = o.shape
    H = D // HEAD_DIM
    T = _tile(S, 256, 8)
    nq = N_GROUPS + 1

    def body(dy_ref, o_ref, z_ref, do_ref, dl_ref, dqz_ref):
        dy = dy_ref[...].astype(F32)
        ov = o_ref[...].astype(F32)
        z = z_ref[...]
        sz = _sigmoid(z)
        do = dy * (z * sz)
        do_ref[...] = do.astype(BF16)
        dqz_ref[...] = (dy * ov * (sz * (1.0 + z * (1.0 - sz)))).astype(BF16)
        prod = do * ov
        for h in range(H):
            hs = slice(h * HEAD_DIM, (h + 1) * HEAD_DIM)
            dl_ref[:, hs] = jnp.broadcast_to(jnp.sum(prod[:, hs], axis=-1, keepdims=True), (T, HEAD_DIM))

    row = pl.BlockSpec((T, D), lambda i: (i, 0))
    last = pl.BlockSpec((T, D), lambda i: (i, N_GROUPS))
    return _call(body, name=name, grid=(S // T,), in_specs=[row, row, last], out_specs=[row, row, last],
                 out_shape=[jax.ShapeDtypeStruct((S, D), BF16), jax.ShapeDtypeStruct((S, D), F32),
                            jax.ShapeDtypeStruct((S, nq * D), BF16)],
                 args=[dy2, o, qz], sem=("parallel",), jobs=jobs)


def attn_bwd(qz, kv, do, lse, delta, bias, dqz, dkv, g, dil, name, jobs=()):
    S = qz.shape[0]
    D = qz.shape[1] // (N_GROUPS + 1)
    H = D // HEAD_DIM
    halo, rows, nsb, nblk = _attn_geometry(S, dil)
    sub = nblk * BLOCK
    scale = HEAD_DIM ** -0.5
    have_dkv = dkv is not None

    def body(*refs):
        q_ref, do_ref, l_ref, d_ref, kvc_ref, kvp_ref, b_ref = refs[:7]
        n_in = 7 + 1 + (1 if have_dkv else 0)
        (dq_ref, dkv_ref, ds_ref, ks, vs, dks, dvs, dos, dqs, carry_k, carry_v,
         qd, dod, ld, dd, dqd, kd, vd, dkd, dvd, tmp) = refs[n_in:]
        i = pl.program_id(1)
        sb = nsb - 1 - i

        @pl.when(i == 0)
        def _():
            ds_ref[...] = jnp.zeros_like(ds_ref)

        ks[0:halo, :] = kvp_ref[0].astype(F32)
        ks[halo:, :] = kvc_ref[0].astype(F32)
        vs[0:halo, :] = kvp_ref[1].astype(F32)
        vs[halo:, :] = kvc_ref[1].astype(F32)
        dos[...] = do_ref[...].astype(F32)
        _split(kd, ks, sub + BLOCK, dil, tmp)
        _split(vd, vs, sub + BLOCK, dil, tmp)
        for dst, src in ((qd, q_ref), (dod, dos), (ld, l_ref), (dd, d_ref)):
            _split(dst, src, sub, dil, tmp)
        dkd[...] = jnp.zeros_like(dkd)
        dvd[...] = jnp.zeros_like(dvd)
        band, first = _band_masks(sb > 0)
        bias_t = b_ref[0]
        for r in range(dil):
            for jj in range(nblk):
                own = slice(jj * BLOCK, (jj + 1) * BLOCK)
                keys = slice(jj * BLOCK, (jj + 2) * BLOCK)
                q = qd[r, own, :]
                do = dod[r, own, :]
                k = kd[r, keys, :]
                v = vd[r, keys, :]
                lse = ld[r, own, :]
                dlt = dd[r, own, :]
                s = _dot_nt(q, k) * scale + bias_t - jnp.concatenate([lse, lse], axis=-1)
                p = jnp.where(band if jj > 0 else first, jnp.exp(s), 0.0)
                ds = p * (_dot_nt(do, v) - jnp.concatenate([dlt, dlt], axis=-1))
                ds_ref[0] += ds
                dsb = ds.astype(BF16)
                dqd[r, own, :] = jnp.dot(dsb, k, preferred_element_type=F32) * scale
                dkd[r, keys, :] += _dot_tn(dsb, q) * scale
                dvd[r, keys, :] += _dot_tn(p.astype(BF16), do)
        _merge(dqs, dqd, sub, dil, tmp)
        _merge(dks, dkd, sub + BLOCK, dil, tmp)
        _merge(dvs, dvd, sub + BLOCK, dil, tmp)

        @pl.when(i > 0)
        def _():
            dks[rows:rows + halo, :] += carry_k[...]
            dvs[rows:rows + halo, :] += carry_v[...]

        dkv_ref[0] = dks[halo:, :].astype(BF16)
        dkv_ref[1] = dvs[halo:, :].astype(BF16)
        carry_k[...] = dks[0:halo, :]
        carry_v[...] = dvs[0:halo, :]
        dq_ref[...] = dqs[...].astype(BF16)

    per = rows // halo

    def rev(i):
        return nsb - 1 - i

    bias_spec = pl.BlockSpec((1, BLOCK, 2 * BLOCK), lambda h, i: (h, 0, 0))
    row_spec = pl.BlockSpec((rows, HEAD_DIM), lambda h, i: (rev(i), h))
    in_specs = [pl.BlockSpec((rows, HEAD_DIM), lambda h, i: (rev(i), g * H + h)),
                row_spec, row_spec, row_spec,
                pl.BlockSpec((2, rows, HEAD_DIM), lambda h, i: (0, rev(i), g * H + h)),
                pl.BlockSpec((2, halo, HEAD_DIM),
                             lambda h, i: (0, jnp.maximum(rev(i) * per - 1, 0), g * H + h)),
                bias_spec, ANY_SPEC]
    args = [qz, do, lse, delta, kv, kv, bias, dqz]
    aliases = {7: 0}
    if have_dkv:
        in_specs.append(ANY_SPEC)
        args.append(dkv)
        aliases[8] = 1
    blk = (halo + rows, HEAD_DIM)
    own, keys = (dil, sub, HEAD_DIM), (dil, sub + BLOCK, HEAD_DIM)
    return _call(body, name=name, grid=(H, nsb), in_specs=in_specs,
                 out_specs=[pl.BlockSpec((rows, HEAD_DIM), lambda h, i: (rev(i), g * H + h)),
                            pl.BlockSpec((2, rows, HEAD_DIM), lambda h, i: (0, rev(i), g * H + h)),
                            bias_spec],
                 out_shape=[jax.ShapeDtypeStruct(qz.shape, BF16),
                            jax.ShapeDtypeStruct((2, S, N_GROUPS * D), BF16),
                            jax.ShapeDtypeStruct((H, BLOCK, 2 * BLOCK), F32)],
                 scratch_shapes=[pltpu.VMEM(blk, F32), pltpu.VMEM(blk, F32), pltpu.VMEM(blk, F32),
                                 pltpu.VMEM(blk, F32),
                                 pltpu.VMEM((rows, HEAD_DIM), F32), pltpu.VMEM((rows, HEAD_DIM), F32),
                                 pltpu.VMEM((halo, HEAD_DIM), F32), pltpu.VMEM((halo, HEAD_DIM), F32),
                                 pltpu.VMEM(own, BF16), pltpu.VMEM(own, BF16), pltpu.VMEM(own, F32),
                                 pltpu.VMEM(own, F32), pltpu.VMEM(own, F32),
                                 pltpu.VMEM(keys, BF16), pltpu.VMEM(keys, BF16),
                                 pltpu.VMEM(keys, F32), pltpu.VMEM(keys, F32), _split_tmp(sub, dil)],
                 aliases=aliases, args=args, sem=("parallel", "arbitrary"), jobs=jobs)


def _adamw(w, g, m, v):
    m = ADAM_B1 * m + (1.0 - ADAM_B1) * g
    v = ADAM_B2 * v + (1.0 - ADAM_B2) * (g * g)
    m_hat = m / (1.0 - ADAM_B1 ** ADAM_STEP)
    v_hat = v / (1.0 - ADAM_B2 ** ADAM_STEP)
    delta = -ADAM_LR * (m_hat / (jnp.sqrt(v_hat) + ADAM_EPS) + ADAM_WD * w)
    return delta, m, v


def add_pairs(dw, got, name, jobs=()):
    _, K, n = got.shape
    tk = _tile(K, 512, 16)

    def own_block(r, i, p):
        return 4 * ((p[0] + r // 2) % 2) + 2 * ((p[1] + r % 2) % 2) + p[2], i, 0

    def body(p_ref, a_ref, b_ref, o_ref):
        o_ref[...] = (a_ref[...].astype(F32) + b_ref[...].astype(F32)).astype(o_ref.dtype)

    blk = pl.BlockSpec((None, tk, n), lambda r, i, p: (r, i, 0))
    grid_spec = pltpu.PrefetchScalarGridSpec(
        num_scalar_prefetch=1, grid=(N_CHIPS, K // tk),
        in_specs=[pl.BlockSpec((None, tk, n), own_block), blk], out_specs=blk)
    return pl.pallas_call(body, name=name, grid_spec=grid_spec,
                          out_shape=jax.ShapeDtypeStruct(got.shape, got.dtype))(_place_vector(), dw, got)


def adamw_reduce(pieces, w, m, v, name, jobs=()):
    K, n = w.shape
    kp = K // len(pieces)
    tk = _tile(kp, 256, 8)
    sp = kp // tk

    def body(*refs):
        w_ref, m_ref, v_ref, g_ref, d_ref, nm_ref, nv_ref = refs[2 * len(pieces):]
        i = pl.program_id(0)
        g = None
        for q in range(len(pieces)):
            p_ref, r_ref = refs[2 * q], refs[2 * q + 1]
            gq = p_ref[...].astype(F32)
            for r in range(N_CHIPS - 1):
                gq = gq + r_ref[r].astype(F32)
            g = gq if g is None else jnp.where(i >= q * sp, gq, g)
        d, nm, nv = _adamw(w_ref[...], g, m_ref[...], v_ref[...])
        g_ref[...] = g
        d_ref[...] = d
        nm_ref[...] = nm
        nv_ref[...] = nv

    def piece_specs(q):
        def at(i):
            return jnp.clip(i - q * sp, 0, sp - 1)
        return [pl.BlockSpec((None, tk, n), lambda i: (0, at(i), 0)),
                pl.BlockSpec((N_CHIPS - 1, tk, n), lambda i: (0, at(i), 0))]

    blk = pl.BlockSpec((tk, n), lambda i: (i, 0))
    in_specs, args = [], []
    for q, (part, got) in enumerate(pieces):
        in_specs += piece_specs(q)
        args += [part, got]
    return _call(body, name=name, grid=(K // tk,), in_specs=in_specs + [blk, blk, blk],
                 out_specs=[blk] * 4, out_shape=[jax.ShapeDtypeStruct((K, n), F32)] * 4,
                 args=args + [w, m, v], sem=("parallel",), jobs=jobs)


def sum_parts(parts, name):
    _, R, D = parts.shape

    def body(p_ref, o_ref):
        g = p_ref[0]
        for r in range(1, N_DEV):
            g = g + p_ref[r]
        o_ref[...] = g

    return _call(body, name=name, in_specs=[VMEM_SPEC], out_specs=[VMEM_SPEC],
                 out_shape=[jax.ShapeDtypeStruct((R, D), F32)], args=[parts])[0]


def adamw_small(w, g, m, v, name):
    def body(w_ref, g_ref, m_ref, v_ref, d_ref, nm_ref, nv_ref):
        d, nm, nv = _adamw(w_ref[...], g_ref[...], m_ref[...], v_ref[...])
        d_ref[...] = d
        nm_ref[...] = nm
        nv_ref[...] = nv

    return _call(body, name=name, in_specs=[VMEM_SPEC] * 4, out_specs=[VMEM_SPEC] * 3,
                 out_shape=[jax.ShapeDtypeStruct(w.shape, F32)] * 3, args=[w, g, m, v])


def _kv_row_pieces(K):
    return (0, K // 2), (K // 2, K)


def _row(v, at):
    return jnp.pad(v.reshape(1, -1), ((at, 7 - at), (0, 0)))


def _pack_sharded(norm, conv_w, conv_b, ln_g, ln_b):
    n = norm.shape[-1]
    taps = jnp.pad(conv_w.reshape(CONV_WIDTH, n), ((0, HALO - CONV_WIDTH), (0, 0)))
    return jnp.concatenate([_row(norm, 0), _row(ln_g, 0) + _row(ln_b, 1) + _row(conv_b, 2), taps], axis=0)


def _pack_rel(rel_bias, D):
    return jnp.pad(rel_bias.reshape(1, -1), ((0, 7), (0, D - rel_bias.size)))


def _pack_replicated(kv_norm, b_norm, final_norm, rel_bias, D):
    return jnp.concatenate([_row(kv_norm, 0) + _row(b_norm, 1), _row(final_norm, 0), _pack_rel(rel_bias, D)], axis=0)


def kernel(x, a_norm, a_w_in, a_conv_w, a_conv_b, a_ln_g, a_ln_b, a_w_out, kv_norm, w_kv, b_norm, b_w_in, b_w_out, rel_bias, final_norm, loss_target, m_a_norm, m_a_w_in, m_a_conv_w, m_a_conv_b, m_a_ln_g, m_a_ln_b, m_a_w_out, m_kv_norm, m_w_kv, m_b_norm, m_b_w_in, m_b_w_out, m_rel_bias, m_final_norm, v_a_norm, v_a_w_in, v_a_conv_w, v_a_conv_b, v_a_ln_g, v_a_ln_b, v_a_w_out, v_kv_norm, v_w_kv, v_b_norm, v_b_w_in, v_b_w_out, v_rel_bias, v_final_norm):
    _, S, D = x.shape
    E = D
    H = D // HEAD_DIM
    nsh = D // N_DEV
    x0 = x.reshape(S, D)
    target = loss_target.reshape(S, D)
    kvn, bn, fn = kv_norm.reshape(1, D), b_norm.reshape(1, D), final_norm.reshape(1, D)

    names = ["a_w_in", "a_w_out", "w_kv", "b_w_in", "b_w_out"]
    big_w = dict(zip(names, [a_w_in[0], a_w_out[0], w_kv, b_w_in[0], b_w_out[0]]))
    big_m = dict(zip(names, [m_a_w_in[0], m_a_w_out[0], m_w_kv, m_b_w_in[0], m_b_w_out[0]]))
    big_v = dict(zip(names, [v_a_w_in[0], v_a_w_out[0], v_w_kv, v_b_w_in[0], v_b_w_out[0]]))
    sh_w = _pack_sharded(a_norm, a_conv_w, a_conv_b, a_ln_g, a_ln_b)

    wa_in, sh_all = all_gather([big_w["a_w_in"].astype(BF16), sh_w], "gather_first")
    sh_full = sh_all.transpose(1, 0, 2).reshape(SMALL_SH_ROWS, D)
    an, cw, cb = sh_full[0:1], sh_full[16:16 + HALO], sh_full[10:11]
    lg, lb = sh_full[8:9], sh_full[9:10]

    buf = {nm: place_shard(big_w[nm], "place_" + nm) for nm in names[1:]}
    kv_cut = _kv_row_pieces(D)

    riding = []

    def ride(nm, **kw):
        make = gather_relay_job if set(kw) & {"near", "far", "last"} else gather_job
        riding.append((nm, make(buf[nm], **kw)))
        return riding[-1][1]

    def landed():
        while riding:
            nm, job = riding.pop()
            buf[nm] = job.results[0]

    (h0,) = rms_fwd(x0, [an], "rms_a")
    proj = matmul_nn(h0, wa_in, "a_in",
                     jobs=[ride("a_w_out", chips=(0, E // N_DEV)), ride("w_kv", near=kv_cut[0])])
    landed()
    c, y = conf_fwd(proj, cw, cb, lg, lb, "conf_fwd",
                    jobs=[ride("a_w_out", sibling=(0, E // N_DEV)),
                          ride("w_kv", far=kv_cut[0], near=kv_cut[1])])
    landed()
    wa_out = buf["a_w_out"].reshape(1, E, D)
    x1 = matmul_nn(y, wa_out, "a_out", res=x0, jobs=[ride("w_kv", last=kv_cut[0], far=kv_cut[1])])
    landed()
    hk, hb = rms_fwd(x1, [kvn, bn], "rms_b", jobs=[ride("w_kv", last=kv_cut[1])])
    landed()
    wkv = buf["w_kv"]
    kv = matmul_nn(hk, wkv, "kv_proj", out_dtype=BF16, kv_split=True, jobs=[ride("b_w_in", chips=(0, D))])
    landed()

    bt = jnp.asarray(bucket_tables())
    onehot = (bt[None, :] == jnp.arange(N_BUCKETS, dtype=jnp.int32)[:, None]).astype(F32)
    bias = small_dot(rel_bias.T, onehot, "nn", "bias_table", jobs=[ride("b_w_in", sibling=(0, D))])
    landed()
    bias = bias.reshape(H, N_GROUPS, BLOCK, 2 * BLOCK)
    bias = [bias[:, g] for g in range(N_GROUPS)]
    wb_in = buf["b_w_in"]
    qz = matmul_nn(hb, wb_in, "b_in", jobs=[ride("b_w_out", chips=(0, D // N_DEV))])
    landed()

    os_, lses = [], []
    for g, (_, dil) in enumerate(GROUPS):
        o_g, l_g = attn_fwd(qz, kv, bias[g], g, dil, "attn_fwd%d" % g,
                            jobs=[ride("b_w_out", sibling=(0, D // N_DEV))] if g == 0 else ())
        landed()
        os_.append(o_g)
        lses.append(l_g)
    wb_out = buf["b_w_out"].reshape(1, D, D)
    o, lse, y2 = merge_fwd(os_, lses, qz, "merge_fwd")
    x2 = matmul_nn(y2, wb_out, "b_out", res=x1)
    dx2, dx2b, fin_acc = final_loss(x2, fn, target, "final_loss")

    dy2 = matmul_nt(dx2b, wb_out, "b_out_dx")
    dwb_out = matmul_tn(y2, dx2b, 1, "b_out_dw").reshape(N_DEV, D // N_DEV, D)
    r_b_out = reduce_sibling_job(dwb_out)
    do, delta, dqz = merge_bwd(dy2, o, qz, "merge_bwd", jobs=[r_b_out])
    p_b_out = add_pairs(dwb_out, r_b_out.results[0], "pairs_b_out")
    r_b_out2 = reduce_chips_job(p_b_out)
    dkv = None
    ds_tabs = []
    for g, (_, dil) in enumerate(GROUPS):
        dqz, dkv, ds_tab = attn_bwd(qz, kv, do, lse, delta, bias[g], dqz, dkv, g, dil,
                                    "attn_bwd%d" % g, jobs=[r_b_out2] if g == 0 else ())
        ds_tabs.append(ds_tab.reshape(H, 2 * BLOCK * BLOCK))
    d_rel = small_dot(onehot, jnp.concatenate(ds_tabs, axis=1), "nt", "bias_grad")
    dw_kv_lo = matmul_tn(hk, dkv, N_DEV, "kv_dw_lo", kv_split=True, k_tiles=(0, 1))
    r_kv_lo = reduce_sibling_job(dw_kv_lo)
    dw_kv_hi = matmul_tn(hk, dkv, N_DEV, "kv_dw_hi", kv_split=True, k_tiles=(1, 2), jobs=[r_kv_lo])
    p_kv_lo = add_pairs(dw_kv_lo, r_kv_lo.results[0], "pairs_kv_lo")
    r_kv_lo2, r_kv_hi = reduce_chips_job(p_kv_lo), reduce_sibling_job(dw_kv_hi)
    dhk = matmul_nt(dkv, wkv, "kv_dx", kv_split=True, jobs=[r_kv_lo2, r_kv_hi])
    p_kv_hi = add_pairs(dw_kv_hi, r_kv_hi.results[0], "pairs_kv_hi")
    r_kv_hi2 = reduce_chips_job(p_kv_hi)
    dwb_in = matmul_tn(hb, dqz, N_DEV, "b_in_dw", jobs=[r_kv_hi2])
    r_b_in = reduce_sibling_job(dwb_in)
    dhb = matmul_nt(dqz, wb_in, "b_in_dx", jobs=[r_b_in])
    p_b_in = add_pairs(dwb_in, r_b_in.results[0], "pairs_b_in")
    r_b_in2 = reduce_chips_job(p_b_in)
    dx1, dx1b, norm_acc = rms_bwd(x1, [dhk, dhb], [kvn, bn], dx2, "rms_b_bwd", True)

    dy = matmul_nt(dx1b, wa_out, "a_out_dx")
    dwa_out = matmul_tn(y, dx1b, 1, "a_out_dw").reshape(N_DEV, E // N_DEV, D)
    r_a_out = reduce_sibling_job(dwa_out)
    dc, dz, ln_acc = conf_bwd_ln(c, dy, proj, lg, lb, "conf_bwd_ln", jobs=[r_a_out])
    p_a_out = add_pairs(dwa_out, r_a_out.results[0], "pairs_a_out")
    r_a_out2 = reduce_chips_job(p_a_out)
    dproj, dcw = conf_bwd_conv(proj, dc, dz, cw, "conf_bwd_conv", jobs=[r_b_in2, r_a_out2])
    share = share_small_job(jnp.concatenate([ln_acc, dcw, norm_acc, fin_acc, _pack_rel(d_rel, D)], axis=0))
    dwa_lo = matmul_tn(h0, dproj, N_DEV, "a_in_dw_lo", k_tiles=(0, 1), jobs=[share])
    r_lo = reduce_sibling_job(dwa_lo)
    dwa_hi = matmul_tn(h0, dproj, N_DEV, "a_in_dw_hi", k_tiles=(1, 2), jobs=[r_lo])
    p_lo = add_pairs(dwa_lo, r_lo.results[0], "pairs_a_in_lo")
    r_lo2, r_hi = reduce_chips_job(p_lo), reduce_sibling_job(dwa_hi)
    dh0 = matmul_nt(dproj, wa_in, "a_in_dx_lo", rows=(0, 2), jobs=[r_lo2, r_hi])
    p_hi = add_pairs(dwa_hi, r_hi.results[0], "pairs_a_in_hi")
    r_hi2 = reduce_chips_job(p_hi)
    dh0 = matmul_nt(dproj, wa_in, "a_in_dx_hi", rows=(1, 2), into=dh0, jobs=[r_hi2])
    grad_x, a_acc = rms_bwd(x0, [dh0], [an], dx1, "rms_a_bwd", False)

    share_a = share_small_job(a_acc)
    sums = {"w_kv": [(p_kv_lo, r_kv_lo2), (p_kv_hi, r_kv_hi2)], "b_w_in": [(p_b_in, r_b_in2)], "b_w_out": [(p_b_out, r_b_out2)],
            "a_w_out": [(p_a_out, r_a_out2)], "a_w_in": [(p_lo, r_lo2), (p_hi, r_hi2)]}
    big_out = {}
    for nm, pieces in sums.items():
        big_out[nm] = adamw_reduce([(part, job.results[0]) for part, job in pieces],
                                   big_w[nm], big_m[nm], big_v[nm], "adamw_" + nm,
                                   jobs=[share_a] if nm == "a_w_out" else ())

    gsum = jnp.concatenate([sum_parts(share_a.results[0], "sum_small_a"),
                            sum_parts(share.results[0], "sum_small")], axis=0)
    me = 4 * lax.axis_index("x") + 2 * lax.axis_index("y") + lax.axis_index("c")
    g_sh = lax.dynamic_slice(gsum, (0, me * nsh), (SMALL_SH_ROWS, nsh))
    g_rep = gsum[SMALL_SH_ROWS:]
    loss = g_rep[9, 0]
    sh_m = _pack_sharded(m_a_norm, m_a_conv_w, m_a_conv_b, m_a_ln_g, m_a_ln_b)
    sh_v = _pack_sharded(v_a_norm, v_a_conv_w, v_a_conv_b, v_a_ln_g, v_a_ln_b)
    sh_d, sh_nm, sh_nv = adamw_small(sh_w, g_sh, sh_m, sh_v, "adamw_sharded")
    rep_w = _pack_replicated(kv_norm, b_norm, final_norm, rel_bias, D)
    rep_m = _pack_replicated(m_kv_norm, m_b_norm, m_final_norm, m_rel_bias, D)
    rep_v = _pack_replicated(v_kv_norm, v_b_norm, v_final_norm, v_rel_bias, D)
    rep_d, rep_nm, rep_nv = adamw_small(rep_w, g_rep, rep_m, rep_v, "adamw_replicated")

    def unpack(kind):
        sh = (g_sh, sh_d, sh_nm, sh_nv)[kind]
        rep = (g_rep, rep_d, rep_nm, rep_nv)[kind]
        big = {nm: big_out[nm][kind] for nm in names}
        nb = rel_bias.size
        return [
            sh[0:1],
            big["a_w_in"][None],
            sh[16:16 + CONV_WIDTH][None],
            sh[10:11], sh[8:9], sh[9:10],
            big["a_w_out"][None],
            rep[0],
            big["w_kv"],
            rep[1:2],
            big["b_w_in"][None],
            big["b_w_out"][None],
            rep[16, :nb].reshape(rel_bias.shape),
            rep[8],
        ]

    return (loss, grad_x.reshape(1, S, D), *unpack(0), *unpack(1), *unpack(2), *unpack(3))
```
